```python
import math
import jax, jax.numpy as jnp
from jax import lax
import numpy as np

D_MODEL = 1024
BATCH = 8
SEQ = 8192
DEPTH = 1

HEAD_DIM = 128
ATTN_GROUPS = ((128, 1), (512, 4), (2048, 16))
N_GROUPS = 3
HEADS_PER_GROUP = 4
ATTN_QKV_W = N_GROUPS * HEADS_PER_GROUP * HEAD_DIM
ATTN_OUT = HEADS_PER_GROUP * HEAD_DIM
CONV_CH = 512
CONV_WIDTH = 31
N_MEM = 256
N_XHEADS = 4
XATTN_W = N_XHEADS * HEAD_DIM
N_BRANCH = 3
D_FF = 2816
FFN_CONV_WIDTH = 3
N_BUCKETS = 32
MAX_DISTANCE = 2048
N_REL_HEADS = N_GROUPS * HEADS_PER_GROUP
RMS_EPS = 1e-6
LN_EPS = 1e-5
N_IN = 3 * ATTN_QKV_W + 2 * CONV_CH + XATTN_W + N_BRANCH * D_MODEL
IN_SPLITS = (ATTN_QKV_W, 2 * ATTN_QKV_W, 3 * ATTN_QKV_W,
             3 * ATTN_QKV_W + CONV_CH, 3 * ATTN_QKV_W + 2 * CONV_CH,
             3 * ATTN_QKV_W + 2 * CONV_CH + XATTN_W)

kernel_name = "hybrid_gated_dilated_conformer_block"


def rms_norm(x, w, eps=RMS_EPS):
    xf = x.astype(jnp.float32)
    y = xf * lax.rsqrt(jnp.mean(xf * xf, axis=-1, keepdims=True) + eps)
    return (y * w.astype(jnp.float32)).astype(x.dtype)


def layer_norm(x, w, b, eps=LN_EPS):
    xf = x.astype(jnp.float32)
    mu = jnp.mean(xf, axis=-1, keepdims=True)
    xc = xf - mu
    y = xc * lax.rsqrt(jnp.mean(xc * xc, axis=-1, keepdims=True) + eps)
    return (y * w.astype(jnp.float32) + b.astype(jnp.float32)).astype(x.dtype)


def causal_depthwise_conv(x, w, b):
    K, C = w.shape
    y = lax.conv_general_dilated(
        x, w[:, None, :].astype(x.dtype), window_strides=(1,), padding=[(K - 1, 0)],
        dimension_numbers=("NWC", "WIO", "NWC"), feature_group_count=C)
    return y + b.astype(x.dtype)


def t5_bucket(dist):
    max_exact = N_BUCKETS // 2
    d = jnp.maximum(dist.astype(jnp.float32), 1.0)
    large = max_exact + (jnp.log(d / max_exact) / math.log(MAX_DISTANCE / max_exact)
                         * (N_BUCKETS - max_exact)).astype(jnp.int32)
    large = jnp.minimum(large, N_BUCKETS - 1)
    return jnp.where(dist < max_exact, dist, large)


def dilated_window_attention(q, k, v, bias_cols, window, dilation):
    B, S, H, Dh = q.shape
    n = window // dilation
    span = n * dilation
    s_pad = -(-S // span) * span
    nb = s_pad // span

    def to_blocks(a):
        a = jnp.pad(a, ((0, 0), (0, s_pad - S), (0, 0), (0, 0)))
        a = a.reshape(B, nb * n, dilation, H, Dh).transpose(0, 2, 1, 3, 4)
        return a.reshape(B, dilation, nb, n, H, Dh)

    def with_prev(a):
        prev = jnp.pad(a, ((0, 0), (0, 0), (1, 0), (0, 0), (0, 0), (0, 0)))[:, :, :-1]
        return jnp.concatenate([prev, a], axis=3)

    qb = to_blocks(q)
    kk = with_prev(to_blocks(k))
    vv = with_prev(to_blocks(v))

    qi = jnp.arange(n)[:, None]
    kj = jnp.arange(2 * n)[None, :]
    step = qi + n - kj
    band = (step >= 0) & (step <= n)
    not_before_start = (jnp.arange(nb)[:, None, None] > 0) | (kj[None] >= n)
    mask = band[None] & not_before_start
    bucket = t5_bucket(jnp.clip(step, 0, None) * dilation)
    bias = jnp.transpose(bias_cols[bucket].astype(jnp.float32), (2, 0, 1))

    scores = jnp.einsum("brnqhd,brnkhd->brnhqk", qb, kk).astype(jnp.float32) * (Dh ** -0.5) + bias
    scores = jnp.where(mask[:, None], scores, -jnp.inf)
    m = jnp.max(scores, axis=-1, keepdims=True)
    p = jnp.exp(scores - m)
    l = jnp.sum(p, axis=-1, keepdims=True)
    o = jnp.einsum("brnhqk,brnkhd->brnqhd", p.astype(v.dtype), vv).astype(jnp.float32)
    l_t = jnp.swapaxes(l[..., 0], 3, 4)
    lse_t = jnp.swapaxes((m + jnp.log(l))[..., 0], 3, 4)
    o = o / l_t[..., None]

    def from_blocks(a):
        a = a.reshape(B, dilation, nb * n, H, -1).transpose(0, 2, 1, 3, 4)
        return a.reshape(B, s_pad, H, -1)[:, :S]

    return from_blocks(o), from_blocks(lse_t[..., None])[..., 0]


def memory_cross_attention(zq, mem, mem_norm_w, w_mem_kv, xq_norm_w, xk_norm_w):
    B, S, _ = zq.shape
    mem_n = rms_norm(mem, mem_norm_w)
    kv = mem_n @ w_mem_kv
    mk, mv = jnp.split(kv, 2, axis=-1)
    mk = rms_norm(mk.reshape(B, -1, N_XHEADS, HEAD_DIM), xk_norm_w)
    mv = mv.reshape(B, -1, N_XHEADS, HEAD_DIM)
    q = rms_norm(zq.reshape(B, S, N_XHEADS, HEAD_DIM), xq_norm_w)
    s = jnp.einsum("bshd,bmhd->bhsm", q, mk).astype(jnp.float32) * (HEAD_DIM ** -0.5)
    p = jax.nn.softmax(s, axis=-1).astype(mv.dtype)
    o = jnp.einsum("bhsm,bmhd->bshd", p, mv)
    return o.reshape(B, S, XATTN_W)


def _fwd_setup_inputs(seed: int = 0) -> dict:
    key = jax.random.key(seed)
    ks = iter(jax.random.split(key, 32))
    nrm = lambda shape, scale: scale * jax.random.normal(next(ks), shape, jnp.float32)
    L = DEPTH
    return {
        "x": nrm((BATCH, SEQ, D_MODEL), 1.0),
        "mem": nrm((BATCH, N_MEM, D_MODEL), 1.0),
        "rel_bias_table": nrm((N_BUCKETS, N_REL_HEADS), 0.2),
        "attn_norm_w": 1.0 + nrm((L, D_MODEL), 0.02),
        "w_in": nrm((L, D_MODEL, N_IN), D_MODEL ** -0.5),
        "b_gate": nrm((L, N_BRANCH * D_MODEL), 0.02),
        "q_norm_w": 1.0 + nrm((L, N_GROUPS, HEAD_DIM), 0.02),
        "k_norm_w": 1.0 + nrm((L, N_GROUPS, HEAD_DIM), 0.02),
        "w_attn_o": nrm((L, ATTN_OUT, D_MODEL), ATTN_OUT ** -0.5),
        "conv_dw_w": nrm((L, CONV_WIDTH, CONV_CH), CONV_WIDTH ** -0.5),
        "conv_dw_b": nrm((L, CONV_CH), 0.02),
        "conv_ln_w": 1.0 + nrm((L, CONV_CH), 0.02),
        "conv_ln_b": nrm((L, CONV_CH), 0.02),
        "w_conv_o": nrm((L, CONV_CH, D_MODEL), CONV_CH ** -0.5),
        "mem_norm_w": 1.0 + nrm((L, D_MODEL), 0.02),
        "w_mem_kv": nrm((L, D_MODEL, 2 * XATTN_W), D_MODEL ** -0.5),
        "xq_norm_w": 1.0 + nrm((L, HEAD_DIM), 0.02),
        "xk_norm_w": 1.0 + nrm((L, HEAD_DIM), 0.02),
        "w_cross_o": nrm((L, XATTN_W, D_MODEL), XATTN_W ** -0.5),
        "w_out": nrm((L, D_MODEL, D_MODEL), D_MODEL ** -0.5),
        "ffn_norm_w": 1.0 + nrm((L, D_MODEL), 0.02),
        "w_up": nrm((L, D_MODEL, 2 * D_FF), D_MODEL ** -0.5),
        "ffn_conv_w": nrm((L, FFN_CONV_WIDTH, 2 * D_FF), FFN_CONV_WIDTH ** -0.5),
        "ffn_conv_b": nrm((L, 2 * D_FF), 0.02),
        "w_down": nrm((L, D_FF, D_MODEL), D_FF ** -0.5),
    }


def _fwd_reference(x, mem, rel_bias_table, attn_norm_w, w_in, b_gate, q_norm_w, k_norm_w, w_attn_o,
              conv_dw_w, conv_dw_b, conv_ln_w, conv_ln_b, w_conv_o, mem_norm_w, w_mem_kv,
              xq_norm_w, xk_norm_w, w_cross_o, w_out, ffn_norm_w, w_up, ffn_conv_w, ffn_conv_b,
              w_down):
    B, S, _ = x.shape
    h = x
    for l in range(DEPTH):
        xn = rms_norm(h, attn_norm_w[l])
        z = xn @ w_in[l]
        zq, zk, zv, zc_val, zc_gate, zx_q, zg = jnp.split(z, IN_SPLITS, axis=-1)

        q = rms_norm(zq.reshape(B, S, N_GROUPS, HEADS_PER_GROUP, HEAD_DIM), q_norm_w[l][:, None, :])
        k = rms_norm(zk.reshape(B, S, N_GROUPS, HEADS_PER_GROUP, HEAD_DIM), k_norm_w[l][:, None, :])
        v = zv.reshape(B, S, N_GROUPS, HEADS_PER_GROUP, HEAD_DIM)
        outs, lses = [], []
        for g, (win, dil) in enumerate(ATTN_GROUPS):
            cols = rel_bias_table[:, g * HEADS_PER_GROUP:(g + 1) * HEADS_PER_GROUP]
            o_g, lse_g = dilated_window_attention(q[:, :, g], k[:, :, g], v[:, :, g], cols, win, dil)
            outs.append(o_g)
            lses.append(lse_g)
        wts = jax.nn.softmax(jnp.stack(lses), axis=0)
        attn = jnp.sum(wts[..., None] * jnp.stack(outs), axis=0).astype(x.dtype)
        y_attn = attn.reshape(B, S, ATTN_OUT) @ w_attn_o[l]

        u = zc_val * jax.nn.sigmoid(zc_gate)
        u = causal_depthwise_conv(u, conv_dw_w[l], conv_dw_b[l])
        u = jax.nn.silu(layer_norm(u, conv_ln_w[l], conv_ln_b[l]))
        y_conv = u @ w_conv_o[l]

        y_cross = memory_cross_attention(zx_q, mem, mem_norm_w[l], w_mem_kv[l],
                                         xq_norm_w[l], xk_norm_w[l]) @ w_cross_o[l]

        gates = jax.nn.sigmoid((zg + b_gate[l]).astype(jnp.float32)).astype(x.dtype)
        gates = gates.reshape(B, S, N_BRANCH, D_MODEL)
        merged = gates[:, :, 0] * y_attn + gates[:, :, 1] * y_conv + gates[:, :, 2] * y_cross
        h = h + merged @ w_out[l]

        hn = rms_norm(h, ffn_norm_w[l])
        up = causal_depthwise_conv(hn @ w_up[l], ffn_conv_w[l], ffn_conv_b[l])
        a, gt = jnp.split(up, 2, axis=-1)
        h = h + (jax.nn.silu(gt) * a) @ w_down[l]
    return h


import jax as _jax
import jax.numpy as _jnp

TWIN_FORMAT = 'train_step'
FWD_PARAMS = ['x', 'mem', 'rel_bias_table', 'attn_norm_w', 'w_in', 'b_gate', 'q_norm_w', 'k_norm_w', 'w_attn_o', 'conv_dw_w', 'conv_dw_b', 'conv_ln_w', 'conv_ln_b', 'w_conv_o', 'mem_norm_w', 'w_mem_kv', 'xq_norm_w', 'xk_norm_w', 'w_cross_o', 'w_out', 'ffn_norm_w', 'w_up', 'ffn_conv_w', 'ffn_conv_b', 'w_down']
TWIN_WEIGHTS = ['rel_bias_table', 'attn_norm_w', 'w_in', 'b_gate', 'q_norm_w', 'k_norm_w', 'w_attn_o', 'conv_dw_w', 'conv_dw_b', 'conv_ln_w', 'conv_ln_b', 'w_conv_o', 'mem_norm_w', 'w_mem_kv', 'xq_norm_w', 'xk_norm_w', 'w_cross_o', 'w_out', 'ffn_norm_w', 'w_up', 'ffn_conv_w', 'ffn_conv_b', 'w_down']
TWIN_DIFF_INPUT = 'x'
TWIN_INPUTS = ['x', 'mem', 'rel_bias_table', 'attn_norm_w', 'w_in', 'b_gate', 'q_norm_w', 'k_norm_w', 'w_attn_o', 'conv_dw_w', 'conv_dw_b', 'conv_ln_w', 'conv_ln_b', 'w_conv_o', 'mem_norm_w', 'w_mem_kv', 'xq_norm_w', 'xk_norm_w', 'w_cross_o', 'w_out', 'ffn_norm_w', 'w_up', 'ffn_conv_w', 'ffn_conv_b', 'w_down', 'loss_target', 'm_rel_bias_table', 'm_attn_norm_w', 'm_w_in', 'm_b_gate', 'm_q_norm_w', 'm_k_norm_w', 'm_w_attn_o', 'm_conv_dw_w', 'm_conv_dw_b', 'm_conv_ln_w', 'm_conv_ln_b', 'm_w_conv_o', 'm_mem_norm_w', 'm_w_mem_kv', 'm_xq_norm_w', 'm_xk_norm_w', 'm_w_cross_o', 'm_w_out', 'm_ffn_norm_w', 'm_w_up', 'm_ffn_conv_w', 'm_ffn_conv_b', 'm_w_down', 'v_rel_bias_table', 'v_attn_norm_w', 'v_w_in', 'v_b_gate', 'v_q_norm_w', 'v_k_norm_w', 'v_w_attn_o', 'v_conv_dw_w', 'v_conv_dw_b', 'v_conv_ln_w', 'v_conv_ln_b', 'v_w_conv_o', 'v_mem_norm_w', 'v_w_mem_kv', 'v_xq_norm_w', 'v_xk_norm_w', 'v_w_cross_o', 'v_w_out', 'v_ffn_norm_w', 'v_w_up', 'v_ffn_conv_w', 'v_ffn_conv_b', 'v_w_down']
TWIN_OUTPUTS = ['loss', 'grad_x', 'grad_rel_bias_table', 'grad_attn_norm_w', 'grad_w_in', 'grad_b_gate', 'grad_q_norm_w', 'grad_k_norm_w', 'grad_w_attn_o', 'grad_conv_dw_w', 'grad_conv_dw_b', 'grad_conv_ln_w', 'grad_conv_ln_b', 'grad_w_conv_o', 'grad_mem_norm_w', 'grad_w_mem_kv', 'grad_xq_norm_w', 'grad_xk_norm_w', 'grad_w_cross_o', 'grad_w_out', 'grad_ffn_norm_w', 'grad_w_up', 'grad_ffn_conv_w', 'grad_ffn_conv_b', 'grad_w_down', 'delta_rel_bias_table', 'delta_attn_norm_w', 'delta_w_in', 'delta_b_gate', 'delta_q_norm_w', 'delta_k_norm_w', 'delta_w_attn_o', 'delta_conv_dw_w', 'delta_conv_dw_b', 'delta_conv_ln_w', 'delta_conv_ln_b', 'delta_w_conv_o', 'delta_mem_norm_w', 'delta_w_mem_kv', 'delta_xq_norm_w', 'delta_xk_norm_w', 'delta_w_cross_o', 'delta_w_out', 'delta_ffn_norm_w', 'delta_w_up', 'delta_ffn_conv_w', 'delta_ffn_conv_b', 'delta_w_down', 'new_m_rel_bias_table', 'new_m_attn_norm_w', 'new_m_w_in', 'new_m_b_gate', 'new_m_q_norm_w', 'new_m_k_norm_w', 'new_m_w_attn_o', 'new_m_conv_dw_w', 'new_m_conv_dw_b', 'new_m_conv_ln_w', 'new_m_conv_ln_b', 'new_m_w_conv_o', 'new_m_mem_norm_w', 'new_m_w_mem_kv', 'new_m_xq_norm_w', 'new_m_xk_norm_w', 'new_m_w_cross_o', 'new_m_w_out', 'new_m_ffn_norm_w', 'new_m_w_up', 'new_m_ffn_conv_w', 'new_m_ffn_conv_b', 'new_m_w_down', 'new_v_rel_bias_table', 'new_v_attn_norm_w', 'new_v_w_in', 'new_v_b_gate', 'new_v_q_norm_w', 'new_v_k_norm_w', 'new_v_w_attn_o', 'new_v_conv_dw_w', 'new_v_conv_dw_b', 'new_v_conv_ln_w', 'new_v_conv_ln_b', 'new_v_w_conv_o', 'new_v_mem_norm_w', 'new_v_w_mem_kv', 'new_v_xq_norm_w', 'new_v_xk_norm_w', 'new_v_w_cross_o', 'new_v_w_out', 'new_v_ffn_norm_w', 'new_v_w_up', 'new_v_ffn_conv_w', 'new_v_ffn_conv_b', 'new_v_w_down']
TWIN_LEAF_KINDS = {'loss': 'loss', 'grad_x': 'grad_x', 'grad_rel_bias_table': 'grad_w', 'grad_attn_norm_w': 'grad_w', 'grad_w_in': 'grad_w', 'grad_b_gate': 'grad_w', 'grad_q_norm_w': 'grad_w', 'grad_k_norm_w': 'grad_w', 'grad_w_attn_o': 'grad_w', 'grad_conv_dw_w': 'grad_w', 'grad_conv_dw_b': 'grad_w', 'grad_conv_ln_w': 'grad_w', 'grad_conv_ln_b': 'grad_w', 'grad_w_conv_o': 'grad_w', 'grad_mem_norm_w': 'grad_w', 'grad_w_mem_kv': 'grad_w', 'grad_xq_norm_w': 'grad_w', 'grad_xk_norm_w': 'grad_w', 'grad_w_cross_o': 'grad_w', 'grad_w_out': 'grad_w', 'grad_ffn_norm_w': 'grad_w', 'grad_w_up': 'grad_w', 'grad_ffn_conv_w': 'grad_w', 'grad_ffn_conv_b': 'grad_w', 'grad_w_down': 'grad_w', 'delta_rel_bias_table': 'delta_w', 'delta_attn_norm_w': 'delta_w', 'delta_w_in': 'delta_w', 'delta_b_gate': 'delta_w', 'delta_q_norm_w': 'delta_w', 'delta_k_norm_w': 'delta_w', 'delta_w_attn_o': 'delta_w', 'delta_conv_dw_w': 'delta_w', 'delta_conv_dw_b': 'delta_w', 'delta_conv_ln_w': 'delta_w', 'delta_conv_ln_b': 'delta_w', 'delta_w_conv_o': 'delta_w', 'delta_mem_norm_w': 'delta_w', 'delta_w_mem_kv': 'delta_w', 'delta_xq_norm_w': 'delta_w', 'delta_xk_norm_w': 'delta_w', 'delta_w_cross_o': 'delta_w', 'delta_w_out': 'delta_w', 'delta_ffn_norm_w': 'delta_w', 'delta_w_up': 'delta_w', 'delta_ffn_conv_w': 'delta_w', 'delta_ffn_conv_b': 'delta_w', 'delta_w_down': 'delta_w', 'new_m_rel_bias_table': 'new_m', 'new_m_attn_norm_w': 'new_m', 'new_m_w_in': 'new_m', 'new_m_b_gate': 'new_m', 'new_m_q_norm_w': 'new_m', 'new_m_k_norm_w': 'new_m', 'new_m_w_attn_o': 'new_m', 'new_m_conv_dw_w': 'new_m', 'new_m_conv_dw_b': 'new_m', 'new_m_conv_ln_w': 'new_m', 'new_m_conv_ln_b': 'new_m', 'new_m_w_conv_o': 'new_m', 'new_m_mem_norm_w': 'new_m', 'new_m_w_mem_kv': 'new_m', 'new_m_xq_norm_w': 'new_m', 'new_m_xk_norm_w': 'new_m', 'new_m_w_cross_o': 'new_m', 'new_m_w_out': 'new_m', 'new_m_ffn_norm_w': 'new_m', 'new_m_w_up': 'new_m', 'new_m_ffn_conv_w': 'new_m', 'new_m_ffn_conv_b': 'new_m', 'new_m_w_down': 'new_m', 'new_v_rel_bias_table': 'new_v', 'new_v_attn_norm_w': 'new_v', 'new_v_w_in': 'new_v', 'new_v_b_gate': 'new_v', 'new_v_q_norm_w': 'new_v', 'new_v_k_norm_w': 'new_v', 'new_v_w_attn_o': 'new_v', 'new_v_conv_dw_w': 'new_v', 'new_v_conv_dw_b': 'new_v', 'new_v_conv_ln_w': 'new_v', 'new_v_conv_ln_b': 'new_v', 'new_v_w_conv_o': 'new_v', 'new_v_mem_norm_w': 'new_v', 'new_v_w_mem_kv': 'new_v', 'new_v_xq_norm_w': 'new_v', 'new_v_xk_norm_w': 'new_v', 'new_v_w_cross_o': 'new_v', 'new_v_w_out': 'new_v', 'new_v_ffn_norm_w': 'new_v', 'new_v_w_up': 'new_v', 'new_v_ffn_conv_w': 'new_v', 'new_v_ffn_conv_b': 'new_v', 'new_v_w_down': 'new_v'}


def _forward(args):
    return _fwd_reference(*[args[k] for k in FWD_PARAMS])


def _output_shape():
    def fwd():
        inp = _fwd_setup_inputs(0)
        return _fwd_reference(*[inp[k] for k in FWD_PARAMS])
    out = _jax.eval_shape(fwd)
    return out.shape, out.dtype

N_MICROBATCH = 1
ADAM_LR = 0.001
ADAM_B1 = 0.9
ADAM_B2 = 0.999
ADAM_EPS = 1e-08
ADAM_WD = 0.01
ADAM_STEP = 10
PER_EXAMPLE_BATCH_AXIS = {'x': 0, 'mem': 0, 'loss_target': 0}
SHARED_INPUTS = []
_WEIGHT_DTYPES = {'rel_bias_table': _jnp.float32, 'attn_norm_w': _jnp.float32, 'w_in': _jnp.float32, 'b_gate': _jnp.float32, 'q_norm_w': _jnp.float32, 'k_norm_w': _jnp.float32, 'w_attn_o': _jnp.float32, 'conv_dw_w': _jnp.float32, 'conv_dw_b': _jnp.float32, 'conv_ln_w': _jnp.float32, 'conv_ln_b': _jnp.float32, 'w_conv_o': _jnp.float32, 'mem_norm_w': _jnp.float32, 'w_mem_kv': _jnp.float32, 'xq_norm_w': _jnp.float32, 'xk_norm_w': _jnp.float32, 'w_cross_o': _jnp.float32, 'w_out': _jnp.float32, 'ffn_norm_w': _jnp.float32, 'w_up': _jnp.float32, 'ffn_conv_w': _jnp.float32, 'ffn_conv_b': _jnp.float32, 'w_down': _jnp.float32}
MOMENT_SCALE = {'rel_bias_table': 1.377878e-01, 'attn_norm_w': 6.633062e-01, 'w_in': 9.742788e-02, 'b_gate': 1.506177e+00, 'q_norm_w': 2.942233e-01, 'k_norm_w': 2.953547e-01, 'w_attn_o': 8.040575e-02, 'conv_dw_w': 7.561736e-01, 'conv_dw_b': 1.502830e+01, 'conv_ln_w': 1.757802e+01, 'conv_ln_b': 1.351536e+01, 'w_conv_o': 2.402701e+00, 'mem_norm_w': 2.898324e-01, 'w_mem_kv': 2.120311e-01, 'xq_norm_w': 1.555201e+00, 'xk_norm_w': 1.559930e+00, 'w_cross_o': 2.317448e-01, 'w_out': 2.252204e+00, 'ffn_norm_w': 5.294340e+01, 'w_up': 8.924974e-01, 'ffn_conv_w': 7.562342e+00, 'ffn_conv_b': 6.703548e+00, 'w_down': 6.750046e-01}


def _to_microbatches(a, axis):
    t = _jnp.moveaxis(a, axis, 0)
    t = t.reshape((N_MICROBATCH, t.shape[0] // N_MICROBATCH) + t.shape[1:])
    return _jnp.moveaxis(t, 1, axis + 1)


def setup_inputs(seed: int = 0) -> dict:
    inp = _fwd_setup_inputs(seed)
    key = _jax.random.fold_in(_jax.random.key(seed), 7919)
    shape, _ = _output_shape()
    out = dict(inp)
    out["loss_target"] = _jax.random.normal(_jax.random.fold_in(key, 0), shape, _jnp.float32)
    for i, name in enumerate(TWIN_WEIGHTS):
        w = inp[name].astype(_jnp.float32)
        if MOMENT_SCALE is None:
            s = _jnp.sqrt(_jnp.mean(_jnp.square(w)) + 1e-30)
        else:
            s = MOMENT_SCALE[name]
        km, kv = _jax.random.split(_jax.random.fold_in(key, i + 1))
        out[name] = w
        out["m_" + name] = s * _jax.random.normal(km, w.shape, _jnp.float32)
        out["v_" + name] = (s * s) * _jax.random.uniform(kv, w.shape, _jnp.float32, 0.5, 1.5)
    if N_MICROBATCH > 1:
        for name, axis in PER_EXAMPLE_BATCH_AXIS.items():
            out[name] = _to_microbatches(out[name], axis)
    return {'x': out['x'], 'mem': out['mem'], 'rel_bias_table': out['rel_bias_table'], 'attn_norm_w': out['attn_norm_w'], 'w_in': out['w_in'], 'b_gate': out['b_gate'], 'q_norm_w': out['q_norm_w'], 'k_norm_w': out['k_norm_w'], 'w_attn_o': out['w_attn_o'], 'conv_dw_w': out['conv_dw_w'], 'conv_dw_b': out['conv_dw_b'], 'conv_ln_w': out['conv_ln_w'], 'conv_ln_b': out['conv_ln_b'], 'w_conv_o': out['w_conv_o'], 'mem_norm_w': out['mem_norm_w'], 'w_mem_kv': out['w_mem_kv'], 'xq_norm_w': out['xq_norm_w'], 'xk_norm_w': out['xk_norm_w'], 'w_cross_o': out['w_cross_o'], 'w_out': out['w_out'], 'ffn_norm_w': out['ffn_norm_w'], 'w_up': out['w_up'], 'ffn_conv_w': out['ffn_conv_w'], 'ffn_conv_b': out['ffn_conv_b'], 'w_down': out['w_down'], 'loss_target': out['loss_target'], 'm_rel_bias_table': out['m_rel_bias_table'], 'm_attn_norm_w': out['m_attn_norm_w'], 'm_w_in': out['m_w_in'], 'm_b_gate': out['m_b_gate'], 'm_q_norm_w': out['m_q_norm_w'], 'm_k_norm_w': out['m_k_norm_w'], 'm_w_attn_o': out['m_w_attn_o'], 'm_conv_dw_w': out['m_conv_dw_w'], 'm_conv_dw_b': out['m_conv_dw_b'], 'm_conv_ln_w': out['m_conv_ln_w'], 'm_conv_ln_b': out['m_conv_ln_b'], 'm_w_conv_o': out['m_w_conv_o'], 'm_mem_norm_w': out['m_mem_norm_w'], 'm_w_mem_kv': out['m_w_mem_kv'], 'm_xq_norm_w': out['m_xq_norm_w'], 'm_xk_norm_w': out['m_xk_norm_w'], 'm_w_cross_o': out['m_w_cross_o'], 'm_w_out': out['m_w_out'], 'm_ffn_norm_w': out['m_ffn_norm_w'], 'm_w_up': out['m_w_up'], 'm_ffn_conv_w': out['m_ffn_conv_w'], 'm_ffn_conv_b': out['m_ffn_conv_b'], 'm_w_down': out['m_w_down'], 'v_rel_bias_table': out['v_rel_bias_table'], 'v_attn_norm_w': out['v_attn_norm_w'], 'v_w_in': out['v_w_in'], 'v_b_gate': out['v_b_gate'], 'v_q_norm_w': out['v_q_norm_w'], 'v_k_norm_w': out['v_k_norm_w'], 'v_w_attn_o': out['v_w_attn_o'], 'v_conv_dw_w': out['v_conv_dw_w'], 'v_conv_dw_b': out['v_conv_dw_b'], 'v_conv_ln_w': out['v_conv_ln_w'], 'v_conv_ln_b': out['v_conv_ln_b'], 'v_w_conv_o': out['v_w_conv_o'], 'v_mem_norm_w': out['v_mem_norm_w'], 'v_w_mem_kv': out['v_w_mem_kv'], 'v_xq_norm_w': out['v_xq_norm_w'], 'v_xk_norm_w': out['v_xk_norm_w'], 'v_w_cross_o': out['v_w_cross_o'], 'v_w_out': out['v_w_out'], 'v_ffn_norm_w': out['v_ffn_norm_w'], 'v_w_up': out['v_w_up'], 'v_ffn_conv_w': out['v_ffn_conv_w'], 'v_ffn_conv_b': out['v_ffn_conv_b'], 'v_w_down': out['v_w_down']}


def _loss(weights, diff, rest, loss_target):
    with _jax.named_scope("forward"):
        args = {**rest, TWIN_DIFF_INPUT: diff, **{k: w.astype(_WEIGHT_DTYPES[k]) for k, w in weights.items()}}
        y = _forward(args)
    with _jax.named_scope("loss_head"):
        err = _jnp.square(y.astype(_jnp.float32) - loss_target)
        return 0.5 * _jnp.sum(_jnp.mean(err, axis=-1)) if err.ndim else 0.5 * err


def _adamw(w, g, m, v):
    m = ADAM_B1 * m + (1.0 - ADAM_B1) * g
    v = ADAM_B2 * v + (1.0 - ADAM_B2) * _jnp.square(g)
    m_hat = m / (1.0 - ADAM_B1 ** ADAM_STEP)
    v_hat = v / (1.0 - ADAM_B2 ** ADAM_STEP)
    delta = -ADAM_LR * (m_hat / (_jnp.sqrt(v_hat) + ADAM_EPS) + ADAM_WD * w)
    return delta, m, v


def reference(x, mem, rel_bias_table, attn_norm_w, w_in, b_gate, q_norm_w, k_norm_w, w_attn_o, conv_dw_w, conv_dw_b, conv_ln_w, conv_ln_b, w_conv_o, mem_norm_w, w_mem_kv, xq_norm_w, xk_norm_w, w_cross_o, w_out, ffn_norm_w, w_up, ffn_conv_w, ffn_conv_b, w_down, loss_target, m_rel_bias_table, m_attn_norm_w, m_w_in, m_b_gate, m_q_norm_w, m_k_norm_w, m_w_attn_o, m_conv_dw_w, m_conv_dw_b, m_conv_ln_w, m_conv_ln_b, m_w_conv_o, m_mem_norm_w, m_w_mem_kv, m_xq_norm_w, m_xk_norm_w, m_w_cross_o, m_w_out, m_ffn_norm_w, m_w_up, m_ffn_conv_w, m_ffn_conv_b, m_w_down, v_rel_bias_table, v_attn_norm_w, v_w_in, v_b_gate, v_q_norm_w, v_k_norm_w, v_w_attn_o, v_conv_dw_w, v_conv_dw_b, v_conv_ln_w, v_conv_ln_b, v_w_conv_o, v_mem_norm_w, v_w_mem_kv, v_xq_norm_w, v_xk_norm_w, v_w_cross_o, v_w_out, v_ffn_norm_w, v_w_up, v_ffn_conv_w, v_ffn_conv_b, v_w_down):
    given = dict(x=x, mem=mem, rel_bias_table=rel_bias_table, attn_norm_w=attn_norm_w, w_in=w_in, b_gate=b_gate, q_norm_w=q_norm_w, k_norm_w=k_norm_w, w_attn_o=w_attn_o, conv_dw_w=conv_dw_w, conv_dw_b=conv_dw_b, conv_ln_w=conv_ln_w, conv_ln_b=conv_ln_b, w_conv_o=w_conv_o, mem_norm_w=mem_norm_w, w_mem_kv=w_mem_kv, xq_norm_w=xq_norm_w, xk_norm_w=xk_norm_w, w_cross_o=w_cross_o, w_out=w_out, ffn_norm_w=ffn_norm_w, w_up=w_up, ffn_conv_w=ffn_conv_w, ffn_conv_b=ffn_conv_b, w_down=w_down, loss_target=loss_target, m_rel_bias_table=m_rel_bias_table, m_attn_norm_w=m_attn_norm_w, m_w_in=m_w_in, m_b_gate=m_b_gate, m_q_norm_w=m_q_norm_w, m_k_norm_w=m_k_norm_w, m_w_attn_o=m_w_attn_o, m_conv_dw_w=m_conv_dw_w, m_conv_dw_b=m_conv_dw_b, m_conv_ln_w=m_conv_ln_w, m_conv_ln_b=m_conv_ln_b, m_w_conv_o=m_w_conv_o, m_mem_norm_w=m_mem_norm_w, m_w_mem_kv=m_w_mem_kv, m_xq_norm_w=m_xq_norm_w, m_xk_norm_w=m_xk_norm_w, m_w_cross_o=m_w_cross_o, m_w_out=m_w_out, m_ffn_norm_w=m_ffn_norm_w, m_w_up=m_w_up, m_ffn_conv_w=m_ffn_conv_w, m_ffn_conv_b=m_ffn_conv_b, m_w_down=m_w_down, v_rel_bias_table=v_rel_bias_table, v_attn_norm_w=v_attn_norm_w, v_w_in=v_w_in, v_b_gate=v_b_gate, v_q_norm_w=v_q_norm_w, v_k_norm_w=v_k_norm_w, v_w_attn_o=v_w_attn_o, v_conv_dw_w=v_conv_dw_w, v_conv_dw_b=v_conv_dw_b, v_conv_ln_w=v_conv_ln_w, v_conv_ln_b=v_conv_ln_b, v_w_conv_o=v_w_conv_o, v_mem_norm_w=v_mem_norm_w, v_w_mem_kv=v_w_mem_kv, v_xq_norm_w=v_xq_norm_w, v_xk_norm_w=v_xk_norm_w, v_w_cross_o=v_w_cross_o, v_w_out=v_w_out, v_ffn_norm_w=v_ffn_norm_w, v_w_up=v_w_up, v_ffn_conv_w=v_ffn_conv_w, v_ffn_conv_b=v_ffn_conv_b, v_w_down=v_w_down)
    weights = {n: given[n] for n in TWIN_WEIGHTS}
    shared = {n: given[n] for n in SHARED_INPUTS}
    per_example = {n: given[n] for n in ['x', 'mem']}
    grad_fn = _jax.value_and_grad(_loss, argnums=(0, 1))

    def one_microbatch(ex, loss_target):
        ex = dict(ex)
        diff = ex.pop(TWIN_DIFF_INPUT)
        return grad_fn(weights, diff, {**shared, **ex}, loss_target)

    if N_MICROBATCH == 1:
        loss, (grad_w, grad_x) = one_microbatch(per_example, given["loss_target"])
    else:
        def body(carry, xs):
            loss_sum, grad_sum = carry
            l_k, (gw_k, gx_k) = one_microbatch(xs[0], xs[1])
            with _jax.named_scope("update"):
                return (loss_sum + l_k, _jax.tree.map(_jnp.add, grad_sum, gw_k)), gx_k

        init = (_jnp.zeros((), _jnp.float32), _jax.tree.map(_jnp.zeros_like, weights))
        (loss, grad_w), grad_x = _jax.lax.scan(body, init, (per_example, given["loss_target"]))
    with _jax.named_scope("update"):
        delta_w, new_m, new_v = {}, {}, {}
        for n in TWIN_WEIGHTS:
            delta_w[n], new_m[n], new_v[n] = _adamw(weights[n], grad_w[n], given["m_" + n], given["v_" + n])
    return (loss, grad_x, *[grad_w[n] for n in TWIN_WEIGHTS], *[delta_w[n] for n in TWIN_WEIGHTS],
            *[new_m[n] for n in TWIN_WEIGHTS], *[new_v[n] for n in TWIN_WEIGHTS])
```

```python
import functools
import math

import numpy as np
import jax
import jax.numpy as jnp
from jax import lax
from jax.experimental import pallas as pl
from jax.experimental.pallas import tpu as pltpu

F32 = jnp.float32
MXU = jnp.bfloat16
S = jax.ShapeDtypeStruct

D_MODEL = 1024
HEAD_DIM = 128
ATTN_GROUPS = ((128, 1), (512, 4), (2048, 16))
N_GROUPS = 3
HEADS = 4
GW = HEADS * HEAD_DIM
CONV_WIDTH = 31
N_MEM = 256
D_FF = 2816
FFN_CONV_WIDTH = 3
N_BUCKETS = 32
MAX_DISTANCE = 2048
RMS_EPS = 1e-6
LN_EPS = 1e-5
N_IN = 9216
NCB = N_IN // GW
BLK = 128
SCALE = HEAD_DIM ** -0.5
NEG = -1e30
N_DEV = 8

ADAM_LR, ADAM_B1, ADAM_B2, ADAM_EPS, ADAM_WD, ADAM_STEP = 0.001, 0.9, 0.999, 1e-08, 0.01, 10

VMEM_LIMIT = 48 * 1024 * 1024
CONV_HALO = 32
FFN_HALO = 8
ADAMW_BLOCK_BYTES = 4 * 1024 * 1024


def _cparams(*sem):
    return pltpu.CompilerParams(dimension_semantics=sem or None, vmem_limit_bytes=VMEM_LIMIT)


def _bs(shape, imap):
    return pl.BlockSpec(shape, imap)


def _dot(a, b):
    return lax.dot_general(a.astype(MXU), b.astype(MXU), (((1,), (0,)), ((), ())), preferred_element_type=F32)


def _dot_nt(a, b):
    return lax.dot_general(a.astype(MXU), b.astype(MXU), (((1,), (1,)), ((), ())), preferred_element_type=F32)


def _dot_tn(a, b):
    return lax.dot_general(a.astype(MXU), b.astype(MXU), (((0,), (0,)), ((), ())), preferred_element_type=F32)


def _sigmoid(x):
    return 1.0 / (1.0 + jnp.exp(-x))


def _rmsn(x, w):
    r = lax.rsqrt(jnp.mean(x * x, axis=-1, keepdims=True) + RMS_EPS)
    return x * r * w, r


def _rmsn_bwd(x, r, w, dy):
    g = dy * w
    dx = r * g - x * (r * r * r) * jnp.mean(x * g, axis=-1, keepdims=True)
    dw = jnp.sum(dy * x * r, axis=0, keepdims=True)
    return dx, dw


def _acc_out(ref, val, first):
    @pl.when(first)
    def _():
        ref[...] = val

    @pl.when(jnp.logical_not(first))
    def _():
        ref[...] += val


def _rms_fwd(x, w, name):
    T, D = x.shape
    tm = min(512, T)

    def body(x_ref, w_ref, o_ref):
        y, _ = _rmsn(x_ref[...], w_ref[...])
        o_ref[...] = y.astype(o_ref.dtype)

    return pl.pallas_call(
        body, grid=(T // tm,),
        in_specs=[_bs((tm, D), lambda i: (i, 0)), _bs((1, D), lambda i: (0, 0))],
        out_specs=_bs((tm, D), lambda i: (i, 0)),
        out_shape=S((T, D), MXU), name=name, compiler_params=_cparams("parallel"))(x, w)


def _rms_bwd(x, w, dy, resid, name):
    T, D = x.shape
    tm = min(512, T)

    def body(x_ref, w_ref, dy_ref, res_ref, dx_ref, dw_ref):
        xv = x_ref[...]
        _, r = _rmsn(xv, w_ref[...])
        dx, dw = _rmsn_bwd(xv, r, w_ref[...], dy_ref[...])
        dx_ref[...] = res_ref[...] + dx
        _acc_out(dw_ref, dw, pl.program_id(0) == 0)

    row = _bs((tm, D), lambda i: (i, 0))
    vec = _bs((1, D), lambda i: (0, 0))
    return pl.pallas_call(
        body, grid=(T // tm,), in_specs=[row, vec, row, row], out_specs=[row, vec],
        out_shape=[S((T, D), F32), S((1, D), F32)], name=name, compiler_params=_cparams("arbitrary"))(x, w, dy, resid)


def _pick(n, cands):
    for c in cands:
        if n % c == 0:
            return c
    return n


def _matmul(a, b, *, ta=False, tb=False, out_dtype=F32, residual=None, tm=None, tn=None, tk=None, name):
    M, K = (a.shape[1], a.shape[0]) if ta else a.shape
    N = b.shape[0] if tb else b.shape[1]
    tm = tm or _pick(M, (1024, 1408, 512, 256, 128))
    tn = tn or _pick(N, (512, 256, 128))
    tk = tk or _pick(K, (1024, 1408, 512, 256, 128))
    nk = K // tk
    dn = (((0 if ta else 1,), (1 if tb else 0,)), ((), ()))
    has_res = residual is not None

    def body(*refs):
        a_ref, b_ref = refs[0], refs[1]
        res_ref = refs[2] if has_res else None
        o_ref = refs[3] if has_res else refs[2]
        p = lax.dot_general(a_ref[...].astype(MXU), b_ref[...].astype(MXU), dn, preferred_element_type=F32)

        def finish(acc):
            if has_res:
                acc = acc + res_ref[...]
            o_ref[...] = acc.astype(o_ref.dtype)

        if nk == 1:
            finish(p)
        else:
            acc_ref = refs[-1]
            k = pl.program_id(2)

            @pl.when(k == 0)
            def _():
                acc_ref[...] = p

            @pl.when(k > 0)
            def _():
                acc_ref[...] += p

            @pl.when(k == nk - 1)
            def _():
                finish(acc_ref[...])

    a_spec = _bs((tk, tm), lambda i, j, k: (k, i)) if ta else _bs((tm, tk), lambda i, j, k: (i, k))
    b_spec = _bs((tn, tk), lambda i, j, k: (j, k)) if tb else _bs((tk, tn), lambda i, j, k: (k, j))
    o_spec = _bs((tm, tn), lambda i, j, k: (i, j))
    in_specs = [a_spec, b_spec] + ([o_spec] if has_res else [])
    args = (a, b) + ((residual,) if has_res else ())
    return pl.pallas_call(
        body, grid=(M // tm, N // tn, nk), in_specs=in_specs, out_specs=o_spec,
        out_shape=S((M, N), out_dtype), scratch_shapes=[pltpu.VMEM((tm, tn), F32)] if nk > 1 else [],
        name=name, compiler_params=_cparams("parallel", "parallel", "arbitrary"))(*args)


def _bucket_matrix(dilation):
    n = BLK
    qi = np.arange(n)[:, None]
    kj = np.arange(2 * n)[None, :]
    step = qi + n - kj
    dist = np.clip(step, 0, None) * dilation
    max_exact = N_BUCKETS // 2
    d = np.maximum(dist.astype(np.float32), np.float32(1.0))
    large = max_exact + (np.log(d / np.float32(max_exact)) / np.float32(math.log(MAX_DISTANCE / max_exact))
                         * np.float32(N_BUCKETS - max_exact)).astype(np.int32)
    large = np.minimum(large, N_BUCKETS - 1)
    bucket = np.where(dist < max_exact, dist, large)
    band = (step >= 0) & (step <= n)
    return np.where(band, bucket, -1).astype(np.int32)


def _build_bias(tbl_ref, bkt_ref, bias_ref, g):
    bk = bkt_ref[...]
    for h in range(HEADS):
        acc = jnp.full(bk.shape, NEG, F32)
        for b in range(N_BUCKETS):
            acc = jnp.where(bk == b, tbl_ref[b, HEADS * g + h], acc)
        bias_ref[h] = acc


def _attn_specs(g, dil):
    def zspec(off, prev):
        if prev:
            return _bs((BLK, GW), lambda c, i: (jnp.maximum(i - 1, 0), c * NCB + off + g))
        return _bs((BLK, GW), lambda c, i: (i, c * NCB + off + g))
    return zspec


def _attn_fwd(z, table, qw, kw, g, dil):
    T = z.shape[0]
    tc = T // dil
    nb = tc // BLK
    zv = z.reshape(tc, dil * N_IN)
    bkt = jnp.asarray(_bucket_matrix(dil))
    zspec = _attn_specs(g, dil)

    def body(tbl_ref, bkt_ref, qw_ref, kw_ref, q_ref, kp_ref, kc_ref, vp_ref, vc_ref, o_ref, lse_ref, bias_ref):
        c, i = pl.program_id(0), pl.program_id(1)

        @pl.when((c == 0) & (i == 0))
        def _():
            _build_bias(tbl_ref, bkt_ref, bias_ref, g)

        kj = lax.broadcasted_iota(jnp.int32, (BLK, 2 * BLK), 1)
        no_prev = jnp.logical_and(i == 0, kj < BLK)
        for h in range(HEADS):
            sl = slice(h * HEAD_DIM, (h + 1) * HEAD_DIM)
            qn, _ = _rmsn(q_ref[:, sl], qw_ref[...])
            kn, _ = _rmsn(jnp.concatenate([kp_ref[:, sl], kc_ref[:, sl]], axis=0), kw_ref[...])
            s = _dot_nt(qn, kn) * SCALE + bias_ref[h]
            s = jnp.where(no_prev, NEG, s)
            m = jnp.max(s, axis=-1, keepdims=True)
            p = jnp.exp(s - m)
            l = jnp.sum(p, axis=-1, keepdims=True)
            v = jnp.concatenate([vp_ref[:, sl], vc_ref[:, sl]], axis=0)
            o_ref[:, sl] = _dot(p, v) / l
            lse_ref[:, sl] = jnp.broadcast_to(m + jnp.log(l), (BLK, HEAD_DIM))

    ospec = _bs((BLK, GW), lambda c, i: (i, c))
    vec = _bs((1, HEAD_DIM), lambda c, i: (0, 0))
    o, lse = pl.pallas_call(
        body, grid=(dil, nb),
        in_specs=[pl.BlockSpec(memory_space=pltpu.SMEM), _bs((BLK, 2 * BLK), lambda c, i: (0, 0)), vec, vec,
                  zspec(0, False), zspec(3, True), zspec(3, False), zspec(6, True), zspec(6, False)],
        out_specs=[ospec, ospec],
        out_shape=[S((tc, dil * GW), F32), S((tc, dil * GW), F32)],
        scratch_shapes=[pltpu.VMEM((HEADS, BLK, 2 * BLK), F32)],
        name=f"attn_fwd_g{g}", compiler_params=_cparams("arbitrary", "arbitrary"))(table, bkt, qw, kw, zv, zv, zv, zv, zv)
    return o.reshape(T, GW), lse.reshape(T, GW)


def _attn_merge(os_, lses):
    T = os_[0].shape[0]
    tm = min(512, T)

    def body(o0, o1, o2, l0, l1, l2, a_ref, lse_ref):
        ls = [l0[...], l1[...], l2[...]]
        mx = jnp.maximum(jnp.maximum(ls[0], ls[1]), ls[2])
        tot = mx + jnp.log(jnp.exp(ls[0] - mx) + jnp.exp(ls[1] - mx) + jnp.exp(ls[2] - mx))
        a_ref[...] = (jnp.exp(ls[0] - tot) * o0[...] + jnp.exp(ls[1] - tot) * o1[...] + jnp.exp(ls[2] - tot) * o2[...])
        lse_ref[...] = tot

    row = _bs((tm, GW), lambda i: (i, 0))
    return pl.pallas_call(
        body, grid=(T // tm,), in_specs=[row] * 6, out_specs=[row, row],
        out_shape=[S((T, GW), F32), S((T, GW), F32)], name="attn_merge",
        compiler_params=_cparams("parallel"))(*os_, *lses)


def _attn_bwd(z, table, qw, kw, d_attn, attn, lse, g, dil):
    T = z.shape[0]
    tc = T // dil
    nb = tc // BLK
    zv = z.reshape(tc, dil * N_IN)
    bkt = jnp.asarray(_bucket_matrix(dil))

    def zspec(off, prev):
        if prev:
            return _bs((BLK, GW), lambda c, i: (jnp.clip(i - 1, 0, nb - 1), c * NCB + off + g))
        return _bs((BLK, GW), lambda c, i: (jnp.minimum(i, nb - 1), c * NCB + off + g))

    def body(tbl_ref, bkt_ref, qw_ref, kw_ref, q_ref, kp_ref, kc_ref, vp_ref, vc_ref, da_ref, at_ref, lse_ref,
             dq_ref, dk_ref, dv_ref, dqw_ref, dkw_ref, dtab_ref, bias_ref, dbias_ref, ck_ref, cv_ref):
        c, i = pl.program_id(0), pl.program_id(1)

        @pl.when((c == 0) & (i == 0))
        def _():
            _build_bias(tbl_ref, bkt_ref, bias_ref, g)
            dbias_ref[...] = jnp.zeros_like(dbias_ref)
            dqw_ref[...] = jnp.zeros_like(dqw_ref)
            dkw_ref[...] = jnp.zeros_like(dkw_ref)

        @pl.when(i == 0)
        def _():
            ck_ref[...] = jnp.zeros_like(ck_ref)
            cv_ref[...] = jnp.zeros_like(cv_ref)

        @pl.when(i < nb)
        def _():
            kj = lax.broadcasted_iota(jnp.int32, (BLK, 2 * BLK), 1)
            no_prev = jnp.logical_and(i == 0, kj < BLK)
            dqw_acc = jnp.zeros((1, HEAD_DIM), F32)
            dkw_acc = jnp.zeros((1, HEAD_DIM), F32)
            for h in range(HEADS):
                sl = slice(h * HEAD_DIM, (h + 1) * HEAD_DIM)
                qh = q_ref[:, sl]
                k = jnp.concatenate([kp_ref[:, sl], kc_ref[:, sl]], axis=0)
                qn, rq = _rmsn(qh, qw_ref[...])
                kn, rk = _rmsn(k, kw_ref[...])
                s = _dot_nt(qn, kn) * SCALE + bias_ref[h]
                s = jnp.where(no_prev, NEG, s)
                p = jnp.exp(s - lse_ref[:, h * HEAD_DIM:h * HEAD_DIM + 1])
                do = da_ref[:, sl]
                delta = jnp.sum(do * at_ref[:, sl], axis=-1, keepdims=True)
                v = jnp.concatenate([vp_ref[:, sl], vc_ref[:, sl]], axis=0)
                ds = p * (_dot_nt(do, v) - delta)
                dbias_ref[h] += ds
                dv = _dot_tn(p, do)
                dqn = _dot(ds, kn) * SCALE
                dkn = _dot_tn(ds, qn) * SCALE
                dq, dqw = _rmsn_bwd(qh, rq, qw_ref[...], dqn)
                dk, dkw = _rmsn_bwd(k, rk, kw_ref[...], dkn)
                dqw_acc += dqw
                dkw_acc += dkw
                dq_ref[:, sl] = dq.astype(dq_ref.dtype)
                dk_ref[:, sl] = (ck_ref[:, sl] + dk[:BLK]).astype(dk_ref.dtype)
                dv_ref[:, sl] = (cv_ref[:, sl] + dv[:BLK]).astype(dv_ref.dtype)
                ck_ref[:, sl] = dk[BLK:]
                cv_ref[:, sl] = dv[BLK:]
            dqw_ref[...] += dqw_acc
            dkw_ref[...] += dkw_acc

        @pl.when(i == nb)
        def _():
            dk_ref[...] = ck_ref[...].astype(dk_ref.dtype)
            dv_ref[...] = cv_ref[...].astype(dv_ref.dtype)

        @pl.when((c == dil - 1) & (i == nb))
        def _():
            bk = bkt_ref[...]
            rows = lax.broadcasted_iota(jnp.int32, (N_BUCKETS, HEAD_DIM), 0)
            lanes = lax.broadcasted_iota(jnp.int32, (N_BUCKETS, HEAD_DIM), 1)
            out = jnp.zeros((N_BUCKETS, HEAD_DIM), F32)
            for h in range(HEADS):
                acc = dbias_ref[h]
                for b in range(N_BUCKETS):
                    val = jnp.sum(jnp.where(bk == b, acc, 0.0))
                    out = jnp.where((rows == b) & (lanes == h), val, out)
            dtab_ref[...] = out

    cur = _bs((BLK, GW), lambda c, i: (jnp.minimum(i, nb - 1), c))
    prv = _bs((BLK, GW), lambda c, i: (jnp.maximum(i - 1, 0), c))
    vec = _bs((1, HEAD_DIM), lambda c, i: (0, 0))
    tabs = _bs((N_BUCKETS, HEAD_DIM), lambda c, i: (0, 0))
    view = lambda a: a.reshape(tc, dil * GW)
    dq, dk, dv, dqw, dkw, dtab = pl.pallas_call(
        body, grid=(dil, nb + 1),
        in_specs=[pl.BlockSpec(memory_space=pltpu.SMEM), _bs((BLK, 2 * BLK), lambda c, i: (0, 0)), vec, vec,
                  zspec(0, False), zspec(3, True), zspec(3, False), zspec(6, True), zspec(6, False), cur, cur, cur],
        out_specs=[cur, prv, prv, vec, vec, tabs],
        out_shape=[S((tc, dil * GW), MXU)] * 3 + [S((1, HEAD_DIM), F32)] * 2 + [S((N_BUCKETS, HEAD_DIM), F32)],
        scratch_shapes=[pltpu.VMEM((HEADS, BLK, 2 * BLK), F32), pltpu.VMEM((HEADS, BLK, 2 * BLK), F32),
                        pltpu.VMEM((BLK, GW), F32), pltpu.VMEM((BLK, GW), F32)],
        name=f"attn_bwd_g{g}", compiler_params=_cparams("arbitrary", "arbitrary"))(
            table, bkt, qw, kw, zv, zv, zv, zv, zv, view(d_attn), view(attn), view(lse))
    return dq.reshape(T, GW), dk.reshape(T, GW), dv.reshape(T, GW), dqw, dkw, dtab[:, :HEADS]


def _mem_fwd(mem, mem_norm_w, w_mem_kv, xk_w):
    def body(mem_ref, nw_ref, w_ref, xk_ref, mk_ref, mv_ref):
        mn, _ = _rmsn(mem_ref[...], nw_ref[...])
        kv = _dot(mn, w_ref[...])
        for h in range(HEADS):
            sl = slice(h * HEAD_DIM, (h + 1) * HEAD_DIM)
            kn, _ = _rmsn(kv[:, sl], xk_ref[...])
            mk_ref[:, sl] = kn.astype(mk_ref.dtype)
        mv_ref[...] = kv[:, GW:].astype(mv_ref.dtype)

    return pl.pallas_call(body, out_shape=[S((N_MEM, GW), MXU), S((N_MEM, GW), MXU)], name="mem_fwd",
                          compiler_params=_cparams())(mem, mem_norm_w, w_mem_kv, xk_w)


def _mem_bwd(mem, mem_norm_w, w_mem_kv, xk_w, dmk, dmv):
    def body(mem_ref, nw_ref, w_ref, xk_ref, dmk_ref, dmv_ref, dw_ref, dnw_ref, dxk_ref):
        memv = mem_ref[...]
        mn, r = _rmsn(memv, nw_ref[...])
        kv = _dot(mn, w_ref[...])
        dxk = jnp.zeros((1, HEAD_DIM), F32)
        parts = []
        for h in range(HEADS):
            sl = slice(h * HEAD_DIM, (h + 1) * HEAD_DIM)
            kh = kv[:, sl]
            _, rk = _rmsn(kh, xk_ref[...])
            dk, dw = _rmsn_bwd(kh, rk, xk_ref[...], dmk_ref[:, sl])
            dxk += dw
            parts.append(dk)
        dkv = jnp.concatenate(parts + [dmv_ref[...]], axis=1)
        dw_ref[...] = _dot_tn(mn, dkv)
        dmn = _dot_nt(dkv, w_ref[...])
        dnw_ref[...] = jnp.sum(dmn * memv * r, axis=0, keepdims=True)
        dxk_ref[...] = dxk

    return pl.pallas_call(
        body, out_shape=[S((D_MODEL, 2 * GW), F32), S((1, D_MODEL), F32), S((1, HEAD_DIM), F32)], name="mem_bwd",
        compiler_params=_cparams())(mem, mem_norm_w, w_mem_kv, xk_w, dmk, dmv)


def _cross_fwd(z, mk, mv, xq_w):
    T = z.shape[0]
    tm = min(512, T)

    def body(q_ref, mk_ref, mv_ref, w_ref, o_ref):
        for h in range(HEADS):
            sl = slice(h * HEAD_DIM, (h + 1) * HEAD_DIM)
            qn, _ = _rmsn(q_ref[:, sl], w_ref[...])
            s = _dot_nt(qn, mk_ref[:, sl]) * SCALE
            e = jnp.exp(s - jnp.max(s, axis=-1, keepdims=True))
            p = e / jnp.sum(e, axis=-1, keepdims=True)
            o_ref[:, sl] = _dot(p, mv_ref[:, sl]).astype(o_ref.dtype)

    full = _bs((N_MEM, GW), lambda i: (0, 0))
    return pl.pallas_call(
        body, grid=(T // tm,),
        in_specs=[_bs((tm, GW), lambda i: (i, 11)), full, full, _bs((1, HEAD_DIM), lambda i: (0, 0))],
        out_specs=_bs((tm, GW), lambda i: (i, 0)), out_shape=S((T, GW), MXU), name="cross_fwd",
        compiler_params=_cparams("parallel"))(z, mk, mv, xq_w)


def _cross_bwd(z, mk, mv, xq_w, d_cross):
    T = z.shape[0]
    tm = min(512, T)

    def body(q_ref, mk_ref, mv_ref, w_ref, do_ref, dq_ref, dmk_ref, dmv_ref, dw_ref):
        first = pl.program_id(0) == 0
        dw_acc = jnp.zeros((1, HEAD_DIM), F32)
        dmk_parts, dmv_parts = [], []
        for h in range(HEADS):
            sl = slice(h * HEAD_DIM, (h + 1) * HEAD_DIM)
            qh = q_ref[:, sl]
            qn, r = _rmsn(qh, w_ref[...])
            s = _dot_nt(qn, mk_ref[:, sl]) * SCALE
            e = jnp.exp(s - jnp.max(s, axis=-1, keepdims=True))
            p = e / jnp.sum(e, axis=-1, keepdims=True)
            do = do_ref[:, sl]
            dp = _dot_nt(do, mv_ref[:, sl])
            ds = p * (dp - jnp.sum(dp * p, axis=-1, keepdims=True)) * SCALE
            dmv_parts.append(_dot_tn(p, do))
            dmk_parts.append(_dot_tn(ds, qn))
            dq, dw = _rmsn_bwd(qh, r, w_ref[...], _dot(ds, mk_ref[:, sl]))
            dw_acc += dw
            dq_ref[:, sl] = dq.astype(dq_ref.dtype)
        _acc_out(dmk_ref, jnp.concatenate(dmk_parts, axis=1), first)
        _acc_out(dmv_ref, jnp.concatenate(dmv_parts, axis=1), first)
        _acc_out(dw_ref, dw_acc, first)

    full = _bs((N_MEM, GW), lambda i: (0, 0))
    vec = _bs((1, HEAD_DIM), lambda i: (0, 0))
    row = _bs((tm, GW), lambda i: (i, 0))
    return pl.pallas_call(
        body, grid=(T // tm,),
        in_specs=[_bs((tm, GW), lambda i: (i, 11)), full, full, vec, row],
        out_specs=[row, full, full, vec],
        out_shape=[S((T, GW), MXU), S((N_MEM, GW), F32), S((N_MEM, GW), F32), S((1, HEAD_DIM), F32)],
        name="cross_bwd", compiler_params=_cparams("arbitrary"))(z, mk, mv, xq_w, d_cross)


def _taps(ext, w_ref, width, base, rows):
    acc = None
    for k in range(width):
        s = base - (width - 1) + k
        term = ext[s:s + rows, :] * w_ref[k:k + 1, :]
        acc = term if acc is None else acc + term
    return acc


def _taps_t(ext, w_ref, width, base, rows):
    acc = None
    for k in range(width):
        s = base + (width - 1) - k
        term = ext[s:s + rows, :] * w_ref[k:k + 1, :]
        acc = term if acc is None else acc + term
    return acc


def _conv_fwd(z, cw, cb, lw, lb):
    T = z.shape[0]
    tm = min(512, T)
    hb = tm // CONV_HALO

    def body(val_ref, gate_ref, hval_ref, hgate_ref, cw_ref, cb_ref, lw_ref, lb_ref, o_ref):
        i = pl.program_id(0)
        halo = hval_ref[...] * _sigmoid(hgate_ref[...])
        halo = jnp.where(i == 0, 0.0, halo)
        ext = jnp.concatenate([halo, val_ref[...] * _sigmoid(gate_ref[...])], axis=0)
        y = _taps(ext, cw_ref, CONV_WIDTH, CONV_HALO, tm) + cb_ref[...]
        xc = y - jnp.mean(y, axis=-1, keepdims=True)
        a = xc * lax.rsqrt(jnp.mean(xc * xc, axis=-1, keepdims=True) + LN_EPS) * lw_ref[...] + lb_ref[...]
        o_ref[...] = (a * _sigmoid(a)).astype(o_ref.dtype)

    vec = _bs((1, GW), lambda i: (0, 0))
    halo_spec = lambda col: _bs((CONV_HALO, GW), lambda i: (jnp.maximum(i * hb - 1, 0), col))
    return pl.pallas_call(
        body, grid=(T // tm,),
        in_specs=[_bs((tm, GW), lambda i: (i, 9)), _bs((tm, GW), lambda i: (i, 10)), halo_spec(9), halo_spec(10),
                  _bs((CONV_WIDTH, GW), lambda i: (0, 0)), vec, vec, vec],
        out_specs=_bs((tm, GW), lambda i: (i, 0)), out_shape=S((T, GW), MXU), name="conv_fwd",
        compiler_params=_cparams("parallel"))(z, z, z, z, cw, cb, lw, lb)


def _conv_bwd(z, cw, cb, lw, lb, d_u):
    T = z.shape[0]
    tm = min(512, T)
    hb = tm // CONV_HALO
    nt = T // tm
    H = CONV_HALO

    def body(val_ref, gate_ref, pval_ref, pgate_ref, nval_ref, ngate_ref, du_ref, ndu_ref, cw_ref, cb_ref, lw_ref,
             lb_ref, dval_ref, dgate_ref, dcw_ref, dcb_ref, dlw_ref, dlb_ref):
        i = pl.program_id(0)
        first = i == 0
        val = jnp.concatenate([pval_ref[...], val_ref[...], nval_ref[...]], axis=0)
        sg = _sigmoid(jnp.concatenate([pgate_ref[...], gate_ref[...], ngate_ref[...]], axis=0))
        rid = lax.broadcasted_iota(jnp.int32, (tm + 2 * H, 1), 0)
        u0 = jnp.where(jnp.logical_and(first, rid < H), 0.0, val * sg)
        y = _taps(u0, cw_ref, CONV_WIDTH, H, tm + H) + cb_ref[...]
        xc = y - jnp.mean(y, axis=-1, keepdims=True)
        rs = lax.rsqrt(jnp.mean(xc * xc, axis=-1, keepdims=True) + LN_EPS)
        nh = xc * rs
        a = nh * lw_ref[...] + lb_ref[...]
        sa = _sigmoid(a)
        du = jnp.concatenate([du_ref[...], ndu_ref[...]], axis=0)
        rid2 = lax.broadcasted_iota(jnp.int32, (tm + H, 1), 0)
        du = jnp.where(jnp.logical_and(i == nt - 1, rid2 >= tm), 0.0, du)
        da = du * (sa * (1.0 + a * (1.0 - sa)))
        dn = da * lw_ref[...]
        dy = rs * (dn - jnp.mean(dn, axis=-1, keepdims=True) - nh * jnp.mean(dn * nh, axis=-1, keepdims=True))
        du0 = _taps_t(dy, cw_ref, CONV_WIDTH, 0, tm)
        v0, s0 = val[H:H + tm], sg[H:H + tm]
        dval_ref[...] = (du0 * s0).astype(dval_ref.dtype)
        dgate_ref[...] = (du0 * v0 * s0 * (1.0 - s0)).astype(dgate_ref.dtype)
        dy0 = dy[:tm]
        rows = [jnp.sum(dy0 * u0[H - (CONV_WIDTH - 1) + k:H - (CONV_WIDTH - 1) + k + tm], axis=0, keepdims=True)
                for k in range(CONV_WIDTH)]
        _acc_out(dcw_ref, jnp.concatenate(rows, axis=0), first)
        _acc_out(dcb_ref, jnp.sum(dy0, axis=0, keepdims=True), first)
        _acc_out(dlw_ref, jnp.sum(da[:tm] * nh[:tm], axis=0, keepdims=True), first)
        _acc_out(dlb_ref, jnp.sum(da[:tm], axis=0, keepdims=True), first)

    vec = _bs((1, GW), lambda i: (0, 0))
    cwspec = _bs((CONV_WIDTH, GW), lambda i: (0, 0))
    prev = lambda col: _bs((H, GW), lambda i: (jnp.maximum(i * hb - 1, 0), col))
    nxt = lambda col: _bs((H, GW), lambda i: (jnp.minimum((i + 1) * hb, nt * hb - 1), col))
    row = _bs((tm, GW), lambda i: (i, 0))
    return pl.pallas_call(
        body, grid=(nt,),
        in_specs=[_bs((tm, GW), lambda i: (i, 9)), _bs((tm, GW), lambda i: (i, 10)), prev(9), prev(10), nxt(9), nxt(10),
                  row, nxt(0), cwspec, vec, vec, vec],
        out_specs=[row, row, cwspec, vec, vec, vec],
        out_shape=[S((T, GW), MXU), S((T, GW), MXU), S((CONV_WIDTH, GW), F32)] + [S((1, GW), F32)] * 3,
        name="conv_bwd", compiler_params=_cparams("arbitrary"))(z, z, z, z, z, z, d_u, d_u, cw, cb, lw, lb)


def _ffn_act_fwd(up0, fw, fb):
    T = up0.shape[0]
    tm = min(256, T)
    hb = tm // FFN_HALO
    H = FFN_HALO

    def body(a_ref, g_ref, pa_ref, pg_ref, wa_ref, wg_ref, ba_ref, bg_ref, o_ref):
        i = pl.program_id(0)
        keep = jnp.where(i == 0, 0.0, 1.0)
        ea = jnp.concatenate([pa_ref[...] * keep, a_ref[...]], axis=0)
        eg = jnp.concatenate([pg_ref[...] * keep, g_ref[...]], axis=0)
        av = _taps(ea, wa_ref, FFN_CONV_WIDTH, H, tm) + ba_ref[...]
        gv = _taps(eg, wg_ref, FFN_CONV_WIDTH, H, tm) + bg_ref[...]
        o_ref[...] = (gv * _sigmoid(gv) * av).astype(o_ref.dtype)

    col = lambda j: _bs((tm, D_FF), lambda i: (i, j))
    prev = lambda j: _bs((H, D_FF), lambda i: (jnp.maximum(i * hb - 1, 0), j))
    wspec = lambda j: _bs((FFN_CONV_WIDTH, D_FF), lambda i: (0, j))
    bspec = lambda j: _bs((1, D_FF), lambda i: (0, j))
    return pl.pallas_call(
        body, grid=(T // tm,),
        in_specs=[col(0), col(1), prev(0), prev(1), wspec(0), wspec(1), bspec(0), bspec(1)],
        out_specs=_bs((tm, D_FF), lambda i: (i, 0)), out_shape=S((T, D_FF), MXU), name="ffn_act_fwd",
        compiler_params=_cparams("parallel"))(up0, up0, up0, up0, fw, fw, fb, fb)


def _ffn_act_bwd(up0, fw, fb, d_f):
    T = up0.shape[0]
    tm = min(256, T)
    hb = tm // FFN_HALO
    nt = T // tm
    H = FFN_HALO
    W = FFN_CONV_WIDTH

    def body(a_ref, g_ref, pa_ref, pg_ref, na_ref, ng_ref, df_ref, ndf_ref, wa_ref, wg_ref, ba_ref, bg_ref,
             dup_ref, dw_ref, db_ref):
        i = pl.program_id(0)
        first = i == 0
        keep = jnp.where(first, 0.0, 1.0)
        ea = jnp.concatenate([pa_ref[...] * keep, a_ref[...], na_ref[...]], axis=0)
        eg = jnp.concatenate([pg_ref[...] * keep, g_ref[...], ng_ref[...]], axis=0)
        av = _taps(ea, wa_ref, W, H, tm + H) + ba_ref[...]
        gv = _taps(eg, wg_ref, W, H, tm + H) + bg_ref[...]
        df = jnp.concatenate([df_ref[...], ndf_ref[...]], axis=0)
        rid = lax.broadcasted_iota(jnp.int32, (tm + H, 1), 0)
        df = jnp.where(jnp.logical_and(i == nt - 1, rid >= tm), 0.0, df)
        sg = _sigmoid(gv)
        d_av = df * gv * sg
        d_gv = df * av * (sg * (1.0 + gv * (1.0 - sg)))
        dup_ref[:, :D_FF] = _taps_t(d_av, wa_ref, W, 0, tm).astype(dup_ref.dtype)
        dup_ref[:, D_FF:] = _taps_t(d_gv, wg_ref, W, 0, tm).astype(dup_ref.dtype)
        dwa = [jnp.sum(d_av[:tm] * ea[H - (W - 1) + k:H - (W - 1) + k + tm], axis=0, keepdims=True) for k in range(W)]
        dwg = [jnp.sum(d_gv[:tm] * eg[H - (W - 1) + k:H - (W - 1) + k + tm], axis=0, keepdims=True) for k in range(W)]
        dw = jnp.concatenate([jnp.concatenate(dwa, axis=0), jnp.concatenate(dwg, axis=0)], axis=1)
        db = jnp.concatenate([jnp.sum(d_av[:tm], axis=0, keepdims=True), jnp.sum(d_gv[:tm], axis=0, keepdims=True)], axis=1)
        _acc_out(dw_ref, dw, first)
        _acc_out(db_ref, db, first)

    col = lambda j: _bs((tm, D_FF), lambda i: (i, j))
    prev = lambda j: _bs((H, D_FF), lambda i: (jnp.maximum(i * hb - 1, 0), j))
    nxt = lambda j: _bs((H, D_FF), lambda i: (jnp.minimum((i + 1) * hb, nt * hb - 1), j))
    wspec = lambda j: _bs((W, D_FF), lambda i: (0, j))
    bspec = lambda j: _bs((1, D_FF), lambda i: (0, j))
    return pl.pallas_call(
        body, grid=(nt,),
        in_specs=[col(0), col(1), prev(0), prev(1), nxt(0), nxt(1), col(0), nxt(0), wspec(0), wspec(1), bspec(0), bspec(1)],
        out_specs=[_bs((tm, 2 * D_FF), lambda i: (i, 0)), _bs((W, 2 * D_FF), lambda i: (0, 0)),
                   _bs((1, 2 * D_FF), lambda i: (0, 0))],
        out_shape=[S((T, 2 * D_FF), MXU), S((W, 2 * D_FF), F32), S((1, 2 * D_FF), F32)],
        name="ffn_act_bwd", compiler_params=_cparams("arbitrary"))(
            up0, up0, up0, up0, up0, up0, d_f, d_f, fw, fw, fb, fb)


def _branch_fwd(attn, u, cross, z, b_gate, wa, wc, wx):
    T = z.shape[0]
    tm = min(512, T)

    def body(a_ref, u_ref, x_ref, g0_ref, g1_ref, g2_ref, b_ref, wa_ref, wc_ref, wx_ref, o_ref):
        acc = None
        for j, (act, g_ref, w_ref) in enumerate(((a_ref, g0_ref, wa_ref), (u_ref, g1_ref, wc_ref), (x_ref, g2_ref, wx_ref))):
            gate = _sigmoid(g_ref[...] + b_ref[:, j * D_MODEL:(j + 1) * D_MODEL])
            term = gate * _dot(act[...], w_ref[...])
            acc = term if acc is None else acc + term
        o_ref[...] = acc.astype(o_ref.dtype)

    act = _bs((tm, GW), lambda i: (i, 0))
    gcol = lambda j: _bs((tm, D_MODEL), lambda i: (i, 6 + j))
    wfull = _bs((GW, D_MODEL), lambda i: (0, 0))
    return pl.pallas_call(
        body, grid=(T // tm,),
        in_specs=[act, act, act, gcol(0), gcol(1), gcol(2), _bs((1, 3 * D_MODEL), lambda i: (0, 0)), wfull, wfull, wfull],
        out_specs=_bs((tm, D_MODEL), lambda i: (i, 0)), out_shape=S((T, D_MODEL), MXU), name="branch_fwd",
        compiler_params=_cparams("parallel"))(attn, u, cross, z, z, z, b_gate, wa, wc, wx)


def _branch_bwd(d_merged, attn, u, cross, z, b_gate, wa, wc, wx):
    T = z.shape[0]
    tm = min(512, T)

    def body(dm_ref, a_ref, u_ref, x_ref, g0_ref, g1_ref, g2_ref, b_ref, wa_ref, wc_ref, wx_ref,
             dzg_ref, da_ref, du_ref, dx_ref, dwa_ref, dwc_ref, dwx_ref, db_ref):
        first = pl.program_id(0) == 0
        dm = dm_ref[...]
        dbs = []
        for j, (act, g_ref, w_ref, dact_ref, dw_ref) in enumerate((
                (a_ref, g0_ref, wa_ref, da_ref, dwa_ref), (u_ref, g1_ref, wc_ref, du_ref, dwc_ref),
                (x_ref, g2_ref, wx_ref, dx_ref, dwx_ref))):
            av = act[...]
            gate = _sigmoid(g_ref[...] + b_ref[:, j * D_MODEL:(j + 1) * D_MODEL])
            y = _dot(av, w_ref[...])
            dzg = dm * y * gate * (1.0 - gate)
            dzg_ref[:, j * D_MODEL:(j + 1) * D_MODEL] = dzg.astype(dzg_ref.dtype)
            dbs.append(jnp.sum(dzg, axis=0, keepdims=True))
            dy = (gate * dm).astype(MXU)
            dact_ref[...] = _dot_nt(dy, w_ref[...])
            _acc_out(dw_ref, _dot_tn(av, dy), first)
        _acc_out(db_ref, jnp.concatenate(dbs, axis=1), first)

    act = _bs((tm, GW), lambda i: (i, 0))
    gcol = lambda j: _bs((tm, D_MODEL), lambda i: (i, 6 + j))
    wfull = _bs((GW, D_MODEL), lambda i: (0, 0))
    bvec = _bs((1, 3 * D_MODEL), lambda i: (0, 0))
    return pl.pallas_call(
        body, grid=(T // tm,),
        in_specs=[_bs((tm, D_MODEL), lambda i: (i, 0)), act, act, act, gcol(0), gcol(1), gcol(2), bvec, wfull, wfull, wfull],
        out_specs=[_bs((tm, 3 * D_MODEL), lambda i: (i, 0)), act, act, act, wfull, wfull, wfull, bvec],
        out_shape=[S((T, 3 * D_MODEL), MXU)] + [S((T, GW), F32)] * 3 + [S((GW, D_MODEL), F32)] * 3 + [S((1, 3 * D_MODEL), F32)],
        name="branch_bwd", compiler_params=_cparams("arbitrary"))(d_merged, attn, u, cross, z, z, z, b_gate, wa, wc, wx)


def _loss_head(y, target):
    T, D = y.shape
    tm = min(512, T)

    def body(y_ref, t_ref, dy_ref, l_ref):
        e = y_ref[...] - t_ref[...]
        dy_ref[...] = e * (1.0 / D)
        part = jnp.full((8, 128), jnp.sum(e * e), F32)
        _acc_out(l_ref, part, pl.program_id(0) == 0)

    row = _bs((tm, D), lambda i: (i, 0))
    return pl.pallas_call(
        body, grid=(T // tm,), in_specs=[row, row], out_specs=[row, _bs((8, 128), lambda i: (0, 0))],
        out_shape=[S((T, D), F32), S((8, 128), F32)], name="loss_head", compiler_params=_cparams("arbitrary"))(y, target)


def _peer(mask):
    x, y, c = lax.axis_index("x"), lax.axis_index("y"), lax.axis_index("c")
    px = 1 - x if mask & 4 else x
    py = 1 - y if mask & 2 else y
    pc = 1 - c if mask & 1 else c
    return (px, py, pc), 4 * px + 2 * py + pc


def _exchange(arrs, scatter, name):
    n = len(arrs)
    outs_shape = [S(a.shape if scatter else (N_DEV,) + a.shape, a.dtype) for a in arrs]

    def body(*refs):
        ins, outs = refs[:n], refs[n:2 * n]
        send_sems, recv_sems, local_sems = refs[2 * n:]
        me = 4 * lax.axis_index("x") + 2 * lax.axis_index("y") + lax.axis_index("c")
        copies = []
        for w in range(n):
            src = ins[w].at[me] if scatter else ins[w]
            cp = pltpu.make_async_copy(src, outs[w].at[me], local_sems.at[w])
            cp.start()
            copies.append(cp)
        for k in range(1, N_DEV):
            peer, pidx = _peer(k)
            for w in range(n):
                src = ins[w].at[pidx] if scatter else ins[w]
                cp = pltpu.make_async_remote_copy(
                    src_ref=src, dst_ref=outs[w].at[me], send_sem=send_sems.at[w, k - 1], recv_sem=recv_sems.at[w, k - 1],
                    device_id=peer, device_id_type=pl.DeviceIdType.MESH)
                cp.start()
                copies.append(cp)
        for cp in copies:
            cp.wait()

    hbm = pl.BlockSpec(memory_space=pl.ANY)
    return pl.pallas_call(
        body, in_specs=[hbm] * n, out_specs=[hbm] * n, out_shape=outs_shape,
        scratch_shapes=[pltpu.SemaphoreType.DMA((n, N_DEV - 1)), pltpu.SemaphoreType.DMA((n, N_DEV - 1)),
                        pltpu.SemaphoreType.DMA((n,))],
        name=name)(*arrs)


def _adamw(w, m, v, parts, name):
    R, C = w.shape
    P = parts.shape[0]
    tr = _pick(R, tuple(t for t in (256, 176, 128, 64, 32, 16, 8) if P * t * C * 4 <= ADAMW_BLOCK_BYTES))
    c1 = 1.0 / (1.0 - ADAM_B1 ** ADAM_STEP)
    c2 = 1.0 / (1.0 - ADAM_B2 ** ADAM_STEP)

    def body(w_ref, m_ref, v_ref, p_ref, g_ref, d_ref, nm_ref, nv_ref):
        g = p_ref[0]
        for j in range(1, P):
            g = g + p_ref[j]
        m2 = ADAM_B1 * m_ref[...] + (1.0 - ADAM_B1) * g
        v2 = ADAM_B2 * v_ref[...] + (1.0 - ADAM_B2) * (g * g)
        g_ref[...] = g
        nm_ref[...] = m2
        nv_ref[...] = v2
        d_ref[...] = -ADAM_LR * ((m2 * c1) / (jnp.sqrt(v2 * c2) + ADAM_EPS) + ADAM_WD * w_ref[...])

    row = _bs((tr, C), lambda i: (i, 0))
    return pl.pallas_call(
        body, grid=(R // tr,), in_specs=[row, row, row, _bs((P, tr, C), lambda i: (0, i, 0))], out_specs=[row] * 4,
        out_shape=[S((R, C), F32)] * 4, name=name, compiler_params=_cparams("parallel"))(w, m, v, parts)


def _sum_parts(parts, name):
    P, R, C = parts.shape

    def body(p_ref, o_ref):
        g = p_ref[0]
        for j in range(1, P):
            g = g + p_ref[j]
        o_ref[...] = g

    return pl.pallas_call(body, out_shape=S((R, C), F32), name=name, compiler_params=_cparams())(parts)


def _pack(arrs):
    flat = jnp.concatenate([a.reshape(-1) for a in arrs])
    rows = -(-flat.shape[0] // 1024) * 8
    return jnp.pad(flat, (0, rows * 128 - flat.shape[0])).reshape(rows, 128)


def _unpack(packed, shapes):
    flat = packed.reshape(-1)
    out, off = [], 0
    for s in shapes:
        n = int(np.prod(s))
        out.append(flat[off:off + n].reshape(s))
        off += n
    return out


def _local_step(x, mem, target, p):
    T = x.shape[0]
    table = p["rel_bias_table"]
    xn = _rms_fwd(x, p["attn_norm_w"], "attn_norm_fwd")
    z = _matmul(xn, p["w_in"], name="mm_in")
    os_, lses = [], []
    for g, (win, dil) in enumerate(ATTN_GROUPS):
        o, l = _attn_fwd(z, table, p["q_norm_w"][g:g + 1], p["k_norm_w"][g:g + 1], g, dil)
        os_.append(o)
        lses.append(l)
    attn, lse = _attn_merge(os_, lses)
    u = _conv_fwd(z, p["conv_dw_w"], p["conv_dw_b"], p["conv_ln_w"], p["conv_ln_b"])
    mk, mv = _mem_fwd(mem, p["mem_norm_w"], p["w_mem_kv"], p["xk_norm_w"])
    cross = _cross_fwd(z, mk, mv, p["xq_norm_w"])
    merged = _branch_fwd(attn, u, cross, z, p["b_gate"], p["w_attn_o"], p["w_conv_o"], p["w_cross_o"])
    h1 = _matmul(merged, p["w_out"], residual=x, name="mm_out")
    hn = _rms_fwd(h1, p["ffn_norm_w"], "ffn_norm_fwd")
    up0 = _matmul(hn, p["w_up"], name="mm_up")
    f = _ffn_act_fwd(up0, p["ffn_conv_w"], p["ffn_conv_b"])
    h2 = _matmul(f, p["w_down"], residual=h1, name="mm_down")
    dh2, lsum = _loss_head(h2, target)
    g = {}
    d_f = _matmul(dh2, p["w_down"], tb=True, name="mm_down_dx")
    g["w_down"] = _matmul(f, dh2, ta=True, name="mm_down_dw")
    d_up0, g["ffn_conv_w"], g["ffn_conv_b"] = _ffn_act_bwd(up0, p["ffn_conv_w"], p["ffn_conv_b"], d_f)
    dhn = _matmul(d_up0, p["w_up"], tb=True, name="mm_up_dx")
    g["w_up"] = _matmul(hn, d_up0, ta=True, name="mm_up_dw")
    dh1, g["ffn_norm_w"] = _rms_bwd(h1, p["ffn_norm_w"], dhn, dh2, "ffn_norm_bwd")
    d_merged = _matmul(dh1, p["w_out"], tb=True, name="mm_out_dx")
    g["w_out"] = _matmul(merged, dh1, ta=True, name="mm_out_dw")
    (d_zg, d_attn, d_u, d_cross, g["w_attn_o"], g["w_conv_o"], g["w_cross_o"], g["b_gate"]) = _branch_bwd(
        d_merged, attn, u, cross, z, p["b_gate"], p["w_attn_o"], p["w_conv_o"], p["w_cross_o"])
    d_xq, dmk, dmv, g["xq_norm_w"] = _cross_bwd(z, mk, mv, p["xq_norm_w"], d_cross)
    g["w_mem_kv"], g["mem_norm_w"], g["xk_norm_w"] = _mem_bwd(mem, p["mem_norm_w"], p["w_mem_kv"], p["xk_norm_w"], dmk, dmv)
    d_val, d_gate, g["conv_dw_w"], g["conv_dw_b"], g["conv_ln_w"], g["conv_ln_b"] = _conv_bwd(
        z, p["conv_dw_w"], p["conv_dw_b"], p["conv_ln_w"], p["conv_ln_b"], d_u)
    dqs, dks, dvs, dqw, dkw, dtab = [], [], [], [], [], []
    for gi, (win, dil) in enumerate(ATTN_GROUPS):
        r = _attn_bwd(z, table, p["q_norm_w"][gi:gi + 1], p["k_norm_w"][gi:gi + 1], d_attn, attn, lse, gi, dil)
        for lst, val in zip((dqs, dks, dvs, dqw, dkw, dtab), r):
            lst.append(val)
    g["q_norm_w"] = jnp.concatenate(dqw, axis=0)
    g["k_norm_w"] = jnp.concatenate(dkw, axis=0)
    g["rel_bias_table"] = jnp.concatenate(dtab, axis=1)
    dz = jnp.concatenate(dqs + dks + dvs + [d_val, d_gate, d_xq, d_zg], axis=1)
    dxn = _matmul(dz, p["w_in"], tb=True, name="mm_in_dx")
    g["w_in"] = _matmul(xn, dz, ta=True, name="mm_in_dw")
    grad_x, g["attn_norm_w"] = _rms_bwd(x, p["attn_norm_w"], dxn, dh1, "attn_norm_bwd")
    return lsum[0, 0], grad_x, g


WEIGHT_NAMES = ["rel_bias_table", "attn_norm_w", "w_in", "b_gate", "q_norm_w", "k_norm_w", "w_attn_o", "conv_dw_w",
                "conv_dw_b", "conv_ln_w", "conv_ln_b", "w_conv_o", "mem_norm_w", "w_mem_kv", "xq_norm_w", "xk_norm_w",
                "w_cross_o", "w_out", "ffn_norm_w", "w_up", "ffn_conv_w", "ffn_conv_b", "w_down"]
COL_SHARDED = ("w_in", "w_attn_o", "w_conv_o", "w_cross_o", "w_up")
ROW_SHARDED = ("w_mem_kv", "w_out", "w_down")
SMALL_COL_SHARDED = ("conv_dw_w", "ffn_conv_w")
BIG = COL_SHARDED + ROW_SHARDED


def _cols_to_blocks(a):
    k, n8 = a.shape
    return a.reshape(k, N_DEV, n8 // N_DEV).transpose(1, 0, 2)


def _blocks_to_cols(a):
    return a.transpose(1, 0, 2).reshape(a.shape[1], N_DEV * a.shape[2])


def _step(x, mem, target, w, m, v):
    me = 4 * lax.axis_index("x") + 2 * lax.axis_index("y") + lax.axis_index("c")
    send = [w[n].astype(MXU) for n in BIG] + [w[n] for n in SMALL_COL_SHARDED]
    got = dict(zip(BIG + SMALL_COL_SHARDED, _exchange(send, False, "gather_weights")))
    p = {n: w[n] for n in WEIGHT_NAMES if n not in got}
    for n in COL_SHARDED + SMALL_COL_SHARDED:
        p[n] = _blocks_to_cols(got[n])
    for n in ROW_SHARDED:
        p[n] = got[n].reshape(-1, got[n].shape[-1])
    lsum, grad_x, g = _local_step(x, mem, target, p)
    small_names = [n for n in WEIGHT_NAMES if n not in BIG]
    small_shapes = [g[n].shape for n in small_names]
    send = [_cols_to_blocks(g[n]) for n in COL_SHARDED] + [g[n].reshape(N_DEV, -1, g[n].shape[-1]) for n in ROW_SHARDED]
    parts = dict(zip(BIG, _exchange(send, True, "scatter_grads")))
    small_parts = _exchange([_pack([g[n] for n in small_names])], False, "gather_small_grads")[0]
    gsmall = dict(zip(small_names, _unpack(_sum_parts(small_parts, "sum_small_grads"), small_shapes)))
    for n in SMALL_COL_SHARDED:
        width = w[n].shape[-1]
        gsmall[n] = lax.dynamic_slice_in_dim(gsmall[n], me * width, width, axis=1)
    res = {}
    for n in BIG:
        res[n] = _adamw(w[n], m[n], v[n], parts[n], "adamw_" + n)
    shapes = [w[n].shape for n in small_names]
    packed = [_pack([d[n] for n in small_names]) for d in (w, m, v, gsmall)]
    outs = _adamw(packed[0], packed[1], packed[2], packed[3][None], "adamw_small")
    unpacked = [_unpack(o, shapes) for o in outs]
    for j, n in enumerate(small_names):
        res[n] = tuple(unpacked[q][j] for q in range(4))
    return lsum, grad_x, res


def kernel(x, mem, rel_bias_table, attn_norm_w, w_in, b_gate, q_norm_w, k_norm_w, w_attn_o, conv_dw_w, conv_dw_b, conv_ln_w, conv_ln_b, w_conv_o, mem_norm_w, w_mem_kv, xq_norm_w, xk_norm_w, w_cross_o, w_out, ffn_norm_w, w_up, ffn_conv_w, ffn_conv_b, w_down, loss_target, m_rel_bias_table, m_attn_norm_w, m_w_in, m_b_gate, m_q_norm_w, m_k_norm_w, m_w_attn_o, m_conv_dw_w, m_conv_dw_b, m_conv_ln_w, m_conv_ln_b, m_w_conv_o, m_mem_norm_w, m_w_mem_kv, m_xq_norm_w, m_xk_norm_w, m_w_cross_o, m_w_out, m_ffn_norm_w, m_w_up, m_ffn_conv_w, m_ffn_conv_b, m_w_down, v_rel_bias_table, v_attn_norm_w, v_w_in, v_b_gate, v_q_norm_w, v_k_norm_w, v_w_attn_o, v_conv_dw_w, v_conv_dw_b, v_conv_ln_w, v_conv_ln_b, v_w_conv_o, v_mem_norm_w, v_w_mem_kv, v_xq_norm_w, v_xk_norm_w, v_w_cross_o, v_w_out, v_ffn_norm_w, v_w_up, v_ffn_conv_w, v_ffn_conv_b, v_w_down):
    ws = dict(zip(WEIGHT_NAMES, (rel_bias_table, attn_norm_w, w_in, b_gate, q_norm_w, k_norm_w, w_attn_o, conv_dw_w, conv_dw_b, conv_ln_w, conv_ln_b, w_conv_o, mem_norm_w, w_mem_kv, xq_norm_w, xk_norm_w, w_cross_o, w_out, ffn_norm_w, w_up, ffn_conv_w, ffn_conv_b, w_down)))
    ms = dict(zip(WEIGHT_NAMES, (m_rel_bias_table, m_attn_norm_w, m_w_in, m_b_gate, m_q_norm_w, m_k_norm_w, m_w_attn_o, m_conv_dw_w, m_conv_dw_b, m_conv_ln_w, m_conv_ln_b, m_w_conv_o, m_mem_norm_w, m_w_mem_kv, m_xq_norm_w, m_xk_norm_w, m_w_cross_o, m_w_out, m_ffn_norm_w, m_w_up, m_ffn_conv_w, m_ffn_conv_b, m_w_down)))
    vs = dict(zip(WEIGHT_NAMES, (v_rel_bias_table, v_attn_norm_w, v_w_in, v_b_gate, v_q_norm_w, v_k_norm_w, v_w_attn_o, v_conv_dw_w, v_conv_dw_b, v_conv_ln_w, v_conv_ln_b, v_w_conv_o, v_mem_norm_w, v_w_mem_kv, v_xq_norm_w, v_xk_norm_w, v_w_cross_o, v_w_out, v_ffn_norm_w, v_w_up, v_ffn_conv_w, v_ffn_conv_b, v_w_down)))
    full_shapes = {n: ws[n].shape for n in WEIGHT_NAMES}

    def squeeze(d):
        return {n: (a if n == "rel_bias_table" else a[0]) for n, a in d.items()}

    w, m, v = squeeze(ws), squeeze(ms), squeeze(vs)
    for d in (w, m, v):
        for n in WEIGHT_NAMES:
            if d[n].ndim == 1:
                d[n] = d[n][None]
    lsum, grad_x, res = _step(x[0], mem[0], loss_target[0], w, m, v)
    loss = lax.psum(0.5 / D_MODEL * lsum, ("x", "y", "c"))
    outs = [loss, grad_x[None]]
    for q in range(4):
        outs += [res[n][q].reshape(full_shapes[n]) for n in WEIGHT_NAMES]
    return tuple(outs)
```

```python
import functools
import math

import numpy as np
import jax
import jax.numpy as jnp
from jax import lax
from jax.experimental import pallas as pl
from jax.experimental.pallas import tpu as pltpu

F32 = jnp.float32
MXU = jnp.bfloat16
S = jax.ShapeDtypeStruct

D_MODEL = 1024
HEAD_DIM = 128
ATTN_GROUPS = ((128, 1), (512, 4), (2048, 16))
N_GROUPS = 3
HEADS = 4
GW = HEADS * HEAD_DIM
CONV_WIDTH = 31
N_MEM = 256
D_FF = 2816
FFN_CONV_WIDTH = 3
N_BUCKETS = 32
MAX_DISTANCE = 2048
RMS_EPS = 1e-6
LN_EPS = 1e-5
N_IN = 9216
NCB = N_IN // GW
BLK = 128
SCALE = HEAD_DIM ** -0.5
NEG = -1e30
N_DEV = 8

ADAM_LR, ADAM_B1, ADAM_B2, ADAM_EPS, ADAM_WD, ADAM_STEP = 0.001, 0.9, 0.999, 1e-08, 0.01, 10

VMEM_LIMIT = 48 * 1024 * 1024
CONV_HALO = 32
FFN_HALO = 8
ADAMW_BLOCK_BYTES = 4 * 1024 * 1024


def _cparams(*sem):
    return pltpu.CompilerParams(dimension_semantics=sem or None, vmem_limit_bytes=VMEM_LIMIT)


def _bs(shape, imap):
    return pl.BlockSpec(shape, imap)


def _dot(a, b):
    return lax.dot_general(a.astype(MXU), b.astype(MXU), (((1,), (0,)), ((), ())), preferred_element_type=F32)


def _dot_nt(a, b):
    return lax.dot_general(a.astype(MXU), b.astype(MXU), (((1,), (1,)), ((), ())), preferred_element_type=F32)


def _dot_tn(a, b):
    return lax.dot_general(a.astype(MXU), b.astype(MXU), (((0,), (0,)), ((), ())), preferred_element_type=F32)


def _sigmoid(x):
    return 1.0 / (1.0 + jnp.exp(-x))


def _rmsn(x, w):
    r = lax.rsqrt(jnp.mean(x * x, axis=-1, keepdims=True) + RMS_EPS)
    return x * r * w, r


def _rmsn_bwd(x, r, w, dy):
    g = dy * w
    dx = r * g - x * (r * r * r) * jnp.mean(x * g, axis=-1, keepdims=True)
    dw = jnp.sum(dy * x * r, axis=0, keepdims=True)
    return dx, dw


def _acc_out(ref, val, first):
    @pl.when(first)
    def _():
        ref[...] = val

    @pl.when(jnp.logical_not(first))
    def _():
        ref[...] += val


def _rms_fwd(x, w, name):
    T, D = x.shape
    tm = min(512, T)

    def body(x_ref, w_ref, o_ref):
        y, _ = _rmsn(x_ref[...], w_ref[...])
        o_ref[...] = y.astype(o_ref.dtype)

    return pl.pallas_call(
        body, grid=(T // tm,),
        in_specs=[_bs((tm, D), lambda i: (i, 0)), _bs((1, D), lambda i: (0, 0))],
        out_specs=_bs((tm, D), lambda i: (i, 0)),
        out_shape=S((T, D), MXU), name=name, compiler_params=_cparams("parallel"))(x, w)


def _rms_bwd(x, w, dy, resid, name):
    T, D = x.shape
    tm = min(512, T)

    def body(x_ref, w_ref, dy_ref, res_ref, dx_ref, dw_ref):
        xv = x_ref[...]
        _, r = _rmsn(xv, w_ref[...])
        dx, dw = _rmsn_bwd(xv, r, w_ref[...], dy_ref[...])
        dx_ref[...] = res_ref[...] + dx
        _acc_out(dw_ref, dw, pl.program_id(0) == 0)

    row = _bs((tm, D), lambda i: (i, 0))
    vec = _bs((1, D), lambda i: (0, 0))
    return pl.pallas_call(
        body, grid=(T // tm,), in_specs=[row, vec, row, row], out_specs=[row, vec],
        out_shape=[S((T, D), F32), S((1, D), F32)], name=name, compiler_params=_cparams("arbitrary"))(x, w, dy, resid)


def _pick(n, cands):
    for c in cands:
        if n % c == 0:
            return c
    return n


def _matmul(a, b, *, ta=False, tb=False, out_dtype=F32, residual=None, after=None, tm=None, tn=None, tk=None, name):
    M, K = (a.shape[1], a.shape[0]) if ta else a.shape
    N = b.shape[0] if tb else b.shape[1]
    tm = tm or _pick(M, (1024, 1408, 512, 256, 128))
    tn = tn or _pick(N, (512, 256, 128))
    tk = tk or _pick(K, (1024, 1408, 512, 256, 128))
    nk = K // tk
    dn = (((0 if ta else 1,), (1 if tb else 0,)), ((), ()))
    has_res = residual is not None
    n_in = 2 + has_res + (after is not None)

    def body(*refs):
        a_ref, b_ref = refs[0], refs[1]
        res_ref = refs[2] if has_res else None
        o_ref = refs[n_in]
        p = lax.dot_general(a_ref[...].astype(MXU), b_ref[...].astype(MXU), dn, preferred_element_type=F32)

        def finish(acc):
            if has_res:
                acc = acc + res_ref[...]
            o_ref[...] = acc.astype(o_ref.dtype)

        if nk == 1:
            finish(p)
        else:
            acc_ref = refs[-1]
            k = pl.program_id(2)

            @pl.when(k == 0)
            def _():
                acc_ref[...] = p

            @pl.when(k > 0)
            def _():
                acc_ref[...] += p

            @pl.when(k == nk - 1)
            def _():
                finish(acc_ref[...])

    a_spec = _bs((tk, tm), lambda i, j, k: (k, i)) if ta else _bs((tm, tk), lambda i, j, k: (i, k))
    b_spec = _bs((tn, tk), lambda i, j, k: (j, k)) if tb else _bs((tk, tn), lambda i, j, k: (k, j))
    o_spec = _bs((tm, tn), lambda i, j, k: (i, j))
    in_specs = [a_spec, b_spec] + ([o_spec] if has_res else [])
    args = (a, b) + ((residual,) if has_res else ())
    if after is not None:
        in_specs.append(_bs((8, 128), lambda i, j, k: (0, 0)))
        args += (after,)
    return pl.pallas_call(
        body, grid=(M // tm, N // tn, nk), in_specs=in_specs, out_specs=o_spec,
        out_shape=S((M, N), out_dtype), scratch_shapes=[pltpu.VMEM((tm, tn), F32)] if nk > 1 else [],
        name=name, compiler_params=_cparams("parallel", "parallel", "arbitrary"))(*args)


def _bucket_matrix(dilation):
    n = BLK
    qi = np.arange(n)[:, None]
    kj = np.arange(2 * n)[None, :]
    step = qi + n - kj
    dist = np.clip(step, 0, None) * dilation
    max_exact = N_BUCKETS // 2
    d = np.maximum(dist.astype(np.float32), np.float32(1.0))
    large = max_exact + (np.log(d / np.float32(max_exact)) / np.float32(math.log(MAX_DISTANCE / max_exact))
                         * np.float32(N_BUCKETS - max_exact)).astype(np.int32)
    large = np.minimum(large, N_BUCKETS - 1)
    bucket = np.where(dist < max_exact, dist, large)
    band = (step >= 0) & (step <= n)
    return np.where(band, bucket, -1).astype(np.int32)


def _build_bias(tbl_ref, bkt_ref, bias_ref, g):
    bk = bkt_ref[...]
    for h in range(HEADS):
        acc = jnp.full(bk.shape, NEG, F32)
        for b in range(N_BUCKETS):
            acc = jnp.where(bk == b, tbl_ref[b, HEADS * g + h], acc)
        bias_ref[h] = acc


def _attn_specs(g, dil):
    def zspec(off, prev):
        if prev:
            return _bs((BLK, GW), lambda c, i: (jnp.maximum(i - 1, 0), c * NCB + off + g))
        return _bs((BLK, GW), lambda c, i: (i, c * NCB + off + g))
    return zspec


def _attn_fwd(z, table, qw, kw, g, dil):
    T = z.shape[0]
    tc = T // dil
    nb = tc // BLK
    zv = z.reshape(tc, dil * N_IN)
    bkt = jnp.asarray(_bucket_matrix(dil))
    zspec = _attn_specs(g, dil)

    def body(tbl_ref, bkt_ref, qw_ref, kw_ref, q_ref, kp_ref, kc_ref, vp_ref, vc_ref, o_ref, lse_ref, bias_ref):
        c, i = pl.program_id(0), pl.program_id(1)

        @pl.when((c == 0) & (i == 0))
        def _():
            _build_bias(tbl_ref, bkt_ref, bias_ref, g)

        kj = lax.broadcasted_iota(jnp.int32, (BLK, 2 * BLK), 1)
        no_prev = jnp.logical_and(i == 0, kj < BLK)
        for h in range(HEADS):
            sl = slice(h * HEAD_DIM, (h + 1) * HEAD_DIM)
            qn, _ = _rmsn(q_ref[:, sl], qw_ref[...])
            kn, _ = _rmsn(jnp.concatenate([kp_ref[:, sl], kc_ref[:, sl]], axis=0), kw_ref[...])
            s = _dot_nt(qn, kn) * SCALE + bias_ref[h]
            s = jnp.where(no_prev, NEG, s)
            m = jnp.max(s, axis=-1, keepdims=True)
            p = jnp.exp(s - m)
            l = jnp.sum(p, axis=-1, keepdims=True)
            v = jnp.concatenate([vp_ref[:, sl], vc_ref[:, sl]], axis=0)
            o_ref[:, sl] = _dot(p, v) / l
            lse_ref[:, sl] = jnp.broadcast_to(m + jnp.log(l), (BLK, HEAD_DIM))

    ospec = _bs((BLK, GW), lambda c, i: (i, c))
    vec = _bs((1, HEAD_DIM), lambda c, i: (0, 0))
    o, lse = pl.pallas_call(
        body, grid=(dil, nb),
        in_specs=[pl.BlockSpec(memory_space=pltpu.SMEM), _bs((BLK, 2 * BLK), lambda c, i: (0, 0)), vec, vec,
                  zspec(0, False), zspec(3, True), zspec(3, False), zspec(6, True), zspec(6, False)],
        out_specs=[ospec, ospec],
        out_shape=[S((tc, dil * GW), F32), S((tc, dil * GW), F32)],
        scratch_shapes=[pltpu.VMEM((HEADS, BLK, 2 * BLK), F32)],
        name=f"attn_fwd_g{g}", compiler_params=_cparams("arbitrary", "arbitrary"))(table, bkt, qw, kw, zv, zv, zv, zv, zv)
    return o.reshape(T, GW), lse.reshape(T, GW)


def _attn_merge(os_, lses):
    T = os_[0].shape[0]
    tm = min(512, T)

    def body(o0, o1, o2, l0, l1, l2, a_ref, lse_ref):
        ls = [l0[...], l1[...], l2[...]]
        mx = jnp.maximum(jnp.maximum(ls[0], ls[1]), ls[2])
        tot = mx + jnp.log(jnp.exp(ls[0] - mx) + jnp.exp(ls[1] - mx) + jnp.exp(ls[2] - mx))
        a_ref[...] = (jnp.exp(ls[0] - tot) * o0[...] + jnp.exp(ls[1] - tot) * o1[...] + jnp.exp(ls[2] - tot) * o2[...])
        lse_ref[...] = tot

    row = _bs((tm, GW), lambda i: (i, 0))
    return pl.pallas_call(
        body, grid=(T // tm,), in_specs=[row] * 6, out_specs=[row, row],
        out_shape=[S((T, GW), F32), S((T, GW), F32)], name="attn_merge",
        compiler_params=_cparams("parallel"))(*os_, *lses)


def _attn_bwd(z, table, qw, kw, d_attn, attn, lse, g, dil):
    T = z.shape[0]
    tc = T // dil
    nb = tc // BLK
    zv = z.reshape(tc, dil * N_IN)
    bkt = jnp.asarray(_bucket_matrix(dil))

    def zspec(off, prev):
        if prev:
            return _bs((BLK, GW), lambda c, i: (jnp.clip(i - 1, 0, nb - 1), c * NCB + off + g))
        return _bs((BLK, GW), lambda c, i: (jnp.minimum(i, nb - 1), c * NCB + off + g))

    def body(tbl_ref, bkt_ref, qw_ref, kw_ref, q_ref, kp_ref, kc_ref, vp_ref, vc_ref, da_ref, at_ref, lse_ref,
             dq_ref, dk_ref, dv_ref, dqw_ref, dkw_ref, dtab_ref, bias_ref, dbias_ref, ck_ref, cv_ref):
        c, i = pl.program_id(0), pl.program_id(1)

        @pl.when((c == 0) & (i == 0))
        def _():
            _build_bias(tbl_ref, bkt_ref, bias_ref, g)
            dbias_ref[...] = jnp.zeros_like(dbias_ref)
            dqw_ref[...] = jnp.zeros_like(dqw_ref)
            dkw_ref[...] = jnp.zeros_like(dkw_ref)

        @pl.when(i == 0)
        def _():
            ck_ref[...] = jnp.zeros_like(ck_ref)
            cv_ref[...] = jnp.zeros_like(cv_ref)

        @pl.when(i < nb)
        def _():
            kj = lax.broadcasted_iota(jnp.int32, (BLK, 2 * BLK), 1)
            no_prev = jnp.logical_and(i == 0, kj < BLK)
            dqw_acc = jnp.zeros((1, HEAD_DIM), F32)
            dkw_acc = jnp.zeros((1, HEAD_DIM), F32)
            for h in range(HEADS):
                sl = slice(h * HEAD_DIM, (h + 1) * HEAD_DIM)
                qh = q_ref[:, sl]
                k = jnp.concatenate([kp_ref[:, sl], kc_ref[:, sl]], axis=0)
                qn, rq = _rmsn(qh, qw_ref[...])
                kn, rk = _rmsn(k, kw_ref[...])
                s = _dot_nt(qn, kn) * SCALE + bias_ref[h]
                s = jnp.where(no_prev, NEG, s)
                p = jnp.exp(s - lse_ref[:, h * HEAD_DIM:h * HEAD_DIM + 1])
                do = da_ref[:, sl]
                delta = jnp.sum(do * at_ref[:, sl], axis=-1, keepdims=True)
                v = jnp.concatenate([vp_ref[:, sl], vc_ref[:, sl]], axis=0)
                ds = p * (_dot_nt(do, v) - delta)
                dbias_ref[h] += ds
                dv = _dot_tn(p, do)
                dqn = _dot(ds, kn) * SCALE
                dkn = _dot_tn(ds, qn) * SCALE
                dq, dqw = _rmsn_bwd(qh, rq, qw_ref[...], dqn)
                dk, dkw = _rmsn_bwd(k, rk, kw_ref[...], dkn)
                dqw_acc += dqw
                dkw_acc += dkw
                dq_ref[:, sl] = dq.astype(dq_ref.dtype)
                dk_ref[:, sl] = (ck_ref[:, sl] + dk[:BLK]).astype(dk_ref.dtype)
                dv_ref[:, sl] = (cv_ref[:, sl] + dv[:BLK]).astype(dv_ref.dtype)
                ck_ref[:, sl] = dk[BLK:]
                cv_ref[:, sl] = dv[BLK:]
            dqw_ref[...] += dqw_acc
            dkw_ref[...] += dkw_acc

        @pl.when(i == nb)
        def _():
            dk_ref[...] = ck_ref[...].astype(dk_ref.dtype)
            dv_ref[...] = cv_ref[...].astype(dv_ref.dtype)

        @pl.when((c == dil - 1) & (i == nb))
        def _():
            bk = bkt_ref[...]
            rows = lax.broadcasted_iota(jnp.int32, (N_BUCKETS, HEAD_DIM), 0)
            lanes = lax.broadcasted_iota(jnp.int32, (N_BUCKETS, HEAD_DIM), 1)
            out = jnp.zeros((N_BUCKETS, HEAD_DIM), F32)
            for h in range(HEADS):
                acc = dbias_ref[h]
                for b in range(N_BUCKETS):
                    val = jnp.sum(jnp.where(bk == b, acc, 0.0))
                    out = jnp.where((rows == b) & (lanes == h), val, out)
            dtab_ref[...] = out

    cur = _bs((BLK, GW), lambda c, i: (jnp.minimum(i, nb - 1), c))
    prv = _bs((BLK, GW), lambda c, i: (jnp.maximum(i - 1, 0), c))
    vec = _bs((1, HEAD_DIM), lambda c, i: (0, 0))
    tabs = _bs((N_BUCKETS, HEAD_DIM), lambda c, i: (0, 0))
    view = lambda a: a.reshape(tc, dil * GW)
    dq, dk, dv, dqw, dkw, dtab = pl.pallas_call(
        body, grid=(dil, nb + 1),
        in_specs=[pl.BlockSpec(memory_space=pltpu.SMEM), _bs((BLK, 2 * BLK), lambda c, i: (0, 0)), vec, vec,
                  zspec(0, False), zspec(3, True), zspec(3, False), zspec(6, True), zspec(6, False), cur, cur, cur],
        out_specs=[cur, prv, prv, vec, vec, tabs],
        out_shape=[S((tc, dil * GW), MXU)] * 3 + [S((1, HEAD_DIM), F32)] * 2 + [S((N_BUCKETS, HEAD_DIM), F32)],
        scratch_shapes=[pltpu.VMEM((HEADS, BLK, 2 * BLK), F32), pltpu.VMEM((HEADS, BLK, 2 * BLK), F32),
                        pltpu.VMEM((BLK, GW), F32), pltpu.VMEM((BLK, GW), F32)],
        name=f"attn_bwd_g{g}", compiler_params=_cparams("arbitrary", "arbitrary"))(
            table, bkt, qw, kw, zv, zv, zv, zv, zv, view(d_attn), view(attn), view(lse))
    return dq.reshape(T, GW), dk.reshape(T, GW), dv.reshape(T, GW), dqw, dkw, dtab[:, :HEADS]


def _mem_fwd(mem, mem_norm_w, w_mem_kv, xk_w):
    def body(mem_ref, nw_ref, w_ref, xk_ref, mk_ref, mv_ref):
        mn, _ = _rmsn(mem_ref[...], nw_ref[...])
        kv = _dot(mn, w_ref[...])
        for h in range(HEADS):
            sl = slice(h * HEAD_DIM, (h + 1) * HEAD_DIM)
            kn, _ = _rmsn(kv[:, sl], xk_ref[...])
            mk_ref[:, sl] = kn.astype(mk_ref.dtype)
        mv_ref[...] = kv[:, GW:].astype(mv_ref.dtype)

    return pl.pallas_call(body, out_shape=[S((N_MEM, GW), MXU), S((N_MEM, GW), MXU)], name="mem_fwd",
                          compiler_params=_cparams())(mem, mem_norm_w, w_mem_kv, xk_w)


def _mem_bwd(mem, mem_norm_w, w_mem_kv, xk_w, dmk, dmv):
    def body(mem_ref, nw_ref, w_ref, xk_ref, dmk_ref, dmv_ref, dw_ref, dnw_ref, dxk_ref):
        memv = mem_ref[...]
        mn, r = _rmsn(memv, nw_ref[...])
        kv = _dot(mn, w_ref[...])
        dxk = jnp.zeros((1, HEAD_DIM), F32)
        parts = []
        for h in range(HEADS):
            sl = slice(h * HEAD_DIM, (h + 1) * HEAD_DIM)
            kh = kv[:, sl]
            _, rk = _rmsn(kh, xk_ref[...])
            dk, dw = _rmsn_bwd(kh, rk, xk_ref[...], dmk_ref[:, sl])
            dxk += dw
            parts.append(dk)
        dkv = jnp.concatenate(parts + [dmv_ref[...]], axis=1)
        dw_ref[...] = _dot_tn(mn, dkv)
        dmn = _dot_nt(dkv, w_ref[...])
        dnw_ref[...] = jnp.sum(dmn * memv * r, axis=0, keepdims=True)
        dxk_ref[...] = dxk

    return pl.pallas_call(
        body, out_shape=[S((D_MODEL, 2 * GW), F32), S((1, D_MODEL), F32), S((1, HEAD_DIM), F32)], name="mem_bwd",
        compiler_params=_cparams())(mem, mem_norm_w, w_mem_kv, xk_w, dmk, dmv)


def _cross_fwd(z, mk, mv, xq_w):
    T = z.shape[0]
    tm = min(512, T)

    def body(q_ref, mk_ref, mv_ref, w_ref, o_ref):
        for h in range(HEADS):
            sl = slice(h * HEAD_DIM, (h + 1) * HEAD_DIM)
            qn, _ = _rmsn(q_ref[:, sl], w_ref[...])
            s = _dot_nt(qn, mk_ref[:, sl]) * SCALE
            e = jnp.exp(s - jnp.max(s, axis=-1, keepdims=True))
            p = e / jnp.sum(e, axis=-1, keepdims=True)
            o_ref[:, sl] = _dot(p, mv_ref[:, sl]).astype(o_ref.dtype)

    full = _bs((N_MEM, GW), lambda i: (0, 0))
    return pl.pallas_call(
        body, grid=(T // tm,),
        in_specs=[_bs((tm, GW), lambda i: (i, 11)), full, full, _bs((1, HEAD_DIM), lambda i: (0, 0))],
        out_specs=_bs((tm, GW), lambda i: (i, 0)), out_shape=S((T, GW), MXU), name="cross_fwd",
        compiler_params=_cparams("parallel"))(z, mk, mv, xq_w)


def _cross_bwd(z, mk, mv, xq_w, d_cross):
    T = z.shape[0]
    tm = min(512, T)

    def body(q_ref, mk_ref, mv_ref, w_ref, do_ref, dq_ref, dmk_ref, dmv_ref, dw_ref):
        first = pl.program_id(0) == 0
        dw_acc = jnp.zeros((1, HEAD_DIM), F32)
        dmk_parts, dmv_parts = [], []
        for h in range(HEADS):
            sl = slice(h * HEAD_DIM, (h + 1) * HEAD_DIM)
            qh = q_ref[:, sl]
            qn, r = _rmsn(qh, w_ref[...])
            s = _dot_nt(qn, mk_ref[:, sl]) * SCALE
            e = jnp.exp(s - jnp.max(s, axis=-1, keepdims=True))
            p = e / jnp.sum(e, axis=-1, keepdims=True)
            do = do_ref[:, sl]
            dp = _dot_nt(do, mv_ref[:, sl])
            ds = p * (dp - jnp.sum(dp * p, axis=-1, keepdims=True)) * SCALE
            dmv_parts.append(_dot_tn(p, do))
            dmk_parts.append(_dot_tn(ds, qn))
            dq, dw = _rmsn_bwd(qh, r, w_ref[...], _dot(ds, mk_ref[:, sl]))
            dw_acc += dw
            dq_ref[:, sl] = dq.astype(dq_ref.dtype)
        _acc_out(dmk_ref, jnp.concatenate(dmk_parts, axis=1), first)
        _acc_out(dmv_ref, jnp.concatenate(dmv_parts, axis=1), first)
        _acc_out(dw_ref, dw_acc, first)

    full = _bs((N_MEM, GW), lambda i: (0, 0))
    vec = _bs((1, HEAD_DIM), lambda i: (0, 0))
    row = _bs((tm, GW), lambda i: (i, 0))
    return pl.pallas_call(
        body, grid=(T // tm,),
        in_specs=[_bs((tm, GW), lambda i: (i, 11)), full, full, vec, row],
        out_specs=[row, full, full, vec],
        out_shape=[S((T, GW), MXU), S((N_MEM, GW), F32), S((N_MEM, GW), F32), S((1, HEAD_DIM), F32)],
        name="cross_bwd", compiler_params=_cparams("arbitrary"))(z, mk, mv, xq_w, d_cross)


def _taps(ext, w_ref, width, base, rows):
    acc = None
    for k in range(width):
        s = base - (width - 1) + k
        term = ext[s:s + rows, :] * w_ref[k:k + 1, :]
        acc = term if acc is None else acc + term
    return acc


def _taps_t(ext, w_ref, width, base, rows):
    acc = None
    for k in range(width):
        s = base + (width - 1) - k
        term = ext[s:s + rows, :] * w_ref[k:k + 1, :]
        acc = term if acc is None else acc + term
    return acc


def _conv_fwd(z, cw, cb, lw, lb):
    T = z.shape[0]
    tm = min(512, T)
    hb = tm // CONV_HALO

    def body(val_ref, gate_ref, hval_ref, hgate_ref, cw_ref, cb_ref, lw_ref, lb_ref, o_ref):
        i = pl.program_id(0)
        halo = hval_ref[...] * _sigmoid(hgate_ref[...])
        halo = jnp.where(i == 0, 0.0, halo)
        ext = jnp.concatenate([halo, val_ref[...] * _sigmoid(gate_ref[...])], axis=0)
        y = _taps(ext, cw_ref, CONV_WIDTH, CONV_HALO, tm) + cb_ref[...]
        xc = y - jnp.mean(y, axis=-1, keepdims=True)
        a = xc * lax.rsqrt(jnp.mean(xc * xc, axis=-1, keepdims=True) + LN_EPS) * lw_ref[...] + lb_ref[...]
        o_ref[...] = (a * _sigmoid(a)).astype(o_ref.dtype)

    vec = _bs((1, GW), lambda i: (0, 0))
    halo_spec = lambda col: _bs((CONV_HALO, GW), lambda i: (jnp.maximum(i * hb - 1, 0), col))
    return pl.pallas_call(
        body, grid=(T // tm,),
        in_specs=[_bs((tm, GW), lambda i: (i, 9)), _bs((tm, GW), lambda i: (i, 10)), halo_spec(9), halo_spec(10),
                  _bs((CONV_WIDTH, GW), lambda i: (0, 0)), vec, vec, vec],
        out_specs=_bs((tm, GW), lambda i: (i, 0)), out_shape=S((T, GW), MXU), name="conv_fwd",
        compiler_params=_cparams("parallel"))(z, z, z, z, cw, cb, lw, lb)


def _conv_bwd(z, cw, cb, lw, lb, d_u):
    T = z.shape[0]
    tm = min(512, T)
    hb = tm // CONV_HALO
    nt = T // tm
    H = CONV_HALO

    def body(val_ref, gate_ref, pval_ref, pgate_ref, nval_ref, ngate_ref, du_ref, ndu_ref, cw_ref, cb_ref, lw_ref,
             lb_ref, dval_ref, dgate_ref, dcw_ref, dcb_ref, dlw_ref, dlb_ref):
        i = pl.program_id(0)
        first = i == 0
        val = jnp.concatenate([pval_ref[...], val_ref[...], nval_ref[...]], axis=0)
        sg = _sigmoid(jnp.concatenate([pgate_ref[...], gate_ref[...], ngate_ref[...]], axis=0))
        rid = lax.broadcasted_iota(jnp.int32, (tm + 2 * H, 1), 0)
        u0 = jnp.where(jnp.logical_and(first, rid < H), 0.0, val * sg)
        y = _taps(u0, cw_ref, CONV_WIDTH, H, tm + H) + cb_ref[...]
        xc = y - jnp.mean(y, axis=-1, keepdims=True)
        rs = lax.rsqrt(jnp.mean(xc * xc, axis=-1, keepdims=True) + LN_EPS)
        nh = xc * rs
        a = nh * lw_ref[...] + lb_ref[...]
        sa = _sigmoid(a)
        du = jnp.concatenate([du_ref[...], ndu_ref[...]], axis=0)
        rid2 = lax.broadcasted_iota(jnp.int32, (tm + H, 1), 0)
        du = jnp.where(jnp.logical_and(i == nt - 1, rid2 >= tm), 0.0, du)
        da = du * (sa * (1.0 + a * (1.0 - sa)))
        dn = da * lw_ref[...]
        dy = rs * (dn - jnp.mean(dn, axis=-1, keepdims=True) - nh * jnp.mean(dn * nh, axis=-1, keepdims=True))
        du0 = _taps_t(dy, cw_ref, CONV_WIDTH, 0, tm)
        v0, s0 = val[H:H + tm], sg[H:H + tm]
        dval_ref[...] = (du0 * s0).astype(dval_ref.dtype)
        dgate_ref[...] = (du0 * v0 * s0 * (1.0 - s0)).astype(dgate_ref.dtype)
        dy0 = dy[:tm]
        rows = [jnp.sum(dy0 * u0[H - (CONV_WIDTH - 1) + k:H - (CONV_WIDTH - 1) + k + tm], axis=0, keepdims=True)
                for k in range(CONV_WIDTH)]
        _acc_out(dcw_ref, jnp.concatenate(rows, axis=0), first)
        _acc_out(dcb_ref, jnp.sum(dy0, axis=0, keepdims=True), first)
        _acc_out(dlw_ref, jnp.sum(da[:tm] * nh[:tm], axis=0, keepdims=True), first)
        _acc_out(dlb_ref, jnp.sum(da[:tm], axis=0, keepdims=True), first)

    vec = _bs((1, GW), lambda i: (0, 0))
    cwspec = _bs((CONV_WIDTH, GW), lambda i: (0, 0))
    prev = lambda col: _bs((H, GW), lambda i: (jnp.maximum(i * hb - 1, 0), col))
    nxt = lambda col: _bs((H, GW), lambda i: (jnp.minimum((i + 1) * hb, nt * hb - 1), col))
    row = _bs((tm, GW), lambda i: (i, 0))
    return pl.pallas_call(
        body, grid=(nt,),
        in_specs=[_bs((tm, GW), lambda i: (i, 9)), _bs((tm, GW), lambda i: (i, 10)), prev(9), prev(10), nxt(9), nxt(10),
                  row, nxt(0), cwspec, vec, vec, vec],
        out_specs=[row, row, cwspec, vec, vec, vec],
        out_shape=[S((T, GW), MXU), S((T, GW), MXU), S((CONV_WIDTH, GW), F32)] + [S((1, GW), F32)] * 3,
        name="conv_bwd", compiler_params=_cparams("arbitrary"))(z, z, z, z, z, z, d_u, d_u, cw, cb, lw, lb)


def _ffn_act_fwd(up0, fw, fb):
    T = up0.shape[0]
    tm = min(256, T)
    hb = tm // FFN_HALO
    H = FFN_HALO

    def body(a_ref, g_ref, pa_ref, pg_ref, wa_ref, wg_ref, ba_ref, bg_ref, o_ref):
        i = pl.program_id(0)
        keep = jnp.where(i == 0, 0.0, 1.0)
        ea = jnp.concatenate([pa_ref[...] * keep, a_ref[...]], axis=0)
        eg = jnp.concatenate([pg_ref[...] * keep, g_ref[...]], axis=0)
        av = _taps(ea, wa_ref, FFN_CONV_WIDTH, H, tm) + ba_ref[...]
        gv = _taps(eg, wg_ref, FFN_CONV_WIDTH, H, tm) + bg_ref[...]
        o_ref[...] = (gv * _sigmoid(gv) * av).astype(o_ref.dtype)

    col = lambda j: _bs((tm, D_FF), lambda i: (i, j))
    prev = lambda j: _bs((H, D_FF), lambda i: (jnp.maximum(i * hb - 1, 0), j))
    wspec = lambda j: _bs((FFN_CONV_WIDTH, D_FF), lambda i: (0, j))
    bspec = lambda j: _bs((1, D_FF), lambda i: (0, j))
    return pl.pallas_call(
        body, grid=(T // tm,),
        in_specs=[col(0), col(1), prev(0), prev(1), wspec(0), wspec(1), bspec(0), bspec(1)],
        out_specs=_bs((tm, D_FF), lambda i: (i, 0)), out_shape=S((T, D_FF), MXU), name="ffn_act_fwd",
        compiler_params=_cparams("parallel"))(up0, up0, up0, up0, fw, fw, fb, fb)


def _ffn_act_bwd(up0, fw, fb, d_f):
    T = up0.shape[0]
    tm = min(256, T)
    hb = tm // FFN_HALO
    nt = T // tm
    H = FFN_HALO
    W = FFN_CONV_WIDTH

    def body(a_ref, g_ref, pa_ref, pg_ref, na_ref, ng_ref, df_ref, ndf_ref, wa_ref, wg_ref, ba_ref, bg_ref,
             dup_ref, dw_ref, db_ref):
        i = pl.program_id(0)
        first = i == 0
        keep = jnp.where(first, 0.0, 1.0)
        ea = jnp.concatenate([pa_ref[...] * keep, a_ref[...], na_ref[...]], axis=0)
        eg = jnp.concatenate([pg_ref[...] * keep, g_ref[...], ng_ref[...]], axis=0)
        av = _taps(ea, wa_ref, W, H, tm + H) + ba_ref[...]
        gv = _taps(eg, wg_ref, W, H, tm + H) + bg_ref[...]
        df = jnp.concatenate([df_ref[...], ndf_ref[...]], axis=0)
        rid = lax.broadcasted_iota(jnp.int32, (tm + H, 1), 0)
        df = jnp.where(jnp.logical_and(i == nt - 1, rid >= tm), 0.0, df)
        sg = _sigmoid(gv)
        d_av = df * gv * sg
        d_gv = df * av * (sg * (1.0 + gv * (1.0 - sg)))
        dup_ref[:, :D_FF] = _taps_t(d_av, wa_ref, W, 0, tm).astype(dup_ref.dtype)
        dup_ref[:, D_FF:] = _taps_t(d_gv, wg_ref, W, 0, tm).astype(dup_ref.dtype)
        dwa = [jnp.sum(d_av[:tm] * ea[H - (W - 1) + k:H - (W - 1) + k + tm], axis=0, keepdims=True) for k in range(W)]
        dwg = [jnp.sum(d_gv[:tm] * eg[H - (W - 1) + k:H - (W - 1) + k + tm], axis=0, keepdims=True) for k in range(W)]
        dw = jnp.concatenate([jnp.concatenate(dwa, axis=0), jnp.concatenate(dwg, axis=0)], axis=1)
        db = jnp.concatenate([jnp.sum(d_av[:tm], axis=0, keepdims=True), jnp.sum(d_gv[:tm], axis=0, keepdims=True)], axis=1)
        _acc_out(dw_ref, dw, first)
        _acc_out(db_ref, db, first)

    col = lambda j: _bs((tm, D_FF), lambda i: (i, j))
    prev = lambda j: _bs((H, D_FF), lambda i: (jnp.maximum(i * hb - 1, 0), j))
    nxt = lambda j: _bs((H, D_FF), lambda i: (jnp.minimum((i + 1) * hb, nt * hb - 1), j))
    wspec = lambda j: _bs((W, D_FF), lambda i: (0, j))
    bspec = lambda j: _bs((1, D_FF), lambda i: (0, j))
    return pl.pallas_call(
        body, grid=(nt,),
        in_specs=[col(0), col(1), prev(0), prev(1), nxt(0), nxt(1), col(0), nxt(0), wspec(0), wspec(1), bspec(0), bspec(1)],
        out_specs=[_bs((tm, 2 * D_FF), lambda i: (i, 0)), _bs((W, 2 * D_FF), lambda i: (0, 0)),
                   _bs((1, 2 * D_FF), lambda i: (0, 0))],
        out_shape=[S((T, 2 * D_FF), MXU), S((W, 2 * D_FF), F32), S((1, 2 * D_FF), F32)],
        name="ffn_act_bwd", compiler_params=_cparams("arbitrary"))(
            up0, up0, up0, up0, up0, up0, d_f, d_f, fw, fw, fb, fb)


def _branch_fwd(attn, u, cross, z, b_gate, wa, wc, wx):
    T = z.shape[0]
    tm = min(512, T)

    def body(a_ref, u_ref, x_ref, g0_ref, g1_ref, g2_ref, b_ref, wa_ref, wc_ref, wx_ref, o_ref):
        acc = None
        for j, (act, g_ref, w_ref) in enumerate(((a_ref, g0_ref, wa_ref), (u_ref, g1_ref, wc_ref), (x_ref, g2_ref, wx_ref))):
            gate = _sigmoid(g_ref[...] + b_ref[:, j * D_MODEL:(j + 1) * D_MODEL])
            term = gate * _dot(act[...], w_ref[...])
            acc = term if acc is None else acc + term
        o_ref[...] = acc.astype(o_ref.dtype)

    act = _bs((tm, GW), lambda i: (i, 0))
    gcol = lambda j: _bs((tm, D_MODEL), lambda i: (i, 6 + j))
    wfull = _bs((GW, D_MODEL), lambda i: (0, 0))
    return pl.pallas_call(
        body, grid=(T // tm,),
        in_specs=[act, act, act, gcol(0), gcol(1), gcol(2), _bs((1, 3 * D_MODEL), lambda i: (0, 0)), wfull, wfull, wfull],
        out_specs=_bs((tm, D_MODEL), lambda i: (i, 0)), out_shape=S((T, D_MODEL), MXU), name="branch_fwd",
        compiler_params=_cparams("parallel"))(attn, u, cross, z, z, z, b_gate, wa, wc, wx)


def _branch_bwd(d_merged, attn, u, cross, z, b_gate, wa, wc, wx):
    T = z.shape[0]
    tm = min(512, T)

    def body(dm_ref, a_ref, u_ref, x_ref, g0_ref, g1_ref, g2_ref, b_ref, wa_ref, wc_ref, wx_ref,
             dzg_ref, da_ref, du_ref, dx_ref, dwa_ref, dwc_ref, dwx_ref, db_ref):
        first = pl.program_id(0) == 0
        dm = dm_ref[...]
        dbs = []
        for j, (act, g_ref, w_ref, dact_ref, dw_ref) in enumerate((
                (a_ref, g0_ref, wa_ref, da_ref, dwa_ref), (u_ref, g1_ref, wc_ref, du_ref, dwc_ref),
                (x_ref, g2_ref, wx_ref, dx_ref, dwx_ref))):
            av = act[...]
            gate = _sigmoid(g_ref[...] + b_ref[:, j * D_MODEL:(j + 1) * D_MODEL])
            y = _dot(av, w_ref[...])
            dzg = dm * y * gate * (1.0 - gate)
            dzg_ref[:, j * D_MODEL:(j + 1) * D_MODEL] = dzg.astype(dzg_ref.dtype)
            dbs.append(jnp.sum(dzg, axis=0, keepdims=True))
            dy = (gate * dm).astype(MXU)
            dact_ref[...] = _dot_nt(dy, w_ref[...])
            _acc_out(dw_ref, _dot_tn(av, dy), first)
        _acc_out(db_ref, jnp.concatenate(dbs, axis=1), first)

    act = _bs((tm, GW), lambda i: (i, 0))
    gcol = lambda j: _bs((tm, D_MODEL), lambda i: (i, 6 + j))
    wfull = _bs((GW, D_MODEL), lambda i: (0, 0))
    bvec = _bs((1, 3 * D_MODEL), lambda i: (0, 0))
    return pl.pallas_call(
        body, grid=(T // tm,),
        in_specs=[_bs((tm, D_MODEL), lambda i: (i, 0)), act, act, act, gcol(0), gcol(1), gcol(2), bvec, wfull, wfull, wfull],
        out_specs=[_bs((tm, 3 * D_MODEL), lambda i: (i, 0)), act, act, act, wfull, wfull, wfull, bvec],
        out_shape=[S((T, 3 * D_MODEL), MXU)] + [S((T, GW), F32)] * 3 + [S((GW, D_MODEL), F32)] * 3 + [S((1, 3 * D_MODEL), F32)],
        name="branch_bwd", compiler_params=_cparams("arbitrary"))(d_merged, attn, u, cross, z, z, z, b_gate, wa, wc, wx)


def _loss_head(y, target):
    T, D = y.shape
    tm = min(512, T)

    def body(y_ref, t_ref, dy_ref, l_ref):
        e = y_ref[...] - t_ref[...]
        dy_ref[...] = e * (1.0 / D)
        part = jnp.full((8, 128), jnp.sum(e * e), F32)
        _acc_out(l_ref, part, pl.program_id(0) == 0)

    row = _bs((tm, D), lambda i: (i, 0))
    return pl.pallas_call(
        body, grid=(T // tm,), in_specs=[row, row], out_specs=[row, _bs((8, 128), lambda i: (0, 0))],
        out_shape=[S((T, D), F32), S((8, 128), F32)], name="loss_head", compiler_params=_cparams("arbitrary"))(y, target)


def _peer(mask):
    x, y, c = lax.axis_index("x"), lax.axis_index("y"), lax.axis_index("c")
    px = 1 - x if mask & 4 else x
    py = 1 - y if mask & 2 else y
    pc = 1 - c if mask & 1 else c
    return (px, py, pc), 4 * px + 2 * py + pc


def _exchange(arrs, scatter, name):
    n = len(arrs)
    outs_shape = [S(a.shape if scatter else (N_DEV,) + a.shape, a.dtype) for a in arrs]

    def body(*refs):
        ins, outs = refs[:n], refs[n:2 * n]
        send_sems, recv_sems, local_sems = refs[2 * n:]
        me = 4 * lax.axis_index("x") + 2 * lax.axis_index("y") + lax.axis_index("c")
        copies = []
        for w in range(n):
            src = ins[w].at[me] if scatter else ins[w]
            cp = pltpu.make_async_copy(src, outs[w].at[me], local_sems.at[w])
            cp.start()
            copies.append(cp)
        for k in range(1, N_DEV):
            peer, pidx = _peer(k)
            for w in range(n):
                src = ins[w].at[pidx] if scatter else ins[w]
                cp = pltpu.make_async_remote_copy(
                    src_ref=src, dst_ref=outs[w].at[me], send_sem=send_sems.at[w, k - 1], recv_sem=recv_sems.at[w, k - 1],
                    device_id=peer, device_id_type=pl.DeviceIdType.MESH)
                cp.start()
                copies.append(cp)
        for cp in copies:
            cp.wait()

    hbm = pl.BlockSpec(memory_space=pl.ANY)
    return pl.pallas_call(
        body, in_specs=[hbm] * n, out_specs=[hbm] * n, out_shape=outs_shape,
        scratch_shapes=[pltpu.SemaphoreType.DMA((n, N_DEV - 1)), pltpu.SemaphoreType.DMA((n, N_DEV - 1)),
                        pltpu.SemaphoreType.DMA((n,))],
        name=name)(*arrs)


def _exchange_copies(ins, lands, send_sems, recv_sems, local_sems, scatter):
    n = len(ins)
    me = 4 * lax.axis_index("x") + 2 * lax.axis_index("y") + lax.axis_index("c")
    copies = []
    for w in range(n):
        src = ins[w].at[me] if scatter else ins[w]
        copies.append(pltpu.make_async_copy(src, lands[w].at[me], local_sems.at[w]))
    for k in range(1, N_DEV):
        peer, pidx = _peer(k)
        for w in range(n):
            src = ins[w].at[pidx] if scatter else ins[w]
            copies.append(pltpu.make_async_remote_copy(
                src_ref=src, dst_ref=lands[w].at[me], send_sem=send_sems.at[w * (N_DEV - 1) + k - 1],
                recv_sem=recv_sems.at[w * (N_DEV - 1) + k - 1],
                device_id=peer, device_id_type=pl.DeviceIdType.MESH))
    return copies


_HBM_SPEC = pl.BlockSpec(memory_space=pltpu.HBM)
_SEM_SPEC = pl.BlockSpec(memory_space=pltpu.SEMAPHORE)
_DATAFLOW = pltpu.SideEffectType.DATAFLOW_SIDE_EFFECTING


def _exchange_start(arrs, scatter, name):
    n = len(arrs)
    land_shapes = [a.shape if scatter else (N_DEV,) + a.shape for a in arrs]

    def body(*refs):
        ins, lands = refs[:n], refs[n:2 * n]
        send_sems, recv_sems, local_sems = refs[2 * n:2 * n + 3]
        token = refs[-1]
        for cp in _exchange_copies(ins, lands, send_sems, recv_sems, local_sems, scatter):
            cp.start()
        token[...] = jnp.zeros_like(token)

    out_shape = ([pltpu.SemaphoreType.DMA((n * (N_DEV - 1),)), pltpu.SemaphoreType.DMA((n * (N_DEV - 1),)),
                  pltpu.SemaphoreType.DMA((n,))]
                 + [pltpu.HBM(a.shape, a.dtype) for a in arrs]
                 + [pltpu.HBM(s, a.dtype) for s, a in zip(land_shapes, arrs)]
                 + [S((8, 128), F32)])
    args = ([pltpu.with_memory_space_constraint(a, pltpu.HBM) for a in arrs]
            + [pltpu.with_memory_space_constraint(lax.empty(s, a.dtype), pltpu.HBM) for s, a in zip(land_shapes, arrs)])
    outs = pl.pallas_call(
        body, in_specs=[_HBM_SPEC] * (2 * n),
        out_specs=[_SEM_SPEC] * 3 + [_HBM_SPEC] * (2 * n) + [pl.BlockSpec(memory_space=pltpu.VMEM)],
        out_shape=out_shape, input_output_aliases={j: 3 + j for j in range(2 * n)},
        name=name, compiler_params=pltpu.CompilerParams(has_side_effects=_DATAFLOW))(*args)
    return (n, scatter, outs[:3], outs[3:3 + n], outs[3 + n:3 + 2 * n]), outs[-1]


def _exchange_wait(state, after, name):
    n, scatter, sems, ins, lands = state

    def body(*refs):
        ins_r, lands_r = refs[:n], refs[n:2 * n]
        send_sems, recv_sems, local_sems = refs[2 * n:2 * n + 3]
        for cp in _exchange_copies(ins_r, lands_r, send_sems, recv_sems, local_sems, scatter):
            cp.wait()

    outs = pl.pallas_call(
        body, in_specs=[_HBM_SPEC] * (2 * n) + [_SEM_SPEC] * 3 + [pl.BlockSpec(memory_space=pl.ANY)],
        out_specs=[_HBM_SPEC] * (2 * n),
        out_shape=[pltpu.HBM(a.shape, a.dtype) for a in ins] + [pltpu.HBM(a.shape, a.dtype) for a in lands],
        input_output_aliases={j: j for j in range(2 * n)},
        name=name, compiler_params=pltpu.CompilerParams(has_side_effects=_DATAFLOW))(*ins, *lands, *sems, after)
    return list(outs[n:])


def _adamw(w, m, v, parts, name):
    R, C = w.shape
    P = parts.shape[0]
    tr = _pick(R, tuple(t for t in (256, 176, 128, 64, 32, 16, 8) if P * t * C * 4 <= ADAMW_BLOCK_BYTES))
    c1 = 1.0 / (1.0 - ADAM_B1 ** ADAM_STEP)
    c2 = 1.0 / (1.0 - ADAM_B2 ** ADAM_STEP)

    def body(w_ref, m_ref, v_ref, p_ref, g_ref, d_ref, nm_ref, nv_ref):
        g = p_ref[0].astype(F32)
        for j in range(1, P):
            g = g + p_ref[j].astype(F32)
        m2 = ADAM_B1 * m_ref[...] + (1.0 - ADAM_B1) * g
        v2 = ADAM_B2 * v_ref[...] + (1.0 - ADAM_B2) * (g * g)
        g_ref[...] = g
        nm_ref[...] = m2
        nv_ref[...] = v2
        d_ref[...] = -ADAM_LR * ((m2 * c1) / (jnp.sqrt(v2 * c2) + ADAM_EPS) + ADAM_WD * w_ref[...])

    row = _bs((tr, C), lambda i: (i, 0))
    return pl.pallas_call(
        body, grid=(R // tr,), in_specs=[row, row, row, _bs((P, tr, C), lambda i: (0, i, 0))], out_specs=[row] * 4,
        out_shape=[S((R, C), F32)] * 4, name=name, compiler_params=_cparams("parallel"))(w, m, v, parts)


def _sum_parts(parts, name):
    P, R, C = parts.shape

    def body(p_ref, o_ref):
        g = p_ref[0]
        for j in range(1, P):
            g = g + p_ref[j]
        o_ref[...] = g

    return pl.pallas_call(body, out_shape=S((R, C), F32), name=name, compiler_params=_cparams())(parts)


def _pack(arrs):
    flat = jnp.concatenate([a.reshape(-1) for a in arrs])
    rows = -(-flat.shape[0] // 1024) * 8
    return jnp.pad(flat, (0, rows * 128 - flat.shape[0])).reshape(rows, 128)


def _unpack(packed, shapes):
    flat = packed.reshape(-1)
    out, off = [], 0
    for s in shapes:
        n = int(np.prod(s))
        out.append(flat[off:off + n].reshape(s))
        off += n
    return out


def _behind(a, token):
    return a if token is None else a + token[0, 0]


def _local_step(x, mem, target, p, comm=None):
    table = p["rel_bias_table"]
    xn = _rms_fwd(x, p["attn_norm_w"], "attn_norm_fwd")
    z = _matmul(xn, p["w_in"], name="mm_in")
    os_, lses = [], []
    for g, (win, dil) in enumerate(ATTN_GROUPS):
        o, l = _attn_fwd(z, table, p["q_norm_w"][g:g + 1], p["k_norm_w"][g:g + 1], g, dil)
        os_.append(o)
        lses.append(l)
    attn, lse = _attn_merge(os_, lses)
    u = _conv_fwd(z, p["conv_dw_w"], p["conv_dw_b"], p["conv_ln_w"], p["conv_ln_b"])
    if comm is not None:
        p = {**p, **comm.late_weights(after=u)}
    mk, mv = _mem_fwd(mem, p["mem_norm_w"], p["w_mem_kv"], p["xk_norm_w"])
    cross = _cross_fwd(z, mk, mv, p["xq_norm_w"])
    merged = _branch_fwd(attn, u, cross, z, p["b_gate"], p["w_attn_o"], p["w_conv_o"], p["w_cross_o"])
    h1 = _matmul(merged, p["w_out"], residual=x, name="mm_out")
    hn = _rms_fwd(h1, p["ffn_norm_w"], "ffn_norm_fwd")
    up0 = _matmul(hn, p["w_up"], name="mm_up")
    f = _ffn_act_fwd(up0, p["ffn_conv_w"], p["ffn_conv_b"])
    h2 = _matmul(f, p["w_down"], residual=h1, name="mm_down")
    dh2, lsum = _loss_head(h2, target)
    g = {}
    d_f = _matmul(dh2, p["w_down"], tb=True, name="mm_down_dx")
    g["w_down"] = _matmul(f, dh2, ta=True, name="mm_down_dw")
    d_up0, g["ffn_conv_w"], g["ffn_conv_b"] = _ffn_act_bwd(up0, p["ffn_conv_w"], p["ffn_conv_b"], d_f)
    dhn = _matmul(d_up0, p["w_up"], tb=True, name="mm_up_dx")
    g["w_up"] = _matmul(hn, d_up0, ta=True, name="mm_up_dw")
    dh1, g["ffn_norm_w"] = _rms_bwd(h1, p["ffn_norm_w"], dhn, dh2, "ffn_norm_bwd")
    d_merged = _matmul(dh1, p["w_out"], tb=True, name="mm_out_dx")
    g["w_out"] = _matmul(merged, dh1, ta=True, name="mm_out_dw")
    (d_zg, d_attn, d_u, d_cross, g["w_attn_o"], g["w_conv_o"], g["w_cross_o"], g["b_gate"]) = _branch_bwd(
        d_merged, attn, u, cross, z, p["b_gate"], p["w_attn_o"], p["w_conv_o"], p["w_cross_o"])
    d_xq, dmk, dmv, g["xq_norm_w"] = _cross_bwd(z, mk, mv, p["xq_norm_w"], d_cross)
    g["w_mem_kv"], g["mem_norm_w"], g["xk_norm_w"] = _mem_bwd(mem, p["mem_norm_w"], p["w_mem_kv"], p["xk_norm_w"], dmk, dmv)
    tok = comm.start_early_grads(g) if comm is not None else None
    d_val, d_gate, g["conv_dw_w"], g["conv_dw_b"], g["conv_ln_w"], g["conv_ln_b"] = _conv_bwd(
        z, p["conv_dw_w"], _behind(p["conv_dw_b"], tok), p["conv_ln_w"], p["conv_ln_b"], d_u)
    dqs, dks, dvs, dqw, dkw, dtab = [], [], [], [], [], []
    for gi, (win, dil) in enumerate(ATTN_GROUPS):
        r = _attn_bwd(z, table, p["q_norm_w"][gi:gi + 1], p["k_norm_w"][gi:gi + 1], d_attn, attn, lse, gi, dil)
        for lst, val in zip((dqs, dks, dvs, dqw, dkw, dtab), r):
            lst.append(val)
    g["q_norm_w"] = jnp.concatenate(dqw, axis=0)
    g["k_norm_w"] = jnp.concatenate(dkw, axis=0)
    g["rel_bias_table"] = jnp.concatenate(dtab, axis=1)
    dz = jnp.concatenate(dqs + dks + dvs + [d_val, d_gate, d_xq, d_zg], axis=1)
    g["w_in"] = _matmul(xn, dz, ta=True, name="mm_in_dw")
    tok = comm.start_w_in_grad(g["w_in"]) if comm is not None else None
    dxn = _matmul(dz, p["w_in"], tb=True, after=tok, name="mm_in_dx")
    grad_x, g["attn_norm_w"] = _rms_bwd(x, p["attn_norm_w"], dxn, dh1, "attn_norm_bwd")
    return lsum[0, 0], grad_x, g


WEIGHT_NAMES = ["rel_bias_table", "attn_norm_w", "w_in", "b_gate", "q_norm_w", "k_norm_w", "w_attn_o", "conv_dw_w",
                "conv_dw_b", "conv_ln_w", "conv_ln_b", "w_conv_o", "mem_norm_w", "w_mem_kv", "xq_norm_w", "xk_norm_w",
                "w_cross_o", "w_out", "ffn_norm_w", "w_up", "ffn_conv_w", "ffn_conv_b", "w_down"]
COL_SHARDED = ("w_in", "w_attn_o", "w_conv_o", "w_cross_o", "w_up")
ROW_SHARDED = ("w_mem_kv", "w_out", "w_down")
SMALL_COL_SHARDED = ("conv_dw_w", "ffn_conv_w")
BIG = COL_SHARDED + ROW_SHARDED


def _cols_to_blocks(a):
    k, n8 = a.shape
    return a.reshape(k, N_DEV, n8 // N_DEV).transpose(1, 0, 2)


def _blocks_to_cols(a):
    return a.transpose(1, 0, 2).reshape(a.shape[1], N_DEV * a.shape[2])


def _step(x, mem, target, w, m, v):
    me = 4 * lax.axis_index("x") + 2 * lax.axis_index("y") + lax.axis_index("c")

    def to_full(n, blocks):
        return _blocks_to_cols(blocks) if n in COL_SHARDED + SMALL_COL_SHARDED else blocks.reshape(-1, blocks.shape[-1])

    def to_blocks(n, grad):
        blocks = _cols_to_blocks(grad) if n in COL_SHARDED else grad.reshape(N_DEV, -1, grad.shape[-1])
        return blocks.astype(MXU)

    first = ("w_in",) + SMALL_COL_SHARDED
    late = tuple(n for n in BIG if n != "w_in")
    cast = lambda n: w[n].astype(MXU) if n in BIG else w[n]
    first_state, _ = _exchange_start([cast(n) for n in first], False, "gather_first_start")
    late_state, late_token = _exchange_start([cast(n) for n in late], False, "gather_late_start")
    got = _exchange_wait(first_state, late_token, "gather_first_wait")
    p = {n: w[n] for n in WEIGHT_NAMES if n not in BIG + SMALL_COL_SHARDED}
    p.update({n: to_full(n, b) for n, b in zip(first, got)})

    class Comm:
        def late_weights(self, after):
            return {n: to_full(n, b) for n, b in zip(late, _exchange_wait(late_state, after, "gather_late_wait"))}

        def start_early_grads(self, g):
            self.early_state, token = _exchange_start([to_blocks(n, g[n]) for n in late], True, "scatter_early_start")
            return token

        def start_w_in_grad(self, grad):
            self.w_in_state, token = _exchange_start([to_blocks("w_in", grad)], True, "scatter_w_in_start")
            return token

    comm = Comm()
    lsum, grad_x, g = _local_step(x, mem, target, p, comm)
    small_names = [n for n in WEIGHT_NAMES if n not in BIG]
    small_shapes = [g[n].shape for n in small_names]
    small_parts = _exchange([_pack([g[n] for n in small_names])], False, "gather_small_grads")[0]
    gsmall = dict(zip(small_names, _unpack(_sum_parts(small_parts, "sum_small_grads"), small_shapes)))
    for n in SMALL_COL_SHARDED:
        width = w[n].shape[-1]
        gsmall[n] = lax.dynamic_slice_in_dim(gsmall[n], me * width, width, axis=1)
    res = {}
    parts = dict(zip(late, _exchange_wait(comm.early_state, grad_x, "scatter_early_wait")))
    for n in late:
        res[n] = _adamw(w[n], m[n], v[n], parts[n], "adamw_" + n)
    w_in_parts = _exchange_wait(comm.w_in_state, res[late[-1]][1], "scatter_w_in_wait")[0]
    res["w_in"] = _adamw(w["w_in"], m["w_in"], v["w_in"], w_in_parts, "adamw_w_in")
    shapes = [w[n].shape for n in small_names]
    packed = [_pack([d[n] for n in small_names]) for d in (w, m, v, gsmall)]
    outs = _adamw(packed[0], packed[1], packed[2], packed[3][None], "adamw_small")
    unpacked = [_unpack(o, shapes) for o in outs]
    for j, n in enumerate(small_names):
        res[n] = tuple(unpacked[q][j] for q in range(4))
    return lsum, grad_x, res


def kernel(x, mem, rel_bias_table, attn_norm_w, w_in, b_gate, q_norm_w, k_norm_w, w_attn_o, conv_dw_w, conv_dw_b, conv_ln_w, conv_ln_b, w_conv_o, mem_norm_w, w_mem_kv, xq_norm_w, xk_norm_w, w_cross_o, w_out, ffn_norm_w, w_up, ffn_conv_w, ffn_conv_b, w_down, loss_target, m_rel_bias_table, m_attn_norm_w, m_w_in, m_b_gate, m_q_norm_w, m_k_norm_w, m_w_attn_o, m_conv_dw_w, m_conv_dw_b, m_conv_ln_w, m_conv_ln_b, m_w_conv_o, m_mem_norm_w, m_w_mem_kv, m_xq_norm_w, m_xk_norm_w, m_w_cross_o, m_w_out, m_ffn_norm_w, m_w_up, m_ffn_conv_w, m_ffn_conv_b, m_w_down, v_rel_bias_table, v_attn_norm_w, v_w_in, v_b_gate, v_q_norm_w, v_k_norm_w, v_w_attn_o, v_conv_dw_w, v_conv_dw_b, v_conv_ln_w, v_conv_ln_b, v_w_conv_o, v_mem_norm_w, v_w_mem_kv, v_xq_norm_w, v_xk_norm_w, v_w_cross_o, v_w_out, v_ffn_norm_w, v_w_up, v_ffn_conv_w, v_ffn_conv_b, v_w_down):
    ws = dict(zip(WEIGHT_NAMES, (rel_bias_table, attn_norm_w, w_in, b_gate, q_norm_w, k_norm_w, w_attn_o, conv_dw_w, conv_dw_b, conv_ln_w, conv_ln_b, w_conv_o, mem_norm_w, w_mem_kv, xq_norm_w, xk_norm_w, w_cross_o, w_out, ffn_norm_w, w_up, ffn_conv_w, ffn_conv_b, w_down)))
    ms = dict(zip(WEIGHT_NAMES, (m_rel_bias_table, m_attn_norm_w, m_w_in, m_b_gate, m_q_norm_w, m_k_norm_w, m_w_attn_o, m_conv_dw_w, m_conv_dw_b, m_conv_ln_w, m_conv_ln_b, m_w_conv_o, m_mem_norm_w, m_w_mem_kv, m_xq_norm_w, m_xk_norm_w, m_w_cross_o, m_w_out, m_ffn_norm_w, m_w_up, m_ffn_conv_w, m_ffn_conv_b, m_w_down)))
    vs = dict(zip(WEIGHT_NAMES, (v_rel_bias_table, v_attn_norm_w, v_w_in, v_b_gate, v_q_norm_w, v_k_norm_w, v_w_attn_o, v_conv_dw_w, v_conv_dw_b, v_conv_ln_w, v_conv_ln_b, v_w_conv_o, v_mem_norm_w, v_w_mem_kv, v_xq_norm_w, v_xk_norm_w, v_w_cross_o, v_w_out, v_ffn_norm_w, v_w_up, v_ffn_conv_w, v_ffn_conv_b, v_w_down)))
    full_shapes = {n: ws[n].shape for n in WEIGHT_NAMES}

    def squeeze(d):
        return {n: (a if n == "rel_bias_table" else a[0]) for n, a in d.items()}

    w, m, v = squeeze(ws), squeeze(ms), squeeze(vs)
    for d in (w, m, v):
        for n in WEIGHT_NAMES:
            if d[n].ndim == 1:
                d[n] = d[n][None]
    lsum, grad_x, res = _step(x[0], mem[0], loss_target[0], w, m, v)
    loss = lax.psum(0.5 / D_MODEL * lsum, ("x", "y", "c"))
    outs = [loss, grad_x[None]]
    for q in range(4):
        outs += [res[n][q].reshape(full_shapes[n]) for n in WEIGHT_NAMES]
    return tuple(outs)
```

```python
import functools
import math

import numpy as np
import jax
import jax.numpy as jnp
from jax import lax
from jax.experimental import pallas as pl
from jax.experimental.pallas import tpu as pltpu

F32 = jnp.float32
MXU = jnp.bfloat16
S = jax.ShapeDtypeStruct

D_MODEL = 1024
HEAD_DIM = 128
ATTN_GROUPS = ((128, 1), (512, 4), (2048, 16))
N_GROUPS = 3
HEADS = 4
GW = HEADS * HEAD_DIM
CONV_WIDTH = 31
N_MEM = 256
D_FF = 2816
FFN_CONV_WIDTH = 3
N_BUCKETS = 32
MAX_DISTANCE = 2048
RMS_EPS = 1e-6
LN_EPS = 1e-5
N_IN = 9216
NCB = N_IN // GW
BLK = 128
SCALE = HEAD_DIM ** -0.5
NEG = -1e30
N_DEV = 8

ADAM_LR, ADAM_B1, ADAM_B2, ADAM_EPS, ADAM_WD, ADAM_STEP = 0.001, 0.9, 0.999, 1e-08, 0.01, 10

VMEM_LIMIT = 48 * 1024 * 1024
CONV_HALO = 32
FFN_HALO = 8
ADAMW_BLOCK_BYTES = 4 * 1024 * 1024


def _cparams(*sem):
    return pltpu.CompilerParams(dimension_semantics=sem or None, vmem_limit_bytes=VMEM_LIMIT)


def _bs(shape, imap):
    return pl.BlockSpec(shape, imap)


def _dot(a, b):
    return lax.dot_general(a.astype(MXU), b.astype(MXU), (((1,), (0,)), ((), ())), preferred_element_type=F32)


def _dot_nt(a, b):
    return lax.dot_general(a.astype(MXU), b.astype(MXU), (((1,), (1,)), ((), ())), preferred_element_type=F32)


def _dot_tn(a, b):
    return lax.dot_general(a.astype(MXU), b.astype(MXU), (((0,), (0,)), ((), ())), preferred_element_type=F32)


def _sigmoid(x):
    return 1.0 / (1.0 + jnp.exp(-x))


def _rmsn(x, w):
    r = lax.rsqrt(jnp.mean(x * x, axis=-1, keepdims=True) + RMS_EPS)
    return x * r * w, r


def _rmsn_bwd(x, r, w, dy):
    g = dy * w
    dx = r * g - x * (r * r * r) * jnp.mean(x * g, axis=-1, keepdims=True)
    dw = jnp.sum(dy * x * r, axis=0, keepdims=True)
    return dx, dw


def _acc_out(ref, val, first):
    @pl.when(first)
    def _():
        ref[...] = val

    @pl.when(jnp.logical_not(first))
    def _():
        ref[...] += val


def _rms_fwd(x, w, name):
    T, D = x.shape
    tm = min(512, T)

    def body(x_ref, w_ref, o_ref):
        y, _ = _rmsn(x_ref[...], w_ref[...])
        o_ref[...] = y.astype(o_ref.dtype)

    return pl.pallas_call(
        body, grid=(T // tm,),
        in_specs=[_bs((tm, D), lambda i: (i, 0)), _bs((1, D), lambda i: (0, 0))],
        out_specs=_bs((tm, D), lambda i: (i, 0)),
        out_shape=S((T, D), MXU), name=name, compiler_params=_cparams("parallel"))(x, w)


def _rms_bwd(x, w, dys, resid, name):
    T, D = x.shape
    tm = min(512, T)
    n = len(dys)

    def body(*refs):
        x_ref, w_ref, res_ref = refs[0], refs[1], refs[2 + n]
        dx_ref, dw_ref = refs[3 + n], refs[4 + n]
        xv = x_ref[...]
        dy = refs[2][...]
        for dy_ref in refs[3:2 + n]:
            dy = dy + dy_ref[...]
        _, r = _rmsn(xv, w_ref[...])
        dx, dw = _rmsn_bwd(xv, r, w_ref[...], dy)
        dx_ref[...] = res_ref[...] + dx
        _acc_out(dw_ref, dw, pl.program_id(0) == 0)

    row = _bs((tm, D), lambda i: (i, 0))
    vec = _bs((1, D), lambda i: (0, 0))
    return pl.pallas_call(
        body, grid=(T // tm,), in_specs=[row, vec] + [row] * (n + 1), out_specs=[row, vec],
        out_shape=[S((T, D), F32), S((1, D), F32)], name=name, compiler_params=_cparams("arbitrary"))(x, w, *dys, resid)


def _pick(n, cands):
    for c in cands:
        if n % c == 0:
            return c
    return n


def _matmul(a, b, *, ta=False, tb=False, out_dtype=F32, residual=None, after=None, bcols=None, tm=None, tn=None,
            tk=None, name):
    M, K = (a.shape[1], a.shape[0]) if ta else a.shape
    N = b.shape[0] if tb else b.shape[1]
    boff, bstride = 0, 1
    if bcols is not None:
        boff, bstride, count = bcols
        if tb:
            K, tk = count * GW, GW
        else:
            N, tn = count * GW, GW
    tm = tm or _pick(M, (1024, 1408, 512, 256, 128))
    tn = tn or _pick(N, (512, 256, 128))
    tk = tk or _pick(K, (1024, 1408, 512, 256, 128))
    nk = K // tk
    dn = (((0 if ta else 1,), (1 if tb else 0,)), ((), ()))
    has_res = residual is not None
    n_in = 2 + has_res + (after is not None)

    def body(*refs):
        a_ref, b_ref = refs[0], refs[1]
        res_ref = refs[2] if has_res else None
        o_ref = refs[n_in]
        p = lax.dot_general(a_ref[...].astype(MXU), b_ref[...].astype(MXU), dn, preferred_element_type=F32)

        def finish(acc):
            if has_res:
                acc = acc + res_ref[...]
            o_ref[...] = acc.astype(o_ref.dtype)

        if nk == 1:
            finish(p)
        else:
            acc_ref = refs[-1]
            k = pl.program_id(2)

            @pl.when(k == 0)
            def _():
                acc_ref[...] = p

            @pl.when(k > 0)
            def _():
                acc_ref[...] += p

            @pl.when(k == nk - 1)
            def _():
                finish(acc_ref[...])

    a_spec = _bs((tk, tm), lambda i, j, k: (k, i)) if ta else _bs((tm, tk), lambda i, j, k: (i, k))
    if tb:
        b_spec = _bs((tn, tk), lambda i, j, k: (j, boff + bstride * k))
    else:
        b_spec = _bs((tk, tn), lambda i, j, k: (k, boff + bstride * j))
    o_spec = _bs((tm, tn), lambda i, j, k: (i, j))
    in_specs = [a_spec, b_spec] + ([o_spec] if has_res else [])
    args = (a, b) + ((residual,) if has_res else ())
    if after is not None:
        in_specs.append(_bs((8, 128), lambda i, j, k: (0, 0)))
        args += (after,)
    return pl.pallas_call(
        body, grid=(M // tm, N // tn, nk), in_specs=in_specs, out_specs=o_spec,
        out_shape=S((M, N), out_dtype), scratch_shapes=[pltpu.VMEM((tm, tn), F32)] if nk > 1 else [],
        name=name, compiler_params=_cparams("parallel", "parallel", "arbitrary"))(*args)


def _bucket_matrix(dilation):
    n = BLK
    qi = np.arange(n)[:, None]
    kj = np.arange(2 * n)[None, :]
    step = qi + n - kj
    dist = np.clip(step, 0, None) * dilation
    max_exact = N_BUCKETS // 2
    d = np.maximum(dist.astype(np.float32), np.float32(1.0))
    large = max_exact + (np.log(d / np.float32(max_exact)) / np.float32(math.log(MAX_DISTANCE / max_exact))
                         * np.float32(N_BUCKETS - max_exact)).astype(np.int32)
    large = np.minimum(large, N_BUCKETS - 1)
    bucket = np.where(dist < max_exact, dist, large)
    band = (step >= 0) & (step <= n)
    return np.where(band, bucket, -1).astype(np.int32)


def _build_bias(tbl_ref, bkt_ref, bias_ref, g):
    bk = bkt_ref[...]
    for h in range(HEADS):
        acc = jnp.full(bk.shape, NEG, F32)
        for b in range(N_BUCKETS):
            acc = jnp.where(bk == b, tbl_ref[b, HEADS * g + h], acc)
        bias_ref[h] = acc


def _to_classes(a, dil):
    if dil == 1:
        return a
    t, n = a.shape
    return a.reshape(t // dil, dil, n).transpose(1, 0, 2).reshape(t, n)


def _from_classes(a, dil):
    if dil == 1:
        return a
    t, n = a.shape
    return a.reshape(dil, t // dil, n).transpose(1, 0, 2).reshape(t, n)


def _attn_fwd(zq, table, qw, kw, g, dil):
    T = zq.shape[0]
    nb = T // dil // BLK
    bkt = jnp.asarray(_bucket_matrix(dil))

    def zspec(part, prev):
        if prev:
            return _bs((BLK, GW), lambda c, i: (c * nb + jnp.maximum(i - 1, 0), part))
        return _bs((BLK, GW), lambda c, i: (c * nb + i, part))

    def body(tbl_ref, bkt_ref, qw_ref, kw_ref, q_ref, kp_ref, kc_ref, vp_ref, vc_ref, o_ref, lse_ref, bias_ref):
        c, i = pl.program_id(0), pl.program_id(1)

        @pl.when((c == 0) & (i == 0))
        def _():
            _build_bias(tbl_ref, bkt_ref, bias_ref, g)

        kj = lax.broadcasted_iota(jnp.int32, (BLK, 2 * BLK), 1)
        no_prev = jnp.logical_and(i == 0, kj < BLK)
        for h in range(HEADS):
            sl = slice(h * HEAD_DIM, (h + 1) * HEAD_DIM)
            qn, _ = _rmsn(q_ref[:, sl], qw_ref[...])
            kn, _ = _rmsn(jnp.concatenate([kp_ref[:, sl], kc_ref[:, sl]], axis=0), kw_ref[...])
            s = _dot_nt(qn, kn) * SCALE + bias_ref[h]
            s = jnp.where(no_prev, NEG, s)
            m = jnp.max(s, axis=-1, keepdims=True)
            p = jnp.exp(s - m)
            l = jnp.sum(p, axis=-1, keepdims=True)
            v = jnp.concatenate([vp_ref[:, sl], vc_ref[:, sl]], axis=0)
            o_ref[:, sl] = _dot(p, v) / l
            lse_ref[:, sl] = jnp.broadcast_to(m + jnp.log(l), (BLK, HEAD_DIM))

    ospec = _bs((BLK, GW), lambda c, i: (c * nb + i, 0))
    vec = _bs((1, HEAD_DIM), lambda c, i: (0, 0))
    return pl.pallas_call(
        body, grid=(dil, nb),
        in_specs=[pl.BlockSpec(memory_space=pltpu.SMEM), _bs((BLK, 2 * BLK), lambda c, i: (0, 0)), vec, vec,
                  zspec(0, False), zspec(1, True), zspec(1, False), zspec(2, True), zspec(2, False)],
        out_specs=[ospec, ospec],
        out_shape=[S((T, GW), F32), S((T, GW), F32)],
        scratch_shapes=[pltpu.VMEM((HEADS, BLK, 2 * BLK), F32)],
        name=f"attn_fwd_g{g}", compiler_params=_cparams("arbitrary", "arbitrary"))(table, bkt, qw, kw, zq, zq, zq, zq, zq)


def _attn_merge(os_, lses):
    T = os_[0].shape[0]
    tm = min(512, T)

    def body(o0, o1, o2, l0, l1, l2, a_ref, lse_ref):
        ls = [l0[...], l1[...], l2[...]]
        mx = jnp.maximum(jnp.maximum(ls[0], ls[1]), ls[2])
        tot = mx + jnp.log(jnp.exp(ls[0] - mx) + jnp.exp(ls[1] - mx) + jnp.exp(ls[2] - mx))
        a_ref[...] = (jnp.exp(ls[0] - tot) * o0[...] + jnp.exp(ls[1] - tot) * o1[...] + jnp.exp(ls[2] - tot) * o2[...])
        lse_ref[...] = tot

    row = _bs((tm, GW), lambda i: (i, 0))
    return pl.pallas_call(
        body, grid=(T // tm,), in_specs=[row] * 6, out_specs=[row, row],
        out_shape=[S((T, GW), F32), S((T, GW), F32)], name="attn_merge",
        compiler_params=_cparams("parallel"))(*os_, *lses)


def _attn_bwd(zq, table, qw, kw, d_attn, attn, lse, g, dil):
    T = zq.shape[0]
    nb = T // dil // BLK
    bkt = jnp.asarray(_bucket_matrix(dil))

    def zspec(part, prev):
        if prev:
            return _bs((BLK, GW), lambda c, i: (c * nb + jnp.clip(i - 1, 0, nb - 1), part))
        return _bs((BLK, GW), lambda c, i: (c * nb + jnp.minimum(i, nb - 1), part))

    def body(tbl_ref, bkt_ref, qw_ref, kw_ref, q_ref, kp_ref, kc_ref, vp_ref, vc_ref, da_ref, at_ref, lse_ref,
             dz_ref, dqw_ref, dkw_ref, dtab_ref, bias_ref, dbias_ref, cq_ref, ck_ref, cv_ref):
        c, i = pl.program_id(0), pl.program_id(1)

        @pl.when((c == 0) & (i == 0))
        def _():
            _build_bias(tbl_ref, bkt_ref, bias_ref, g)
            dbias_ref[...] = jnp.zeros_like(dbias_ref)
            dqw_ref[...] = jnp.zeros_like(dqw_ref)
            dkw_ref[...] = jnp.zeros_like(dkw_ref)

        @pl.when(i == 0)
        def _():
            cq_ref[...] = jnp.zeros_like(cq_ref)
            ck_ref[...] = jnp.zeros_like(ck_ref)
            cv_ref[...] = jnp.zeros_like(cv_ref)

        @pl.when(i < nb)
        def _():
            kj = lax.broadcasted_iota(jnp.int32, (BLK, 2 * BLK), 1)
            no_prev = jnp.logical_and(i == 0, kj < BLK)
            dqw_acc = jnp.zeros((1, HEAD_DIM), F32)
            dkw_acc = jnp.zeros((1, HEAD_DIM), F32)
            for h in range(HEADS):
                sl = slice(h * HEAD_DIM, (h + 1) * HEAD_DIM)
                qh = q_ref[:, sl]
                k = jnp.concatenate([kp_ref[:, sl], kc_ref[:, sl]], axis=0)
                qn, rq = _rmsn(qh, qw_ref[...])
                kn, rk = _rmsn(k, kw_ref[...])
                s = _dot_nt(qn, kn) * SCALE + bias_ref[h]
                s = jnp.where(no_prev, NEG, s)
                p = jnp.exp(s - lse_ref[:, h * HEAD_DIM:h * HEAD_DIM + 1])
                do = da_ref[:, sl]
                delta = jnp.sum(do * at_ref[:, sl], axis=-1, keepdims=True)
                v = jnp.concatenate([vp_ref[:, sl], vc_ref[:, sl]], axis=0)
                ds = p * (_dot_nt(do, v) - delta)
                dbias_ref[h] += ds
                dv = _dot_tn(p, do)
                dqn = _dot(ds, kn) * SCALE
                dkn = _dot_tn(ds, qn) * SCALE
                dq, dqw = _rmsn_bwd(qh, rq, qw_ref[...], dqn)
                dk, dkw = _rmsn_bwd(k, rk, kw_ref[...], dkn)
                dqw_acc += dqw
                dkw_acc += dkw
                lo = h * HEAD_DIM
                dz_ref[:, lo:lo + HEAD_DIM] = cq_ref[:, sl].astype(dz_ref.dtype)
                dz_ref[:, GW + lo:GW + lo + HEAD_DIM] = (ck_ref[:, sl] + dk[:BLK]).astype(dz_ref.dtype)
                dz_ref[:, 2 * GW + lo:2 * GW + lo + HEAD_DIM] = (cv_ref[:, sl] + dv[:BLK]).astype(dz_ref.dtype)
                cq_ref[:, sl] = dq
                ck_ref[:, sl] = dk[BLK:]
                cv_ref[:, sl] = dv[BLK:]
            dqw_ref[...] += dqw_acc
            dkw_ref[...] += dkw_acc

        @pl.when(i == nb)
        def _():
            dz_ref[:, 0:GW] = cq_ref[...].astype(dz_ref.dtype)
            dz_ref[:, GW:2 * GW] = ck_ref[...].astype(dz_ref.dtype)
            dz_ref[:, 2 * GW:3 * GW] = cv_ref[...].astype(dz_ref.dtype)

        @pl.when((c == dil - 1) & (i == nb))
        def _():
            bk = bkt_ref[...]
            rows = lax.broadcasted_iota(jnp.int32, (N_BUCKETS, HEAD_DIM), 0)
            lanes = lax.broadcasted_iota(jnp.int32, (N_BUCKETS, HEAD_DIM), 1)
            out = jnp.zeros((N_BUCKETS, HEAD_DIM), F32)
            for h in range(HEADS):
                acc = dbias_ref[h]
                for b in range(N_BUCKETS):
                    val = jnp.sum(jnp.where(bk == b, acc, 0.0))
                    out = jnp.where((rows == b) & (lanes == h), val, out)
            dtab_ref[...] = out

    cur = _bs((BLK, GW), lambda c, i: (c * nb + jnp.minimum(i, nb - 1), 0))
    prv = _bs((BLK, 3 * GW), lambda c, i: (c * nb + jnp.maximum(i - 1, 0), 0))
    vec = _bs((1, HEAD_DIM), lambda c, i: (0, 0))
    tabs = _bs((N_BUCKETS, HEAD_DIM), lambda c, i: (0, 0))
    dzq, dqw, dkw, dtab = pl.pallas_call(
        body, grid=(dil, nb + 1),
        in_specs=[pl.BlockSpec(memory_space=pltpu.SMEM), _bs((BLK, 2 * BLK), lambda c, i: (0, 0)), vec, vec,
                  zspec(0, False), zspec(1, True), zspec(1, False), zspec(2, True), zspec(2, False), cur, cur, cur],
        out_specs=[prv, vec, vec, tabs],
        out_shape=[S((T, 3 * GW), MXU)] + [S((1, HEAD_DIM), F32)] * 2 + [S((N_BUCKETS, HEAD_DIM), F32)],
        scratch_shapes=[pltpu.VMEM((HEADS, BLK, 2 * BLK), F32), pltpu.VMEM((HEADS, BLK, 2 * BLK), F32),
                        pltpu.VMEM((BLK, GW), F32), pltpu.VMEM((BLK, GW), F32), pltpu.VMEM((BLK, GW), F32)],
        name=f"attn_bwd_g{g}", compiler_params=_cparams("arbitrary", "arbitrary"))(
            table, bkt, qw, kw, zq, zq, zq, zq, zq, d_attn, attn, lse)
    return dzq, dqw, dkw, dtab[:, :HEADS]


def _mem_fwd(mem, mem_norm_w, w_mem_kv, xk_w):
    def body(mem_ref, nw_ref, w_ref, xk_ref, mk_ref, mv_ref):
        mn, _ = _rmsn(mem_ref[...], nw_ref[...])
        kv = _dot(mn, w_ref[...])
        for h in range(HEADS):
            sl = slice(h * HEAD_DIM, (h + 1) * HEAD_DIM)
            kn, _ = _rmsn(kv[:, sl], xk_ref[...])
            mk_ref[:, sl] = kn.astype(mk_ref.dtype)
        mv_ref[...] = kv[:, GW:].astype(mv_ref.dtype)

    return pl.pallas_call(body, out_shape=[S((N_MEM, GW), MXU), S((N_MEM, GW), MXU)], name="mem_fwd",
                          compiler_params=_cparams())(mem, mem_norm_w, w_mem_kv, xk_w)


def _mem_bwd(mem, mem_norm_w, w_mem_kv, xk_w, dmk, dmv):
    def body(mem_ref, nw_ref, w_ref, xk_ref, dmk_ref, dmv_ref, dw_ref, dnw_ref, dxk_ref):
        memv = mem_ref[...]
        mn, r = _rmsn(memv, nw_ref[...])
        kv = _dot(mn, w_ref[...])
        dxk = jnp.zeros((1, HEAD_DIM), F32)
        parts = []
        for h in range(HEADS):
            sl = slice(h * HEAD_DIM, (h + 1) * HEAD_DIM)
            kh = kv[:, sl]
            _, rk = _rmsn(kh, xk_ref[...])
            dk, dw = _rmsn_bwd(kh, rk, xk_ref[...], dmk_ref[:, sl])
            dxk += dw
            parts.append(dk)
        dkv = jnp.concatenate(parts + [dmv_ref[...]], axis=1)
        dw_ref[...] = _dot_tn(mn, dkv)
        dmn = _dot_nt(dkv, w_ref[...])
        dnw_ref[...] = jnp.sum(dmn * memv * r, axis=0, keepdims=True)
        dxk_ref[...] = dxk

    return pl.pallas_call(
        body, out_shape=[S((D_MODEL, 2 * GW), F32), S((1, D_MODEL), F32), S((1, HEAD_DIM), F32)], name="mem_bwd",
        compiler_params=_cparams())(mem, mem_norm_w, w_mem_kv, xk_w, dmk, dmv)


def _cross_fwd(z, mk, mv, xq_w):
    T = z.shape[0]
    tm = min(512, T)

    def body(q_ref, mk_ref, mv_ref, w_ref, o_ref):
        for h in range(HEADS):
            sl = slice(h * HEAD_DIM, (h + 1) * HEAD_DIM)
            qn, _ = _rmsn(q_ref[:, sl], w_ref[...])
            s = _dot_nt(qn, mk_ref[:, sl]) * SCALE
            e = jnp.exp(s - jnp.max(s, axis=-1, keepdims=True))
            p = e / jnp.sum(e, axis=-1, keepdims=True)
            o_ref[:, sl] = _dot(p, mv_ref[:, sl]).astype(o_ref.dtype)

    full = _bs((N_MEM, GW), lambda i: (0, 0))
    return pl.pallas_call(
        body, grid=(T // tm,),
        in_specs=[_bs((tm, GW), lambda i: (i, 2)), full, full, _bs((1, HEAD_DIM), lambda i: (0, 0))],
        out_specs=_bs((tm, GW), lambda i: (i, 0)), out_shape=S((T, GW), MXU), name="cross_fwd",
        compiler_params=_cparams("parallel"))(z, mk, mv, xq_w)


def _cross_bwd(z, mk, mv, xq_w, d_cross):
    T = z.shape[0]
    tm = min(512, T)

    def body(q_ref, mk_ref, mv_ref, w_ref, do_ref, dq_ref, dmk_ref, dmv_ref, dw_ref):
        first = pl.program_id(0) == 0
        dw_acc = jnp.zeros((1, HEAD_DIM), F32)
        dmk_parts, dmv_parts = [], []
        for h in range(HEADS):
            sl = slice(h * HEAD_DIM, (h + 1) * HEAD_DIM)
            qh = q_ref[:, sl]
            qn, r = _rmsn(qh, w_ref[...])
            s = _dot_nt(qn, mk_ref[:, sl]) * SCALE
            e = jnp.exp(s - jnp.max(s, axis=-1, keepdims=True))
            p = e / jnp.sum(e, axis=-1, keepdims=True)
            do = do_ref[:, sl]
            dp = _dot_nt(do, mv_ref[:, sl])
            ds = p * (dp - jnp.sum(dp * p, axis=-1, keepdims=True)) * SCALE
            dmv_parts.append(_dot_tn(p, do))
            dmk_parts.append(_dot_tn(ds, qn))
            dq, dw = _rmsn_bwd(qh, r, w_ref[...], _dot(ds, mk_ref[:, sl]))
            dw_acc += dw
            dq_ref[:, sl] = dq.astype(dq_ref.dtype)
        _acc_out(dmk_ref, jnp.concatenate(dmk_parts, axis=1), first)
        _acc_out(dmv_ref, jnp.concatenate(dmv_parts, axis=1), first)
        _acc_out(dw_ref, dw_acc, first)

    full = _bs((N_MEM, GW), lambda i: (0, 0))
    vec = _bs((1, HEAD_DIM), lambda i: (0, 0))
    row = _bs((tm, GW), lambda i: (i, 0))
    return pl.pallas_call(
        body, grid=(T // tm,),
        in_specs=[_bs((tm, GW), lambda i: (i, 2)), full, full, vec, row],
        out_specs=[row, full, full, vec],
        out_shape=[S((T, GW), MXU), S((N_MEM, GW), F32), S((N_MEM, GW), F32), S((1, HEAD_DIM), F32)],
        name="cross_bwd", compiler_params=_cparams("arbitrary"))(z, mk, mv, xq_w, d_cross)


def _taps(ext, w_ref, width, base, rows):
    acc = None
    for k in range(width):
        s = base - (width - 1) + k
        term = ext[s:s + rows, :] * w_ref[k:k + 1, :]
        acc = term if acc is None else acc + term
    return acc


def _taps_t(ext, w_ref, width, base, rows):
    acc = None
    for k in range(width):
        s = base + (width - 1) - k
        term = ext[s:s + rows, :] * w_ref[k:k + 1, :]
        acc = term if acc is None else acc + term
    return acc


def _conv_fwd(z, cw, cb, lw, lb):
    T = z.shape[0]
    tm = min(512, T)
    hb = tm // CONV_HALO

    def body(val_ref, gate_ref, hval_ref, hgate_ref, cw_ref, cb_ref, lw_ref, lb_ref, o_ref):
        i = pl.program_id(0)
        halo = hval_ref[...] * _sigmoid(hgate_ref[...])
        halo = jnp.where(i == 0, 0.0, halo)
        ext = jnp.concatenate([halo, val_ref[...] * _sigmoid(gate_ref[...])], axis=0)
        y = _taps(ext, cw_ref, CONV_WIDTH, CONV_HALO, tm) + cb_ref[...]
        xc = y - jnp.mean(y, axis=-1, keepdims=True)
        a = xc * lax.rsqrt(jnp.mean(xc * xc, axis=-1, keepdims=True) + LN_EPS) * lw_ref[...] + lb_ref[...]
        o_ref[...] = (a * _sigmoid(a)).astype(o_ref.dtype)

    vec = _bs((1, GW), lambda i: (0, 0))
    halo_spec = lambda col: _bs((CONV_HALO, GW), lambda i: (jnp.maximum(i * hb - 1, 0), col))
    return pl.pallas_call(
        body, grid=(T // tm,),
        in_specs=[_bs((tm, GW), lambda i: (i, 0)), _bs((tm, GW), lambda i: (i, 1)), halo_spec(0), halo_spec(1),
                  _bs((CONV_WIDTH, GW), lambda i: (0, 0)), vec, vec, vec],
        out_specs=_bs((tm, GW), lambda i: (i, 0)), out_shape=S((T, GW), MXU), name="conv_fwd",
        compiler_params=_cparams("parallel"))(z, z, z, z, cw, cb, lw, lb)


def _conv_bwd(z, cw, cb, lw, lb, d_u):
    T = z.shape[0]
    tm = min(512, T)
    hb = tm // CONV_HALO
    nt = T // tm
    H = CONV_HALO

    def body(val_ref, gate_ref, pval_ref, pgate_ref, nval_ref, ngate_ref, du_ref, ndu_ref, cw_ref, cb_ref, lw_ref,
             lb_ref, dval_ref, dgate_ref, dcw_ref, dcb_ref, dlw_ref, dlb_ref):
        i = pl.program_id(0)
        first = i == 0
        val = jnp.concatenate([pval_ref[...], val_ref[...], nval_ref[...]], axis=0)
        sg = _sigmoid(jnp.concatenate([pgate_ref[...], gate_ref[...], ngate_ref[...]], axis=0))
        rid = lax.broadcasted_iota(jnp.int32, (tm + 2 * H, 1), 0)
        u0 = jnp.where(jnp.logical_and(first, rid < H), 0.0, val * sg)
        y = _taps(u0, cw_ref, CONV_WIDTH, H, tm + H) + cb_ref[...]
        xc = y - jnp.mean(y, axis=-1, keepdims=True)
        rs = lax.rsqrt(jnp.mean(xc * xc, axis=-1, keepdims=True) + LN_EPS)
        nh = xc * rs
        a = nh * lw_ref[...] + lb_ref[...]
        sa = _sigmoid(a)
        du = jnp.concatenate([du_ref[...], ndu_ref[...]], axis=0)
        rid2 = lax.broadcasted_iota(jnp.int32, (tm + H, 1), 0)
        du = jnp.where(jnp.logical_and(i == nt - 1, rid2 >= tm), 0.0, du)
        da = du * (sa * (1.0 + a * (1.0 - sa)))
        dn = da * lw_ref[...]
        dy = rs * (dn - jnp.mean(dn, axis=-1, keepdims=True) - nh * jnp.mean(dn * nh, axis=-1, keepdims=True))
        du0 = _taps_t(dy, cw_ref, CONV_WIDTH, 0, tm)
        v0, s0 = val[H:H + tm], sg[H:H + tm]
        dval_ref[...] = (du0 * s0).astype(dval_ref.dtype)
        dgate_ref[...] = (du0 * v0 * s0 * (1.0 - s0)).astype(dgate_ref.dtype)
        dy0 = dy[:tm]
        rows = [jnp.sum(dy0 * u0[H - (CONV_WIDTH - 1) + k:H - (CONV_WIDTH - 1) + k + tm], axis=0, keepdims=True)
                for k in range(CONV_WIDTH)]
        _acc_out(dcw_ref, jnp.concatenate(rows, axis=0), first)
        _acc_out(dcb_ref, jnp.sum(dy0, axis=0, keepdims=True), first)
        _acc_out(dlw_ref, jnp.sum(da[:tm] * nh[:tm], axis=0, keepdims=True), first)
        _acc_out(dlb_ref, jnp.sum(da[:tm], axis=0, keepdims=True), first)

    vec = _bs((1, GW), lambda i: (0, 0))
    cwspec = _bs((CONV_WIDTH, GW), lambda i: (0, 0))
    prev = lambda col: _bs((H, GW), lambda i: (jnp.maximum(i * hb - 1, 0), col))
    nxt = lambda col: _bs((H, GW), lambda i: (jnp.minimum((i + 1) * hb, nt * hb - 1), col))
    row = _bs((tm, GW), lambda i: (i, 0))
    return pl.pallas_call(
        body, grid=(nt,),
        in_specs=[_bs((tm, GW), lambda i: (i, 0)), _bs((tm, GW), lambda i: (i, 1)), prev(0), prev(1), nxt(0), nxt(1),
                  row, nxt(0), cwspec, vec, vec, vec],
        out_specs=[row, row, cwspec, vec, vec, vec],
        out_shape=[S((T, GW), MXU), S((T, GW), MXU), S((CONV_WIDTH, GW), F32)] + [S((1, GW), F32)] * 3,
        name="conv_bwd", compiler_params=_cparams("arbitrary"))(z, z, z, z, z, z, d_u, d_u, cw, cb, lw, lb)


def _ffn_act_fwd(up0, fw, fb):
    T = up0.shape[0]
    tm = min(256, T)
    hb = tm // FFN_HALO
    H = FFN_HALO

    def body(a_ref, g_ref, pa_ref, pg_ref, wa_ref, wg_ref, ba_ref, bg_ref, o_ref):
        i = pl.program_id(0)
        keep = jnp.where(i == 0, 0.0, 1.0)
        ea = jnp.concatenate([pa_ref[...] * keep, a_ref[...]], axis=0)
        eg = jnp.concatenate([pg_ref[...] * keep, g_ref[...]], axis=0)
        av = _taps(ea, wa_ref, FFN_CONV_WIDTH, H, tm) + ba_ref[...]
        gv = _taps(eg, wg_ref, FFN_CONV_WIDTH, H, tm) + bg_ref[...]
        o_ref[...] = (gv * _sigmoid(gv) * av).astype(o_ref.dtype)

    col = lambda j: _bs((tm, D_FF), lambda i: (i, j))
    prev = lambda j: _bs((H, D_FF), lambda i: (jnp.maximum(i * hb - 1, 0), j))
    wspec = lambda j: _bs((FFN_CONV_WIDTH, D_FF), lambda i: (0, j))
    bspec = lambda j: _bs((1, D_FF), lambda i: (0, j))
    return pl.pallas_call(
        body, grid=(T // tm,),
        in_specs=[col(0), col(1), prev(0), prev(1), wspec(0), wspec(1), bspec(0), bspec(1)],
        out_specs=_bs((tm, D_FF), lambda i: (i, 0)), out_shape=S((T, D_FF), MXU), name="ffn_act_fwd",
        compiler_params=_cparams("parallel"))(up0, up0, up0, up0, fw, fw, fb, fb)


def _ffn_act_bwd(up0, fw, fb, d_f):
    T = up0.shape[0]
    tm = min(256, T)
    hb = tm // FFN_HALO
    nt = T // tm
    H = FFN_HALO
    W = FFN_CONV_WIDTH

    def body(a_ref, g_ref, pa_ref, pg_ref, na_ref, ng_ref, df_ref, ndf_ref, wa_ref, wg_ref, ba_ref, bg_ref,
             dup_ref, dw_ref, db_ref):
        i = pl.program_id(0)
        first = i == 0
        keep = jnp.where(first, 0.0, 1.0)
        ea = jnp.concatenate([pa_ref[...] * keep, a_ref[...], na_ref[...]], axis=0)
        eg = jnp.concatenate([pg_ref[...] * keep, g_ref[...], ng_ref[...]], axis=0)
        av = _taps(ea, wa_ref, W, H, tm + H) + ba_ref[...]
        gv = _taps(eg, wg_ref, W, H, tm + H) + bg_ref[...]
        df = jnp.concatenate([df_ref[...], ndf_ref[...]], axis=0)
        rid = lax.broadcasted_iota(jnp.int32, (tm + H, 1), 0)
        df = jnp.where(jnp.logical_and(i == nt - 1, rid >= tm), 0.0, df)
        sg = _sigmoid(gv)
        d_av = df * gv * sg
        d_gv = df * av * (sg * (1.0 + gv * (1.0 - sg)))
        dup_ref[:, :D_FF] = _taps_t(d_av, wa_ref, W, 0, tm).astype(dup_ref.dtype)
        dup_ref[:, D_FF:] = _taps_t(d_gv, wg_ref, W, 0, tm).astype(dup_ref.dtype)
        dwa = [jnp.sum(d_av[:tm] * ea[H - (W - 1) + k:H - (W - 1) + k + tm], axis=0, keepdims=True) for k in range(W)]
        dwg = [jnp.sum(d_gv[:tm] * eg[H - (W - 1) + k:H - (W - 1) + k + tm], axis=0, keepdims=True) for k in range(W)]
        dw = jnp.concatenate([jnp.concatenate(dwa, axis=0), jnp.concatenate(dwg, axis=0)], axis=1)
        db = jnp.concatenate([jnp.sum(d_av[:tm], axis=0, keepdims=True), jnp.sum(d_gv[:tm], axis=0, keepdims=True)], axis=1)
        _acc_out(dw_ref, dw, first)
        _acc_out(db_ref, db, first)

    col = lambda j: _bs((tm, D_FF), lambda i: (i, j))
    prev = lambda j: _bs((H, D_FF), lambda i: (jnp.maximum(i * hb - 1, 0), j))
    nxt = lambda j: _bs((H, D_FF), lambda i: (jnp.minimum((i + 1) * hb, nt * hb - 1), j))
    wspec = lambda j: _bs((W, D_FF), lambda i: (0, j))
    bspec = lambda j: _bs((1, D_FF), lambda i: (0, j))
    return pl.pallas_call(
        body, grid=(nt,),
        in_specs=[col(0), col(1), prev(0), prev(1), nxt(0), nxt(1), col(0), nxt(0), wspec(0), wspec(1), bspec(0), bspec(1)],
        out_specs=[_bs((tm, 2 * D_FF), lambda i: (i, 0)), _bs((W, 2 * D_FF), lambda i: (0, 0)),
                   _bs((1, 2 * D_FF), lambda i: (0, 0))],
        out_shape=[S((T, 2 * D_FF), MXU), S((W, 2 * D_FF), F32), S((1, 2 * D_FF), F32)],
        name="ffn_act_bwd", compiler_params=_cparams("arbitrary"))(
            up0, up0, up0, up0, up0, up0, d_f, d_f, fw, fw, fb, fb)


def _branch_fwd(attn, u, cross, z, b_gate, wa, wc, wx):
    T = z.shape[0]
    tm = min(512, T)

    def body(a_ref, u_ref, x_ref, g0_ref, g1_ref, g2_ref, b_ref, wa_ref, wc_ref, wx_ref, o_ref):
        acc = None
        for j, (act, g_ref, w_ref) in enumerate(((a_ref, g0_ref, wa_ref), (u_ref, g1_ref, wc_ref), (x_ref, g2_ref, wx_ref))):
            gate = _sigmoid(g_ref[...] + b_ref[:, j * D_MODEL:(j + 1) * D_MODEL])
            term = gate * _dot(act[...], w_ref[...])
            acc = term if acc is None else acc + term
        o_ref[...] = acc.astype(o_ref.dtype)

    act = _bs((tm, GW), lambda i: (i, 0))
    gcol = lambda j: _bs((tm, D_MODEL), lambda i: (i, j))
    wfull = _bs((GW, D_MODEL), lambda i: (0, 0))
    return pl.pallas_call(
        body, grid=(T // tm,),
        in_specs=[act, act, act, gcol(0), gcol(1), gcol(2), _bs((1, 3 * D_MODEL), lambda i: (0, 0)), wfull, wfull, wfull],
        out_specs=_bs((tm, D_MODEL), lambda i: (i, 0)), out_shape=S((T, D_MODEL), MXU), name="branch_fwd",
        compiler_params=_cparams("parallel"))(attn, u, cross, z, z, z, b_gate, wa, wc, wx)


def _branch_bwd(d_merged, attn, u, cross, z, b_gate, wa, wc, wx):
    T = z.shape[0]
    tm = min(512, T)

    def body(dm_ref, a_ref, u_ref, x_ref, g0_ref, g1_ref, g2_ref, b_ref, wa_ref, wc_ref, wx_ref,
             dzg_ref, da_ref, du_ref, dx_ref, dwa_ref, dwc_ref, dwx_ref, db_ref):
        first = pl.program_id(0) == 0
        dm = dm_ref[...]
        dbs = []
        for j, (act, g_ref, w_ref, dact_ref, dw_ref) in enumerate((
                (a_ref, g0_ref, wa_ref, da_ref, dwa_ref), (u_ref, g1_ref, wc_ref, du_ref, dwc_ref),
                (x_ref, g2_ref, wx_ref, dx_ref, dwx_ref))):
            av = act[...]
            gate = _sigmoid(g_ref[...] + b_ref[:, j * D_MODEL:(j + 1) * D_MODEL])
            y = _dot(av, w_ref[...])
            dzg = dm * y * gate * (1.0 - gate)
            dzg_ref[:, j * D_MODEL:(j + 1) * D_MODEL] = dzg.astype(dzg_ref.dtype)
            dbs.append(jnp.sum(dzg, axis=0, keepdims=True))
            dy = (gate * dm).astype(MXU)
            dact_ref[...] = _dot_nt(dy, w_ref[...])
            _acc_out(dw_ref, _dot_tn(av, dy), first)
        _acc_out(db_ref, jnp.concatenate(dbs, axis=1), first)

    act = _bs((tm, GW), lambda i: (i, 0))
    gcol = lambda j: _bs((tm, D_MODEL), lambda i: (i, j))
    wfull = _bs((GW, D_MODEL), lambda i: (0, 0))
    bvec = _bs((1, 3 * D_MODEL), lambda i: (0, 0))
    return pl.pallas_call(
        body, grid=(T // tm,),
        in_specs=[_bs((tm, D_MODEL), lambda i: (i, 0)), act, act, act, gcol(0), gcol(1), gcol(2), bvec, wfull, wfull, wfull],
        out_specs=[_bs((tm, 3 * D_MODEL), lambda i: (i, 0)), act, act, act, wfull, wfull, wfull, bvec],
        out_shape=[S((T, 3 * D_MODEL), MXU)] + [S((T, GW), F32)] * 3 + [S((GW, D_MODEL), F32)] * 3 + [S((1, 3 * D_MODEL), F32)],
        name="branch_bwd", compiler_params=_cparams("arbitrary"))(d_merged, attn, u, cross, z, z, z, b_gate, wa, wc, wx)


def _loss_head(y, target):
    T, D = y.shape
    tm = min(512, T)

    def body(y_ref, t_ref, dy_ref, l_ref):
        e = y_ref[...] - t_ref[...]
        dy_ref[...] = e * (1.0 / D)
        part = jnp.full((8, 128), jnp.sum(e * e), F32)
        _acc_out(l_ref, part, pl.program_id(0) == 0)

    row = _bs((tm, D), lambda i: (i, 0))
    return pl.pallas_call(
        body, grid=(T // tm,), in_specs=[row, row], out_specs=[row, _bs((8, 128), lambda i: (0, 0))],
        out_shape=[S((T, D), F32), S((8, 128), F32)], name="loss_head", compiler_params=_cparams("arbitrary"))(y, target)


def _peer(mask):
    x, y, c = lax.axis_index("x"), lax.axis_index("y"), lax.axis_index("c")
    px = 1 - x if mask & 4 else x
    py = 1 - y if mask & 2 else y
    pc = 1 - c if mask & 1 else c
    return (px, py, pc), 4 * px + 2 * py + pc


def _exchange(arrs, scatter, name):
    n = len(arrs)
    outs_shape = [S(a.shape if scatter else (N_DEV,) + a.shape, a.dtype) for a in arrs]

    def body(*refs):
        ins, outs = refs[:n], refs[n:2 * n]
        send_sems, recv_sems, local_sems = refs[2 * n:]
        me = 4 * lax.axis_index("x") + 2 * lax.axis_index("y") + lax.axis_index("c")
        copies = []
        for w in range(n):
            src = ins[w].at[me] if scatter else ins[w]
            cp = pltpu.make_async_copy(src, outs[w].at[me], local_sems.at[w])
            cp.start()
            copies.append(cp)
        for k in range(1, N_DEV):
            peer, pidx = _peer(k)
            for w in range(n):
                src = ins[w].at[pidx] if scatter else ins[w]
                cp = pltpu.make_async_remote_copy(
                    src_ref=src, dst_ref=outs[w].at[me], send_sem=send_sems.at[w, k - 1], recv_sem=recv_sems.at[w, k - 1],
                    device_id=peer, device_id_type=pl.DeviceIdType.MESH)
                cp.start()
                copies.append(cp)
        for cp in copies:
            cp.wait()

    hbm = pl.BlockSpec(memory_space=pl.ANY)
    return pl.pallas_call(
        body, in_specs=[hbm] * n, out_specs=[hbm] * n, out_shape=outs_shape,
        scratch_shapes=[pltpu.SemaphoreType.DMA((n, N_DEV - 1)), pltpu.SemaphoreType.DMA((n, N_DEV - 1)),
                        pltpu.SemaphoreType.DMA((n,))],
        name=name)(*arrs)


def _exchange_copies(ins, lands, send_sems, recv_sems, local_sems, scatter):
    n = len(ins)
    me = 4 * lax.axis_index("x") + 2 * lax.axis_index("y") + lax.axis_index("c")
    copies = []
    for w in range(n):
        src = ins[w].at[me] if scatter else ins[w]
        copies.append(pltpu.make_async_copy(src, lands[w].at[me], local_sems.at[w]))
    for k in range(1, N_DEV):
        peer, pidx = _peer(k)
        for w in range(n):
            src = ins[w].at[pidx] if scatter else ins[w]
            copies.append(pltpu.make_async_remote_copy(
                src_ref=src, dst_ref=lands[w].at[me], send_sem=send_sems.at[w * (N_DEV - 1) + k - 1],
                recv_sem=recv_sems.at[w * (N_DEV - 1) + k - 1],
                device_id=peer, device_id_type=pl.DeviceIdType.MESH))
    return copies


_HBM_SPEC = pl.BlockSpec(memory_space=pltpu.HBM)
_SEM_SPEC = pl.BlockSpec(memory_space=pltpu.SEMAPHORE)
_DATAFLOW = pltpu.SideEffectType.DATAFLOW_SIDE_EFFECTING


def _exchange_start(arrs, scatter, name):
    n = len(arrs)
    land_shapes = [a.shape if scatter else (N_DEV,) + a.shape for a in arrs]

    def body(*refs):
        ins, lands = refs[:n], refs[n:2 * n]
        send_sems, recv_sems, local_sems = refs[2 * n:2 * n + 3]
        token = refs[-1]
        for cp in _exchange_copies(ins, lands, send_sems, recv_sems, local_sems, scatter):
            cp.start()
        token[...] = jnp.zeros_like(token)

    out_shape = ([pltpu.SemaphoreType.DMA((n * (N_DEV - 1),)), pltpu.SemaphoreType.DMA((n * (N_DEV - 1),)),
                  pltpu.SemaphoreType.DMA((n,))]
                 + [pltpu.HBM(a.shape, a.dtype) for a in arrs]
                 + [pltpu.HBM(s, a.dtype) for s, a in zip(land_shapes, arrs)]
                 + [S((8, 128), F32)])
    args = ([pltpu.with_memory_space_constraint(a, pltpu.HBM) for a in arrs]
            + [pltpu.with_memory_space_constraint(lax.empty(s, a.dtype), pltpu.HBM) for s, a in zip(land_shapes, arrs)])
    outs = pl.pallas_call(
        body, in_specs=[_HBM_SPEC] * (2 * n),
        out_specs=[_SEM_SPEC] * 3 + [_HBM_SPEC] * (2 * n) + [pl.BlockSpec(memory_space=pltpu.VMEM)],
        out_shape=out_shape, input_output_aliases={j: 3 + j for j in range(2 * n)},
        name=name, compiler_params=pltpu.CompilerParams(has_side_effects=_DATAFLOW))(*args)
    return (n, scatter, outs[:3], outs[3:3 + n], outs[3 + n:3 + 2 * n]), outs[-1]


def _exchange_wait(state, after, name):
    n, scatter, sems, ins, lands = state

    def body(*refs):
        ins_r, lands_r = refs[:n], refs[n:2 * n]
        send_sems, recv_sems, local_sems = refs[2 * n:2 * n + 3]
        for cp in _exchange_copies(ins_r, lands_r, send_sems, recv_sems, local_sems, scatter):
            cp.wait()

    outs = pl.pallas_call(
        body, in_specs=[_HBM_SPEC] * (2 * n) + [_SEM_SPEC] * 3 + [pl.BlockSpec(memory_space=pl.ANY)],
        out_specs=[_HBM_SPEC] * (2 * n),
        out_shape=[pltpu.HBM(a.shape, a.dtype) for a in ins] + [pltpu.HBM(a.shape, a.dtype) for a in lands],
        input_output_aliases={j: j for j in range(2 * n)},
        name=name, compiler_params=pltpu.CompilerParams(has_side_effects=_DATAFLOW))(*ins, *lands, *sems, after)
    return list(outs[n:])


def _adamw(w, m, v, parts, name):
    R, C = w.shape
    P = parts.shape[0]
    tr = _pick(R, tuple(t for t in (256, 176, 128, 64, 32, 16, 8) if P * t * C * 4 <= ADAMW_BLOCK_BYTES))
    c1 = 1.0 / (1.0 - ADAM_B1 ** ADAM_STEP)
    c2 = 1.0 / (1.0 - ADAM_B2 ** ADAM_STEP)

    def body(w_ref, m_ref, v_ref, p_ref, g_ref, d_ref, nm_ref, nv_ref):
        g = p_ref[0].astype(F32)
        for j in range(1, P):
            g = g + p_ref[j].astype(F32)
        m2 = ADAM_B1 * m_ref[...] + (1.0 - ADAM_B1) * g
        v2 = ADAM_B2 * v_ref[...] + (1.0 - ADAM_B2) * (g * g)
        g_ref[...] = g
        nm_ref[...] = m2
        nv_ref[...] = v2
        d_ref[...] = -ADAM_LR * ((m2 * c1) / (jnp.sqrt(v2 * c2) + ADAM_EPS) + ADAM_WD * w_ref[...])

    row = _bs((tr, C), lambda i: (i, 0))
    return pl.pallas_call(
        body, grid=(R // tr,), in_specs=[row, row, row, _bs((P, tr, C), lambda i: (0, i, 0))], out_specs=[row] * 4,
        out_shape=[S((R, C), F32)] * 4, name=name, compiler_params=_cparams("parallel"))(w, m, v, parts)


def _sum_parts(parts, name):
    P, R, C = parts.shape

    def body(p_ref, o_ref):
        g = p_ref[0]
        for j in range(1, P):
            g = g + p_ref[j]
        o_ref[...] = g

    return pl.pallas_call(body, out_shape=S((R, C), F32), name=name, compiler_params=_cparams())(parts)


def _pack(arrs):
    flat = jnp.concatenate([a.reshape(-1) for a in arrs])
    rows = -(-flat.shape[0] // 1024) * 8
    return jnp.pad(flat, (0, rows * 128 - flat.shape[0])).reshape(rows, 128)


def _unpack(packed, shapes):
    flat = packed.reshape(-1)
    out, off = [], 0
    for s in shapes:
        n = int(np.prod(s))
        out.append(flat[off:off + n].reshape(s))
        off += n
    return out


def _behind(a, token):
    return a if token is None else a + token[0, 0]


def _local_step(x, mem, target, p, comm=None):
    table = p["rel_bias_table"]
    xn = _rms_fwd(x, p["attn_norm_w"], "attn_norm_fwd")
    dils = [dil for _, dil in ATTN_GROUPS]
    xn_c = [_to_classes(xn, dil) for dil in dils]
    qkv_cols = [(g, N_GROUPS, 3) for g in range(N_GROUPS)]
    c_cols, g_cols = (9, 1, 3), (12, 1, 6)
    zq = [_matmul(xn_c[g], p["w_in"], bcols=qkv_cols[g], name=f"mm_in_qkv{g}") for g in range(N_GROUPS)]
    zc = _matmul(xn, p["w_in"], bcols=c_cols, name="mm_in_c")
    zg = _matmul(xn, p["w_in"], bcols=g_cols, name="mm_in_g")
    os_, lses = [], []
    for g, dil in enumerate(dils):
        o, l = _attn_fwd(zq[g], table, p["q_norm_w"][g:g + 1], p["k_norm_w"][g:g + 1], g, dil)
        os_.append(_from_classes(o, dil))
        lses.append(_from_classes(l, dil))
    attn, lse = _attn_merge(os_, lses)
    u = _conv_fwd(zc, p["conv_dw_w"], p["conv_dw_b"], p["conv_ln_w"], p["conv_ln_b"])
    if comm is not None:
        p = {**p, **comm.late_weights(after=u)}
    mk, mv = _mem_fwd(mem, p["mem_norm_w"], p["w_mem_kv"], p["xk_norm_w"])
    cross = _cross_fwd(zc, mk, mv, p["xq_norm_w"])
    merged = _branch_fwd(attn, u, cross, zg, p["b_gate"], p["w_attn_o"], p["w_conv_o"], p["w_cross_o"])
    h1 = _matmul(merged, p["w_out"], residual=x, name="mm_out")
    hn = _rms_fwd(h1, p["ffn_norm_w"], "ffn_norm_fwd")
    up0 = _matmul(hn, p["w_up"], name="mm_up")
    f = _ffn_act_fwd(up0, p["ffn_conv_w"], p["ffn_conv_b"])
    h2 = _matmul(f, p["w_down"], residual=h1, name="mm_down")
    dh2, lsum = _loss_head(h2, target)
    g = {}
    d_f = _matmul(dh2, p["w_down"], tb=True, name="mm_down_dx")
    g["w_down"] = _matmul(f, dh2, ta=True, name="mm_down_dw")
    d_up0, g["ffn_conv_w"], g["ffn_conv_b"] = _ffn_act_bwd(up0, p["ffn_conv_w"], p["ffn_conv_b"], d_f)
    dhn = _matmul(d_up0, p["w_up"], tb=True, name="mm_up_dx")
    g["w_up"] = _matmul(hn, d_up0, ta=True, name="mm_up_dw")
    dh1, g["ffn_norm_w"] = _rms_bwd(h1, p["ffn_norm_w"], [dhn], dh2, "ffn_norm_bwd")
    d_merged = _matmul(dh1, p["w_out"], tb=True, name="mm_out_dx")
    g["w_out"] = _matmul(merged, dh1, ta=True, name="mm_out_dw")
    (d_zg, d_attn, d_u, d_cross, g["w_attn_o"], g["w_conv_o"], g["w_cross_o"], g["b_gate"]) = _branch_bwd(
        d_merged, attn, u, cross, zg, p["b_gate"], p["w_attn_o"], p["w_conv_o"], p["w_cross_o"])
    d_xq, dmk, dmv, g["xq_norm_w"] = _cross_bwd(zc, mk, mv, p["xq_norm_w"], d_cross)
    g["w_mem_kv"], g["mem_norm_w"], g["xk_norm_w"] = _mem_bwd(mem, p["mem_norm_w"], p["w_mem_kv"], p["xk_norm_w"], dmk, dmv)
    tok = comm.start_early_grads(g) if comm is not None else None
    d_val, d_gate, g["conv_dw_w"], g["conv_dw_b"], g["conv_ln_w"], g["conv_ln_b"] = _conv_bwd(
        zc, p["conv_dw_w"], _behind(p["conv_dw_b"], tok), p["conv_ln_w"], p["conv_ln_b"], d_u)
    dzq, dqw, dkw, dtab = [], [], [], []
    for gi, dil in enumerate(dils):
        r = _attn_bwd(zq[gi], table, p["q_norm_w"][gi:gi + 1], p["k_norm_w"][gi:gi + 1], _to_classes(d_attn, dil),
                      _to_classes(attn, dil), _to_classes(lse, dil), gi, dil)
        for lst, val in zip((dzq, dqw, dkw, dtab), r):
            lst.append(val)
    g["q_norm_w"] = jnp.concatenate(dqw, axis=0)
    g["k_norm_w"] = jnp.concatenate(dkw, axis=0)
    g["rel_bias_table"] = jnp.concatenate(dtab, axis=1)
    d_zc = jnp.concatenate([d_val, d_gate, d_xq], axis=1)
    gq = [_matmul(xn_c[gi], dzq[gi], ta=True, name=f"mm_in_qkv{gi}_dw") for gi in range(N_GROUPS)]
    gc = _matmul(xn, d_zc, ta=True, name="mm_in_c_dw")
    gg = _matmul(xn, d_zg, ta=True, name="mm_in_g_dw")
    g["w_in"] = jnp.concatenate(
        [gq[gi][:, part * GW:(part + 1) * GW] for part in range(3) for gi in range(N_GROUPS)] + [gc, gg], axis=1)
    tok = comm.start_w_in_grad(g["w_in"]) if comm is not None else None
    dxn = _matmul(d_zg, p["w_in"], tb=True, bcols=g_cols, after=tok, name="mm_in_g_dx")
    dxn = _matmul(d_zc, p["w_in"], tb=True, bcols=c_cols, residual=dxn, name="mm_in_c_dx")
    dxn = _matmul(dzq[0], p["w_in"], tb=True, bcols=qkv_cols[0], residual=dxn, name="mm_in_qkv0_dx")
    dxs = [dxn] + [_from_classes(_matmul(dzq[gi], p["w_in"], tb=True, bcols=qkv_cols[gi], name=f"mm_in_qkv{gi}_dx"),
                                 dils[gi]) for gi in range(1, N_GROUPS)]
    grad_x, g["attn_norm_w"] = _rms_bwd(x, p["attn_norm_w"], dxs, dh1, "attn_norm_bwd")
    return lsum[0, 0], grad_x, g


WEIGHT_NAMES = ["rel_bias_table", "attn_norm_w", "w_in", "b_gate", "q_norm_w", "k_norm_w", "w_attn_o", "conv_dw_w",
                "conv_dw_b", "conv_ln_w", "conv_ln_b", "w_conv_o", "mem_norm_w", "w_mem_kv", "xq_norm_w", "xk_norm_w",
                "w_cross_o", "w_out", "ffn_norm_w", "w_up", "ffn_conv_w", "ffn_conv_b", "w_down"]
COL_SHARDED = ("w_in", "w_attn_o", "w_conv_o", "w_cross_o", "w_up")
ROW_SHARDED = ("w_mem_kv", "w_out", "w_down")
SMALL_COL_SHARDED = ("conv_dw_w", "ffn_conv_w")
BIG = COL_SHARDED + ROW_SHARDED


def _cols_to_blocks(a):
    k, n8 = a.shape
    return a.reshape(k, N_DEV, n8 // N_DEV).transpose(1, 0, 2)


def _blocks_to_cols(a):
    return a.transpose(1, 0, 2).reshape(a.shape[1], N_DEV * a.shape[2])


def _step(x, mem, target, w, m, v):
    me = 4 * lax.axis_index("x") + 2 * lax.axis_index("y") + lax.axis_index("c")

    def to_full(n, blocks):
        return _blocks_to_cols(blocks) if n in COL_SHARDED + SMALL_COL_SHARDED else blocks.reshape(-1, blocks.shape[-1])

    def to_blocks(n, grad):
        blocks = _cols_to_blocks(grad) if n in COL_SHARDED else grad.reshape(N_DEV, -1, grad.shape[-1])
        return blocks.astype(MXU)

    first = ("w_in",) + SMALL_COL_SHARDED
    late = tuple(n for n in BIG if n != "w_in")
    cast = lambda n: w[n].astype(MXU) if n in BIG else w[n]
    first_state, _ = _exchange_start([cast(n) for n in first], False, "gather_first_start")
    late_state, late_token = _exchange_start([cast(n) for n in late], False, "gather_late_start")
    got = _exchange_wait(first_state, late_token, "gather_first_wait")
    p = {n: w[n] for n in WEIGHT_NAMES if n not in BIG + SMALL_COL_SHARDED}
    p.update({n: to_full(n, b) for n, b in zip(first, got)})

    class Comm:
        def late_weights(self, after):
            return {n: to_full(n, b) for n, b in zip(late, _exchange_wait(late_state, after, "gather_late_wait"))}

        def start_early_grads(self, g):
            self.early_state, token = _exchange_start([to_blocks(n, g[n]) for n in late], True, "scatter_early_start")
            return token

        def start_w_in_grad(self, grad):
            self.w_in_state, token = _exchange_start([to_blocks("w_in", grad)], True, "scatter_w_in_start")
            return token

    comm = Comm()
    lsum, grad_x, g = _local_step(x, mem, target, p, comm)
    small_names = [n for n in WEIGHT_NAMES if n not in BIG]
    small_shapes = [g[n].shape for n in small_names]
    small_parts = _exchange([_pack([g[n] for n in small_names])], False, "gather_small_grads")[0]
    gsmall = dict(zip(small_names, _unpack(_sum_parts(small_parts, "sum_small_grads"), small_shapes)))
    for n in SMALL_COL_SHARDED:
        width = w[n].shape[-1]
        gsmall[n] = lax.dynamic_slice_in_dim(gsmall[n], me * width, width, axis=1)
    res = {}
    parts = dict(zip(late, _exchange_wait(comm.early_state, grad_x, "scatter_early_wait")))
    for n in late:
        res[n] = _adamw(w[n], m[n], v[n], parts[n], "adamw_" + n)
    w_in_parts = _exchange_wait(comm.w_in_state, res[late[-1]][1], "scatter_w_in_wait")[0]
    res["w_in"] = _adamw(w["w_in"], m["w_in"], v["w_in"], w_in_parts, "adamw_w_in")
    shapes = [w[n].shape for n in small_names]
    packed = [_pack([d[n] for n in small_names]) for d in (w, m, v, gsmall)]
    outs = _adamw(packed[0], packed[1], packed[2], packed[3][None], "adamw_small")
    unpacked = [_unpack(o, shapes) for o in outs]
    for j, n in enumerate(small_names):
        res[n] = tuple(unpacked[q][j] for q in range(4))
    return lsum, grad_x, res


def kernel(x, mem, rel_bias_table, attn_norm_w, w_in, b_gate, q_norm_w, k_norm_w, w_attn_o, conv_dw_w, conv_dw_b, conv_ln_w, conv_ln_b, w_conv_o, mem_norm_w, w_mem_kv, xq_norm_w, xk_norm_w, w_cross_o, w_out, ffn_norm_w, w_up, ffn_conv_w, ffn_conv_b, w_down, loss_target, m_rel_bias_table, m_attn_norm_w, m_w_in, m_b_gate, m_q_norm_w, m_k_norm_w, m_w_attn_o, m_conv_dw_w, m_conv_dw_b, m_conv_ln_w, m_conv_ln_b, m_w_conv_o, m_mem_norm_w, m_w_mem_kv, m_xq_norm_w, m_xk_norm_w, m_w_cross_o, m_w_out, m_ffn_norm_w, m_w_up, m_ffn_conv_w, m_ffn_conv_b, m_w_down, v_rel_bias_table, v_attn_norm_w, v_w_in, v_b_gate, v_q_norm_w, v_k_norm_w, v_w_attn_o, v_conv_dw_w, v_conv_dw_b, v_conv_ln_w, v_conv_ln_b, v_w_conv_o, v_mem_norm_w, v_w_mem_kv, v_xq_norm_w, v_xk_norm_w, v_w_cross_o, v_w_out, v_ffn_norm_w, v_w_up, v_ffn_conv_w, v_ffn_conv_b, v_w_down):
    ws = dict(zip(WEIGHT_NAMES, (rel_bias_table, attn_norm_w, w_in, b_gate, q_norm_w, k_norm_w, w_attn_o, conv_dw_w, conv_dw_b, conv_ln_w, conv_ln_b, w_conv_o, mem_norm_w, w_mem_kv, xq_norm_w, xk_norm_w, w_cross_o, w_out, ffn_norm_w, w_up, ffn_conv_w, ffn_conv_b, w_down)))
    ms = dict(zip(WEIGHT_NAMES, (m_rel_bias_table, m_attn_norm_w, m_w_in, m_b_gate, m_q_norm_w, m_k_norm_w, m_w_attn_o, m_conv_dw_w, m_conv_dw_b, m_conv_ln_w, m_conv_ln_b, m_w_conv_o, m_mem_norm_w, m_w_mem_kv, m_xq_norm_w, m_xk_norm_w, m_w_cross_o, m_w_out, m_ffn_norm_w, m_w_up, m_ffn_conv_w, m_ffn_conv_b, m_w_down)))
    vs = dict(zip(WEIGHT_NAMES, (v_rel_bias_table, v_attn_norm_w, v_w_in, v_b_gate, v_q_norm_w, v_k_norm_w, v_w_attn_o, v_conv_dw_w, v_conv_dw_b, v_conv_ln_w, v_conv_ln_b, v_w_conv_o, v_mem_norm_w, v_w_mem_kv, v_xq_norm_w, v_xk_norm_w, v_w_cross_o, v_w_out, v_ffn_norm_w, v_w_up, v_ffn_conv_w, v_ffn_conv_b, v_w_down)))
    full_shapes = {n: ws[n].shape for n in WEIGHT_NAMES}

    def squeeze(d):
        return {n: (a if n == "rel_bias_table" else a[0]) for n, a in d.items()}

    w, m, v = squeeze(ws), squeeze(ms), squeeze(vs)
    for d in (w, m, v):
        for n in WEIGHT_NAMES:
            if d[n].ndim == 1:
                d[n] = d[n][None]
    lsum, grad_x, res = _step(x[0], mem[0], loss_target[0], w, m, v)
    loss = lax.psum(0.5 / D_MODEL * lsum, ("x", "y", "c"))
    outs = [loss, grad_x[None]]
    for q in range(4):
        outs += [res[n][q].reshape(full_shapes[n]) for n in WEIGHT_NAMES]
    return tuple(outs)
```

```python
import functools
import math

import numpy as np
import jax
import jax.numpy as jnp
from jax import lax
from jax.experimental import pallas as pl
from jax.experimental.pallas import tpu as pltpu

F32 = jnp.float32
MXU = jnp.bfloat16
S = jax.ShapeDtypeStruct

D_MODEL = 1024
HEAD_DIM = 128
ATTN_GROUPS = ((128, 1), (512, 4), (2048, 16))
N_GROUPS = 3
HEADS = 4
GW = HEADS * HEAD_DIM
CONV_WIDTH = 31
N_MEM = 256
D_FF = 2816
FFN_CONV_WIDTH = 3
N_BUCKETS = 32
MAX_DISTANCE = 2048
RMS_EPS = 1e-6
LN_EPS = 1e-5
N_IN = 9216
NCB = N_IN // GW
BLK = 128
SCALE = HEAD_DIM ** -0.5
NEG = -1e30
N_DEV = 8

ADAM_LR, ADAM_B1, ADAM_B2, ADAM_EPS, ADAM_WD, ADAM_STEP = 0.001, 0.9, 0.999, 1e-08, 0.01, 10

VMEM_LIMIT = 48 * 1024 * 1024
CONV_HALO = 32
FFN_HALO = 8
ADAMW_BLOCK_BYTES = 4 * 1024 * 1024


def _cparams(*sem):
    return pltpu.CompilerParams(dimension_semantics=sem or None, vmem_limit_bytes=VMEM_LIMIT)


def _bs(shape, imap):
    return pl.BlockSpec(shape, imap)


def _dot(a, b):
    return lax.dot_general(a.astype(MXU), b.astype(MXU), (((1,), (0,)), ((), ())), preferred_element_type=F32)


def _dot_nt(a, b):
    return lax.dot_general(a.astype(MXU), b.astype(MXU), (((1,), (1,)), ((), ())), preferred_element_type=F32)


def _dot_tn(a, b):
    return lax.dot_general(a.astype(MXU), b.astype(MXU), (((0,), (0,)), ((), ())), preferred_element_type=F32)


def _sigmoid(x):
    return 0.5 * jnp.tanh(0.5 * x) + 0.5


def _rmsn(x, w):
    r = lax.rsqrt(jnp.mean(x * x, axis=-1, keepdims=True) + RMS_EPS)
    return x * r * w, r


def _rmsn_bwd(x, r, w, dy):
    g = dy * w
    dx = r * g - x * (r * r * r) * jnp.mean(x * g, axis=-1, keepdims=True)
    dw = jnp.sum(dy * x * r, axis=0, keepdims=True)
    return dx, dw


def _acc_out(ref, val, first):
    @pl.when(first)
    def _():
        ref[...] = val

    @pl.when(jnp.logical_not(first))
    def _():
        ref[...] += val


def _rms_fwd(x, w, name):
    T, D = x.shape
    tm = min(512, T)

    def body(x_ref, w_ref, o_ref):
        y, _ = _rmsn(x_ref[...], w_ref[...])
        o_ref[...] = y.astype(o_ref.dtype)

    return pl.pallas_call(
        body, grid=(T // tm,),
        in_specs=[_bs((tm, D), lambda i: (i, 0)), _bs((1, D), lambda i: (0, 0))],
        out_specs=_bs((tm, D), lambda i: (i, 0)),
        out_shape=S((T, D), MXU), name=name, compiler_params=_cparams("parallel"))(x, w)


def _rms_bwd(x, w, dys, resid, name):
    T, D = x.shape
    tm = min(512, T)
    n = len(dys)

    def body(*refs):
        x_ref, w_ref, res_ref = refs[0], refs[1], refs[2 + n]
        dx_ref, dw_ref = refs[3 + n], refs[4 + n]
        xv = x_ref[...]
        dy = refs[2][...]
        for dy_ref in refs[3:2 + n]:
            dy = dy + dy_ref[...]
        _, r = _rmsn(xv, w_ref[...])
        dx, dw = _rmsn_bwd(xv, r, w_ref[...], dy)
        dx_ref[...] = res_ref[...] + dx
        _acc_out(dw_ref, dw, pl.program_id(0) == 0)

    row = _bs((tm, D), lambda i: (i, 0))
    vec = _bs((1, D), lambda i: (0, 0))
    return pl.pallas_call(
        body, grid=(T // tm,), in_specs=[row, vec] + [row] * (n + 1), out_specs=[row, vec],
        out_shape=[S((T, D), F32), S((1, D), F32)], name=name, compiler_params=_cparams("arbitrary"))(x, w, *dys, resid)


def _pick(n, cands):
    for c in cands:
        if n % c == 0:
            return c
    return n


def _matmul(a, b, *, ta=False, tb=False, out_dtype=F32, residual=None, after=None, tm=None, tn=None, tk=None, name):
    M, K = (a.shape[1], a.shape[0]) if ta else a.shape
    N = b.shape[0] if tb else b.shape[1]
    tm = tm or _pick(M, (1024, 1408, 512, 256, 128))
    tn = tn or _pick(N, (512, 256, 128))
    tk = tk or _pick(K, (3072, 1536, 1024, 1408, 512, 256, 128))
    nk = K // tk
    dn = (((0 if ta else 1,), (1 if tb else 0,)), ((), ()))
    has_res = residual is not None
    n_in = 2 + has_res + (after is not None)

    def body(*refs):
        a_ref, b_ref = refs[0], refs[1]
        res_ref = refs[2] if has_res else None
        o_ref = refs[n_in]
        p = lax.dot_general(a_ref[...].astype(MXU), b_ref[...].astype(MXU), dn, preferred_element_type=F32)

        def finish(acc):
            if has_res:
                acc = acc + res_ref[...]
            o_ref[...] = acc.astype(o_ref.dtype)

        if nk == 1:
            finish(p)
        else:
            acc_ref = refs[-1]
            k = pl.program_id(2)

            @pl.when(k == 0)
            def _():
                acc_ref[...] = p

            @pl.when(k > 0)
            def _():
                acc_ref[...] += p

            @pl.when(k == nk - 1)
            def _():
                finish(acc_ref[...])

    a_spec = _bs((tk, tm), lambda i, j, k: (k, i)) if ta else _bs((tm, tk), lambda i, j, k: (i, k))
    b_spec = _bs((tn, tk), lambda i, j, k: (j, k)) if tb else _bs((tk, tn), lambda i, j, k: (k, j))
    o_spec = _bs((tm, tn), lambda i, j, k: (i, j))
    in_specs = [a_spec, b_spec] + ([o_spec] if has_res else [])
    args = (a, b) + ((residual,) if has_res else ())
    if after is not None:
        in_specs.append(_bs((8, 128), lambda i, j, k: (0, 0)))
        args += (after,)
    return pl.pallas_call(
        body, grid=(M // tm, N // tn, nk), in_specs=in_specs, out_specs=o_spec,
        out_shape=S((M, N), out_dtype), scratch_shapes=[pltpu.VMEM((tm, tn), F32)] if nk > 1 else [],
        name=name, compiler_params=_cparams("parallel", "parallel", "arbitrary"))(*args)


def _bucket_matrix(dilation):
    n = BLK
    qi = np.arange(n)[:, None]
    kj = np.arange(2 * n)[None, :]
    step = qi + n - kj
    dist = np.clip(step, 0, None) * dilation
    max_exact = N_BUCKETS // 2
    d = np.maximum(dist.astype(np.float32), np.float32(1.0))
    large = max_exact + (np.log(d / np.float32(max_exact)) / np.float32(math.log(MAX_DISTANCE / max_exact))
                         * np.float32(N_BUCKETS - max_exact)).astype(np.int32)
    large = np.minimum(large, N_BUCKETS - 1)
    bucket = np.where(dist < max_exact, dist, large)
    band = (step >= 0) & (step <= n)
    return np.where(band, bucket, -1).astype(np.int32)


def _build_bias(tbl_ref, bkt_ref, bias_ref, g):
    bk = bkt_ref[...]
    for h in range(HEADS):
        acc = jnp.full(bk.shape, NEG, F32)
        for b in range(N_BUCKETS):
            acc = jnp.where(bk == b, tbl_ref[b, HEADS * g + h], acc)
        bias_ref[h] = acc


def _to_classes(a, dil):
    if dil == 1:
        return a
    t, n = a.shape
    return a.reshape(t // dil, dil, n).transpose(1, 0, 2).reshape(t, n)


def _from_classes(a, dil):
    if dil == 1:
        return a
    t, n = a.shape
    return a.reshape(dil, t // dil, n).transpose(1, 0, 2).reshape(t, n)


def _attn_fwd(zq, table, qw, kw, g, dil):
    T = zq.shape[0]
    nb = T // dil // BLK
    bkt = jnp.asarray(_bucket_matrix(dil))

    def zspec(part, prev):
        if prev:
            return _bs((BLK, GW), lambda c, i: (c * nb + jnp.maximum(i - 1, 0), part))
        return _bs((BLK, GW), lambda c, i: (c * nb + i, part))

    def body(tbl_ref, bkt_ref, qw_ref, kw_ref, q_ref, kp_ref, kc_ref, vp_ref, vc_ref, o_ref, lse_ref, bias_ref):
        c, i = pl.program_id(0), pl.program_id(1)

        @pl.when((c == 0) & (i == 0))
        def _():
            _build_bias(tbl_ref, bkt_ref, bias_ref, g)

        kj = lax.broadcasted_iota(jnp.int32, (BLK, 2 * BLK), 1)
        no_prev = jnp.logical_and(i == 0, kj < BLK)
        for h in range(HEADS):
            sl = slice(h * HEAD_DIM, (h + 1) * HEAD_DIM)
            qn, _ = _rmsn(q_ref[:, sl], qw_ref[...])
            kn, _ = _rmsn(jnp.concatenate([kp_ref[:, sl], kc_ref[:, sl]], axis=0), kw_ref[...])
            s = _dot_nt(qn, kn) * SCALE + bias_ref[h]
            s = jnp.where(no_prev, NEG, s)
            m = jnp.max(s, axis=-1, keepdims=True)
            p = jnp.exp(s - m)
            l = jnp.sum(p, axis=-1, keepdims=True)
            v = jnp.concatenate([vp_ref[:, sl], vc_ref[:, sl]], axis=0)
            o_ref[:, sl] = _dot(p, v) / l
            lse_ref[:, sl] = jnp.broadcast_to(m + jnp.log(l), (BLK, HEAD_DIM))

    ospec = _bs((BLK, GW), lambda c, i: (c * nb + i, 0))
    vec = _bs((1, HEAD_DIM), lambda c, i: (0, 0))
    return pl.pallas_call(
        body, grid=(dil, nb),
        in_specs=[pl.BlockSpec(memory_space=pltpu.SMEM), _bs((BLK, 2 * BLK), lambda c, i: (0, 0)), vec, vec,
                  zspec(0, False), zspec(1, True), zspec(1, False), zspec(2, True), zspec(2, False)],
        out_specs=[ospec, ospec],
        out_shape=[S((T, GW), F32), S((T, GW), F32)],
        scratch_shapes=[pltpu.VMEM((HEADS, BLK, 2 * BLK), F32)],
        name=f"attn_fwd_g{g}", compiler_params=_cparams("arbitrary", "arbitrary"))(table, bkt, qw, kw, zq, zq, zq, zq, zq)


def _attn_merge(os_, lses):
    T = os_[0].shape[0]
    tm = min(512, T)

    def body(o0, o1, o2, l0, l1, l2, a_ref, lse_ref):
        ls = [l0[...], l1[...], l2[...]]
        mx = jnp.maximum(jnp.maximum(ls[0], ls[1]), ls[2])
        tot = mx + jnp.log(jnp.exp(ls[0] - mx) + jnp.exp(ls[1] - mx) + jnp.exp(ls[2] - mx))
        a_ref[...] = (jnp.exp(ls[0] - tot) * o0[...] + jnp.exp(ls[1] - tot) * o1[...] + jnp.exp(ls[2] - tot) * o2[...])
        lse_ref[...] = tot

    row = _bs((tm, GW), lambda i: (i, 0))
    return pl.pallas_call(
        body, grid=(T // tm,), in_specs=[row] * 6, out_specs=[row, row],
        out_shape=[S((T, GW), F32), S((T, GW), F32)], name="attn_merge",
        compiler_params=_cparams("parallel"))(*os_, *lses)


def _attn_bwd(zq, table, qw, kw, d_attn, attn, lse, g, dil):
    T = zq.shape[0]
    nb = T // dil // BLK
    bkt = jnp.asarray(_bucket_matrix(dil))

    def zspec(part, prev):
        if prev:
            return _bs((BLK, GW), lambda c, i: (c * nb + jnp.clip(i - 1, 0, nb - 1), part))
        return _bs((BLK, GW), lambda c, i: (c * nb + jnp.minimum(i, nb - 1), part))

    def body(tbl_ref, bkt_ref, qw_ref, kw_ref, q_ref, kp_ref, kc_ref, vp_ref, vc_ref, da_ref, at_ref, lse_ref,
             dz_ref, dqw_ref, dkw_ref, dtab_ref, bias_ref, dbias_ref, cq_ref, ck_ref, cv_ref):
        c, i = pl.program_id(0), pl.program_id(1)

        @pl.when((c == 0) & (i == 0))
        def _():
            _build_bias(tbl_ref, bkt_ref, bias_ref, g)
            dbias_ref[...] = jnp.zeros_like(dbias_ref)
            dqw_ref[...] = jnp.zeros_like(dqw_ref)
            dkw_ref[...] = jnp.zeros_like(dkw_ref)

        @pl.when(i == 0)
        def _():
            cq_ref[...] = jnp.zeros_like(cq_ref)
            ck_ref[...] = jnp.zeros_like(ck_ref)
            cv_ref[...] = jnp.zeros_like(cv_ref)

        @pl.when(i < nb)
        def _():
            kj = lax.broadcasted_iota(jnp.int32, (BLK, 2 * BLK), 1)
            no_prev = jnp.logical_and(i == 0, kj < BLK)
            dqw_acc = jnp.zeros((1, HEAD_DIM), F32)
            dkw_acc = jnp.zeros((1, HEAD_DIM), F32)
            for h in range(HEADS):
                sl = slice(h * HEAD_DIM, (h + 1) * HEAD_DIM)
                qh = q_ref[:, sl]
                k = jnp.concatenate([kp_ref[:, sl], kc_ref[:, sl]], axis=0)
                qn, rq = _rmsn(qh, qw_ref[...])
                kn, rk = _rmsn(k, kw_ref[...])
                s = _dot_nt(qn, kn) * SCALE + bias_ref[h]
                s = jnp.where(no_prev, NEG, s)
                p = jnp.exp(s - lse_ref[:, h * HEAD_DIM:h * HEAD_DIM + 1])
                do = da_ref[:, sl]
                delta = jnp.sum(do * at_ref[:, sl], axis=-1, keepdims=True)
                v = jnp.concatenate([vp_ref[:, sl], vc_ref[:, sl]], axis=0)
                ds = p * (_dot_nt(do, v) - delta)
                dbias_ref[h] += ds
                dv = _dot_tn(p, do)
                dqn = _dot(ds, kn) * SCALE
                dkn = _dot_tn(ds, qn) * SCALE
                dq, dqw = _rmsn_bwd(qh, rq, qw_ref[...], dqn)
                dk, dkw = _rmsn_bwd(k, rk, kw_ref[...], dkn)
                dqw_acc += dqw
                dkw_acc += dkw
                lo = h * HEAD_DIM
                dz_ref[:, lo:lo + HEAD_DIM] = cq_ref[:, sl].astype(dz_ref.dtype)
                dz_ref[:, GW + lo:GW + lo + HEAD_DIM] = (ck_ref[:, sl] + dk[:BLK]).astype(dz_ref.dtype)
                dz_ref[:, 2 * GW + lo:2 * GW + lo + HEAD_DIM] = (cv_ref[:, sl] + dv[:BLK]).astype(dz_ref.dtype)
                cq_ref[:, sl] = dq
                ck_ref[:, sl] = dk[BLK:]
                cv_ref[:, sl] = dv[BLK:]
            dqw_ref[...] += dqw_acc
            dkw_ref[...] += dkw_acc

        @pl.when(i == nb)
        def _():
            dz_ref[:, 0:GW] = cq_ref[...].astype(dz_ref.dtype)
            dz_ref[:, GW:2 * GW] = ck_ref[...].astype(dz_ref.dtype)
            dz_ref[:, 2 * GW:3 * GW] = cv_ref[...].astype(dz_ref.dtype)

        @pl.when((c == dil - 1) & (i == nb))
        def _():
            bk = bkt_ref[...]
            rows = lax.broadcasted_iota(jnp.int32, (N_BUCKETS, HEAD_DIM), 0)
            lanes = lax.broadcasted_iota(jnp.int32, (N_BUCKETS, HEAD_DIM), 1)
            out = jnp.zeros((N_BUCKETS, HEAD_DIM), F32)
            for h in range(HEADS):
                acc = dbias_ref[h]
                for b in range(N_BUCKETS):
                    val = jnp.sum(jnp.where(bk == b, acc, 0.0))
                    out = jnp.where((rows == b) & (lanes == h), val, out)
            dtab_ref[...] = out

    cur = _bs((BLK, GW), lambda c, i: (c * nb + jnp.minimum(i, nb - 1), 0))
    prv = _bs((BLK, 3 * GW), lambda c, i: (c * nb + jnp.maximum(i - 1, 0), 0))
    vec = _bs((1, HEAD_DIM), lambda c, i: (0, 0))
    tabs = _bs((N_BUCKETS, HEAD_DIM), lambda c, i: (0, 0))
    dzq, dqw, dkw, dtab = pl.pallas_call(
        body, grid=(dil, nb + 1),
        in_specs=[pl.BlockSpec(memory_space=pltpu.SMEM), _bs((BLK, 2 * BLK), lambda c, i: (0, 0)), vec, vec,
                  zspec(0, False), zspec(1, True), zspec(1, False), zspec(2, True), zspec(2, False), cur, cur, cur],
        out_specs=[prv, vec, vec, tabs],
        out_shape=[S((T, 3 * GW), MXU)] + [S((1, HEAD_DIM), F32)] * 2 + [S((N_BUCKETS, HEAD_DIM), F32)],
        scratch_shapes=[pltpu.VMEM((HEADS, BLK, 2 * BLK), F32), pltpu.VMEM((HEADS, BLK, 2 * BLK), F32),
                        pltpu.VMEM((BLK, GW), F32), pltpu.VMEM((BLK, GW), F32), pltpu.VMEM((BLK, GW), F32)],
        name=f"attn_bwd_g{g}", compiler_params=_cparams("arbitrary", "arbitrary"))(
            table, bkt, qw, kw, zq, zq, zq, zq, zq, d_attn, attn, lse)
    return dzq, dqw, dkw, dtab[:, :HEADS]


def _mem_fwd(mem, mem_norm_w, w_mem_kv, xk_w):
    def body(mem_ref, nw_ref, w_ref, xk_ref, mk_ref, mv_ref):
        mn, _ = _rmsn(mem_ref[...], nw_ref[...])
        kv = _dot(mn, w_ref[...])
        for h in range(HEADS):
            sl = slice(h * HEAD_DIM, (h + 1) * HEAD_DIM)
            kn, _ = _rmsn(kv[:, sl], xk_ref[...])
            mk_ref[:, sl] = kn.astype(mk_ref.dtype)
        mv_ref[...] = kv[:, GW:].astype(mv_ref.dtype)

    return pl.pallas_call(body, out_shape=[S((N_MEM, GW), MXU), S((N_MEM, GW), MXU)], name="mem_fwd",
                          compiler_params=_cparams())(mem, mem_norm_w, w_mem_kv, xk_w)


def _mem_bwd(mem, mem_norm_w, w_mem_kv, xk_w, dmk, dmv):
    def body(mem_ref, nw_ref, w_ref, xk_ref, dmk_ref, dmv_ref, dw_ref, dnw_ref, dxk_ref):
        memv = mem_ref[...]
        mn, r = _rmsn(memv, nw_ref[...])
        kv = _dot(mn, w_ref[...])
        dxk = jnp.zeros((1, HEAD_DIM), F32)
        parts = []
        for h in range(HEADS):
            sl = slice(h * HEAD_DIM, (h + 1) * HEAD_DIM)
            kh = kv[:, sl]
            _, rk = _rmsn(kh, xk_ref[...])
            dk, dw = _rmsn_bwd(kh, rk, xk_ref[...], dmk_ref[:, sl])
            dxk += dw
            parts.append(dk)
        dkv = jnp.concatenate(parts + [dmv_ref[...]], axis=1)
        dw_ref[...] = _dot_tn(mn, dkv)
        dmn = _dot_nt(dkv, w_ref[...])
        dnw_ref[...] = jnp.sum(dmn * memv * r, axis=0, keepdims=True)
        dxk_ref[...] = dxk

    return pl.pallas_call(
        body, out_shape=[S((D_MODEL, 2 * GW), F32), S((1, D_MODEL), F32), S((1, HEAD_DIM), F32)], name="mem_bwd",
        compiler_params=_cparams())(mem, mem_norm_w, w_mem_kv, xk_w, dmk, dmv)


def _cross_fwd(z, mk, mv, xq_w):
    T = z.shape[0]
    tm = min(512, T)

    def body(q_ref, mk_ref, mv_ref, w_ref, o_ref):
        for h in range(HEADS):
            sl = slice(h * HEAD_DIM, (h + 1) * HEAD_DIM)
            qn, _ = _rmsn(q_ref[:, sl], w_ref[...])
            s = _dot_nt(qn, mk_ref[:, sl]) * SCALE
            e = jnp.exp(s - jnp.max(s, axis=-1, keepdims=True))
            p = e / jnp.sum(e, axis=-1, keepdims=True)
            o_ref[:, sl] = _dot(p, mv_ref[:, sl]).astype(o_ref.dtype)

    full = _bs((N_MEM, GW), lambda i: (0, 0))
    return pl.pallas_call(
        body, grid=(T // tm,),
        in_specs=[_bs((tm, GW), lambda i: (i, 2)), full, full, _bs((1, HEAD_DIM), lambda i: (0, 0))],
        out_specs=_bs((tm, GW), lambda i: (i, 0)), out_shape=S((T, GW), MXU), name="cross_fwd",
        compiler_params=_cparams("parallel"))(z, mk, mv, xq_w)


def _cross_bwd(z, mk, mv, xq_w, d_cross):
    T = z.shape[0]
    tm = min(512, T)

    def body(q_ref, mk_ref, mv_ref, w_ref, do_ref, dq_ref, dmk_ref, dmv_ref, dw_ref):
        first = pl.program_id(0) == 0
        dw_acc = jnp.zeros((1, HEAD_DIM), F32)
        dmk_parts, dmv_parts = [], []
        for h in range(HEADS):
            sl = slice(h * HEAD_DIM, (h + 1) * HEAD_DIM)
            qh = q_ref[:, sl]
            qn, r = _rmsn(qh, w_ref[...])
            s = _dot_nt(qn, mk_ref[:, sl]) * SCALE
            e = jnp.exp(s - jnp.max(s, axis=-1, keepdims=True))
            p = e / jnp.sum(e, axis=-1, keepdims=True)
            do = do_ref[:, sl]
            dp = _dot_nt(do, mv_ref[:, sl])
            ds = p * (dp - jnp.sum(dp * p, axis=-1, keepdims=True)) * SCALE
            dmv_parts.append(_dot_tn(p, do))
            dmk_parts.append(_dot_tn(ds, qn))
            dq, dw = _rmsn_bwd(qh, r, w_ref[...], _dot(ds, mk_ref[:, sl]))
            dw_acc += dw
            dq_ref[:, sl] = dq.astype(dq_ref.dtype)
        _acc_out(dmk_ref, jnp.concatenate(dmk_parts, axis=1), first)
        _acc_out(dmv_ref, jnp.concatenate(dmv_parts, axis=1), first)
        _acc_out(dw_ref, dw_acc, first)

    full = _bs((N_MEM, GW), lambda i: (0, 0))
    vec = _bs((1, HEAD_DIM), lambda i: (0, 0))
    row = _bs((tm, GW), lambda i: (i, 0))
    return pl.pallas_call(
        body, grid=(T // tm,),
        in_specs=[_bs((tm, GW), lambda i: (i, 2)), full, full, vec, row],
        out_specs=[row, full, full, vec],
        out_shape=[S((T, GW), MXU), S((N_MEM, GW), F32), S((N_MEM, GW), F32), S((1, HEAD_DIM), F32)],
        name="cross_bwd", compiler_params=_cparams("arbitrary"))(z, mk, mv, xq_w, d_cross)


SUBLANES = 8


def _row_windows(ext, first, count, rows):
    for b in range(SUBLANES):
        js = [j for j in range(count) if (first + j) % SUBLANES == b]
        if not js:
            continue
        top = max(first + j for j in js) - b
        shifted = ext[b:b + top + rows, :]
        for j in js:
            a = first + j - b
            yield j, shifted[a:a + rows, :]


def _taps(ext, w_ref, width, base, rows):
    acc = None
    for k, win in _row_windows(ext, base - (width - 1), width, rows):
        term = win * w_ref[k:k + 1, :]
        acc = term if acc is None else acc + term
    return acc


def _taps_bwd(d_ext, x, w_ref, width, rows):
    acc = None
    dw = [None] * width
    for j, win in _row_windows(d_ext, 0, width, rows):
        k = width - 1 - j
        term = win * w_ref[k:k + 1, :]
        acc = term if acc is None else acc + term
        dw[k] = jnp.sum(win * x, axis=0, keepdims=True)
    return acc, jnp.concatenate(dw, axis=0)


def _conv_fwd(z, cw, cb, lw, lb):
    T = z.shape[0]
    tm = min(512, T)
    hb = tm // CONV_HALO

    def body(val_ref, gate_ref, hval_ref, hgate_ref, cw_ref, cb_ref, lw_ref, lb_ref, o_ref):
        i = pl.program_id(0)
        halo = hval_ref[...] * _sigmoid(hgate_ref[...])
        halo = jnp.where(i == 0, 0.0, halo)
        ext = jnp.concatenate([halo, val_ref[...] * _sigmoid(gate_ref[...])], axis=0)
        y = _taps(ext, cw_ref, CONV_WIDTH, CONV_HALO, tm) + cb_ref[...]
        xc = y - jnp.mean(y, axis=-1, keepdims=True)
        a = xc * lax.rsqrt(jnp.mean(xc * xc, axis=-1, keepdims=True) + LN_EPS) * lw_ref[...] + lb_ref[...]
        o_ref[...] = (a * _sigmoid(a)).astype(o_ref.dtype)

    vec = _bs((1, GW), lambda i: (0, 0))
    halo_spec = lambda col: _bs((CONV_HALO, GW), lambda i: (jnp.maximum(i * hb - 1, 0), col))
    return pl.pallas_call(
        body, grid=(T // tm,),
        in_specs=[_bs((tm, GW), lambda i: (i, 0)), _bs((tm, GW), lambda i: (i, 1)), halo_spec(0), halo_spec(1),
                  _bs((CONV_WIDTH, GW), lambda i: (0, 0)), vec, vec, vec],
        out_specs=_bs((tm, GW), lambda i: (i, 0)), out_shape=S((T, GW), MXU), name="conv_fwd",
        compiler_params=_cparams("parallel"))(z, z, z, z, cw, cb, lw, lb)


def _conv_bwd(z, cw, cb, lw, lb, d_u):
    T = z.shape[0]
    tm = min(512, T)
    hb = tm // CONV_HALO
    nt = T // tm
    H = CONV_HALO

    def body(val_ref, gate_ref, pval_ref, pgate_ref, nval_ref, ngate_ref, du_ref, ndu_ref, cw_ref, cb_ref, lw_ref,
             lb_ref, dval_ref, dgate_ref, dcw_ref, dcb_ref, dlw_ref, dlb_ref):
        i = pl.program_id(0)
        first = i == 0
        val = jnp.concatenate([pval_ref[...] * jnp.where(first, 0.0, 1.0), val_ref[...], nval_ref[...]], axis=0)
        sg = _sigmoid(jnp.concatenate([pgate_ref[...], gate_ref[...], ngate_ref[...]], axis=0))
        u0 = val * sg
        y = _taps(u0, cw_ref, CONV_WIDTH, H, tm + H) + cb_ref[...]
        xc = y - jnp.mean(y, axis=-1, keepdims=True)
        rs = lax.rsqrt(jnp.mean(xc * xc, axis=-1, keepdims=True) + LN_EPS)
        nh = xc * rs
        a = nh * lw_ref[...] + lb_ref[...]
        sa = _sigmoid(a)
        du = jnp.concatenate([du_ref[...], ndu_ref[...] * jnp.where(i == nt - 1, 0.0, 1.0)], axis=0)
        da = du * (sa * (1.0 + a * (1.0 - sa)))
        dn = da * lw_ref[...]
        dy = rs * (dn - jnp.mean(dn, axis=-1, keepdims=True) - nh * jnp.mean(dn * nh, axis=-1, keepdims=True))
        du0, dcw = _taps_bwd(dy, u0[H:H + tm], cw_ref, CONV_WIDTH, tm)
        v0, s0 = val[H:H + tm], sg[H:H + tm]
        dval_ref[...] = (du0 * s0).astype(dval_ref.dtype)
        dgate_ref[...] = (du0 * v0 * s0 * (1.0 - s0)).astype(dgate_ref.dtype)
        dy0 = dy[:tm]
        _acc_out(dcw_ref, dcw, first)
        _acc_out(dcb_ref, jnp.sum(dy0, axis=0, keepdims=True), first)
        _acc_out(dlw_ref, jnp.sum(da[:tm] * nh[:tm], axis=0, keepdims=True), first)
        _acc_out(dlb_ref, jnp.sum(da[:tm], axis=0, keepdims=True), first)

    vec = _bs((1, GW), lambda i: (0, 0))
    cwspec = _bs((CONV_WIDTH, GW), lambda i: (0, 0))
    prev = lambda col: _bs((H, GW), lambda i: (jnp.maximum(i * hb - 1, 0), col))
    nxt = lambda col: _bs((H, GW), lambda i: (jnp.minimum((i + 1) * hb, nt * hb - 1), col))
    row = _bs((tm, GW), lambda i: (i, 0))
    return pl.pallas_call(
        body, grid=(nt,),
        in_specs=[_bs((tm, GW), lambda i: (i, 0)), _bs((tm, GW), lambda i: (i, 1)), prev(0), prev(1), nxt(0), nxt(1),
                  row, nxt(0), cwspec, vec, vec, vec],
        out_specs=[row, row, cwspec, vec, vec, vec],
        out_shape=[S((T, GW), MXU), S((T, GW), MXU), S((CONV_WIDTH, GW), F32)] + [S((1, GW), F32)] * 3,
        name="conv_bwd", compiler_params=_cparams("arbitrary"))(z, z, z, z, z, z, d_u, d_u, cw, cb, lw, lb)


def _ffn_act_fwd(up0, fw, fb):
    T = up0.shape[0]
    tm = min(256, T)
    hb = tm // FFN_HALO
    H = FFN_HALO

    def body(a_ref, g_ref, pa_ref, pg_ref, wa_ref, wg_ref, ba_ref, bg_ref, o_ref):
        i = pl.program_id(0)
        keep = jnp.where(i == 0, 0.0, 1.0)
        ea = jnp.concatenate([pa_ref[...] * keep, a_ref[...]], axis=0)
        eg = jnp.concatenate([pg_ref[...] * keep, g_ref[...]], axis=0)
        av = _taps(ea, wa_ref, FFN_CONV_WIDTH, H, tm) + ba_ref[...]
        gv = _taps(eg, wg_ref, FFN_CONV_WIDTH, H, tm) + bg_ref[...]
        o_ref[...] = (gv * _sigmoid(gv) * av).astype(o_ref.dtype)

    col = lambda j: _bs((tm, D_FF), lambda i: (i, j))
    prev = lambda j: _bs((H, D_FF), lambda i: (jnp.maximum(i * hb - 1, 0), j))
    wspec = lambda j: _bs((FFN_CONV_WIDTH, D_FF), lambda i: (0, j))
    bspec = lambda j: _bs((1, D_FF), lambda i: (0, j))
    return pl.pallas_call(
        body, grid=(T // tm,),
        in_specs=[col(0), col(1), prev(0), prev(1), wspec(0), wspec(1), bspec(0), bspec(1)],
        out_specs=_bs((tm, D_FF), lambda i: (i, 0)), out_shape=S((T, D_FF), MXU), name="ffn_act_fwd",
        compiler_params=_cparams("parallel"))(up0, up0, up0, up0, fw, fw, fb, fb)


def _ffn_act_bwd(up0, fw, fb, d_f):
    T = up0.shape[0]
    tm = min(256, T)
    hb = tm // FFN_HALO
    nt = T // tm
    H = FFN_HALO
    W = FFN_CONV_WIDTH

    def body(a_ref, g_ref, pa_ref, pg_ref, na_ref, ng_ref, df_ref, ndf_ref, wa_ref, wg_ref, ba_ref, bg_ref,
             dup_ref, dw_ref, db_ref):
        i = pl.program_id(0)
        first = i == 0
        keep = jnp.where(first, 0.0, 1.0)
        ea = jnp.concatenate([pa_ref[...] * keep, a_ref[...], na_ref[...]], axis=0)
        eg = jnp.concatenate([pg_ref[...] * keep, g_ref[...], ng_ref[...]], axis=0)
        av = _taps(ea, wa_ref, W, H, tm + H) + ba_ref[...]
        gv = _taps(eg, wg_ref, W, H, tm + H) + bg_ref[...]
        df = jnp.concatenate([df_ref[...], ndf_ref[...] * jnp.where(i == nt - 1, 0.0, 1.0)], axis=0)
        sg = _sigmoid(gv)
        d_av = df * gv * sg
        d_gv = df * av * (sg * (1.0 + gv * (1.0 - sg)))
        dua, dwa = _taps_bwd(d_av, a_ref[...], wa_ref, W, tm)
        dug, dwg = _taps_bwd(d_gv, g_ref[...], wg_ref, W, tm)
        dup_ref[:, :D_FF] = dua.astype(dup_ref.dtype)
        dup_ref[:, D_FF:] = dug.astype(dup_ref.dtype)
        dw = jnp.concatenate([dwa, dwg], axis=1)
        db = jnp.concatenate([jnp.sum(d_av[:tm], axis=0, keepdims=True), jnp.sum(d_gv[:tm], axis=0, keepdims=True)], axis=1)
        _acc_out(dw_ref, dw, first)
        _acc_out(db_ref, db, first)

    col = lambda j: _bs((tm, D_FF), lambda i: (i, j))
    prev = lambda j: _bs((H, D_FF), lambda i: (jnp.maximum(i * hb - 1, 0), j))
    nxt = lambda j: _bs((H, D_FF), lambda i: (jnp.minimum((i + 1) * hb, nt * hb - 1), j))
    wspec = lambda j: _bs((W, D_FF), lambda i: (0, j))
    bspec = lambda j: _bs((1, D_FF), lambda i: (0, j))
    return pl.pallas_call(
        body, grid=(nt,),
        in_specs=[col(0), col(1), prev(0), prev(1), nxt(0), nxt(1), col(0), nxt(0), wspec(0), wspec(1), bspec(0), bspec(1)],
        out_specs=[_bs((tm, 2 * D_FF), lambda i: (i, 0)), _bs((W, 2 * D_FF), lambda i: (0, 0)),
                   _bs((1, 2 * D_FF), lambda i: (0, 0))],
        out_shape=[S((T, 2 * D_FF), MXU), S((W, 2 * D_FF), F32), S((1, 2 * D_FF), F32)],
        name="ffn_act_bwd", compiler_params=_cparams("arbitrary"))(
            up0, up0, up0, up0, up0, up0, d_f, d_f, fw, fw, fb, fb)


def _branch_fwd(attn, u, cross, z, b_gate, wa, wc, wx):
    T = z.shape[0]
    tm = min(512, T)

    def body(a_ref, u_ref, x_ref, g0_ref, g1_ref, g2_ref, b_ref, wa_ref, wc_ref, wx_ref, o_ref):
        acc = None
        for j, (act, g_ref, w_ref) in enumerate(((a_ref, g0_ref, wa_ref), (u_ref, g1_ref, wc_ref), (x_ref, g2_ref, wx_ref))):
            gate = _sigmoid(g_ref[...] + b_ref[:, j * D_MODEL:(j + 1) * D_MODEL])
            term = gate * _dot(act[...], w_ref[...])
            acc = term if acc is None else acc + term
        o_ref[...] = acc.astype(o_ref.dtype)

    act = _bs((tm, GW), lambda i: (i, 0))
    gcol = lambda j: _bs((tm, D_MODEL), lambda i: (i, j))
    wfull = _bs((GW, D_MODEL), lambda i: (0, 0))
    return pl.pallas_call(
        body, grid=(T // tm,),
        in_specs=[act, act, act, gcol(0), gcol(1), gcol(2), _bs((1, 3 * D_MODEL), lambda i: (0, 0)), wfull, wfull, wfull],
        out_specs=_bs((tm, D_MODEL), lambda i: (i, 0)), out_shape=S((T, D_MODEL), MXU), name="branch_fwd",
        compiler_params=_cparams("parallel"))(attn, u, cross, z, z, z, b_gate, wa, wc, wx)


def _branch_bwd(d_merged, attn, u, cross, z, b_gate, wa, wc, wx):
    T = z.shape[0]
    tm = min(512, T)

    def body(dm_ref, a_ref, u_ref, x_ref, g0_ref, g1_ref, g2_ref, b_ref, wa_ref, wc_ref, wx_ref,
             dzg_ref, da_ref, du_ref, dx_ref, dwa_ref, dwc_ref, dwx_ref, db_ref):
        first = pl.program_id(0) == 0
        dm = dm_ref[...]
        dbs = []
        for j, (act, g_ref, w_ref, dact_ref, dw_ref) in enumerate((
                (a_ref, g0_ref, wa_ref, da_ref, dwa_ref), (u_ref, g1_ref, wc_ref, du_ref, dwc_ref),
                (x_ref, g2_ref, wx_ref, dx_ref, dwx_ref))):
            av = act[...]
            gate = _sigmoid(g_ref[...] + b_ref[:, j * D_MODEL:(j + 1) * D_MODEL])
            y = _dot(av, w_ref[...])
            dzg = dm * y * gate * (1.0 - gate)
            dzg_ref[:, j * D_MODEL:(j + 1) * D_MODEL] = dzg.astype(dzg_ref.dtype)
            dbs.append(jnp.sum(dzg, axis=0, keepdims=True))
            dy = (gate * dm).astype(MXU)
            dact_ref[...] = _dot_nt(dy, w_ref[...])
            _acc_out(dw_ref, _dot_tn(av, dy), first)
        _acc_out(db_ref, jnp.concatenate(dbs, axis=1), first)

    act = _bs((tm, GW), lambda i: (i, 0))
    gcol = lambda j: _bs((tm, D_MODEL), lambda i: (i, j))
    wfull = _bs((GW, D_MODEL), lambda i: (0, 0))
    bvec = _bs((1, 3 * D_MODEL), lambda i: (0, 0))
    return pl.pallas_call(
        body, grid=(T // tm,),
        in_specs=[_bs((tm, D_MODEL), lambda i: (i, 0)), act, act, act, gcol(0), gcol(1), gcol(2), bvec, wfull, wfull, wfull],
        out_specs=[_bs((tm, 3 * D_MODEL), lambda i: (i, 0)), act, act, act, wfull, wfull, wfull, bvec],
        out_shape=[S((T, 3 * D_MODEL), MXU)] + [S((T, GW), F32)] * 3 + [S((GW, D_MODEL), F32)] * 3 + [S((1, 3 * D_MODEL), F32)],
        name="branch_bwd", compiler_params=_cparams("arbitrary"))(d_merged, attn, u, cross, z, z, z, b_gate, wa, wc, wx)


def _loss_head(y, target):
    T, D = y.shape
    tm = min(512, T)

    def body(y_ref, t_ref, dy_ref, l_ref):
        e = y_ref[...] - t_ref[...]
        dy_ref[...] = e * (1.0 / D)
        part = jnp.full((8, 128), jnp.sum(e * e), F32)
        _acc_out(l_ref, part, pl.program_id(0) == 0)

    row = _bs((tm, D), lambda i: (i, 0))
    return pl.pallas_call(
        body, grid=(T // tm,), in_specs=[row, row], out_specs=[row, _bs((8, 128), lambda i: (0, 0))],
        out_shape=[S((T, D), F32), S((8, 128), F32)], name="loss_head", compiler_params=_cparams("arbitrary"))(y, target)


def _peer(mask):
    x, y, c = lax.axis_index("x"), lax.axis_index("y"), lax.axis_index("c")
    px = 1 - x if mask & 4 else x
    py = 1 - y if mask & 2 else y
    pc = 1 - c if mask & 1 else c
    return (px, py, pc), 4 * px + 2 * py + pc


def _exchange(arrs, scatter, name):
    n = len(arrs)
    outs_shape = [S(a.shape if scatter else (N_DEV,) + a.shape, a.dtype) for a in arrs]

    def body(*refs):
        ins, outs = refs[:n], refs[n:2 * n]
        send_sems, recv_sems, local_sems = refs[2 * n:]
        me = 4 * lax.axis_index("x") + 2 * lax.axis_index("y") + lax.axis_index("c")
        copies = []
        for w in range(n):
            src = ins[w].at[me] if scatter else ins[w]
            cp = pltpu.make_async_copy(src, outs[w].at[me], local_sems.at[w])
            cp.start()
            copies.append(cp)
        for k in range(1, N_DEV):
            peer, pidx = _peer(k)
            for w in range(n):
                src = ins[w].at[pidx] if scatter else ins[w]
                cp = pltpu.make_async_remote_copy(
                    src_ref=src, dst_ref=outs[w].at[me], send_sem=send_sems.at[w, k - 1], recv_sem=recv_sems.at[w, k - 1],
                    device_id=peer, device_id_type=pl.DeviceIdType.MESH)
                cp.start()
                copies.append(cp)
        for cp in copies:
            cp.wait()

    hbm = pl.BlockSpec(memory_space=pl.ANY)
    return pl.pallas_call(
        body, in_specs=[hbm] * n, out_specs=[hbm] * n, out_shape=outs_shape,
        scratch_shapes=[pltpu.SemaphoreType.DMA((n, N_DEV - 1)), pltpu.SemaphoreType.DMA((n, N_DEV - 1)),
                        pltpu.SemaphoreType.DMA((n,))],
        name=name)(*arrs)


def _exchange_copies(ins, lands, send_sems, recv_sems, local_sems, scatter):
    n = len(ins)
    me = 4 * lax.axis_index("x") + 2 * lax.axis_index("y") + lax.axis_index("c")
    copies = []
    for w in range(n):
        src = ins[w].at[me] if scatter else ins[w]
        copies.append(pltpu.make_async_copy(src, lands[w].at[me], local_sems.at[w]))
    for k in range(1, N_DEV):
        peer, pidx = _peer(k)
        for w in range(n):
            src = ins[w].at[pidx] if scatter else ins[w]
            copies.append(pltpu.make_async_remote_copy(
                src_ref=src, dst_ref=lands[w].at[me], send_sem=send_sems.at[w * (N_DEV - 1) + k - 1],
                recv_sem=recv_sems.at[w * (N_DEV - 1) + k - 1],
                device_id=peer, device_id_type=pl.DeviceIdType.MESH))
    return copies


_HBM_SPEC = pl.BlockSpec(memory_space=pltpu.HBM)
_SEM_SPEC = pl.BlockSpec(memory_space=pltpu.SEMAPHORE)
_DATAFLOW = pltpu.SideEffectType.DATAFLOW_SIDE_EFFECTING


def _exchange_start(arrs, scatter, name):
    n = len(arrs)
    land_shapes = [a.shape if scatter else (N_DEV,) + a.shape for a in arrs]

    def body(*refs):
        ins, lands = refs[:n], refs[n:2 * n]
        send_sems, recv_sems, local_sems = refs[2 * n:2 * n + 3]
        token = refs[-1]
        for cp in _exchange_copies(ins, lands, send_sems, recv_sems, local_sems, scatter):
            cp.start()
        token[...] = jnp.zeros_like(token)

    out_shape = ([pltpu.SemaphoreType.DMA((n * (N_DEV - 1),)), pltpu.SemaphoreType.DMA((n * (N_DEV - 1),)),
                  pltpu.SemaphoreType.DMA((n,))]
                 + [pltpu.HBM(a.shape, a.dtype) for a in arrs]
                 + [pltpu.HBM(s, a.dtype) for s, a in zip(land_shapes, arrs)]
                 + [S((8, 128), F32)])
    args = ([pltpu.with_memory_space_constraint(a, pltpu.HBM) for a in arrs]
            + [pltpu.with_memory_space_constraint(lax.empty(s, a.dtype), pltpu.HBM) for s, a in zip(land_shapes, arrs)])
    outs = pl.pallas_call(
        body, in_specs=[_HBM_SPEC] * (2 * n),
        out_specs=[_SEM_SPEC] * 3 + [_HBM_SPEC] * (2 * n) + [pl.BlockSpec(memory_space=pltpu.VMEM)],
        out_shape=out_shape, input_output_aliases={j: 3 + j for j in range(2 * n)},
        name=name, compiler_params=pltpu.CompilerParams(has_side_effects=_DATAFLOW))(*args)
    return (n, scatter, outs[:3], outs[3:3 + n], outs[3 + n:3 + 2 * n]), outs[-1]


def _exchange_wait(state, after, name):
    n, scatter, sems, ins, lands = state

    def body(*refs):
        ins_r, lands_r = refs[:n], refs[n:2 * n]
        send_sems, recv_sems, local_sems = refs[2 * n:2 * n + 3]
        for cp in _exchange_copies(ins_r, lands_r, send_sems, recv_sems, local_sems, scatter):
            cp.wait()

    outs = pl.pallas_call(
        body, in_specs=[_HBM_SPEC] * (2 * n) + [_SEM_SPEC] * 3 + [pl.BlockSpec(memory_space=pl.ANY)],
        out_specs=[_HBM_SPEC] * (2 * n),
        out_shape=[pltpu.HBM(a.shape, a.dtype) for a in ins] + [pltpu.HBM(a.shape, a.dtype) for a in lands],
        input_output_aliases={j: j for j in range(2 * n)},
        name=name, compiler_params=pltpu.CompilerParams(has_side_effects=_DATAFLOW))(*ins, *lands, *sems, after)
    return list(outs[n:])


def _adamw(w, m, v, parts, name):
    R, C = w.shape
    P = parts.shape[0]
    tr = _pick(R, tuple(t for t in (256, 176, 128, 64, 32, 16, 8) if P * t * C * 4 <= ADAMW_BLOCK_BYTES))
    c1 = 1.0 / (1.0 - ADAM_B1 ** ADAM_STEP)
    c2 = 1.0 / (1.0 - ADAM_B2 ** ADAM_STEP)

    def body(w_ref, m_ref, v_ref, p_ref, g_ref, d_ref, nm_ref, nv_ref):
        g = p_ref[0].astype(F32)
        for j in range(1, P):
            g = g + p_ref[j].astype(F32)
        m2 = ADAM_B1 * m_ref[...] + (1.0 - ADAM_B1) * g
        v2 = ADAM_B2 * v_ref[...] + (1.0 - ADAM_B2) * (g * g)
        g_ref[...] = g
        nm_ref[...] = m2
        nv_ref[...] = v2
        d_ref[...] = -ADAM_LR * ((m2 * c1) / (jnp.sqrt(v2 * c2) + ADAM_EPS) + ADAM_WD * w_ref[...])

    row = _bs((tr, C), lambda i: (i, 0))
    return pl.pallas_call(
        body, grid=(R // tr,), in_specs=[row, row, row, _bs((P, tr, C), lambda i: (0, i, 0))], out_specs=[row] * 4,
        out_shape=[S((R, C), F32)] * 4, name=name, compiler_params=_cparams("parallel"))(w, m, v, parts)


def _sum_parts(parts, name):
    P, R, C = parts.shape

    def body(p_ref, o_ref):
        g = p_ref[0]
        for j in range(1, P):
            g = g + p_ref[j]
        o_ref[...] = g

    return pl.pallas_call(body, out_shape=S((R, C), F32), name=name, compiler_params=_cparams())(parts)


def _pack(arrs):
    flat = jnp.concatenate([a.reshape(-1) for a in arrs])
    rows = -(-flat.shape[0] // 1024) * 8
    return jnp.pad(flat, (0, rows * 128 - flat.shape[0])).reshape(rows, 128)


def _unpack(packed, shapes):
    flat = packed.reshape(-1)
    out, off = [], 0
    for s in shapes:
        n = int(np.prod(s))
        out.append(flat[off:off + n].reshape(s))
        off += n
    return out


def _behind(a, token):
    return a if token is None else a + token[0, 0]


def _local_step(x, mem, target, p, comm=None):
    table = p["rel_bias_table"]
    xn = _rms_fwd(x, p["attn_norm_w"], "attn_norm_fwd")
    dils = [dil for _, dil in ATTN_GROUPS]
    xn_c = [_to_classes(xn, dil) for dil in dils]
    w_in = p["w_in"]
    qkv_w = 3 * N_GROUPS * GW
    wq = [jnp.concatenate([w_in[:, (N_GROUPS * part + g) * GW:(N_GROUPS * part + g + 1) * GW] for part in range(3)], axis=1)
          for g in range(N_GROUPS)]
    wc = w_in[:, qkv_w:qkv_w + 3 * GW]
    wg = w_in[:, qkv_w + 3 * GW:]
    zq = [_matmul(xn_c[g], wq[g], name=f"mm_in_qkv{g}") for g in range(N_GROUPS)]
    zc = _matmul(xn, wc, name="mm_in_c")
    zg = _matmul(xn, wg, name="mm_in_g")
    os_, lses = [], []
    for g, dil in enumerate(dils):
        o, l = _attn_fwd(zq[g], table, p["q_norm_w"][g:g + 1], p["k_norm_w"][g:g + 1], g, dil)
        os_.append(_from_classes(o, dil))
        lses.append(_from_classes(l, dil))
    attn, lse = _attn_merge(os_, lses)
    u = _conv_fwd(zc, p["conv_dw_w"], p["conv_dw_b"], p["conv_ln_w"], p["conv_ln_b"])
    if comm is not None:
        p = {**p, **comm.late_weights(after=u)}
    mk, mv = _mem_fwd(mem, p["mem_norm_w"], p["w_mem_kv"], p["xk_norm_w"])
    cross = _cross_fwd(zc, mk, mv, p["xq_norm_w"])
    merged = _branch_fwd(attn, u, cross, zg, p["b_gate"], p["w_attn_o"], p["w_conv_o"], p["w_cross_o"])
    h1 = _matmul(merged, p["w_out"], residual=x, name="mm_out")
    hn = _rms_fwd(h1, p["ffn_norm_w"], "ffn_norm_fwd")
    up0 = _matmul(hn, p["w_up"], name="mm_up")
    f = _ffn_act_fwd(up0, p["ffn_conv_w"], p["ffn_conv_b"])
    h2 = _matmul(f, p["w_down"], residual=h1, name="mm_down")
    dh2, lsum = _loss_head(h2, target)
    g = {}
    d_f = _matmul(dh2, p["w_down"], tb=True, name="mm_down_dx")
    g["w_down"] = _matmul(f, dh2, ta=True, name="mm_down_dw")
    d_up0, g["ffn_conv_w"], g["ffn_conv_b"] = _ffn_act_bwd(up0, p["ffn_conv_w"], p["ffn_conv_b"], d_f)
    dhn = _matmul(d_up0, p["w_up"], tb=True, name="mm_up_dx")
    g["w_up"] = _matmul(hn, d_up0, ta=True, name="mm_up_dw")
    dh1, g["ffn_norm_w"] = _rms_bwd(h1, p["ffn_norm_w"], [dhn], dh2, "ffn_norm_bwd")
    d_merged = _matmul(dh1, p["w_out"], tb=True, name="mm_out_dx")
    g["w_out"] = _matmul(merged, dh1, ta=True, name="mm_out_dw")
    (d_zg, d_attn, d_u, d_cross, g["w_attn_o"], g["w_conv_o"], g["w_cross_o"], g["b_gate"]) = _branch_bwd(
        d_merged, attn, u, cross, zg, p["b_gate"], p["w_attn_o"], p["w_conv_o"], p["w_cross_o"])
    d_xq, dmk, dmv, g["xq_norm_w"] = _cross_bwd(zc, mk, mv, p["xq_norm_w"], d_cross)
    g["w_mem_kv"], g["mem_norm_w"], g["xk_norm_w"] = _mem_bwd(mem, p["mem_norm_w"], p["w_mem_kv"], p["xk_norm_w"], dmk, dmv)
    tok = comm.start_early_grads(g) if comm is not None else None
    d_val, d_gate, g["conv_dw_w"], g["conv_dw_b"], g["conv_ln_w"], g["conv_ln_b"] = _conv_bwd(
        zc, p["conv_dw_w"], _behind(p["conv_dw_b"], tok), p["conv_ln_w"], p["conv_ln_b"], d_u)
    dzq, dqw, dkw, dtab = [], [], [], []
    for gi, dil in enumerate(dils):
        r = _attn_bwd(zq[gi], table, p["q_norm_w"][gi:gi + 1], p["k_norm_w"][gi:gi + 1], _to_classes(d_attn, dil),
                      _to_classes(attn, dil), _to_classes(lse, dil), gi, dil)
        for lst, val in zip((dzq, dqw, dkw, dtab), r):
            lst.append(val)
    g["q_norm_w"] = jnp.concatenate(dqw, axis=0)
    g["k_norm_w"] = jnp.concatenate(dkw, axis=0)
    g["rel_bias_table"] = jnp.concatenate(dtab, axis=1)
    d_zc = jnp.concatenate([d_val, d_gate, d_xq], axis=1)
    gq = [_matmul(xn_c[gi], dzq[gi], ta=True, name=f"mm_in_qkv{gi}_dw") for gi in range(N_GROUPS)]
    gc = _matmul(xn, d_zc, ta=True, name="mm_in_c_dw")
    gg = _matmul(xn, d_zg, ta=True, name="mm_in_g_dw")
    g["w_in"] = jnp.concatenate(
        [gq[gi][:, part * GW:(part + 1) * GW] for part in range(3) for gi in range(N_GROUPS)] + [gc, gg], axis=1)
    tok = comm.start_w_in_grad(g["w_in"]) if comm is not None else None
    dxn = _matmul(d_zg, wg, tb=True, after=tok, name="mm_in_g_dx")
    dxn = _matmul(d_zc, wc, tb=True, residual=dxn, name="mm_in_c_dx")
    dxn = _matmul(dzq[0], wq[0], tb=True, residual=dxn, name="mm_in_qkv0_dx")
    dxs = [dxn] + [_from_classes(_matmul(dzq[gi], wq[gi], tb=True, name=f"mm_in_qkv{gi}_dx"), dils[gi])
                   for gi in range(1, N_GROUPS)]
    grad_x, g["attn_norm_w"] = _rms_bwd(x, p["attn_norm_w"], dxs, dh1, "attn_norm_bwd")
    return lsum[0, 0], grad_x, g


WEIGHT_NAMES = ["rel_bias_table", "attn_norm_w", "w_in", "b_gate", "q_norm_w", "k_norm_w", "w_attn_o", "conv_dw_w",
                "conv_dw_b", "conv_ln_w", "conv_ln_b", "w_conv_o", "mem_norm_w", "w_mem_kv", "xq_norm_w", "xk_norm_w",
                "w_cross_o", "w_out", "ffn_norm_w", "w_up", "ffn_conv_w", "ffn_conv_b", "w_down"]
COL_SHARDED = ("w_in", "w_attn_o", "w_conv_o", "w_cross_o", "w_up")
ROW_SHARDED = ("w_mem_kv", "w_out", "w_down")
SMALL_COL_SHARDED = ("conv_dw_w", "ffn_conv_w")
BIG = COL_SHARDED + ROW_SHARDED


def _cols_to_blocks(a):
    k, n8 = a.shape
    return a.reshape(k, N_DEV, n8 // N_DEV).transpose(1, 0, 2)


def _blocks_to_cols(a):
    return a.transpose(1, 0, 2).reshape(a.shape[1], N_DEV * a.shape[2])


def _step(x, mem, target, w, m, v):
    me = 4 * lax.axis_index("x") + 2 * lax.axis_index("y") + lax.axis_index("c")

    def to_full(n, blocks):
        return _blocks_to_cols(blocks) if n in COL_SHARDED + SMALL_COL_SHARDED else blocks.reshape(-1, blocks.shape[-1])

    def to_blocks(n, grad):
        blocks = _cols_to_blocks(grad) if n in COL_SHARDED else grad.reshape(N_DEV, -1, grad.shape[-1])
        return blocks.astype(MXU)

    first = ("w_in",) + SMALL_COL_SHARDED
    late = tuple(n for n in BIG if n != "w_in")
    cast = lambda n: w[n].astype(MXU) if n in BIG else w[n]
    first_state, _ = _exchange_start([cast(n) for n in first], False, "gather_first_start")
    late_state, late_token = _exchange_start([cast(n) for n in late], False, "gather_late_start")
    got = _exchange_wait(first_state, late_token, "gather_first_wait")
    p = {n: w[n] for n in WEIGHT_NAMES if n not in BIG + SMALL_COL_SHARDED}
    p.update({n: to_full(n, b) for n, b in zip(first, got)})

    class Comm:
        def late_weights(self, after):
            return {n: to_full(n, b) for n, b in zip(late, _exchange_wait(late_state, after, "gather_late_wait"))}

        def start_early_grads(self, g):
            self.early_state, token = _exchange_start([to_blocks(n, g[n]) for n in late], True, "scatter_early_start")
            return token

        def start_w_in_grad(self, grad):
            self.w_in_state, token = _exchange_start([to_blocks("w_in", grad)], True, "scatter_w_in_start")
            return token

    comm = Comm()
    lsum, grad_x, g = _local_step(x, mem, target, p, comm)
    small_names = [n for n in WEIGHT_NAMES if n not in BIG]
    small_shapes = [g[n].shape for n in small_names]
    small_parts = _exchange([_pack([g[n] for n in small_names])], False, "gather_small_grads")[0]
    gsmall = dict(zip(small_names, _unpack(_sum_parts(small_parts, "sum_small_grads"), small_shapes)))
    for n in SMALL_COL_SHARDED:
        width = w[n].shape[-1]
        gsmall[n] = lax.dynamic_slice_in_dim(gsmall[n], me * width, width, axis=1)
    res = {}
    parts = dict(zip(late, _exchange_wait(comm.early_state, grad_x, "scatter_early_wait")))
    for n in late:
        res[n] = _adamw(w[n], m[n], v[n], parts[n], "adamw_" + n)
    w_in_parts = _exchange_wait(comm.w_in_state, res[late[-1]][1], "scatter_w_in_wait")[0]
    res["w_in"] = _adamw(w["w_in"], m["w_in"], v["w_in"], w_in_parts, "adamw_w_in")
    shapes = [w[n].shape for n in small_names]
    packed = [_pack([d[n] for n in small_names]) for d in (w, m, v, gsmall)]
    outs = _adamw(packed[0], packed[1], packed[2], packed[3][None], "adamw_small")
    unpacked = [_unpack(o, shapes) for o in outs]
    for j, n in enumerate(small_names):
        res[n] = tuple(unpacked[q][j] for q in range(4))
    return lsum, grad_x, res


def kernel(x, mem, rel_bias_table, attn_norm_w, w_in, b_gate, q_norm_w, k_norm_w, w_attn_o, conv_dw_w, conv_dw_b, conv_ln_w, conv_ln_b, w_conv_o, mem_norm_w, w_mem_kv, xq_norm_w, xk_norm_w, w_cross_o, w_out, ffn_norm_w, w_up, ffn_conv_w, ffn_conv_b, w_down, loss_target, m_rel_bias_table, m_attn_norm_w, m_w_in, m_b_gate, m_q_norm_w, m_k_norm_w, m_w_attn_o, m_conv_dw_w, m_conv_dw_b, m_conv_ln_w, m_conv_ln_b, m_w_conv_o, m_mem_norm_w, m_w_mem_kv, m_xq_norm_w, m_xk_norm_w, m_w_cross_o, m_w_out, m_ffn_norm_w, m_w_up, m_ffn_conv_w, m_ffn_conv_b, m_w_down, v_rel_bias_table, v_attn_norm_w, v_w_in, v_b_gate, v_q_norm_w, v_k_norm_w, v_w_attn_o, v_conv_dw_w, v_conv_dw_b, v_conv_ln_w, v_conv_ln_b, v_w_conv_o, v_mem_norm_w, v_w_mem_kv, v_xq_norm_w, v_xk_norm_w, v_w_cross_o, v_w_out, v_ffn_norm_w, v_w_up, v_ffn_conv_w, v_ffn_conv_b, v_w_down):
    ws = dict(zip(WEIGHT_NAMES, (rel_bias_table, attn_norm_w, w_in, b_gate, q_norm_w, k_norm_w, w_attn_o, conv_dw_w, conv_dw_b, conv_ln_w, conv_ln_b, w_conv_o, mem_norm_w, w_mem_kv, xq_norm_w, xk_norm_w, w_cross_o, w_out, ffn_norm_w, w_up, ffn_conv_w, ffn_conv_b, w_down)))
    ms = dict(zip(WEIGHT_NAMES, (m_rel_bias_table, m_attn_norm_w, m_w_in, m_b_gate, m_q_norm_w, m_k_norm_w, m_w_attn_o, m_conv_dw_w, m_conv_dw_b, m_conv_ln_w, m_conv_ln_b, m_w_conv_o, m_mem_norm_w, m_w_mem_kv, m_xq_norm_w, m_xk_norm_w, m_w_cross_o, m_w_out, m_ffn_norm_w, m_w_up, m_ffn_conv_w, m_ffn_conv_b, m_w_down)))
    vs = dict(zip(WEIGHT_NAMES, (v_rel_bias_table, v_attn_norm_w, v_w_in, v_b_gate, v_q_norm_w, v_k_norm_w, v_w_attn_o, v_conv_dw_w, v_conv_dw_b, v_conv_ln_w, v_conv_ln_b, v_w_conv_o, v_mem_norm_w, v_w_mem_kv, v_xq_norm_w, v_xk_norm_w, v_w_cross_o, v_w_out, v_ffn_norm_w, v_w_up, v_ffn_conv_w, v_ffn_conv_b, v_w_down)))
    full_shapes = {n: ws[n].shape for n in WEIGHT_NAMES}

    def squeeze(d):
        return {n: (a if n == "rel_bias_table" else a[0]) for n, a in d.items()}

    w, m, v = squeeze(ws), squeeze(ms), squeeze(vs)
    for d in (w, m, v):
        for n in WEIGHT_NAMES:
            if d[n].ndim == 1:
                d[n] = d[n][None]
    lsum, grad_x, res = _step(x[0], mem[0], loss_target[0], w, m, v)
    loss = lax.psum(0.5 / D_MODEL * lsum, ("x", "y", "c"))
    outs = [loss, grad_x[None]]
    for q in range(4):
        outs += [res[n][q].reshape(full_shapes[n]) for n in WEIGHT_NAMES]
    return tuple(outs)
```

```python
import functools
import math

import numpy as np
import jax
import jax.numpy as jnp
from jax import lax
from jax.experimental import pallas as pl
from jax.experimental.pallas import tpu as pltpu

F32 = jnp.float32
MXU = jnp.bfloat16
S = jax.ShapeDtypeStruct

D_MODEL = 1024
HEAD_DIM = 128
ATTN_GROUPS = ((128, 1), (512, 4), (2048, 16))
N_GROUPS = 3
HEADS = 4
GW = HEADS * HEAD_DIM
CONV_WIDTH = 31
N_MEM = 256
D_FF = 2816
FFN_CONV_WIDTH = 3
N_BUCKETS = 32
MAX_DISTANCE = 2048
RMS_EPS = 1e-6
LN_EPS = 1e-5
N_IN = 9216
NCB = N_IN // GW
BLK = 128
SCALE = HEAD_DIM ** -0.5
NEG = -1e30
N_DEV = 8

ADAM_LR, ADAM_B1, ADAM_B2, ADAM_EPS, ADAM_WD, ADAM_STEP = 0.001, 0.9, 0.999, 1e-08, 0.01, 10

VMEM_LIMIT = 48 * 1024 * 1024
CONV_HALO = 32
FFN_HALO = 8
ADAMW_BLOCK_BYTES = 4 * 1024 * 1024


def _cparams(*sem):
    return pltpu.CompilerParams(dimension_semantics=sem or None, vmem_limit_bytes=VMEM_LIMIT)


def _bs(shape, imap):
    return pl.BlockSpec(shape, imap)


def _dot(a, b):
    return lax.dot_general(a.astype(MXU), b.astype(MXU), (((1,), (0,)), ((), ())), preferred_element_type=F32)


def _dot_nt(a, b):
    return lax.dot_general(a.astype(MXU), b.astype(MXU), (((1,), (1,)), ((), ())), preferred_element_type=F32)


def _dot_tn(a, b):
    return lax.dot_general(a.astype(MXU), b.astype(MXU), (((0,), (0,)), ((), ())), preferred_element_type=F32)


def _sigmoid(x):
    return 0.5 * jnp.tanh(0.5 * x) + 0.5


def _rmsn(x, w):
    r = lax.rsqrt(jnp.mean(x * x, axis=-1, keepdims=True) + RMS_EPS)
    return x * r * w, r


def _rmsn_bwd(x, r, w, dy):
    g = dy * w
    dx = r * g - x * (r * r * r) * jnp.mean(x * g, axis=-1, keepdims=True)
    dw = jnp.sum(dy * x * r, axis=0, keepdims=True)
    return dx, dw


def _acc_out(ref, val, first):
    @pl.when(first)
    def _():
        ref[...] = val

    @pl.when(jnp.logical_not(first))
    def _():
        ref[...] += val


def _rms_fwd(x, w, name):
    T, D = x.shape
    tm = min(512, T)

    def body(x_ref, w_ref, o_ref):
        y, _ = _rmsn(x_ref[...], w_ref[...])
        o_ref[...] = y.astype(o_ref.dtype)

    return pl.pallas_call(
        body, grid=(T // tm,),
        in_specs=[_bs((tm, D), lambda i: (i, 0)), _bs((1, D), lambda i: (0, 0))],
        out_specs=_bs((tm, D), lambda i: (i, 0)),
        out_shape=S((T, D), MXU), name=name, compiler_params=_cparams("parallel"))(x, w)


def _rms_bwd(x, w, dys, resid, name):
    T, D = x.shape
    tm = min(512, T)
    n = len(dys)

    def body(*refs):
        x_ref, w_ref, res_ref = refs[0], refs[1], refs[2 + n]
        dx_ref, dw_ref = refs[3 + n], refs[4 + n]
        xv = x_ref[...]
        dy = refs[2][...]
        for dy_ref in refs[3:2 + n]:
            dy = dy + dy_ref[...]
        _, r = _rmsn(xv, w_ref[...])
        dx, dw = _rmsn_bwd(xv, r, w_ref[...], dy)
        dx_ref[...] = res_ref[...] + dx
        _acc_out(dw_ref, dw, pl.program_id(0) == 0)

    row = _bs((tm, D), lambda i: (i, 0))
    vec = _bs((1, D), lambda i: (0, 0))
    return pl.pallas_call(
        body, grid=(T // tm,), in_specs=[row, vec] + [row] * (n + 1), out_specs=[row, vec],
        out_shape=[S((T, D), F32), S((1, D), F32)], name=name, compiler_params=_cparams("arbitrary"))(x, w, *dys, resid)


def _pick(n, cands):
    for c in cands:
        if n % c == 0:
            return c
    return n


MM_VMEM_BUDGET = 36 * 1024 * 1024


def _mm_tiles(tm, N, K, a_bytes, b_bytes, o_bytes, has_res):
    best = None
    for tn in (1536, 1024, 1408, 512, 256, 128):
        for tk in (3072, 1536, 1024, 1408, 512, 256, 128):
            if N % tn or K % tk:
                continue
            nk = K // tk
            need = 2 * (tm * tk * a_bytes + tk * tn * b_bytes + tm * tn * (o_bytes + 4 * has_res)) + (nk > 1) * tm * tn * 4
            if need > MM_VMEM_BUDGET:
                continue
            key = ((N // tn) * nk, nk)
            if best is None or key < best[0]:
                best = (key, tn, tk)
    if best is None:
        return _pick(N, (128,)), _pick(K, (128,))
    return best[1], best[2]


def _matmul(a, b, *, ta=False, tb=False, out_dtype=F32, residual=None, after=None, tm=None, tn=None, tk=None, name):
    M, K = (a.shape[1], a.shape[0]) if ta else a.shape
    N = b.shape[0] if tb else b.shape[1]
    tm = tm or _pick(M, (1024, 1408, 512, 256, 128))
    if tn is None or tk is None:
        tn, tk = _mm_tiles(tm, N, K, a.dtype.itemsize, b.dtype.itemsize, jnp.dtype(out_dtype).itemsize, residual is not None)
    nk = K // tk
    dn = (((0 if ta else 1,), (1 if tb else 0,)), ((), ()))
    has_res = residual is not None
    n_in = 2 + has_res + (after is not None)

    def body(*refs):
        a_ref, b_ref = refs[0], refs[1]
        res_ref = refs[2] if has_res else None
        o_ref = refs[n_in]
        p = lax.dot_general(a_ref[...].astype(MXU), b_ref[...].astype(MXU), dn, preferred_element_type=F32)

        def finish(acc):
            if has_res:
                acc = acc + res_ref[...]
            o_ref[...] = acc.astype(o_ref.dtype)

        if nk == 1:
            finish(p)
        else:
            acc_ref = refs[-1]
            k = pl.program_id(2)

            @pl.when(k == 0)
            def _():
                acc_ref[...] = p

            @pl.when(k > 0)
            def _():
                acc_ref[...] += p

            @pl.when(k == nk - 1)
            def _():
                finish(acc_ref[...])

    a_spec = _bs((tk, tm), lambda i, j, k: (k, i)) if ta else _bs((tm, tk), lambda i, j, k: (i, k))
    b_spec = _bs((tn, tk), lambda i, j, k: (j, k)) if tb else _bs((tk, tn), lambda i, j, k: (k, j))
    o_spec = _bs((tm, tn), lambda i, j, k: (i, j))
    in_specs = [a_spec, b_spec] + ([o_spec] if has_res else [])
    args = (a, b) + ((residual,) if has_res else ())
    if after is not None:
        in_specs.append(_bs((8, 128), lambda i, j, k: (0, 0)))
        args += (after,)
    return pl.pallas_call(
        body, grid=(M // tm, N // tn, nk), in_specs=in_specs, out_specs=o_spec,
        out_shape=S((M, N), out_dtype), scratch_shapes=[pltpu.VMEM((tm, tn), F32)] if nk > 1 else [],
        name=name, compiler_params=_cparams("parallel", "parallel", "arbitrary"))(*args)


def _bucket_matrix(dilation):
    n = BLK
    qi = np.arange(n)[:, None]
    kj = np.arange(2 * n)[None, :]
    step = qi + n - kj
    dist = np.clip(step, 0, None) * dilation
    max_exact = N_BUCKETS // 2
    d = np.maximum(dist.astype(np.float32), np.float32(1.0))
    large = max_exact + (np.log(d / np.float32(max_exact)) / np.float32(math.log(MAX_DISTANCE / max_exact))
                         * np.float32(N_BUCKETS - max_exact)).astype(np.int32)
    large = np.minimum(large, N_BUCKETS - 1)
    bucket = np.where(dist < max_exact, dist, large)
    band = (step >= 0) & (step <= n)
    return np.where(band, bucket, -1).astype(np.int32)


def _build_bias(tbl_ref, bkt_ref, bias_ref, g):
    bk = bkt_ref[...]
    for h in range(HEADS):
        acc = jnp.full(bk.shape, NEG, F32)
        for b in range(N_BUCKETS):
            acc = jnp.where(bk == b, tbl_ref[b, HEADS * g + h], acc)
        bias_ref[h] = acc


def _to_classes(a, dil):
    if dil == 1:
        return a
    t, n = a.shape
    return a.reshape(t // dil, dil, n).transpose(1, 0, 2).reshape(t, n)


def _from_classes(a, dil):
    if dil == 1:
        return a
    t, n = a.shape
    return a.reshape(dil, t // dil, n).transpose(1, 0, 2).reshape(t, n)


def _attn_fwd(zq, table, qw, kw, g, dil):
    T = zq.shape[0]
    nb = T // dil // BLK
    qb = _pick(nb, (4, 2, 1))
    nt = nb // qb
    bkt = jnp.asarray(_bucket_matrix(dil))

    def zspec(part, prev):
        if prev:
            return _bs((BLK, GW), lambda c, i: (c * nb + jnp.maximum(i * qb - 1, 0), part))
        return _bs((qb * BLK, GW), lambda c, i: (c * nt + i, part))

    def body(tbl_ref, bkt_ref, qw_ref, kw_ref, q_ref, kp_ref, kc_ref, vp_ref, vc_ref, o_ref, lse_ref, bias_ref):
        c, i = pl.program_id(0), pl.program_id(1)

        @pl.when((c == 0) & (i == 0))
        def _():
            _build_bias(tbl_ref, bkt_ref, bias_ref, g)

        kj = lax.broadcasted_iota(jnp.int32, (BLK, 2 * BLK), 1)
        no_prev = jnp.logical_and(i == 0, kj < BLK)
        for h in range(HEADS):
            sl = slice(h * HEAD_DIM, (h + 1) * HEAD_DIM)
            qn, _ = _rmsn(q_ref[:, sl], qw_ref[...])
            kn, _ = _rmsn(jnp.concatenate([kp_ref[:, sl], kc_ref[:, sl]], axis=0), kw_ref[...])
            v = jnp.concatenate([vp_ref[:, sl], vc_ref[:, sl]], axis=0)
            for j in range(qb):
                rows = slice(j * BLK, (j + 1) * BLK)
                keys = slice(j * BLK, (j + 2) * BLK)
                s = _dot_nt(qn[rows], kn[keys]) * SCALE + bias_ref[h]
                if j == 0:
                    s = jnp.where(no_prev, NEG, s)
                m = jnp.max(s, axis=-1, keepdims=True)
                p = jnp.exp(s - m)
                l = jnp.sum(p, axis=-1, keepdims=True)
                o_ref[rows, sl] = _dot(p, v[keys]) / l
                lse_ref[rows, sl] = jnp.broadcast_to(m + jnp.log(l), (BLK, HEAD_DIM))

    ospec = _bs((qb * BLK, GW), lambda c, i: (c * nt + i, 0))
    vec = _bs((1, HEAD_DIM), lambda c, i: (0, 0))
    return pl.pallas_call(
        body, grid=(dil, nt),
        in_specs=[pl.BlockSpec(memory_space=pltpu.SMEM), _bs((BLK, 2 * BLK), lambda c, i: (0, 0)), vec, vec,
                  zspec(0, False), zspec(1, True), zspec(1, False), zspec(2, True), zspec(2, False)],
        out_specs=[ospec, ospec],
        out_shape=[S((T, GW), F32), S((T, GW), F32)],
        scratch_shapes=[pltpu.VMEM((HEADS, BLK, 2 * BLK), F32)],
        name=f"attn_fwd_g{g}", compiler_params=_cparams("arbitrary", "arbitrary"))(table, bkt, qw, kw, zq, zq, zq, zq, zq)


def _attn_merge(os_, lses):
    T = os_[0].shape[0]
    tm = min(512, T)

    def body(o0, o1, o2, l0, l1, l2, a_ref, lse_ref):
        ls = [l0[...], l1[...], l2[...]]
        mx = jnp.maximum(jnp.maximum(ls[0], ls[1]), ls[2])
        tot = mx + jnp.log(jnp.exp(ls[0] - mx) + jnp.exp(ls[1] - mx) + jnp.exp(ls[2] - mx))
        a_ref[...] = (jnp.exp(ls[0] - tot) * o0[...] + jnp.exp(ls[1] - tot) * o1[...] + jnp.exp(ls[2] - tot) * o2[...])
        lse_ref[...] = tot

    row = _bs((tm, GW), lambda i: (i, 0))
    return pl.pallas_call(
        body, grid=(T // tm,), in_specs=[row] * 6, out_specs=[row, row],
        out_shape=[S((T, GW), F32), S((T, GW), F32)], name="attn_merge",
        compiler_params=_cparams("parallel"))(*os_, *lses)


def _attn_bwd(zq, table, qw, kw, d_attn, attn, lse, g, dil):
    T = zq.shape[0]
    nb = T // dil // BLK
    bkt = jnp.asarray(_bucket_matrix(dil))

    def zspec(part, prev):
        if prev:
            return _bs((BLK, GW), lambda c, i: (c * nb + jnp.clip(i - 1, 0, nb - 1), part))
        return _bs((BLK, GW), lambda c, i: (c * nb + jnp.minimum(i, nb - 1), part))

    def body(tbl_ref, bkt_ref, qw_ref, kw_ref, q_ref, kp_ref, kc_ref, vp_ref, vc_ref, da_ref, at_ref, lse_ref,
             dz_ref, dqw_ref, dkw_ref, dtab_ref, bias_ref, dbias_ref, cq_ref, ck_ref, cv_ref):
        c, i = pl.program_id(0), pl.program_id(1)

        @pl.when((c == 0) & (i == 0))
        def _():
            _build_bias(tbl_ref, bkt_ref, bias_ref, g)
            dbias_ref[...] = jnp.zeros_like(dbias_ref)
            dqw_ref[...] = jnp.zeros_like(dqw_ref)
            dkw_ref[...] = jnp.zeros_like(dkw_ref)

        @pl.when(i == 0)
        def _():
            cq_ref[...] = jnp.zeros_like(cq_ref)
            ck_ref[...] = jnp.zeros_like(ck_ref)
            cv_ref[...] = jnp.zeros_like(cv_ref)

        @pl.when(i < nb)
        def _():
            kj = lax.broadcasted_iota(jnp.int32, (BLK, 2 * BLK), 1)
            no_prev = jnp.logical_and(i == 0, kj < BLK)
            dqw_acc = jnp.zeros((1, HEAD_DIM), F32)
            dkw_acc = jnp.zeros((1, HEAD_DIM), F32)
            for h in range(HEADS):
                sl = slice(h * HEAD_DIM, (h + 1) * HEAD_DIM)
                qh = q_ref[:, sl]
                k = jnp.concatenate([kp_ref[:, sl], kc_ref[:, sl]], axis=0)
                qn, rq = _rmsn(qh, qw_ref[...])
                kn, rk = _rmsn(k, kw_ref[...])
                s = _dot_nt(qn, kn) * SCALE + bias_ref[h]
                s = jnp.where(no_prev, NEG, s)
                p = jnp.exp(s - lse_ref[:, h * HEAD_DIM:h * HEAD_DIM + 1])
                do = da_ref[:, sl]
                delta = jnp.sum(do * at_ref[:, sl], axis=-1, keepdims=True)
                v = jnp.concatenate([vp_ref[:, sl], vc_ref[:, sl]], axis=0)
                ds = p * (_dot_nt(do, v) - delta)
                dbias_ref[h] += ds
                dv = _dot_tn(p, do)
                dqn = _dot(ds, kn) * SCALE
                dkn = _dot_tn(ds, qn) * SCALE
                dq, dqw = _rmsn_bwd(qh, rq, qw_ref[...], dqn)
                dk, dkw = _rmsn_bwd(k, rk, kw_ref[...], dkn)
                dqw_acc += dqw
                dkw_acc += dkw
                lo = h * HEAD_DIM
                dz_ref[:, lo:lo + HEAD_DIM] = cq_ref[:, sl].astype(dz_ref.dtype)
                dz_ref[:, GW + lo:GW + lo + HEAD_DIM] = (ck_ref[:, sl] + dk[:BLK]).astype(dz_ref.dtype)
                dz_ref[:, 2 * GW + lo:2 * GW + lo + HEAD_DIM] = (cv_ref[:, sl] + dv[:BLK]).astype(dz_ref.dtype)
                cq_ref[:, sl] = dq
                ck_ref[:, sl] = dk[BLK:]
                cv_ref[:, sl] = dv[BLK:]
            dqw_ref[...] += dqw_acc
            dkw_ref[...] += dkw_acc

        @pl.when(i == nb)
        def _():
            dz_ref[:, 0:GW] = cq_ref[...].astype(dz_ref.dtype)
            dz_ref[:, GW:2 * GW] = ck_ref[...].astype(dz_ref.dtype)
            dz_ref[:, 2 * GW:3 * GW] = cv_ref[...].astype(dz_ref.dtype)

        @pl.when((c == dil - 1) & (i == nb))
        def _():
            bk = bkt_ref[...]
            rows = lax.broadcasted_iota(jnp.int32, (N_BUCKETS, HEAD_DIM), 0)
            lanes = lax.broadcasted_iota(jnp.int32, (N_BUCKETS, HEAD_DIM), 1)
            out = jnp.zeros((N_BUCKETS, HEAD_DIM), F32)
            for h in range(HEADS):
                acc = dbias_ref[h]
                for b in range(N_BUCKETS):
                    val = jnp.sum(jnp.where(bk == b, acc, 0.0))
                    out = jnp.where((rows == b) & (lanes == h), val, out)
            dtab_ref[...] = out

    cur = _bs((BLK, GW), lambda c, i: (c * nb + jnp.minimum(i, nb - 1), 0))
    prv = _bs((BLK, 3 * GW), lambda c, i: (c * nb + jnp.maximum(i - 1, 0), 0))
    vec = _bs((1, HEAD_DIM), lambda c, i: (0, 0))
    tabs = _bs((N_BUCKETS, HEAD_DIM), lambda c, i: (0, 0))
    dzq, dqw, dkw, dtab = pl.pallas_call(
        body, grid=(dil, nb + 1),
        in_specs=[pl.BlockSpec(memory_space=pltpu.SMEM), _bs((BLK, 2 * BLK), lambda c, i: (0, 0)), vec, vec,
                  zspec(0, False), zspec(1, True), zspec(1, False), zspec(2, True), zspec(2, False), cur, cur, cur],
        out_specs=[prv, vec, vec, tabs],
        out_shape=[S((T, 3 * GW), MXU)] + [S((1, HEAD_DIM), F32)] * 2 + [S((N_BUCKETS, HEAD_DIM), F32)],
        scratch_shapes=[pltpu.VMEM((HEADS, BLK, 2 * BLK), F32), pltpu.VMEM((HEADS, BLK, 2 * BLK), F32),
                        pltpu.VMEM((BLK, GW), F32), pltpu.VMEM((BLK, GW), F32), pltpu.VMEM((BLK, GW), F32)],
        name=f"attn_bwd_g{g}", compiler_params=_cparams("arbitrary", "arbitrary"))(
            table, bkt, qw, kw, zq, zq, zq, zq, zq, d_attn, attn, lse)
    return dzq, dqw, dkw, dtab[:, :HEADS]


def _mem_fwd(mem, mem_norm_w, w_mem_kv, xk_w):
    def body(mem_ref, nw_ref, w_ref, xk_ref, mk_ref, mv_ref):
        mn, _ = _rmsn(mem_ref[...], nw_ref[...])
        kv = _dot(mn, w_ref[...])
        for h in range(HEADS):
            sl = slice(h * HEAD_DIM, (h + 1) * HEAD_DIM)
            kn, _ = _rmsn(kv[:, sl], xk_ref[...])
            mk_ref[:, sl] = kn.astype(mk_ref.dtype)
        mv_ref[...] = kv[:, GW:].astype(mv_ref.dtype)

    return pl.pallas_call(body, out_shape=[S((N_MEM, GW), MXU), S((N_MEM, GW), MXU)], name="mem_fwd",
                          compiler_params=_cparams())(mem, mem_norm_w, w_mem_kv, xk_w)


def _mem_bwd(mem, mem_norm_w, w_mem_kv, xk_w, dmk, dmv):
    def body(mem_ref, nw_ref, w_ref, xk_ref, dmk_ref, dmv_ref, dw_ref, dnw_ref, dxk_ref):
        memv = mem_ref[...]
        mn, r = _rmsn(memv, nw_ref[...])
        kv = _dot(mn, w_ref[...])
        dxk = jnp.zeros((1, HEAD_DIM), F32)
        parts = []
        for h in range(HEADS):
            sl = slice(h * HEAD_DIM, (h + 1) * HEAD_DIM)
            kh = kv[:, sl]
            _, rk = _rmsn(kh, xk_ref[...])
            dk, dw = _rmsn_bwd(kh, rk, xk_ref[...], dmk_ref[:, sl])
            dxk += dw
            parts.append(dk)
        dkv = jnp.concatenate(parts + [dmv_ref[...]], axis=1)
        dw_ref[...] = _dot_tn(mn, dkv)
        dmn = _dot_nt(dkv, w_ref[...])
        dnw_ref[...] = jnp.sum(dmn * memv * r, axis=0, keepdims=True)
        dxk_ref[...] = dxk

    return pl.pallas_call(
        body, out_shape=[S((D_MODEL, 2 * GW), F32), S((1, D_MODEL), F32), S((1, HEAD_DIM), F32)], name="mem_bwd",
        compiler_params=_cparams())(mem, mem_norm_w, w_mem_kv, xk_w, dmk, dmv)


def _cross_fwd(z, mk, mv, xq_w):
    T = z.shape[0]
    tm = min(512, T)

    def body(q_ref, mk_ref, mv_ref, w_ref, o_ref):
        for h in range(HEADS):
            sl = slice(h * HEAD_DIM, (h + 1) * HEAD_DIM)
            qn, _ = _rmsn(q_ref[:, sl], w_ref[...])
            s = _dot_nt(qn, mk_ref[:, sl]) * SCALE
            e = jnp.exp(s - jnp.max(s, axis=-1, keepdims=True))
            p = e / jnp.sum(e, axis=-1, keepdims=True)
            o_ref[:, sl] = _dot(p, mv_ref[:, sl]).astype(o_ref.dtype)

    full = _bs((N_MEM, GW), lambda i: (0, 0))
    return pl.pallas_call(
        body, grid=(T // tm,),
        in_specs=[_bs((tm, GW), lambda i: (i, 2)), full, full, _bs((1, HEAD_DIM), lambda i: (0, 0))],
        out_specs=_bs((tm, GW), lambda i: (i, 0)), out_shape=S((T, GW), MXU), name="cross_fwd",
        compiler_params=_cparams("parallel"))(z, mk, mv, xq_w)


def _cross_bwd(z, mk, mv, xq_w, d_cross):
    T = z.shape[0]
    tm = min(512, T)

    def body(q_ref, mk_ref, mv_ref, w_ref, do_ref, dq_ref, dmk_ref, dmv_ref, dw_ref):
        first = pl.program_id(0) == 0
        dw_acc = jnp.zeros((1, HEAD_DIM), F32)
        dmk_parts, dmv_parts = [], []
        for h in range(HEADS):
            sl = slice(h * HEAD_DIM, (h + 1) * HEAD_DIM)
            qh = q_ref[:, sl]
            qn, r = _rmsn(qh, w_ref[...])
            s = _dot_nt(qn, mk_ref[:, sl]) * SCALE
            e = jnp.exp(s - jnp.max(s, axis=-1, keepdims=True))
            p = e / jnp.sum(e, axis=-1, keepdims=True)
            do = do_ref[:, sl]
            dp = _dot_nt(do, mv_ref[:, sl])
            ds = p * (dp - jnp.sum(dp * p, axis=-1, keepdims=True)) * SCALE
            dmv_parts.append(_dot_tn(p, do))
            dmk_parts.append(_dot_tn(ds, qn))
            dq, dw = _rmsn_bwd(qh, r, w_ref[...], _dot(ds, mk_ref[:, sl]))
            dw_acc += dw
            dq_ref[:, sl] = dq.astype(dq_ref.dtype)
        _acc_out(dmk_ref, jnp.concatenate(dmk_parts, axis=1), first)
        _acc_out(dmv_ref, jnp.concatenate(dmv_parts, axis=1), first)
        _acc_out(dw_ref, dw_acc, first)

    full = _bs((N_MEM, GW), lambda i: (0, 0))
    vec = _bs((1, HEAD_DIM), lambda i: (0, 0))
    row = _bs((tm, GW), lambda i: (i, 0))
    return pl.pallas_call(
        body, grid=(T // tm,),
        in_specs=[_bs((tm, GW), lambda i: (i, 2)), full, full, vec, row],
        out_specs=[row, full, full, vec],
        out_shape=[S((T, GW), MXU), S((N_MEM, GW), F32), S((N_MEM, GW), F32), S((1, HEAD_DIM), F32)],
        name="cross_bwd", compiler_params=_cparams("arbitrary"))(z, mk, mv, xq_w, d_cross)


SUBLANES = 8


def _row_windows(ext, first, count, rows, shift_ref=None):
    if shift_ref is None:
        for j in range(count):
            yield j, ext[first + j:first + j + rows, :]
        return
    for b in range(SUBLANES):
        js = [j for j in range(count) if (first + j) % SUBLANES == b]
        if not js:
            continue
        span = max(first + j for j in js) - b + rows
        shift_ref[b, 0:span, :] = ext[b:b + span, :]
        for j in js:
            a = first + j - b
            yield j, shift_ref[b, a:a + rows, :]


def _taps(ext, w_ref, width, base, rows, shift_ref=None):
    acc = None
    for k, win in _row_windows(ext, base - (width - 1), width, rows, shift_ref):
        term = win * w_ref[k:k + 1, :]
        acc = term if acc is None else acc + term
    return acc


def _taps_bwd(d_ext, x, w_ref, width, rows, shift_ref=None):
    acc = None
    dw = [None] * width
    for j, win in _row_windows(d_ext, 0, width, rows, shift_ref):
        k = width - 1 - j
        term = win * w_ref[k:k + 1, :]
        acc = term if acc is None else acc + term
        dw[k] = jnp.sum(win * x, axis=0, keepdims=True)
    return acc, jnp.concatenate(dw, axis=0)


def _conv_fwd(z, cw, cb, lw, lb):
    T = z.shape[0]
    tm = min(512, T)
    hb = tm // CONV_HALO

    def body(val_ref, gate_ref, hval_ref, hgate_ref, cw_ref, cb_ref, lw_ref, lb_ref, o_ref, shift_ref):
        i = pl.program_id(0)
        halo = hval_ref[...] * _sigmoid(hgate_ref[...])
        halo = jnp.where(i == 0, 0.0, halo)
        ext = jnp.concatenate([halo, val_ref[...] * _sigmoid(gate_ref[...])], axis=0)
        y = _taps(ext, cw_ref, CONV_WIDTH, CONV_HALO, tm, shift_ref) + cb_ref[...]
        xc = y - jnp.mean(y, axis=-1, keepdims=True)
        a = xc * lax.rsqrt(jnp.mean(xc * xc, axis=-1, keepdims=True) + LN_EPS) * lw_ref[...] + lb_ref[...]
        o_ref[...] = (a * _sigmoid(a)).astype(o_ref.dtype)

    vec = _bs((1, GW), lambda i: (0, 0))
    halo_spec = lambda col: _bs((CONV_HALO, GW), lambda i: (jnp.maximum(i * hb - 1, 0), col))
    return pl.pallas_call(
        body, grid=(T // tm,),
        in_specs=[_bs((tm, GW), lambda i: (i, 0)), _bs((tm, GW), lambda i: (i, 1)), halo_spec(0), halo_spec(1),
                  _bs((CONV_WIDTH, GW), lambda i: (0, 0)), vec, vec, vec],
        out_specs=_bs((tm, GW), lambda i: (i, 0)), out_shape=S((T, GW), MXU),
        scratch_shapes=[pltpu.VMEM((SUBLANES, tm + CONV_HALO, GW), F32)], name="conv_fwd",
        compiler_params=_cparams("parallel"))(z, z, z, z, cw, cb, lw, lb)


def _conv_bwd(z, cw, cb, lw, lb, d_u):
    T = z.shape[0]
    tm = min(512, T)
    hb = tm // CONV_HALO
    nt = T // tm
    H = CONV_HALO

    def body(val_ref, gate_ref, pval_ref, pgate_ref, nval_ref, ngate_ref, du_ref, ndu_ref, cw_ref, cb_ref, lw_ref,
             lb_ref, dval_ref, dgate_ref, dcw_ref, dcb_ref, dlw_ref, dlb_ref, shift_ref):
        i = pl.program_id(0)
        first = i == 0
        val = jnp.concatenate([pval_ref[...] * jnp.where(first, 0.0, 1.0), val_ref[...], nval_ref[...]], axis=0)
        sg = _sigmoid(jnp.concatenate([pgate_ref[...], gate_ref[...], ngate_ref[...]], axis=0))
        u0 = val * sg
        y = _taps(u0, cw_ref, CONV_WIDTH, H, tm + H, shift_ref) + cb_ref[...]
        xc = y - jnp.mean(y, axis=-1, keepdims=True)
        rs = lax.rsqrt(jnp.mean(xc * xc, axis=-1, keepdims=True) + LN_EPS)
        nh = xc * rs
        a = nh * lw_ref[...] + lb_ref[...]
        sa = _sigmoid(a)
        du = jnp.concatenate([du_ref[...], ndu_ref[...] * jnp.where(i == nt - 1, 0.0, 1.0)], axis=0)
        da = du * (sa * (1.0 + a * (1.0 - sa)))
        dn = da * lw_ref[...]
        dy = rs * (dn - jnp.mean(dn, axis=-1, keepdims=True) - nh * jnp.mean(dn * nh, axis=-1, keepdims=True))
        du0, dcw = _taps_bwd(dy, u0[H:H + tm], cw_ref, CONV_WIDTH, tm, shift_ref)
        v0, s0 = val[H:H + tm], sg[H:H + tm]
        dval_ref[...] = (du0 * s0).astype(dval_ref.dtype)
        dgate_ref[...] = (du0 * v0 * s0 * (1.0 - s0)).astype(dgate_ref.dtype)
        dy0 = dy[:tm]
        _acc_out(dcw_ref, dcw, first)
        _acc_out(dcb_ref, jnp.sum(dy0, axis=0, keepdims=True), first)
        _acc_out(dlw_ref, jnp.sum(da[:tm] * nh[:tm], axis=0, keepdims=True), first)
        _acc_out(dlb_ref, jnp.sum(da[:tm], axis=0, keepdims=True), first)

    vec = _bs((1, GW), lambda i: (0, 0))
    cwspec = _bs((CONV_WIDTH, GW), lambda i: (0, 0))
    prev = lambda col: _bs((H, GW), lambda i: (jnp.maximum(i * hb - 1, 0), col))
    nxt = lambda col: _bs((H, GW), lambda i: (jnp.minimum((i + 1) * hb, nt * hb - 1), col))
    row = _bs((tm, GW), lambda i: (i, 0))
    return pl.pallas_call(
        body, grid=(nt,),
        in_specs=[_bs((tm, GW), lambda i: (i, 0)), _bs((tm, GW), lambda i: (i, 1)), prev(0), prev(1), nxt(0), nxt(1),
                  row, nxt(0), cwspec, vec, vec, vec],
        out_specs=[row, row, cwspec, vec, vec, vec],
        out_shape=[S((T, GW), MXU), S((T, GW), MXU), S((CONV_WIDTH, GW), F32)] + [S((1, GW), F32)] * 3,
        scratch_shapes=[pltpu.VMEM((SUBLANES, tm + 2 * H, GW), F32)], name="conv_bwd", compiler_params=_cparams("arbitrary"))(z, z, z, z, z, z, d_u, d_u, cw, cb, lw, lb)


def _ffn_act_fwd(up0, fw, fb):
    T = up0.shape[0]
    tm = min(256, T)
    hb = tm // FFN_HALO
    H = FFN_HALO

    def body(a_ref, g_ref, pa_ref, pg_ref, wa_ref, wg_ref, ba_ref, bg_ref, o_ref):
        i = pl.program_id(0)
        keep = jnp.where(i == 0, 0.0, 1.0)
        ea = jnp.concatenate([pa_ref[...] * keep, a_ref[...]], axis=0)
        eg = jnp.concatenate([pg_ref[...] * keep, g_ref[...]], axis=0)
        av = _taps(ea, wa_ref, FFN_CONV_WIDTH, H, tm) + ba_ref[...]
        gv = _taps(eg, wg_ref, FFN_CONV_WIDTH, H, tm) + bg_ref[...]
        o_ref[...] = (gv * _sigmoid(gv) * av).astype(o_ref.dtype)

    col = lambda j: _bs((tm, D_FF), lambda i: (i, j))
    prev = lambda j: _bs((H, D_FF), lambda i: (jnp.maximum(i * hb - 1, 0), j))
    wspec = lambda j: _bs((FFN_CONV_WIDTH, D_FF), lambda i: (0, j))
    bspec = lambda j: _bs((1, D_FF), lambda i: (0, j))
    return pl.pallas_call(
        body, grid=(T // tm,),
        in_specs=[col(0), col(1), prev(0), prev(1), wspec(0), wspec(1), bspec(0), bspec(1)],
        out_specs=_bs((tm, D_FF), lambda i: (i, 0)), out_shape=S((T, D_FF), MXU), name="ffn_act_fwd",
        compiler_params=_cparams("parallel"))(up0, up0, up0, up0, fw, fw, fb, fb)


def _ffn_act_bwd(up0, fw, fb, d_f):
    T = up0.shape[0]
    tm = min(256, T)
    hb = tm // FFN_HALO
    nt = T // tm
    H = FFN_HALO
    W = FFN_CONV_WIDTH

    def body(a_ref, g_ref, pa_ref, pg_ref, na_ref, ng_ref, df_ref, ndf_ref, wa_ref, wg_ref, ba_ref, bg_ref,
             dup_ref, dw_ref, db_ref):
        i = pl.program_id(0)
        first = i == 0
        keep = jnp.where(first, 0.0, 1.0)
        ea = jnp.concatenate([pa_ref[...] * keep, a_ref[...], na_ref[...]], axis=0)
        eg = jnp.concatenate([pg_ref[...] * keep, g_ref[...], ng_ref[...]], axis=0)
        av = _taps(ea, wa_ref, W, H, tm + H) + ba_ref[...]
        gv = _taps(eg, wg_ref, W, H, tm + H) + bg_ref[...]
        df = jnp.concatenate([df_ref[...], ndf_ref[...] * jnp.where(i == nt - 1, 0.0, 1.0)], axis=0)
        sg = _sigmoid(gv)
        d_av = df * gv * sg
        d_gv = df * av * (sg * (1.0 + gv * (1.0 - sg)))
        dua, dwa = _taps_bwd(d_av, a_ref[...], wa_ref, W, tm)
        dug, dwg = _taps_bwd(d_gv, g_ref[...], wg_ref, W, tm)
        dup_ref[:, :D_FF] = dua.astype(dup_ref.dtype)
        dup_ref[:, D_FF:] = dug.astype(dup_ref.dtype)
        dw = jnp.concatenate([dwa, dwg], axis=1)
        db = jnp.concatenate([jnp.sum(d_av[:tm], axis=0, keepdims=True), jnp.sum(d_gv[:tm], axis=0, keepdims=True)], axis=1)
        _acc_out(dw_ref, dw, first)
        _acc_out(db_ref, db, first)

    col = lambda j: _bs((tm, D_FF), lambda i: (i, j))
    prev = lambda j: _bs((H, D_FF), lambda i: (jnp.maximum(i * hb - 1, 0), j))
    nxt = lambda j: _bs((H, D_FF), lambda i: (jnp.minimum((i + 1) * hb, nt * hb - 1), j))
    wspec = lambda j: _bs((W, D_FF), lambda i: (0, j))
    bspec = lambda j: _bs((1, D_FF), lambda i: (0, j))
    return pl.pallas_call(
        body, grid=(nt,),
        in_specs=[col(0), col(1), prev(0), prev(1), nxt(0), nxt(1), col(0), nxt(0), wspec(0), wspec(1), bspec(0), bspec(1)],
        out_specs=[_bs((tm, 2 * D_FF), lambda i: (i, 0)), _bs((W, 2 * D_FF), lambda i: (0, 0)),
                   _bs((1, 2 * D_FF), lambda i: (0, 0))],
        out_shape=[S((T, 2 * D_FF), MXU), S((W, 2 * D_FF), F32), S((1, 2 * D_FF), F32)],
        name="ffn_act_bwd", compiler_params=_cparams("arbitrary"))(
            up0, up0, up0, up0, up0, up0, d_f, d_f, fw, fw, fb, fb)


def _branch_fwd(attn, u, cross, z, b_gate, wa, wc, wx):
    T = z.shape[0]
    tm = min(512, T)

    def body(a_ref, u_ref, x_ref, g0_ref, g1_ref, g2_ref, b_ref, wa_ref, wc_ref, wx_ref, o_ref):
        acc = None
        for j, (act, g_ref, w_ref) in enumerate(((a_ref, g0_ref, wa_ref), (u_ref, g1_ref, wc_ref), (x_ref, g2_ref, wx_ref))):
            gate = _sigmoid(g_ref[...] + b_ref[:, j * D_MODEL:(j + 1) * D_MODEL])
            term = gate * _dot(act[...], w_ref[...])
            acc = term if acc is None else acc + term
        o_ref[...] = acc.astype(o_ref.dtype)

    act = _bs((tm, GW), lambda i: (i, 0))
    gcol = lambda j: _bs((tm, D_MODEL), lambda i: (i, j))
    wfull = _bs((GW, D_MODEL), lambda i: (0, 0))
    return pl.pallas_call(
        body, grid=(T // tm,),
        in_specs=[act, act, act, gcol(0), gcol(1), gcol(2), _bs((1, 3 * D_MODEL), lambda i: (0, 0)), wfull, wfull, wfull],
        out_specs=_bs((tm, D_MODEL), lambda i: (i, 0)), out_shape=S((T, D_MODEL), MXU), name="branch_fwd",
        compiler_params=_cparams("parallel"))(attn, u, cross, z, z, z, b_gate, wa, wc, wx)


def _branch_bwd(d_merged, attn, u, cross, z, b_gate, wa, wc, wx):
    T = z.shape[0]
    tm = min(512, T)

    def body(dm_ref, a_ref, u_ref, x_ref, g0_ref, g1_ref, g2_ref, b_ref, wa_ref, wc_ref, wx_ref,
             dzg_ref, da_ref, du_ref, dx_ref, dwa_ref, dwc_ref, dwx_ref, db_ref):
        first = pl.program_id(0) == 0
        dm = dm_ref[...]
        dbs = []
        for j, (act, g_ref, w_ref, dact_ref, dw_ref) in enumerate((
                (a_ref, g0_ref, wa_ref, da_ref, dwa_ref), (u_ref, g1_ref, wc_ref, du_ref, dwc_ref),
                (x_ref, g2_ref, wx_ref, dx_ref, dwx_ref))):
            av = act[...]
            gate = _sigmoid(g_ref[...] + b_ref[:, j * D_MODEL:(j + 1) * D_MODEL])
            y = _dot(av, w_ref[...])
            dzg = dm * y * gate * (1.0 - gate)
            dzg_ref[:, j * D_MODEL:(j + 1) * D_MODEL] = dzg.astype(dzg_ref.dtype)
            dbs.append(jnp.sum(dzg, axis=0, keepdims=True))
            dy = (gate * dm).astype(MXU)
            dact_ref[...] = _dot_nt(dy, w_ref[...])
            _acc_out(dw_ref, _dot_tn(av, dy), first)
        _acc_out(db_ref, jnp.concatenate(dbs, axis=1), first)

    act = _bs((tm, GW), lambda i: (i, 0))
    gcol = lambda j: _bs((tm, D_MODEL), lambda i: (i, j))
    wfull = _bs((GW, D_MODEL), lambda i: (0, 0))
    bvec = _bs((1, 3 * D_MODEL), lambda i: (0, 0))
    return pl.pallas_call(
        body, grid=(T // tm,),
        in_specs=[_bs((tm, D_MODEL), lambda i: (i, 0)), act, act, act, gcol(0), gcol(1), gcol(2), bvec, wfull, wfull, wfull],
        out_specs=[_bs((tm, 3 * D_MODEL), lambda i: (i, 0)), act, act, act, wfull, wfull, wfull, bvec],
        out_shape=[S((T, 3 * D_MODEL), MXU)] + [S((T, GW), F32)] * 3 + [S((GW, D_MODEL), F32)] * 3 + [S((1, 3 * D_MODEL), F32)],
        name="branch_bwd", compiler_params=_cparams("arbitrary"))(d_merged, attn, u, cross, z, z, z, b_gate, wa, wc, wx)


def _loss_head(y, target):
    T, D = y.shape
    tm = min(512, T)

    def body(y_ref, t_ref, dy_ref, l_ref):
        e = y_ref[...] - t_ref[...]
        dy_ref[...] = e * (1.0 / D)
        part = jnp.full((8, 128), jnp.sum(e * e), F32)
        _acc_out(l_ref, part, pl.program_id(0) == 0)

    row = _bs((tm, D), lambda i: (i, 0))
    return pl.pallas_call(
        body, grid=(T // tm,), in_specs=[row, row], out_specs=[row, _bs((8, 128), lambda i: (0, 0))],
        out_shape=[S((T, D), F32), S((8, 128), F32)], name="loss_head", compiler_params=_cparams("arbitrary"))(y, target)


def _peer(mask):
    x, y, c = lax.axis_index("x"), lax.axis_index("y"), lax.axis_index("c")
    px = 1 - x if mask & 4 else x
    py = 1 - y if mask & 2 else y
    pc = 1 - c if mask & 1 else c
    return (px, py, pc), 4 * px + 2 * py + pc


def _exchange(arrs, scatter, name):
    n = len(arrs)
    outs_shape = [S(a.shape if scatter else (N_DEV,) + a.shape, a.dtype) for a in arrs]

    def body(*refs):
        ins, outs = refs[:n], refs[n:2 * n]
        send_sems, recv_sems, local_sems = refs[2 * n:]
        me = 4 * lax.axis_index("x") + 2 * lax.axis_index("y") + lax.axis_index("c")
        copies = []
        for w in range(n):
            src = ins[w].at[me] if scatter else ins[w]
            cp = pltpu.make_async_copy(src, outs[w].at[me], local_sems.at[w])
            cp.start()
            copies.append(cp)
        for k in range(1, N_DEV):
            peer, pidx = _peer(k)
            for w in range(n):
                src = ins[w].at[pidx] if scatter else ins[w]
                cp = pltpu.make_async_remote_copy(
                    src_ref=src, dst_ref=outs[w].at[me], send_sem=send_sems.at[w, k - 1], recv_sem=recv_sems.at[w, k - 1],
                    device_id=peer, device_id_type=pl.DeviceIdType.MESH)
                cp.start()
                copies.append(cp)
        for cp in copies:
            cp.wait()

    hbm = pl.BlockSpec(memory_space=pl.ANY)
    return pl.pallas_call(
        body, in_specs=[hbm] * n, out_specs=[hbm] * n, out_shape=outs_shape,
        scratch_shapes=[pltpu.SemaphoreType.DMA((n, N_DEV - 1)), pltpu.SemaphoreType.DMA((n, N_DEV - 1)),
                        pltpu.SemaphoreType.DMA((n,))],
        name=name)(*arrs)


def _exchange_copies(ins, lands, send_sems, recv_sems, local_sems, scatter):
    n = len(ins)
    me = 4 * lax.axis_index("x") + 2 * lax.axis_index("y") + lax.axis_index("c")
    copies = []
    for w in range(n):
        src = ins[w].at[me] if scatter else ins[w]
        copies.append(pltpu.make_async_copy(src, lands[w].at[me], local_sems.at[w]))
    for k in range(1, N_DEV):
        peer, pidx = _peer(k)
        for w in range(n):
            src = ins[w].at[pidx] if scatter else ins[w]
            copies.append(pltpu.make_async_remote_copy(
                src_ref=src, dst_ref=lands[w].at[me], send_sem=send_sems.at[w * (N_DEV - 1) + k - 1],
                recv_sem=recv_sems.at[w * (N_DEV - 1) + k - 1],
                device_id=peer, device_id_type=pl.DeviceIdType.MESH))
    return copies


_HBM_SPEC = pl.BlockSpec(memory_space=pltpu.HBM)
_SEM_SPEC = pl.BlockSpec(memory_space=pltpu.SEMAPHORE)
_DATAFLOW = pltpu.SideEffectType.DATAFLOW_SIDE_EFFECTING


def _exchange_start(arrs, scatter, name):
    n = len(arrs)
    land_shapes = [a.shape if scatter else (N_DEV,) + a.shape for a in arrs]

    def body(*refs):
        ins, lands = refs[:n], refs[n:2 * n]
        send_sems, recv_sems, local_sems = refs[2 * n:2 * n + 3]
        token = refs[-1]
        for cp in _exchange_copies(ins, lands, send_sems, recv_sems, local_sems, scatter):
            cp.start()
        token[...] = jnp.zeros_like(token)

    out_shape = ([pltpu.SemaphoreType.DMA((n * (N_DEV - 1),)), pltpu.SemaphoreType.DMA((n * (N_DEV - 1),)),
                  pltpu.SemaphoreType.DMA((n,))]
                 + [pltpu.HBM(a.shape, a.dtype) for a in arrs]
                 + [pltpu.HBM(s, a.dtype) for s, a in zip(land_shapes, arrs)]
                 + [S((8, 128), F32)])
    args = ([pltpu.with_memory_space_constraint(a, pltpu.HBM) for a in arrs]
            + [pltpu.with_memory_space_constraint(lax.empty(s, a.dtype), pltpu.HBM) for s, a in zip(land_shapes, arrs)])
    outs = pl.pallas_call(
        body, in_specs=[_HBM_SPEC] * (2 * n),
        out_specs=[_SEM_SPEC] * 3 + [_HBM_SPEC] * (2 * n) + [pl.BlockSpec(memory_space=pltpu.VMEM)],
        out_shape=out_shape, input_output_aliases={j: 3 + j for j in range(2 * n)},
        name=name, compiler_params=pltpu.CompilerParams(has_side_effects=_DATAFLOW))(*args)
    return (n, scatter, outs[:3], outs[3:3 + n], outs[3 + n:3 + 2 * n]), outs[-1]


def _exchange_wait(state, after, name):
    n, scatter, sems, ins, lands = state

    def body(*refs):
        ins_r, lands_r = refs[:n], refs[n:2 * n]
        send_sems, recv_sems, local_sems = refs[2 * n:2 * n + 3]
        for cp in _exchange_copies(ins_r, lands_r, send_sems, recv_sems, local_sems, scatter):
            cp.wait()

    outs = pl.pallas_call(
        body, in_specs=[_HBM_SPEC] * (2 * n) + [_SEM_SPEC] * 3 + [pl.BlockSpec(memory_space=pl.ANY)],
        out_specs=[_HBM_SPEC] * (2 * n),
        out_shape=[pltpu.HBM(a.shape, a.dtype) for a in ins] + [pltpu.HBM(a.shape, a.dtype) for a in lands],
        input_output_aliases={j: j for j in range(2 * n)},
        name=name, compiler_params=pltpu.CompilerParams(has_side_effects=_DATAFLOW))(*ins, *lands, *sems, after)
    return list(outs[n:])


def _adamw(w, m, v, parts, name):
    R, C = w.shape
    P = parts.shape[0]
    tr = _pick(R, tuple(t for t in (256, 176, 128, 64, 32, 16, 8) if P * t * C * 4 <= ADAMW_BLOCK_BYTES))
    c1 = 1.0 / (1.0 - ADAM_B1 ** ADAM_STEP)
    c2 = 1.0 / (1.0 - ADAM_B2 ** ADAM_STEP)

    def body(w_ref, m_ref, v_ref, p_ref, g_ref, d_ref, nm_ref, nv_ref):
        g = p_ref[0].astype(F32)
        for j in range(1, P):
            g = g + p_ref[j].astype(F32)
        m2 = ADAM_B1 * m_ref[...] + (1.0 - ADAM_B1) * g
        v2 = ADAM_B2 * v_ref[...] + (1.0 - ADAM_B2) * (g * g)
        g_ref[...] = g
        nm_ref[...] = m2
        nv_ref[...] = v2
        d_ref[...] = -ADAM_LR * ((m2 * c1) / (jnp.sqrt(v2 * c2) + ADAM_EPS) + ADAM_WD * w_ref[...])

    row = _bs((tr, C), lambda i: (i, 0))
    return pl.pallas_call(
        body, grid=(R // tr,), in_specs=[row, row, row, _bs((P, tr, C), lambda i: (0, i, 0))], out_specs=[row] * 4,
        out_shape=[S((R, C), F32)] * 4, name=name, compiler_params=_cparams("parallel"))(w, m, v, parts)


def _sum_parts(parts, name):
    P, R, C = parts.shape

    def body(p_ref, o_ref):
        g = p_ref[0]
        for j in range(1, P):
            g = g + p_ref[j]
        o_ref[...] = g

    return pl.pallas_call(body, out_shape=S((R, C), F32), name=name, compiler_params=_cparams())(parts)


def _pack(arrs):
    flat = jnp.concatenate([a.reshape(-1) for a in arrs])
    rows = -(-flat.shape[0] // 1024) * 8
    return jnp.pad(flat, (0, rows * 128 - flat.shape[0])).reshape(rows, 128)


def _unpack(packed, shapes):
    flat = packed.reshape(-1)
    out, off = [], 0
    for s in shapes:
        n = int(np.prod(s))
        out.append(flat[off:off + n].reshape(s))
        off += n
    return out


def _behind(a, token):
    return a if token is None else a + token[0, 0]


def _local_step(x, mem, target, p, comm=None):
    table = p["rel_bias_table"]
    xn = _rms_fwd(x, p["attn_norm_w"], "attn_norm_fwd")
    dils = [dil for _, dil in ATTN_GROUPS]
    xn_c = [_to_classes(xn, dil) for dil in dils]
    w_in = p["w_in"]
    qkv_w = 3 * N_GROUPS * GW
    wq = [jnp.concatenate([w_in[:, (N_GROUPS * part + g) * GW:(N_GROUPS * part + g + 1) * GW] for part in range(3)], axis=1)
          for g in range(N_GROUPS)]
    wc = w_in[:, qkv_w:qkv_w + 3 * GW]
    wg = w_in[:, qkv_w + 3 * GW:]
    zq = [_matmul(xn_c[g], wq[g], name=f"mm_in_qkv{g}") for g in range(N_GROUPS)]
    zc = _matmul(xn, wc, name="mm_in_c")
    zg = _matmul(xn, wg, name="mm_in_g")
    os_, lses = [], []
    for g, dil in enumerate(dils):
        o, l = _attn_fwd(zq[g], table, p["q_norm_w"][g:g + 1], p["k_norm_w"][g:g + 1], g, dil)
        os_.append(_from_classes(o, dil))
        lses.append(_from_classes(l, dil))
    attn, lse = _attn_merge(os_, lses)
    u = _conv_fwd(zc, p["conv_dw_w"], p["conv_dw_b"], p["conv_ln_w"], p["conv_ln_b"])
    if comm is not None:
        p = {**p, **comm.late_weights(after=u)}
    mk, mv = _mem_fwd(mem, p["mem_norm_w"], p["w_mem_kv"], p["xk_norm_w"])
    cross = _cross_fwd(zc, mk, mv, p["xq_norm_w"])
    merged = _branch_fwd(attn, u, cross, zg, p["b_gate"], p["w_attn_o"], p["w_conv_o"], p["w_cross_o"])
    h1 = _matmul(merged, p["w_out"], residual=x, name="mm_out")
    hn = _rms_fwd(h1, p["ffn_norm_w"], "ffn_norm_fwd")
    up0 = _matmul(hn, p["w_up"], name="mm_up")
    f = _ffn_act_fwd(up0, p["ffn_conv_w"], p["ffn_conv_b"])
    h2 = _matmul(f, p["w_down"], residual=h1, name="mm_down")
    dh2, lsum = _loss_head(h2, target)
    g = {}
    d_f = _matmul(dh2, p["w_down"], tb=True, name="mm_down_dx")
    g["w_down"] = _matmul(f, dh2, ta=True, name="mm_down_dw")
    d_up0, g["ffn_conv_w"], g["ffn_conv_b"] = _ffn_act_bwd(up0, p["ffn_conv_w"], p["ffn_conv_b"], d_f)
    dhn = _matmul(d_up0, p["w_up"], tb=True, name="mm_up_dx")
    g["w_up"] = _matmul(hn, d_up0, ta=True, name="mm_up_dw")
    dh1, g["ffn_norm_w"] = _rms_bwd(h1, p["ffn_norm_w"], [dhn], dh2, "ffn_norm_bwd")
    d_merged = _matmul(dh1, p["w_out"], tb=True, name="mm_out_dx")
    g["w_out"] = _matmul(merged, dh1, ta=True, name="mm_out_dw")
    (d_zg, d_attn, d_u, d_cross, g["w_attn_o"], g["w_conv_o"], g["w_cross_o"], g["b_gate"]) = _branch_bwd(
        d_merged, attn, u, cross, zg, p["b_gate"], p["w_attn_o"], p["w_conv_o"], p["w_cross_o"])
    d_xq, dmk, dmv, g["xq_norm_w"] = _cross_bwd(zc, mk, mv, p["xq_norm_w"], d_cross)
    g["w_mem_kv"], g["mem_norm_w"], g["xk_norm_w"] = _mem_bwd(mem, p["mem_norm_w"], p["w_mem_kv"], p["xk_norm_w"], dmk, dmv)
    tok = comm.start_early_grads(g) if comm is not None else None
    d_val, d_gate, g["conv_dw_w"], g["conv_dw_b"], g["conv_ln_w"], g["conv_ln_b"] = _conv_bwd(
        zc, p["conv_dw_w"], _behind(p["conv_dw_b"], tok), p["conv_ln_w"], p["conv_ln_b"], d_u)
    dzq, dqw, dkw, dtab = [], [], [], []
    for gi, dil in enumerate(dils):
        r = _attn_bwd(zq[gi], table, p["q_norm_w"][gi:gi + 1], p["k_norm_w"][gi:gi + 1], _to_classes(d_attn, dil),
                      _to_classes(attn, dil), _to_classes(lse, dil), gi, dil)
        for lst, val in zip((dzq, dqw, dkw, dtab), r):
            lst.append(val)
    g["q_norm_w"] = jnp.concatenate(dqw, axis=0)
    g["k_norm_w"] = jnp.concatenate(dkw, axis=0)
    g["rel_bias_table"] = jnp.concatenate(dtab, axis=1)
    d_zc = jnp.concatenate([d_val, d_gate, d_xq], axis=1)
    gq = [_matmul(xn_c[gi], dzq[gi], ta=True, name=f"mm_in_qkv{gi}_dw") for gi in range(N_GROUPS)]
    gc = _matmul(xn, d_zc, ta=True, name="mm_in_c_dw")
    gg = _matmul(xn, d_zg, ta=True, name="mm_in_g_dw")
    g["w_in"] = jnp.concatenate(
        [gq[gi][:, part * GW:(part + 1) * GW] for part in range(3) for gi in range(N_GROUPS)] + [gc, gg], axis=1)
    tok = comm.start_w_in_grad(g["w_in"]) if comm is not None else None
    dxn = _matmul(d_zg, wg, tb=True, after=tok, name="mm_in_g_dx")
    dxn = _matmul(d_zc, wc, tb=True, residual=dxn, name="mm_in_c_dx")
    dxn = _matmul(dzq[0], wq[0], tb=True, residual=dxn, name="mm_in_qkv0_dx")
    dxs = [dxn] + [_from_classes(_matmul(dzq[gi], wq[gi], tb=True, name=f"mm_in_qkv{gi}_dx"), dils[gi])
                   for gi in range(1, N_GROUPS)]
    grad_x, g["attn_norm_w"] = _rms_bwd(x, p["attn_norm_w"], dxs, dh1, "attn_norm_bwd")
    return lsum[0, 0], grad_x, g


WEIGHT_NAMES = ["rel_bias_table", "attn_norm_w", "w_in", "b_gate", "q_norm_w", "k_norm_w", "w_attn_o", "conv_dw_w",
                "conv_dw_b", "conv_ln_w", "conv_ln_b", "w_conv_o", "mem_norm_w", "w_mem_kv", "xq_norm_w", "xk_norm_w",
                "w_cross_o", "w_out", "ffn_norm_w", "w_up", "ffn_conv_w", "ffn_conv_b", "w_down"]
COL_SHARDED = ("w_in", "w_attn_o", "w_conv_o", "w_cross_o", "w_up")
ROW_SHARDED = ("w_mem_kv", "w_out", "w_down")
SMALL_COL_SHARDED = ("conv_dw_w", "ffn_conv_w")
BIG = COL_SHARDED + ROW_SHARDED


def _cols_to_blocks(a):
    k, n8 = a.shape
    return a.reshape(k, N_DEV, n8 // N_DEV).transpose(1, 0, 2)


def _blocks_to_cols(a):
    return a.transpose(1, 0, 2).reshape(a.shape[1], N_DEV * a.shape[2])


def _step(x, mem, target, w, m, v):
    me = 4 * lax.axis_index("x") + 2 * lax.axis_index("y") + lax.axis_index("c")

    def to_full(n, blocks):
        return _blocks_to_cols(blocks) if n in COL_SHARDED + SMALL_COL_SHARDED else blocks.reshape(-1, blocks.shape[-1])

    def to_blocks(n, grad):
        blocks = _cols_to_blocks(grad) if n in COL_SHARDED else grad.reshape(N_DEV, -1, grad.shape[-1])
        return blocks.astype(MXU)

    first = ("w_in",) + SMALL_COL_SHARDED
    late = tuple(n for n in BIG if n != "w_in")
    cast = lambda n: w[n].astype(MXU) if n in BIG else w[n]
    first_state, _ = _exchange_start([cast(n) for n in first], False, "gather_first_start")
    late_state, late_token = _exchange_start([cast(n) for n in late], False, "gather_late_start")
    got = _exchange_wait(first_state, late_token, "gather_first_wait")
    p = {n: w[n] for n in WEIGHT_NAMES if n not in BIG + SMALL_COL_SHARDED}
    p.update({n: to_full(n, b) for n, b in zip(first, got)})

    class Comm:
        def late_weights(self, after):
            return {n: to_full(n, b) for n, b in zip(late, _exchange_wait(late_state, after, "gather_late_wait"))}

        def start_early_grads(self, g):
            self.early_state, token = _exchange_start([to_blocks(n, g[n]) for n in late], True, "scatter_early_start")
            return token

        def start_w_in_grad(self, grad):
            self.w_in_state, token = _exchange_start([to_blocks("w_in", grad)], True, "scatter_w_in_start")
            return token

    comm = Comm()
    lsum, grad_x, g = _local_step(x, mem, target, p, comm)
    small_names = [n for n in WEIGHT_NAMES if n not in BIG]
    small_shapes = [g[n].shape for n in small_names]
    small_parts = _exchange([_pack([g[n] for n in small_names])], False, "gather_small_grads")[0]
    gsmall = dict(zip(small_names, _unpack(_sum_parts(small_parts, "sum_small_grads"), small_shapes)))
    for n in SMALL_COL_SHARDED:
        width = w[n].shape[-1]
        gsmall[n] = lax.dynamic_slice_in_dim(gsmall[n], me * width, width, axis=1)
    res = {}
    parts = dict(zip(late, _exchange_wait(comm.early_state, grad_x, "scatter_early_wait")))
    for n in late:
        res[n] = _adamw(w[n], m[n], v[n], parts[n], "adamw_" + n)
    w_in_parts = _exchange_wait(comm.w_in_state, res[late[-1]][1], "scatter_w_in_wait")[0]
    res["w_in"] = _adamw(w["w_in"], m["w_in"], v["w_in"], w_in_parts, "adamw_w_in")
    shapes = [w[n].shape for n in small_names]
    packed = [_pack([d[n] for n in small_names]) for d in (w, m, v, gsmall)]
    outs = _adamw(packed[0], packed[1], packed[2], packed[3][None], "adamw_small")
    unpacked = [_unpack(o, shapes) for o in outs]
    for j, n in enumerate(small_names):
        res[n] = tuple(unpacked[q][j] for q in range(4))
    return lsum, grad_x, res


def kernel(x, mem, rel_bias_table, attn_norm_w, w_in, b_gate, q_norm_w, k_norm_w, w_attn_o, conv_dw_w, conv_dw_b, conv_ln_w, conv_ln_b, w_conv_o, mem_norm_w, w_mem_kv, xq_norm_w, xk_norm_w, w_cross_o, w_out, ffn_norm_w, w_up, ffn_conv_w, ffn_conv_b, w_down, loss_target, m_rel_bias_table, m_attn_norm_w, m_w_in, m_b_gate, m_q_norm_w, m_k_norm_w, m_w_attn_o, m_conv_dw_w, m_conv_dw_b, m_conv_ln_w, m_conv_ln_b, m_w_conv_o, m_mem_norm_w, m_w_mem_kv, m_xq_norm_w, m_xk_norm_w, m_w_cross_o, m_w_out, m_ffn_norm_w, m_w_up, m_ffn_conv_w, m_ffn_conv_b, m_w_down, v_rel_bias_table, v_attn_norm_w, v_w_in, v_b_gate, v_q_norm_w, v_k_norm_w, v_w_attn_o, v_conv_dw_w, v_conv_dw_b, v_conv_ln_w, v_conv_ln_b, v_w_conv_o, v_mem_norm_w, v_w_mem_kv, v_xq_norm_w, v_xk_norm_w, v_w_cross_o, v_w_out, v_ffn_norm_w, v_w_up, v_ffn_conv_w, v_ffn_conv_b, v_w_down):
    ws = dict(zip(WEIGHT_NAMES, (rel_bias_table, attn_norm_w, w_in, b_gate, q_norm_w, k_norm_w, w_attn_o, conv_dw_w, conv_dw_b, conv_ln_w, conv_ln_b, w_conv_o, mem_norm_w, w_mem_kv, xq_norm_w, xk_norm_w, w_cross_o, w_out, ffn_norm_w, w_up, ffn_conv_w, ffn_conv_b, w_down)))
    ms = dict(zip(WEIGHT_NAMES, (m_rel_bias_table, m_attn_norm_w, m_w_in, m_b_gate, m_q_norm_w, m_k_norm_w, m_w_attn_o, m_conv_dw_w, m_conv_dw_b, m_conv_ln_w, m_conv_ln_b, m_w_conv_o, m_mem_norm_w, m_w_mem_kv, m_xq_norm_w, m_xk_norm_w, m_w_cross_o, m_w_out, m_ffn_norm_w, m_w_up, m_ffn_conv_w, m_ffn_conv_b, m_w_down)))
    vs = dict(zip(WEIGHT_NAMES, (v_rel_bias_table, v_attn_norm_w, v_w_in, v_b_gate, v_q_norm_w, v_k_norm_w, v_w_attn_o, v_conv_dw_w, v_conv_dw_b, v_conv_ln_w, v_conv_ln_b, v_w_conv_o, v_mem_norm_w, v_w_mem_kv, v_xq_norm_w, v_xk_norm_w, v_w_cross_o, v_w_out, v_ffn_norm_w, v_w_up, v_ffn_conv_w, v_ffn_conv_b, v_w_down)))
    full_shapes = {n: ws[n].shape for n in WEIGHT_NAMES}

    def squeeze(d):
        return {n: (a if n == "rel_bias_table" else a[0]) for n, a in d.items()}

    w, m, v = squeeze(ws), squeeze(ms), squeeze(vs)
    for d in (w, m, v):
        for n in WEIGHT_NAMES:
            if d[n].ndim == 1:
                d[n] = d[n][None]
    lsum, grad_x, res = _step(x[0], mem[0], loss_target[0], w, m, v)
    loss = lax.psum(0.5 / D_MODEL * lsum, ("x", "y", "c"))
    outs = [loss, grad_x[None]]
    for q in range(4):
        outs += [res[n][q].reshape(full_shapes[n]) for n in WEIGHT_NAMES]
    return tuple(outs)
```

```python
import functools
import math

import numpy as np
import jax
import jax.numpy as jnp
from jax import lax
from jax.experimental import pallas as pl
from jax.experimental.pallas import tpu as pltpu

F32 = jnp.float32
MXU = jnp.bfloat16
S = jax.ShapeDtypeStruct

D_MODEL = 1024
HEAD_DIM = 128
ATTN_GROUPS = ((128, 1), (512, 4), (2048, 16))
N_GROUPS = 3
HEADS = 4
GW = HEADS * HEAD_DIM
CONV_WIDTH = 31
N_MEM = 256
D_FF = 2816
FFN_CONV_WIDTH = 3
N_BUCKETS = 32
MAX_DISTANCE = 2048
RMS_EPS = 1e-6
LN_EPS = 1e-5
N_IN = 9216
NCB = N_IN // GW
BLK = 128
SCALE = HEAD_DIM ** -0.5
NEG = -1e30
N_DEV = 8

ADAM_LR, ADAM_B1, ADAM_B2, ADAM_EPS, ADAM_WD, ADAM_STEP = 0.001, 0.9, 0.999, 1e-08, 0.01, 10

VMEM_LIMIT = 48 * 1024 * 1024
CONV_HALO = 32
FFN_HALO = 8
ADAMW_BLOCK_BYTES = 4 * 1024 * 1024


def _cparams(*sem):
    return pltpu.CompilerParams(dimension_semantics=sem or None, vmem_limit_bytes=VMEM_LIMIT)


def _bs(shape, imap):
    return pl.BlockSpec(shape, imap)


def _dot(a, b):
    return lax.dot_general(a.astype(MXU), b.astype(MXU), (((1,), (0,)), ((), ())), preferred_element_type=F32)


def _dot_nt(a, b):
    return lax.dot_general(a.astype(MXU), b.astype(MXU), (((1,), (1,)), ((), ())), preferred_element_type=F32)


def _dot_tn(a, b):
    return lax.dot_general(a.astype(MXU), b.astype(MXU), (((0,), (0,)), ((), ())), preferred_element_type=F32)


def _sigmoid(x):
    return 0.5 * jnp.tanh(0.5 * x) + 0.5


def _rmsn(x, w):
    r = lax.rsqrt(jnp.mean(x * x, axis=-1, keepdims=True) + RMS_EPS)
    return x * r * w, r


def _rmsn_bwd(x, r, w, dy):
    g = dy * w
    dx = r * g - x * (r * r * r) * jnp.mean(x * g, axis=-1, keepdims=True)
    dw = jnp.sum(dy * x * r, axis=0, keepdims=True)
    return dx, dw


def _acc_out(ref, val, first):
    @pl.when(first)
    def _():
        ref[...] = val

    @pl.when(jnp.logical_not(first))
    def _():
        ref[...] += val


def _rms_fwd(x, w, name):
    T, D = x.shape
    tm = min(512, T)

    def body(x_ref, w_ref, o_ref):
        y, _ = _rmsn(x_ref[...], w_ref[...])
        o_ref[...] = y.astype(o_ref.dtype)

    return pl.pallas_call(
        body, grid=(T // tm,),
        in_specs=[_bs((tm, D), lambda i: (i, 0)), _bs((1, D), lambda i: (0, 0))],
        out_specs=_bs((tm, D), lambda i: (i, 0)),
        out_shape=S((T, D), MXU), name=name, compiler_params=_cparams("parallel"))(x, w)


def _rms_bwd(x, w, dys, resid, name):
    T, D = x.shape
    tm = min(512, T)
    n = len(dys)

    def body(*refs):
        x_ref, w_ref, res_ref = refs[0], refs[1], refs[2 + n]
        dx_ref, dw_ref = refs[3 + n], refs[4 + n]
        xv = x_ref[...]
        dy = refs[2][...]
        for dy_ref in refs[3:2 + n]:
            dy = dy + dy_ref[...]
        _, r = _rmsn(xv, w_ref[...])
        dx, dw = _rmsn_bwd(xv, r, w_ref[...], dy)
        dx_ref[...] = res_ref[...] + dx
        _acc_out(dw_ref, dw, pl.program_id(0) == 0)

    row = _bs((tm, D), lambda i: (i, 0))
    vec = _bs((1, D), lambda i: (0, 0))
    return pl.pallas_call(
        body, grid=(T // tm,), in_specs=[row, vec] + [row] * (n + 1), out_specs=[row, vec],
        out_shape=[S((T, D), F32), S((1, D), F32)], name=name, compiler_params=_cparams("arbitrary"))(x, w, *dys, resid)


def _pick(n, cands):
    for c in cands:
        if n % c == 0:
            return c
    return n


MM_VMEM_BUDGET = 36 * 1024 * 1024


def _mm_tiles(tm, N, K, a_bytes, b_bytes, o_bytes, has_res):
    best = None
    for tn in (1536, 1024, 1408, 512, 256, 128):
        for tk in (3072, 1536, 1024, 1408, 512, 256, 128):
            if N % tn or K % tk:
                continue
            nk = K // tk
            need = 2 * (tm * tk * a_bytes + tk * tn * b_bytes + tm * tn * (o_bytes + 4 * has_res)) + (nk > 1) * tm * tn * 4
            if need > MM_VMEM_BUDGET:
                continue
            key = ((N // tn) * nk, nk)
            if best is None or key < best[0]:
                best = (key, tn, tk)
    if best is None:
        return _pick(N, (128,)), _pick(K, (128,))
    return best[1], best[2]


def _matmul(a, b, *, ta=False, tb=False, out_dtype=F32, residual=None, after=None, tm=None, tn=None, tk=None, name):
    M, K = (a.shape[1], a.shape[0]) if ta else a.shape
    N = b.shape[0] if tb else b.shape[1]
    tm = tm or _pick(M, (1024, 1408, 512, 256, 128))
    if tn is None or tk is None:
        tn, tk = _mm_tiles(tm, N, K, a.dtype.itemsize, b.dtype.itemsize, jnp.dtype(out_dtype).itemsize, residual is not None)
    nk = K // tk
    dn = (((0 if ta else 1,), (1 if tb else 0,)), ((), ()))
    has_res = residual is not None
    n_in = 2 + has_res + (after is not None)

    def body(*refs):
        a_ref, b_ref = refs[0], refs[1]
        res_ref = refs[2] if has_res else None
        o_ref = refs[n_in]
        p = lax.dot_general(a_ref[...].astype(MXU), b_ref[...].astype(MXU), dn, preferred_element_type=F32)

        def finish(acc):
            if has_res:
                acc = acc + res_ref[...]
            o_ref[...] = acc.astype(o_ref.dtype)

        if nk == 1:
            finish(p)
        else:
            acc_ref = refs[-1]
            k = pl.program_id(2)

            @pl.when(k == 0)
            def _():
                acc_ref[...] = p

            @pl.when(k > 0)
            def _():
                acc_ref[...] += p

            @pl.when(k == nk - 1)
            def _():
                finish(acc_ref[...])

    a_spec = _bs((tk, tm), lambda i, j, k: (k, i)) if ta else _bs((tm, tk), lambda i, j, k: (i, k))
    b_spec = _bs((tn, tk), lambda i, j, k: (j, k)) if tb else _bs((tk, tn), lambda i, j, k: (k, j))
    o_spec = _bs((tm, tn), lambda i, j, k: (i, j))
    in_specs = [a_spec, b_spec] + ([o_spec] if has_res else [])
    args = (a, b) + ((residual,) if has_res else ())
    if after is not None:
        in_specs.append(_bs((8, 128), lambda i, j, k: (0, 0)))
        args += (after,)
    return pl.pallas_call(
        body, grid=(M // tm, N // tn, nk), in_specs=in_specs, out_specs=o_spec,
        out_shape=S((M, N), out_dtype), scratch_shapes=[pltpu.VMEM((tm, tn), F32)] if nk > 1 else [],
        name=name, compiler_params=_cparams("parallel", "parallel", "arbitrary"))(*args)


def _bucket_matrix(dilation):
    n = BLK
    qi = np.arange(n)[:, None]
    kj = np.arange(2 * n)[None, :]
    step = qi + n - kj
    dist = np.clip(step, 0, None) * dilation
    max_exact = N_BUCKETS // 2
    d = np.maximum(dist.astype(np.float32), np.float32(1.0))
    large = max_exact + (np.log(d / np.float32(max_exact)) / np.float32(math.log(MAX_DISTANCE / max_exact))
                         * np.float32(N_BUCKETS - max_exact)).astype(np.int32)
    large = np.minimum(large, N_BUCKETS - 1)
    bucket = np.where(dist < max_exact, dist, large)
    band = (step >= 0) & (step <= n)
    return np.where(band, bucket, -1).astype(np.int32)


def _build_bias(tbl_ref, bkt_ref, bias_ref, g):
    bk = bkt_ref[...]
    for h in range(HEADS):
        acc = jnp.full(bk.shape, NEG, F32)
        for b in range(N_BUCKETS):
            acc = jnp.where(bk == b, tbl_ref[b, HEADS * g + h], acc)
        bias_ref[h] = acc


def _to_classes(a, dil):
    if dil == 1:
        return a
    t, n = a.shape
    return a.reshape(t // dil, dil, n).transpose(1, 0, 2).reshape(t, n)


def _from_classes(a, dil):
    if dil == 1:
        return a
    t, n = a.shape
    return a.reshape(dil, t // dil, n).transpose(1, 0, 2).reshape(t, n)


def _attn_fwd(zq, table, qw, kw, g, dil):
    T = zq.shape[0]
    nb = T // dil // BLK
    qb = _pick(nb, (4, 2, 1))
    nt = nb // qb
    bkt = jnp.asarray(_bucket_matrix(dil))

    def zspec(part, prev):
        if prev:
            return _bs((BLK, GW), lambda c, i: (c * nb + jnp.maximum(i * qb - 1, 0), part))
        return _bs((qb * BLK, GW), lambda c, i: (c * nt + i, part))

    def body(tbl_ref, bkt_ref, qw_ref, kw_ref, q_ref, kp_ref, kc_ref, vp_ref, vc_ref, o_ref, lse_ref, bias_ref):
        c, i = pl.program_id(0), pl.program_id(1)

        @pl.when((c == 0) & (i == 0))
        def _():
            _build_bias(tbl_ref, bkt_ref, bias_ref, g)

        kj = lax.broadcasted_iota(jnp.int32, (BLK, 2 * BLK), 1)
        no_prev = jnp.logical_and(i == 0, kj < BLK)
        for h in range(HEADS):
            sl = slice(h * HEAD_DIM, (h + 1) * HEAD_DIM)
            qn, _ = _rmsn(q_ref[:, sl], qw_ref[...])
            kn, _ = _rmsn(jnp.concatenate([kp_ref[:, sl], kc_ref[:, sl]], axis=0), kw_ref[...])
            v = jnp.concatenate([vp_ref[:, sl], vc_ref[:, sl]], axis=0)
            for j in range(qb):
                rows = slice(j * BLK, (j + 1) * BLK)
                keys = slice(j * BLK, (j + 2) * BLK)
                s = _dot_nt(qn[rows], kn[keys]) * SCALE + bias_ref[h]
                if j == 0:
                    s = jnp.where(no_prev, NEG, s)
                m = jnp.max(s, axis=-1, keepdims=True)
                p = jnp.exp(s - m)
                l = jnp.sum(p, axis=-1, keepdims=True)
                o_ref[rows, sl] = _dot(p, v[keys]) / l
                lse_ref[rows, sl] = jnp.broadcast_to(m + jnp.log(l), (BLK, HEAD_DIM))

    ospec = _bs((qb * BLK, GW), lambda c, i: (c * nt + i, 0))
    vec = _bs((1, HEAD_DIM), lambda c, i: (0, 0))
    return pl.pallas_call(
        body, grid=(dil, nt),
        in_specs=[pl.BlockSpec(memory_space=pltpu.SMEM), _bs((BLK, 2 * BLK), lambda c, i: (0, 0)), vec, vec,
                  zspec(0, False), zspec(1, True), zspec(1, False), zspec(2, True), zspec(2, False)],
        out_specs=[ospec, ospec],
        out_shape=[S((T, GW), F32), S((T, GW), F32)],
        scratch_shapes=[pltpu.VMEM((HEADS, BLK, 2 * BLK), F32)],
        name=f"attn_fwd_g{g}", compiler_params=_cparams("arbitrary", "arbitrary"))(table, bkt, qw, kw, zq, zq, zq, zq, zq)


def _attn_merge(os_, lses):
    T = os_[0].shape[0]
    tm = min(512, T)

    def body(o0, o1, o2, l0, l1, l2, a_ref, lse_ref):
        ls = [l0[...], l1[...], l2[...]]
        mx = jnp.maximum(jnp.maximum(ls[0], ls[1]), ls[2])
        tot = mx + jnp.log(jnp.exp(ls[0] - mx) + jnp.exp(ls[1] - mx) + jnp.exp(ls[2] - mx))
        a_ref[...] = (jnp.exp(ls[0] - tot) * o0[...] + jnp.exp(ls[1] - tot) * o1[...] + jnp.exp(ls[2] - tot) * o2[...])
        lse_ref[...] = tot

    row = _bs((tm, GW), lambda i: (i, 0))
    return pl.pallas_call(
        body, grid=(T // tm,), in_specs=[row] * 6, out_specs=[row, row],
        out_shape=[S((T, GW), F32), S((T, GW), F32)], name="attn_merge",
        compiler_params=_cparams("parallel"))(*os_, *lses)


def _attn_bwd(zq, table, qw, kw, d_attn, attn, lse, g, dil):
    T = zq.shape[0]
    nb = T // dil // BLK
    qb = _pick(nb, (4, 2, 1))
    nt = nb // qb
    bkt = jnp.asarray(_bucket_matrix(dil))

    def body(tbl_ref, bkt_ref, qw_ref, kw_ref, q_ref, k_ref, v_ref, kp_ref, vp_ref, qx_ref, da_ref, at_ref, lse_ref,
             dax_ref, atx_ref, lsex_ref, dz_ref, dqw_ref, dkw_ref, dtab_ref, bias_ref, dbias_ref):
        c, i = pl.program_id(0), pl.program_id(1)

        @pl.when((c == 0) & (i == 0))
        def _():
            _build_bias(tbl_ref, bkt_ref, bias_ref, g)
            dbias_ref[...] = jnp.zeros_like(dbias_ref)
            dqw_ref[...] = jnp.zeros_like(dqw_ref)
            dkw_ref[...] = jnp.zeros_like(dkw_ref)

        kj = lax.broadcasted_iota(jnp.int32, (BLK, 2 * BLK), 1)
        no_prev = jnp.logical_and(i == 0, kj < BLK)
        has_next = i < nt - 1
        last = slice((qb - 1) * BLK, qb * BLK)
        dqw_acc = jnp.zeros((1, HEAD_DIM), F32)
        dkw_acc = jnp.zeros((1, HEAD_DIM), F32)

        def add(parts, t, val):
            parts[t] = val if parts[t] is None else parts[t] + val

        for h in range(HEADS):
            lo = h * HEAD_DIM
            sl = slice(lo, lo + HEAD_DIM)
            q, k = q_ref[:, sl], k_ref[:, sl]
            qn, rq = _rmsn(q, qw_ref[...])
            kn, rk = _rmsn(k, kw_ref[...])
            kpn, _ = _rmsn(kp_ref[:, sl], kw_ref[...])
            kn_ext = jnp.concatenate([kpn, kn], axis=0)
            v_ext = jnp.concatenate([vp_ref[:, sl], v_ref[:, sl]], axis=0)
            dqn, dkn, dv = [None] * qb, [None] * qb, [None] * qb
            for j in range(qb):
                rows = slice(j * BLK, (j + 1) * BLK)
                keys = slice(j * BLK, (j + 2) * BLK)
                s = _dot_nt(qn[rows], kn_ext[keys]) * SCALE + bias_ref[h]
                if j == 0:
                    s = jnp.where(no_prev, NEG, s)
                p = jnp.exp(s - lse_ref[rows, lo:lo + 1])
                do = da_ref[rows, sl]
                delta = jnp.sum(do * at_ref[rows, sl], axis=-1, keepdims=True)
                ds = p * (_dot_nt(do, v_ext[keys]) - delta)
                dbias_ref[h] += ds
                dqn[j] = _dot(ds, kn_ext[keys]) * SCALE
                dv2 = _dot_tn(p, do)
                dk2 = _dot_tn(ds, qn[rows]) * SCALE
                if j >= 1:
                    add(dv, j - 1, dv2[:BLK])
                    add(dkn, j - 1, dk2[:BLK])
                add(dv, j, dv2[BLK:])
                add(dkn, j, dk2[BLK:])
            qxn, _ = _rmsn(qx_ref[:, sl], qw_ref[...])
            sx = _dot_nt(qxn, kn[last]) * SCALE + bias_ref[h, :, 0:BLK]
            px = jnp.where(has_next, jnp.exp(sx - lsex_ref[:, lo:lo + 1]), 0.0)
            dox = dax_ref[:, sl]
            dsx = px * (_dot_nt(dox, v_ref[last, sl]) - jnp.sum(dox * atx_ref[:, sl], axis=-1, keepdims=True))
            add(dv, qb - 1, _dot_tn(px, dox))
            add(dkn, qb - 1, _dot_tn(dsx, qxn) * SCALE)
            dq, dqw = _rmsn_bwd(q, rq, qw_ref[...], jnp.concatenate(dqn, axis=0))
            dk, dkw = _rmsn_bwd(k, rk, kw_ref[...], jnp.concatenate(dkn, axis=0))
            dqw_acc += dqw
            dkw_acc += dkw
            dz_ref[:, lo:lo + HEAD_DIM] = dq.astype(dz_ref.dtype)
            dz_ref[:, GW + lo:GW + lo + HEAD_DIM] = dk.astype(dz_ref.dtype)
            dz_ref[:, 2 * GW + lo:2 * GW + lo + HEAD_DIM] = jnp.concatenate(dv, axis=0).astype(dz_ref.dtype)
        dqw_ref[...] += dqw_acc
        dkw_ref[...] += dkw_acc

        @pl.when((c == dil - 1) & (i == nt - 1))
        def _():
            bk = bkt_ref[...]
            rows = lax.broadcasted_iota(jnp.int32, (N_BUCKETS, HEAD_DIM), 0)
            lanes = lax.broadcasted_iota(jnp.int32, (N_BUCKETS, HEAD_DIM), 1)
            out = jnp.zeros((N_BUCKETS, HEAD_DIM), F32)
            for h in range(HEADS):
                acc = dbias_ref[h]
                for b in range(N_BUCKETS):
                    val = jnp.sum(jnp.where(bk == b, acc, 0.0))
                    out = jnp.where((rows == b) & (lanes == h), val, out)
            dtab_ref[...] = out

    tile = lambda part: _bs((qb * BLK, GW), lambda c, i: (c * nt + i, part))
    before = lambda part: _bs((BLK, GW), lambda c, i: (c * nb + jnp.maximum(i * qb - 1, 0), part))
    after = lambda part: _bs((BLK, GW), lambda c, i: (c * nb + jnp.minimum((i + 1) * qb, nb - 1), part))
    vec = _bs((1, HEAD_DIM), lambda c, i: (0, 0))
    tabs = _bs((N_BUCKETS, HEAD_DIM), lambda c, i: (0, 0))
    dzq, dqw, dkw, dtab = pl.pallas_call(
        body, grid=(dil, nt),
        in_specs=[pl.BlockSpec(memory_space=pltpu.SMEM), _bs((BLK, 2 * BLK), lambda c, i: (0, 0)), vec, vec,
                  tile(0), tile(1), tile(2), before(1), before(2), after(0), tile(0), tile(0), tile(0),
                  after(0), after(0), after(0)],
        out_specs=[_bs((qb * BLK, 3 * GW), lambda c, i: (c * nt + i, 0)), vec, vec, tabs],
        out_shape=[S((T, 3 * GW), MXU)] + [S((1, HEAD_DIM), F32)] * 2 + [S((N_BUCKETS, HEAD_DIM), F32)],
        scratch_shapes=[pltpu.VMEM((HEADS, BLK, 2 * BLK), F32), pltpu.VMEM((HEADS, BLK, 2 * BLK), F32)],
        name=f"attn_bwd_g{g}", compiler_params=_cparams("arbitrary", "arbitrary"))(
            table, bkt, qw, kw, zq, zq, zq, zq, zq, zq, d_attn, attn, lse, d_attn, attn, lse)
    return dzq, dqw, dkw, dtab[:, :HEADS]


def _mem_fwd(mem, mem_norm_w, w_mem_kv, xk_w):
    def body(mem_ref, nw_ref, w_ref, xk_ref, mk_ref, mv_ref):
        mn, _ = _rmsn(mem_ref[...], nw_ref[...])
        kv = _dot(mn, w_ref[...])
        for h in range(HEADS):
            sl = slice(h * HEAD_DIM, (h + 1) * HEAD_DIM)
            kn, _ = _rmsn(kv[:, sl], xk_ref[...])
            mk_ref[:, sl] = kn.astype(mk_ref.dtype)
        mv_ref[...] = kv[:, GW:].astype(mv_ref.dtype)

    return pl.pallas_call(body, out_shape=[S((N_MEM, GW), MXU), S((N_MEM, GW), MXU)], name="mem_fwd",
                          compiler_params=_cparams())(mem, mem_norm_w, w_mem_kv, xk_w)


def _mem_bwd(mem, mem_norm_w, w_mem_kv, xk_w, dmk, dmv):
    def body(mem_ref, nw_ref, w_ref, xk_ref, dmk_ref, dmv_ref, dw_ref, dnw_ref, dxk_ref):
        memv = mem_ref[...]
        mn, r = _rmsn(memv, nw_ref[...])
        kv = _dot(mn, w_ref[...])
        dxk = jnp.zeros((1, HEAD_DIM), F32)
        parts = []
        for h in range(HEADS):
            sl = slice(h * HEAD_DIM, (h + 1) * HEAD_DIM)
            kh = kv[:, sl]
            _, rk = _rmsn(kh, xk_ref[...])
            dk, dw = _rmsn_bwd(kh, rk, xk_ref[...], dmk_ref[:, sl])
            dxk += dw
            parts.append(dk)
        dkv = jnp.concatenate(parts + [dmv_ref[...]], axis=1)
        dw_ref[...] = _dot_tn(mn, dkv)
        dmn = _dot_nt(dkv, w_ref[...])
        dnw_ref[...] = jnp.sum(dmn * memv * r, axis=0, keepdims=True)
        dxk_ref[...] = dxk

    return pl.pallas_call(
        body, out_shape=[S((D_MODEL, 2 * GW), F32), S((1, D_MODEL), F32), S((1, HEAD_DIM), F32)], name="mem_bwd",
        compiler_params=_cparams())(mem, mem_norm_w, w_mem_kv, xk_w, dmk, dmv)


def _cross_fwd(z, mk, mv, xq_w):
    T = z.shape[0]
    tm = min(512, T)

    def body(q_ref, mk_ref, mv_ref, w_ref, o_ref):
        for h in range(HEADS):
            sl = slice(h * HEAD_DIM, (h + 1) * HEAD_DIM)
            qn, _ = _rmsn(q_ref[:, sl], w_ref[...])
            s = _dot_nt(qn, mk_ref[:, sl]) * SCALE
            e = jnp.exp(s - jnp.max(s, axis=-1, keepdims=True))
            p = e / jnp.sum(e, axis=-1, keepdims=True)
            o_ref[:, sl] = _dot(p, mv_ref[:, sl]).astype(o_ref.dtype)

    full = _bs((N_MEM, GW), lambda i: (0, 0))
    return pl.pallas_call(
        body, grid=(T // tm,),
        in_specs=[_bs((tm, GW), lambda i: (i, 2)), full, full, _bs((1, HEAD_DIM), lambda i: (0, 0))],
        out_specs=_bs((tm, GW), lambda i: (i, 0)), out_shape=S((T, GW), MXU), name="cross_fwd",
        compiler_params=_cparams("parallel"))(z, mk, mv, xq_w)


def _cross_bwd(z, mk, mv, xq_w, d_cross):
    T = z.shape[0]
    tm = min(512, T)

    def body(q_ref, mk_ref, mv_ref, w_ref, do_ref, dq_ref, dmk_ref, dmv_ref, dw_ref):
        first = pl.program_id(0) == 0
        dw_acc = jnp.zeros((1, HEAD_DIM), F32)
        dmk_parts, dmv_parts = [], []
        for h in range(HEADS):
            sl = slice(h * HEAD_DIM, (h + 1) * HEAD_DIM)
            qh = q_ref[:, sl]
            qn, r = _rmsn(qh, w_ref[...])
            s = _dot_nt(qn, mk_ref[:, sl]) * SCALE
            e = jnp.exp(s - jnp.max(s, axis=-1, keepdims=True))
            p = e / jnp.sum(e, axis=-1, keepdims=True)
            do = do_ref[:, sl]
            dp = _dot_nt(do, mv_ref[:, sl])
            ds = p * (dp - jnp.sum(dp * p, axis=-1, keepdims=True)) * SCALE
            dmv_parts.append(_dot_tn(p, do))
            dmk_parts.append(_dot_tn(ds, qn))
            dq, dw = _rmsn_bwd(qh, r, w_ref[...], _dot(ds, mk_ref[:, sl]))
            dw_acc += dw
            dq_ref[:, sl] = dq.astype(dq_ref.dtype)
        _acc_out(dmk_ref, jnp.concatenate(dmk_parts, axis=1), first)
        _acc_out(dmv_ref, jnp.concatenate(dmv_parts, axis=1), first)
        _acc_out(dw_ref, dw_acc, first)

    full = _bs((N_MEM, GW), lambda i: (0, 0))
    vec = _bs((1, HEAD_DIM), lambda i: (0, 0))
    row = _bs((tm, GW), lambda i: (i, 0))
    return pl.pallas_call(
        body, grid=(T // tm,),
        in_specs=[_bs((tm, GW), lambda i: (i, 2)), full, full, vec, row],
        out_specs=[row, full, full, vec],
        out_shape=[S((T, GW), MXU), S((N_MEM, GW), F32), S((N_MEM, GW), F32), S((1, HEAD_DIM), F32)],
        name="cross_bwd", compiler_params=_cparams("arbitrary"))(z, mk, mv, xq_w, d_cross)


SUBLANES = 8


def _row_windows(ext, first, count, rows, shift_ref=None):
    for b in range(SUBLANES):
        js = [j for j in range(count) if (first + j) % SUBLANES == b]
        if not js:
            continue
        span = max(first + j for j in js) - b + rows
        shifted = ext[b:b + span, :]
        if shift_ref is not None:
            shift_ref[b, 0:span, :] = shifted
        for j in js:
            a = first + j - b
            yield j, (shifted[a:a + rows, :] if shift_ref is None else shift_ref[b, a:a + rows, :])


def _taps(ext, w_ref, width, base, rows, shift_ref=None):
    acc = None
    for k, win in _row_windows(ext, base - (width - 1), width, rows, shift_ref):
        term = win * w_ref[k:k + 1, :]
        acc = term if acc is None else acc + term
    return acc


def _taps_bwd(d_ext, x, w_ref, width, rows, shift_ref=None):
    acc = None
    dw = [None] * width
    for j, win in _row_windows(d_ext, 0, width, rows, shift_ref):
        k = width - 1 - j
        term = win * w_ref[k:k + 1, :]
        acc = term if acc is None else acc + term
        dw[k] = jnp.sum(win * x, axis=0, keepdims=True)
    return acc, jnp.concatenate(dw, axis=0)


def _conv_fwd(z, cw, cb, lw, lb):
    T = z.shape[0]
    tm = min(512, T)
    hb = tm // CONV_HALO

    def body(val_ref, gate_ref, hval_ref, hgate_ref, cw_ref, cb_ref, lw_ref, lb_ref, o_ref, shift_ref):
        i = pl.program_id(0)
        halo = hval_ref[...] * _sigmoid(hgate_ref[...])
        halo = jnp.where(i == 0, 0.0, halo)
        ext = jnp.concatenate([halo, val_ref[...] * _sigmoid(gate_ref[...])], axis=0)
        y = _taps(ext, cw_ref, CONV_WIDTH, CONV_HALO, tm, shift_ref) + cb_ref[...]
        xc = y - jnp.mean(y, axis=-1, keepdims=True)
        a = xc * lax.rsqrt(jnp.mean(xc * xc, axis=-1, keepdims=True) + LN_EPS) * lw_ref[...] + lb_ref[...]
        o_ref[...] = (a * _sigmoid(a)).astype(o_ref.dtype)

    vec = _bs((1, GW), lambda i: (0, 0))
    halo_spec = lambda col: _bs((CONV_HALO, GW), lambda i: (jnp.maximum(i * hb - 1, 0), col))
    return pl.pallas_call(
        body, grid=(T // tm,),
        in_specs=[_bs((tm, GW), lambda i: (i, 0)), _bs((tm, GW), lambda i: (i, 1)), halo_spec(0), halo_spec(1),
                  _bs((CONV_WIDTH, GW), lambda i: (0, 0)), vec, vec, vec],
        out_specs=_bs((tm, GW), lambda i: (i, 0)), out_shape=S((T, GW), MXU),
        scratch_shapes=[pltpu.VMEM((SUBLANES, tm + CONV_HALO, GW), F32)], name="conv_fwd",
        compiler_params=_cparams("parallel"))(z, z, z, z, cw, cb, lw, lb)


def _conv_bwd(z, cw, cb, lw, lb, d_u):
    T = z.shape[0]
    tm = min(512, T)
    hb = tm // CONV_HALO
    nt = T // tm
    H = CONV_HALO

    def body(val_ref, gate_ref, pval_ref, pgate_ref, nval_ref, ngate_ref, du_ref, ndu_ref, cw_ref, cb_ref, lw_ref,
             lb_ref, dval_ref, dgate_ref, dcw_ref, dcb_ref, dlw_ref, dlb_ref, shift_ref):
        i = pl.program_id(0)
        first = i == 0
        val = jnp.concatenate([pval_ref[...] * jnp.where(first, 0.0, 1.0), val_ref[...], nval_ref[...]], axis=0)
        sg = _sigmoid(jnp.concatenate([pgate_ref[...], gate_ref[...], ngate_ref[...]], axis=0))
        u0 = val * sg
        y = _taps(u0, cw_ref, CONV_WIDTH, H, tm + H, shift_ref) + cb_ref[...]
        xc = y - jnp.mean(y, axis=-1, keepdims=True)
        rs = lax.rsqrt(jnp.mean(xc * xc, axis=-1, keepdims=True) + LN_EPS)
        nh = xc * rs
        a = nh * lw_ref[...] + lb_ref[...]
        sa = _sigmoid(a)
        du = jnp.concatenate([du_ref[...], ndu_ref[...] * jnp.where(i == nt - 1, 0.0, 1.0)], axis=0)
        da = du * (sa * (1.0 + a * (1.0 - sa)))
        dn = da * lw_ref[...]
        dy = rs * (dn - jnp.mean(dn, axis=-1, keepdims=True) - nh * jnp.mean(dn * nh, axis=-1, keepdims=True))
        du0, dcw = _taps_bwd(dy, u0[H:H + tm], cw_ref, CONV_WIDTH, tm, shift_ref)
        v0, s0 = val[H:H + tm], sg[H:H + tm]
        dval_ref[...] = (du0 * s0).astype(dval_ref.dtype)
        dgate_ref[...] = (du0 * v0 * s0 * (1.0 - s0)).astype(dgate_ref.dtype)
        dy0 = dy[:tm]
        _acc_out(dcw_ref, dcw, first)
        _acc_out(dcb_ref, jnp.sum(dy0, axis=0, keepdims=True), first)
        _acc_out(dlw_ref, jnp.sum(da[:tm] * nh[:tm], axis=0, keepdims=True), first)
        _acc_out(dlb_ref, jnp.sum(da[:tm], axis=0, keepdims=True), first)

    vec = _bs((1, GW), lambda i: (0, 0))
    cwspec = _bs((CONV_WIDTH, GW), lambda i: (0, 0))
    prev = lambda col: _bs((H, GW), lambda i: (jnp.maximum(i * hb - 1, 0), col))
    nxt = lambda col: _bs((H, GW), lambda i: (jnp.minimum((i + 1) * hb, nt * hb - 1), col))
    row = _bs((tm, GW), lambda i: (i, 0))
    return pl.pallas_call(
        body, grid=(nt,),
        in_specs=[_bs((tm, GW), lambda i: (i, 0)), _bs((tm, GW), lambda i: (i, 1)), prev(0), prev(1), nxt(0), nxt(1),
                  row, nxt(0), cwspec, vec, vec, vec],
        out_specs=[row, row, cwspec, vec, vec, vec],
        out_shape=[S((T, GW), MXU), S((T, GW), MXU), S((CONV_WIDTH, GW), F32)] + [S((1, GW), F32)] * 3,
        scratch_shapes=[pltpu.VMEM((SUBLANES, tm + 2 * H, GW), F32)], name="conv_bwd", compiler_params=_cparams("arbitrary"))(z, z, z, z, z, z, d_u, d_u, cw, cb, lw, lb)


def _ffn_act_fwd(up0, fw, fb):
    T = up0.shape[0]
    tm = min(256, T)
    hb = tm // FFN_HALO
    H = FFN_HALO

    def body(a_ref, g_ref, pa_ref, pg_ref, wa_ref, wg_ref, ba_ref, bg_ref, o_ref):
        i = pl.program_id(0)
        keep = jnp.where(i == 0, 0.0, 1.0)
        ea = jnp.concatenate([pa_ref[...] * keep, a_ref[...]], axis=0)
        eg = jnp.concatenate([pg_ref[...] * keep, g_ref[...]], axis=0)
        av = _taps(ea, wa_ref, FFN_CONV_WIDTH, H, tm) + ba_ref[...]
        gv = _taps(eg, wg_ref, FFN_CONV_WIDTH, H, tm) + bg_ref[...]
        o_ref[...] = (gv * _sigmoid(gv) * av).astype(o_ref.dtype)

    col = lambda j: _bs((tm, D_FF), lambda i: (i, j))
    prev = lambda j: _bs((H, D_FF), lambda i: (jnp.maximum(i * hb - 1, 0), j))
    wspec = lambda j: _bs((FFN_CONV_WIDTH, D_FF), lambda i: (0, j))
    bspec = lambda j: _bs((1, D_FF), lambda i: (0, j))
    return pl.pallas_call(
        body, grid=(T // tm,),
        in_specs=[col(0), col(1), prev(0), prev(1), wspec(0), wspec(1), bspec(0), bspec(1)],
        out_specs=_bs((tm, D_FF), lambda i: (i, 0)), out_shape=S((T, D_FF), MXU), name="ffn_act_fwd",
        compiler_params=_cparams("parallel"))(up0, up0, up0, up0, fw, fw, fb, fb)


def _ffn_act_bwd(up0, fw, fb, d_f):
    T = up0.shape[0]
    tm = min(256, T)
    hb = tm // FFN_HALO
    nt = T // tm
    H = FFN_HALO
    W = FFN_CONV_WIDTH

    def body(a_ref, g_ref, pa_ref, pg_ref, na_ref, ng_ref, df_ref, ndf_ref, wa_ref, wg_ref, ba_ref, bg_ref,
             dup_ref, dw_ref, db_ref):
        i = pl.program_id(0)
        first = i == 0
        keep = jnp.where(first, 0.0, 1.0)
        ea = jnp.concatenate([pa_ref[...] * keep, a_ref[...], na_ref[...]], axis=0)
        eg = jnp.concatenate([pg_ref[...] * keep, g_ref[...], ng_ref[...]], axis=0)
        av = _taps(ea, wa_ref, W, H, tm + H) + ba_ref[...]
        gv = _taps(eg, wg_ref, W, H, tm + H) + bg_ref[...]
        df = jnp.concatenate([df_ref[...], ndf_ref[...] * jnp.where(i == nt - 1, 0.0, 1.0)], axis=0)
        sg = _sigmoid(gv)
        d_av = df * gv * sg
        d_gv = df * av * (sg * (1.0 + gv * (1.0 - sg)))
        dua, dwa = _taps_bwd(d_av, a_ref[...], wa_ref, W, tm)
        dug, dwg = _taps_bwd(d_gv, g_ref[...], wg_ref, W, tm)
        dup_ref[:, :D_FF] = dua.astype(dup_ref.dtype)
        dup_ref[:, D_FF:] = dug.astype(dup_ref.dtype)
        dw = jnp.concatenate([dwa, dwg], axis=1)
        db = jnp.concatenate([jnp.sum(d_av[:tm], axis=0, keepdims=True), jnp.sum(d_gv[:tm], axis=0, keepdims=True)], axis=1)
        _acc_out(dw_ref, dw, first)
        _acc_out(db_ref, db, first)

    col = lambda j: _bs((tm, D_FF), lambda i: (i, j))
    prev = lambda j: _bs((H, D_FF), lambda i: (jnp.maximum(i * hb - 1, 0), j))
    nxt = lambda j: _bs((H, D_FF), lambda i: (jnp.minimum((i + 1) * hb, nt * hb - 1), j))
    wspec = lambda j: _bs((W, D_FF), lambda i: (0, j))
    bspec = lambda j: _bs((1, D_FF), lambda i: (0, j))
    return pl.pallas_call(
        body, grid=(nt,),
        in_specs=[col(0), col(1), prev(0), prev(1), nxt(0), nxt(1), col(0), nxt(0), wspec(0), wspec(1), bspec(0), bspec(1)],
        out_specs=[_bs((tm, 2 * D_FF), lambda i: (i, 0)), _bs((W, 2 * D_FF), lambda i: (0, 0)),
                   _bs((1, 2 * D_FF), lambda i: (0, 0))],
        out_shape=[S((T, 2 * D_FF), MXU), S((W, 2 * D_FF), F32), S((1, 2 * D_FF), F32)],
        name="ffn_act_bwd", compiler_params=_cparams("arbitrary"))(
            up0, up0, up0, up0, up0, up0, d_f, d_f, fw, fw, fb, fb)


def _branch_fwd(attn, u, cross, z, b_gate, wa, wc, wx):
    T = z.shape[0]
    tm = min(512, T)

    def body(a_ref, u_ref, x_ref, g0_ref, g1_ref, g2_ref, b_ref, wa_ref, wc_ref, wx_ref, o_ref):
        acc = None
        for j, (act, g_ref, w_ref) in enumerate(((a_ref, g0_ref, wa_ref), (u_ref, g1_ref, wc_ref), (x_ref, g2_ref, wx_ref))):
            gate = _sigmoid(g_ref[...] + b_ref[:, j * D_MODEL:(j + 1) * D_MODEL])
            term = gate * _dot(act[...], w_ref[...])
            acc = term if acc is None else acc + term
        o_ref[...] = acc.astype(o_ref.dtype)

    act = _bs((tm, GW), lambda i: (i, 0))
    gcol = lambda j: _bs((tm, D_MODEL), lambda i: (i, j))
    wfull = _bs((GW, D_MODEL), lambda i: (0, 0))
    return pl.pallas_call(
        body, grid=(T // tm,),
        in_specs=[act, act, act, gcol(0), gcol(1), gcol(2), _bs((1, 3 * D_MODEL), lambda i: (0, 0)), wfull, wfull, wfull],
        out_specs=_bs((tm, D_MODEL), lambda i: (i, 0)), out_shape=S((T, D_MODEL), MXU), name="branch_fwd",
        compiler_params=_cparams("parallel"))(attn, u, cross, z, z, z, b_gate, wa, wc, wx)


def _branch_bwd(d_merged, attn, u, cross, z, b_gate, wa, wc, wx):
    T = z.shape[0]
    tm = min(512, T)

    def body(dm_ref, a_ref, u_ref, x_ref, g0_ref, g1_ref, g2_ref, b_ref, wa_ref, wc_ref, wx_ref,
             dzg_ref, da_ref, du_ref, dx_ref, dwa_ref, dwc_ref, dwx_ref, db_ref):
        first = pl.program_id(0) == 0
        dm = dm_ref[...]
        dbs = []
        for j, (act, g_ref, w_ref, dact_ref, dw_ref) in enumerate((
                (a_ref, g0_ref, wa_ref, da_ref, dwa_ref), (u_ref, g1_ref, wc_ref, du_ref, dwc_ref),
                (x_ref, g2_ref, wx_ref, dx_ref, dwx_ref))):
            av = act[...]
            gate = _sigmoid(g_ref[...] + b_ref[:, j * D_MODEL:(j + 1) * D_MODEL])
            y = _dot(av, w_ref[...])
            dzg = dm * y * gate * (1.0 - gate)
            dzg_ref[:, j * D_MODEL:(j + 1) * D_MODEL] = dzg.astype(dzg_ref.dtype)
            dbs.append(jnp.sum(dzg, axis=0, keepdims=True))
            dy = (gate * dm).astype(MXU)
            dact_ref[...] = _dot_nt(dy, w_ref[...])
            _acc_out(dw_ref, _dot_tn(av, dy), first)
        _acc_out(db_ref, jnp.concatenate(dbs, axis=1), first)

    act = _bs((tm, GW), lambda i: (i, 0))
    gcol = lambda j: _bs((tm, D_MODEL), lambda i: (i, j))
    wfull = _bs((GW, D_MODEL), lambda i: (0, 0))
    bvec = _bs((1, 3 * D_MODEL), lambda i: (0, 0))
    return pl.pallas_call(
        body, grid=(T // tm,),
        in_specs=[_bs((tm, D_MODEL), lambda i: (i, 0)), act, act, act, gcol(0), gcol(1), gcol(2), bvec, wfull, wfull, wfull],
        out_specs=[_bs((tm, 3 * D_MODEL), lambda i: (i, 0)), act, act, act, wfull, wfull, wfull, bvec],
        out_shape=[S((T, 3 * D_MODEL), MXU)] + [S((T, GW), F32)] * 3 + [S((GW, D_MODEL), F32)] * 3 + [S((1, 3 * D_MODEL), F32)],
        name="branch_bwd", compiler_params=_cparams("arbitrary"))(d_merged, attn, u, cross, z, z, z, b_gate, wa, wc, wx)


def _loss_head(y, target):
    T, D = y.shape
    tm = min(512, T)

    def body(y_ref, t_ref, dy_ref, l_ref):
        e = y_ref[...] - t_ref[...]
        dy_ref[...] = e * (1.0 / D)
        part = jnp.full((8, 128), jnp.sum(e * e), F32)
        _acc_out(l_ref, part, pl.program_id(0) == 0)

    row = _bs((tm, D), lambda i: (i, 0))
    return pl.pallas_call(
        body, grid=(T // tm,), in_specs=[row, row], out_specs=[row, _bs((8, 128), lambda i: (0, 0))],
        out_shape=[S((T, D), F32), S((8, 128), F32)], name="loss_head", compiler_params=_cparams("arbitrary"))(y, target)


def _peer(mask):
    x, y, c = lax.axis_index("x"), lax.axis_index("y"), lax.axis_index("c")
    px = 1 - x if mask & 4 else x
    py = 1 - y if mask & 2 else y
    pc = 1 - c if mask & 1 else c
    return (px, py, pc), 4 * px + 2 * py + pc


def _exchange(arrs, scatter, name):
    n = len(arrs)
    outs_shape = [S(a.shape if scatter else (N_DEV,) + a.shape, a.dtype) for a in arrs]

    def body(*refs):
        ins, outs = refs[:n], refs[n:2 * n]
        send_sems, recv_sems, local_sems = refs[2 * n:]
        me = 4 * lax.axis_index("x") + 2 * lax.axis_index("y") + lax.axis_index("c")
        copies = []
        for w in range(n):
            src = ins[w].at[me] if scatter else ins[w]
            cp = pltpu.make_async_copy(src, outs[w].at[me], local_sems.at[w])
            cp.start()
            copies.append(cp)
        for k in range(1, N_DEV):
            peer, pidx = _peer(k)
            for w in range(n):
                src = ins[w].at[pidx] if scatter else ins[w]
                cp = pltpu.make_async_remote_copy(
                    src_ref=src, dst_ref=outs[w].at[me], send_sem=send_sems.at[w, k - 1], recv_sem=recv_sems.at[w, k - 1],
                    device_id=peer, device_id_type=pl.DeviceIdType.MESH)
                cp.start()
                copies.append(cp)
        for cp in copies:
            cp.wait()

    hbm = pl.BlockSpec(memory_space=pl.ANY)
    return pl.pallas_call(
        body, in_specs=[hbm] * n, out_specs=[hbm] * n, out_shape=outs_shape,
        scratch_shapes=[pltpu.SemaphoreType.DMA((n, N_DEV - 1)), pltpu.SemaphoreType.DMA((n, N_DEV - 1)),
                        pltpu.SemaphoreType.DMA((n,))],
        name=name)(*arrs)


def _exchange_copies(ins, lands, send_sems, recv_sems, local_sems, scatter):
    n = len(ins)
    me = 4 * lax.axis_index("x") + 2 * lax.axis_index("y") + lax.axis_index("c")
    copies = []
    for w in range(n):
        src = ins[w].at[me] if scatter else ins[w]
        copies.append(pltpu.make_async_copy(src, lands[w].at[me], local_sems.at[w]))
    for k in range(1, N_DEV):
        peer, pidx = _peer(k)
        for w in range(n):
            src = ins[w].at[pidx] if scatter else ins[w]
            copies.append(pltpu.make_async_remote_copy(
                src_ref=src, dst_ref=lands[w].at[me], send_sem=send_sems.at[w * (N_DEV - 1) + k - 1],
                recv_sem=recv_sems.at[w * (N_DEV - 1) + k - 1],
                device_id=peer, device_id_type=pl.DeviceIdType.MESH))
    return copies


_HBM_SPEC = pl.BlockSpec(memory_space=pltpu.HBM)
_SEM_SPEC = pl.BlockSpec(memory_space=pltpu.SEMAPHORE)
_DATAFLOW = pltpu.SideEffectType.DATAFLOW_SIDE_EFFECTING


def _exchange_start(arrs, scatter, name, after=None):
    n = len(arrs)
    land_shapes = [a.shape if scatter else (N_DEV,) + a.shape for a in arrs]
    n_in = 2 * n + (after is not None)

    def body(*refs):
        ins, lands = refs[:n], refs[n:2 * n]
        send_sems, recv_sems, local_sems = refs[n_in:n_in + 3]
        token = refs[-1]
        for cp in _exchange_copies(ins, lands, send_sems, recv_sems, local_sems, scatter):
            cp.start()
        token[...] = jnp.zeros_like(token)

    out_shape = ([pltpu.SemaphoreType.DMA((n * (N_DEV - 1),)), pltpu.SemaphoreType.DMA((n * (N_DEV - 1),)),
                  pltpu.SemaphoreType.DMA((n,))]
                 + [pltpu.HBM(a.shape, a.dtype) for a in arrs]
                 + [pltpu.HBM(s, a.dtype) for s, a in zip(land_shapes, arrs)]
                 + [S((8, 128), F32)])
    args = ([pltpu.with_memory_space_constraint(a, pltpu.HBM) for a in arrs]
            + [pltpu.with_memory_space_constraint(lax.empty(s, a.dtype), pltpu.HBM) for s, a in zip(land_shapes, arrs)])
    if after is not None:
        args.append(after)
    outs = pl.pallas_call(
        body, in_specs=[_HBM_SPEC] * (2 * n) + [pl.BlockSpec(memory_space=pl.ANY)] * (after is not None),
        out_specs=[_SEM_SPEC] * 3 + [_HBM_SPEC] * (2 * n) + [pl.BlockSpec(memory_space=pltpu.VMEM)],
        out_shape=out_shape, input_output_aliases={j: 3 + j for j in range(2 * n)},
        name=name, compiler_params=pltpu.CompilerParams(has_side_effects=_DATAFLOW))(*args)
    return (n, scatter, outs[:3], outs[3:3 + n], outs[3 + n:3 + 2 * n]), outs[-1]


def _exchange_wait(state, after, name):
    n, scatter, sems, ins, lands = state

    def body(*refs):
        ins_r, lands_r = refs[:n], refs[n:2 * n]
        send_sems, recv_sems, local_sems = refs[2 * n:2 * n + 3]
        for cp in _exchange_copies(ins_r, lands_r, send_sems, recv_sems, local_sems, scatter):
            cp.wait()

    outs = pl.pallas_call(
        body, in_specs=[_HBM_SPEC] * (2 * n) + [_SEM_SPEC] * 3 + [pl.BlockSpec(memory_space=pl.ANY)],
        out_specs=[_HBM_SPEC] * (2 * n),
        out_shape=[pltpu.HBM(a.shape, a.dtype) for a in ins] + [pltpu.HBM(a.shape, a.dtype) for a in lands],
        input_output_aliases={j: j for j in range(2 * n)},
        name=name, compiler_params=pltpu.CompilerParams(has_side_effects=_DATAFLOW))(*ins, *lands, *sems, after)
    return list(outs[n:])


def _adamw(w, m, v, parts, name):
    R, C = w.shape
    P = parts.shape[0]
    tr = _pick(R, tuple(t for t in (256, 176, 128, 64, 32, 16, 8) if P * t * C * 4 <= ADAMW_BLOCK_BYTES))
    c1 = 1.0 / (1.0 - ADAM_B1 ** ADAM_STEP)
    c2 = 1.0 / (1.0 - ADAM_B2 ** ADAM_STEP)

    def body(w_ref, m_ref, v_ref, p_ref, g_ref, d_ref, nm_ref, nv_ref):
        g = p_ref[0].astype(F32)
        for j in range(1, P):
            g = g + p_ref[j].astype(F32)
        m2 = ADAM_B1 * m_ref[...] + (1.0 - ADAM_B1) * g
        v2 = ADAM_B2 * v_ref[...] + (1.0 - ADAM_B2) * (g * g)
        g_ref[...] = g
        nm_ref[...] = m2
        nv_ref[...] = v2
        d_ref[...] = -ADAM_LR * ((m2 * c1) / (jnp.sqrt(v2 * c2) + ADAM_EPS) + ADAM_WD * w_ref[...])

    row = _bs((tr, C), lambda i: (i, 0))
    return pl.pallas_call(
        body, grid=(R // tr,), in_specs=[row, row, row, _bs((P, tr, C), lambda i: (0, i, 0))], out_specs=[row] * 4,
        out_shape=[S((R, C), F32)] * 4, name=name, compiler_params=_cparams("parallel"))(w, m, v, parts)


def _sum_parts(parts, name):
    P, R, C = parts.shape

    def body(p_ref, o_ref):
        g = p_ref[0]
        for j in range(1, P):
            g = g + p_ref[j]
        o_ref[...] = g

    return pl.pallas_call(body, out_shape=S((R, C), F32), name=name, compiler_params=_cparams())(parts)


def _pack(arrs):
    flat = jnp.concatenate([a.reshape(-1) for a in arrs])
    rows = -(-flat.shape[0] // 1024) * 8
    return jnp.pad(flat, (0, rows * 128 - flat.shape[0])).reshape(rows, 128)


def _unpack(packed, shapes):
    flat = packed.reshape(-1)
    out, off = [], 0
    for s in shapes:
        n = int(np.prod(s))
        out.append(flat[off:off + n].reshape(s))
        off += n
    return out


def _behind(a, token):
    return a if token is None else a + token[0, 0]


def _local_step(x, mem, target, p, comm=None):
    table = p["rel_bias_table"]
    xn = _rms_fwd(x, p["attn_norm_w"], "attn_norm_fwd")
    dils = [dil for _, dil in ATTN_GROUPS]
    xn_c = [_to_classes(xn, dil) for dil in dils]
    w_in = p["w_in"]
    qkv_w = 3 * N_GROUPS * GW
    wq = [jnp.concatenate([w_in[:, (N_GROUPS * part + g) * GW:(N_GROUPS * part + g + 1) * GW] for part in range(3)], axis=1)
          for g in range(N_GROUPS)]
    wc = w_in[:, qkv_w:qkv_w + 3 * GW]
    wg = w_in[:, qkv_w + 3 * GW:]
    zq = [_matmul(xn_c[g], wq[g], name=f"mm_in_qkv{g}") for g in range(N_GROUPS)]
    zc = _matmul(xn, wc, name="mm_in_c")
    zg = _matmul(xn, wg, name="mm_in_g")
    os_, lses = [], []
    for g, dil in enumerate(dils):
        o, l = _attn_fwd(zq[g], table, p["q_norm_w"][g:g + 1], p["k_norm_w"][g:g + 1], g, dil)
        os_.append(_from_classes(o, dil))
        lses.append(_from_classes(l, dil))
    attn, lse = _attn_merge(os_, lses)
    u = _conv_fwd(zc, p["conv_dw_w"], p["conv_dw_b"], p["conv_ln_w"], p["conv_ln_b"])
    if comm is not None:
        p = {**p, **comm.late_weights(after=u)}
    mk, mv = _mem_fwd(mem, p["mem_norm_w"], p["w_mem_kv"], p["xk_norm_w"])
    cross = _cross_fwd(zc, mk, mv, p["xq_norm_w"])
    merged = _branch_fwd(attn, u, cross, zg, p["b_gate"], p["w_attn_o"], p["w_conv_o"], p["w_cross_o"])
    h1 = _matmul(merged, p["w_out"], residual=x, name="mm_out")
    hn = _rms_fwd(h1, p["ffn_norm_w"], "ffn_norm_fwd")
    up0 = _matmul(hn, p["w_up"], name="mm_up")
    f = _ffn_act_fwd(up0, p["ffn_conv_w"], p["ffn_conv_b"])
    h2 = _matmul(f, p["w_down"], residual=h1, name="mm_down")
    dh2, lsum = _loss_head(h2, target)
    g = {}
    d_f = _matmul(dh2, p["w_down"], tb=True, name="mm_down_dx")
    g["w_down"] = _matmul(f, dh2, ta=True, name="mm_down_dw")
    d_up0, g["ffn_conv_w"], g["ffn_conv_b"] = _ffn_act_bwd(up0, p["ffn_conv_w"], p["ffn_conv_b"], d_f)
    dhn = _matmul(d_up0, p["w_up"], tb=True, name="mm_up_dx")
    g["w_up"] = _matmul(hn, d_up0, ta=True, name="mm_up_dw")
    dh1, g["ffn_norm_w"] = _rms_bwd(h1, p["ffn_norm_w"], [dhn], dh2, "ffn_norm_bwd")
    d_merged = _matmul(dh1, p["w_out"], tb=True, name="mm_out_dx")
    g["w_out"] = _matmul(merged, dh1, ta=True, name="mm_out_dw")
    (d_zg, d_attn, d_u, d_cross, g["w_attn_o"], g["w_conv_o"], g["w_cross_o"], g["b_gate"]) = _branch_bwd(
        d_merged, attn, u, cross, zg, p["b_gate"], p["w_attn_o"], p["w_conv_o"], p["w_cross_o"])
    d_xq, dmk, dmv, g["xq_norm_w"] = _cross_bwd(zc, mk, mv, p["xq_norm_w"], d_cross)
    g["w_mem_kv"], g["mem_norm_w"], g["xk_norm_w"] = _mem_bwd(mem, p["mem_norm_w"], p["w_mem_kv"], p["xk_norm_w"], dmk, dmv)
    tok = comm.start_early_grads(g) if comm is not None else None
    d_val, d_gate, g["conv_dw_w"], g["conv_dw_b"], g["conv_ln_w"], g["conv_ln_b"] = _conv_bwd(
        zc, p["conv_dw_w"], _behind(p["conv_dw_b"], tok), p["conv_ln_w"], p["conv_ln_b"], d_u)
    dzq, dqw, dkw, dtab = [], [], [], []
    for gi, dil in enumerate(dils):
        r = _attn_bwd(zq[gi], table, p["q_norm_w"][gi:gi + 1], p["k_norm_w"][gi:gi + 1], _to_classes(d_attn, dil),
                      _to_classes(attn, dil), _to_classes(lse, dil), gi, dil)
        for lst, val in zip((dzq, dqw, dkw, dtab), r):
            lst.append(val)
    g["q_norm_w"] = jnp.concatenate(dqw, axis=0)
    g["k_norm_w"] = jnp.concatenate(dkw, axis=0)
    g["rel_bias_table"] = jnp.concatenate(dtab, axis=1)
    d_zc = jnp.concatenate([d_val, d_gate, d_xq], axis=1)
    gq = [_matmul(xn_c[gi], dzq[gi], ta=True, name=f"mm_in_qkv{gi}_dw") for gi in range(N_GROUPS)]
    gc = _matmul(xn, d_zc, ta=True, name="mm_in_c_dw")
    gg = _matmul(xn, d_zg, ta=True, name="mm_in_g_dw")
    g["w_in"] = jnp.concatenate(
        [gq[gi][:, part * GW:(part + 1) * GW] for part in range(3) for gi in range(N_GROUPS)] + [gc, gg], axis=1)
    tok = comm.start_w_in_grad(g["w_in"]) if comm is not None else None
    dxn = _matmul(d_zg, wg, tb=True, after=tok, name="mm_in_g_dx")
    dxn = _matmul(d_zc, wc, tb=True, residual=dxn, name="mm_in_c_dx")
    dxn = _matmul(dzq[0], wq[0], tb=True, residual=dxn, name="mm_in_qkv0_dx")
    dxs = [dxn] + [_from_classes(_matmul(dzq[gi], wq[gi], tb=True, name=f"mm_in_qkv{gi}_dx"), dils[gi])
                   for gi in range(1, N_GROUPS)]
    grad_x, g["attn_norm_w"] = _rms_bwd(x, p["attn_norm_w"], dxs, dh1, "attn_norm_bwd")
    return lsum[0, 0], grad_x, g


WEIGHT_NAMES = ["rel_bias_table", "attn_norm_w", "w_in", "b_gate", "q_norm_w", "k_norm_w", "w_attn_o", "conv_dw_w",
                "conv_dw_b", "conv_ln_w", "conv_ln_b", "w_conv_o", "mem_norm_w", "w_mem_kv", "xq_norm_w", "xk_norm_w",
                "w_cross_o", "w_out", "ffn_norm_w", "w_up", "ffn_conv_w", "ffn_conv_b", "w_down"]
COL_SHARDED = ("w_in", "w_attn_o", "w_conv_o", "w_cross_o", "w_up")
ROW_SHARDED = ("w_mem_kv", "w_out", "w_down")
SMALL_COL_SHARDED = ("conv_dw_w", "ffn_conv_w")
BIG = COL_SHARDED + ROW_SHARDED


def _cols_to_blocks(a):
    k, n8 = a.shape
    return a.reshape(k, N_DEV, n8 // N_DEV).transpose(1, 0, 2)


def _blocks_to_cols(a):
    return a.transpose(1, 0, 2).reshape(a.shape[1], N_DEV * a.shape[2])


def _step(x, mem, target, w, m, v):
    me = 4 * lax.axis_index("x") + 2 * lax.axis_index("y") + lax.axis_index("c")

    def to_full(n, blocks):
        return _blocks_to_cols(blocks) if n in COL_SHARDED + SMALL_COL_SHARDED else blocks.reshape(-1, blocks.shape[-1])

    def to_blocks(n, grad):
        blocks = _cols_to_blocks(grad) if n in COL_SHARDED else grad.reshape(N_DEV, -1, grad.shape[-1])
        return blocks.astype(MXU)

    first = ("w_in",) + SMALL_COL_SHARDED
    late = tuple(n for n in BIG if n != "w_in")
    cast = lambda n: w[n].astype(MXU) if n in BIG else w[n]
    first_state, first_token = _exchange_start([cast(n) for n in first], False, "gather_first_start")
    got = _exchange_wait(first_state, first_token, "gather_first_wait")
    late_state, _ = _exchange_start([cast(n) for n in late], False, "gather_late_start", after=got[0])
    p = {n: w[n] for n in WEIGHT_NAMES if n not in BIG + SMALL_COL_SHARDED}
    p.update({n: to_full(n, b) for n, b in zip(first, got)})

    class Comm:
        def late_weights(self, after):
            return {n: to_full(n, b) for n, b in zip(late, _exchange_wait(late_state, after, "gather_late_wait"))}

        def start_early_grads(self, g):
            self.early_state, token = _exchange_start([to_blocks(n, g[n]) for n in late], True, "scatter_early_start")
            return token

        def start_w_in_grad(self, grad):
            self.w_in_state, token = _exchange_start([to_blocks("w_in", grad)], True, "scatter_w_in_start")
            return token

    comm = Comm()
    lsum, grad_x, g = _local_step(x, mem, target, p, comm)
    small_names = [n for n in WEIGHT_NAMES if n not in BIG]
    small_shapes = [g[n].shape for n in small_names]
    small_parts = _exchange([_pack([g[n] for n in small_names])], False, "gather_small_grads")[0]
    gsmall = dict(zip(small_names, _unpack(_sum_parts(small_parts, "sum_small_grads"), small_shapes)))
    for n in SMALL_COL_SHARDED:
        width = w[n].shape[-1]
        gsmall[n] = lax.dynamic_slice_in_dim(gsmall[n], me * width, width, axis=1)
    res = {}
    parts = dict(zip(late, _exchange_wait(comm.early_state, grad_x, "scatter_early_wait")))
    for n in late:
        res[n] = _adamw(w[n], m[n], v[n], parts[n], "adamw_" + n)
    w_in_parts = _exchange_wait(comm.w_in_state, res[late[-1]][1], "scatter_w_in_wait")[0]
    res["w_in"] = _adamw(w["w_in"], m["w_in"], v["w_in"], w_in_parts, "adamw_w_in")
    shapes = [w[n].shape for n in small_names]
    packed = [_pack([d[n] for n in small_names]) for d in (w, m, v, gsmall)]
    outs = _adamw(packed[0], packed[1], packed[2], packed[3][None], "adamw_small")
    unpacked = [_unpack(o, shapes) for o in outs]
    for j, n in enumerate(small_names):
        res[n] = tuple(unpacked[q][j] for q in range(4))
    return lsum, grad_x, res


def kernel(x, mem, rel_bias_table, attn_norm_w, w_in, b_gate, q_norm_w, k_norm_w, w_attn_o, conv_dw_w, conv_dw_b, conv_ln_w, conv_ln_b, w_conv_o, mem_norm_w, w_mem_kv, xq_norm_w, xk_norm_w, w_cross_o, w_out, ffn_norm_w, w_up, ffn_conv_w, ffn_conv_b, w_down, loss_target, m_rel_bias_table, m_attn_norm_w, m_w_in, m_b_gate, m_q_norm_w, m_k_norm_w, m_w_attn_o, m_conv_dw_w, m_conv_dw_b, m_conv_ln_w, m_conv_ln_b, m_w_conv_o, m_mem_norm_w, m_w_mem_kv, m_xq_norm_w, m_xk_norm_w, m_w_cross_o, m_w_out, m_ffn_norm_w, m_w_up, m_ffn_conv_w, m_ffn_conv_b, m_w_down, v_rel_bias_table, v_attn_norm_w, v_w_in, v_b_gate, v_q_norm_w, v_k_norm_w, v_w_attn_o, v_conv_dw_w, v_conv_dw_b, v_conv_ln_w, v_conv_ln_b, v_w_conv_o, v_mem_norm_w, v_w_mem_kv, v_xq_norm_w, v_xk_norm_w, v_w_cross_o, v_w_out, v_ffn_norm_w, v_w_up, v_ffn_conv_w, v_ffn_conv_b, v_w_down):
    ws = dict(zip(WEIGHT_NAMES, (rel_bias_table, attn_norm_w, w_in, b_gate, q_norm_w, k_norm_w, w_attn_o, conv_dw_w, conv_dw_b, conv_ln_w, conv_ln_b, w_conv_o, mem_norm_w, w_mem_kv, xq_norm_w, xk_norm_w, w_cross_o, w_out, ffn_norm_w, w_up, ffn_conv_w, ffn_conv_b, w_down)))
    ms = dict(zip(WEIGHT_NAMES, (m_rel_bias_table, m_attn_norm_w, m_w_in, m_b_gate, m_q_norm_w, m_k_norm_w, m_w_attn_o, m_conv_dw_w, m_conv_dw_b, m_conv_ln_w, m_conv_ln_b, m_w_conv_o, m_mem_norm_w, m_w_mem_kv, m_xq_norm_w, m_xk_norm_w, m_w_cross_o, m_w_out, m_ffn_norm_w, m_w_up, m_ffn_conv_w, m_ffn_conv_b, m_w_down)))
    vs = dict(zip(WEIGHT_NAMES, (v_rel_bias_table, v_attn_norm_w, v_w_in, v_b_gate, v_q_norm_w, v_k_norm_w, v_w_attn_o, v_conv_dw_w, v_conv_dw_b, v_conv_ln_w, v_conv_ln_b, v_w_conv_o, v_mem_norm_w, v_w_mem_kv, v_xq_norm_w, v_xk_norm_w, v_w_cross_o, v_w_out, v_ffn_norm_w, v_w_up, v_ffn_conv_w, v_ffn_conv_b, v_w_down)))
    full_shapes = {n: ws[n].shape for n in WEIGHT_NAMES}

    def squeeze(d):
        return {n: (a if n == "rel_bias_table" else a[0]) for n, a in d.items()}

    w, m, v = squeeze(ws), squeeze(ms), squeeze(vs)
    for d in (w, m, v):
        for n in WEIGHT_NAMES:
            if d[n].ndim == 1:
                d[n] = d[n][None]
    lsum, grad_x, res = _step(x[0], mem[0], loss_target[0], w, m, v)
    loss = lax.psum(0.5 / D_MODEL * lsum, ("x", "y", "c"))
    outs = [loss, grad_x[None]]
    for q in range(4):
        outs += [res[n][q].reshape(full_shapes[n]) for n in WEIGHT_NAMES]
    return tuple(outs)
```

```python
import functools
import math

import numpy as np
import jax
import jax.numpy as jnp
from jax import lax
from jax.experimental import pallas as pl
from jax.experimental.pallas import tpu as pltpu

F32 = jnp.float32
MXU = jnp.bfloat16
S = jax.ShapeDtypeStruct

D_MODEL = 1024
HEAD_DIM = 128
ATTN_GROUPS = ((128, 1), (512, 4), (2048, 16))
N_GROUPS = 3
HEADS = 4
GW = HEADS * HEAD_DIM
CONV_WIDTH = 31
N_MEM = 256
D_FF = 2816
FFN_CONV_WIDTH = 3
N_BUCKETS = 32
MAX_DISTANCE = 2048
RMS_EPS = 1e-6
LN_EPS = 1e-5
N_IN = 9216
NCB = N_IN // GW
BLK = 128
SCALE = HEAD_DIM ** -0.5
NEG = -1e30
N_DEV = 8

ADAM_LR, ADAM_B1, ADAM_B2, ADAM_EPS, ADAM_WD, ADAM_STEP = 0.001, 0.9, 0.999, 1e-08, 0.01, 10

VMEM_LIMIT = 48 * 1024 * 1024
CONV_HALO = 32
FFN_HALO = 8
ADAMW_BLOCK_BYTES = 4 * 1024 * 1024


def _cparams(*sem):
    return pltpu.CompilerParams(dimension_semantics=sem or None, vmem_limit_bytes=VMEM_LIMIT)


def _bs(shape, imap):
    return pl.BlockSpec(shape, imap)


def _dot(a, b):
    return lax.dot_general(a.astype(MXU), b.astype(MXU), (((1,), (0,)), ((), ())), preferred_element_type=F32)


def _dot_nt(a, b):
    return lax.dot_general(a.astype(MXU), b.astype(MXU), (((1,), (1,)), ((), ())), preferred_element_type=F32)


def _dot_tn(a, b):
    return lax.dot_general(a.astype(MXU), b.astype(MXU), (((0,), (0,)), ((), ())), preferred_element_type=F32)


def _sigmoid(x):
    return 0.5 * jnp.tanh(0.5 * x) + 0.5


def _rmsn(x, w):
    r = lax.rsqrt(jnp.mean(x * x, axis=-1, keepdims=True) + RMS_EPS)
    return x * r * w, r


def _rmsn_bwd(x, r, w, dy):
    g = dy * w
    dx = r * g - x * (r * r * r) * jnp.mean(x * g, axis=-1, keepdims=True)
    dw = jnp.sum(dy * x * r, axis=0, keepdims=True)
    return dx, dw


def _acc_out(ref, val, first):
    @pl.when(first)
    def _():
        ref[...] = val

    @pl.when(jnp.logical_not(first))
    def _():
        ref[...] += val


def _rms_fwd(x, w, name):
    T, D = x.shape
    tm = min(512, T)

    def body(x_ref, w_ref, o_ref):
        y, _ = _rmsn(x_ref[...], w_ref[...])
        o_ref[...] = y.astype(o_ref.dtype)

    return pl.pallas_call(
        body, grid=(T // tm,),
        in_specs=[_bs((tm, D), lambda i: (i, 0)), _bs((1, D), lambda i: (0, 0))],
        out_specs=_bs((tm, D), lambda i: (i, 0)),
        out_shape=S((T, D), MXU), name=name, compiler_params=_cparams("parallel"))(x, w)


def _rms_bwd(x, w, dys, resid, name):
    T, D = x.shape
    tm = min(512, T)
    n = len(dys)

    def body(*refs):
        x_ref, w_ref, res_ref = refs[0], refs[1], refs[2 + n]
        dx_ref, dw_ref = refs[3 + n], refs[4 + n]
        xv = x_ref[...]
        dy = refs[2][...]
        for dy_ref in refs[3:2 + n]:
            dy = dy + dy_ref[...]
        _, r = _rmsn(xv, w_ref[...])
        dx, dw = _rmsn_bwd(xv, r, w_ref[...], dy)
        dx_ref[...] = res_ref[...] + dx
        _acc_out(dw_ref, dw, pl.program_id(0) == 0)

    row = _bs((tm, D), lambda i: (i, 0))
    vec = _bs((1, D), lambda i: (0, 0))
    return pl.pallas_call(
        body, grid=(T // tm,), in_specs=[row, vec] + [row] * (n + 1), out_specs=[row, vec],
        out_shape=[S((T, D), F32), S((1, D), F32)], name=name, compiler_params=_cparams("arbitrary"))(x, w, *dys, resid)


def _pick(n, cands):
    for c in cands:
        if n % c == 0:
            return c
    return n


MM_VMEM_BUDGET = 36 * 1024 * 1024


def _mm_tiles(tm, N, K, a_bytes, b_bytes, o_bytes, has_res):
    best = None
    for tn in (1536, 1024, 1408, 512, 256, 128):
        for tk in (3072, 1536, 1024, 1408, 512, 256, 128):
            if N % tn or K % tk:
                continue
            nk = K // tk
            need = 2 * (tm * tk * a_bytes + tk * tn * b_bytes + tm * tn * (o_bytes + 4 * has_res)) + (nk > 1) * tm * tn * 4
            if need > MM_VMEM_BUDGET:
                continue
            key = ((N // tn) * nk, nk)
            if best is None or key < best[0]:
                best = (key, tn, tk)
    if best is None:
        return _pick(N, (128,)), _pick(K, (128,))
    return best[1], best[2]


def _matmul(a, b, *, ta=False, tb=False, out_dtype=F32, residual=None, after=None, tm=None, tn=None, tk=None, name):
    M, K = (a.shape[1], a.shape[0]) if ta else a.shape
    N = b.shape[0] if tb else b.shape[1]
    tm = tm or _pick(M, (1024, 1408, 512, 256, 128))
    if tn is None or tk is None:
        tn, tk = _mm_tiles(tm, N, K, a.dtype.itemsize, b.dtype.itemsize, jnp.dtype(out_dtype).itemsize, residual is not None)
    nk = K // tk
    dn = (((0 if ta else 1,), (1 if tb else 0,)), ((), ()))
    has_res = residual is not None
    n_in = 2 + has_res + (after is not None)

    def body(*refs):
        a_ref, b_ref = refs[0], refs[1]
        res_ref = refs[2] if has_res else None
        o_ref = refs[n_in]
        p = lax.dot_general(a_ref[...].astype(MXU), b_ref[...].astype(MXU), dn, preferred_element_type=F32)

        def finish(acc):
            if has_res:
                acc = acc + res_ref[...]
            o_ref[...] = acc.astype(o_ref.dtype)

        if nk == 1:
            finish(p)
        else:
            acc_ref = refs[-1]
            k = pl.program_id(2)

            @pl.when(k == 0)
            def _():
                acc_ref[...] = p

            @pl.when(k > 0)
            def _():
                acc_ref[...] += p

            @pl.when(k == nk - 1)
            def _():
                finish(acc_ref[...])

    a_spec = _bs((tk, tm), lambda i, j, k: (k, i)) if ta else _bs((tm, tk), lambda i, j, k: (i, k))
    b_spec = _bs((tn, tk), lambda i, j, k: (j, k)) if tb else _bs((tk, tn), lambda i, j, k: (k, j))
    o_spec = _bs((tm, tn), lambda i, j, k: (i, j))
    in_specs = [a_spec, b_spec] + ([o_spec] if has_res else [])
    args = (a, b) + ((residual,) if has_res else ())
    if after is not None:
        in_specs.append(_bs((8, 128), lambda i, j, k: (0, 0)))
        args += (after,)
    return pl.pallas_call(
        body, grid=(M // tm, N // tn, nk), in_specs=in_specs, out_specs=o_spec,
        out_shape=S((M, N), out_dtype), scratch_shapes=[pltpu.VMEM((tm, tn), F32)] if nk > 1 else [],
        name=name, compiler_params=_cparams("parallel", "parallel", "arbitrary"))(*args)


def _bucket_matrix(dilation):
    n = BLK
    qi = np.arange(n)[:, None]
    kj = np.arange(2 * n)[None, :]
    step = qi + n - kj
    dist = np.clip(step, 0, None) * dilation
    max_exact = N_BUCKETS // 2
    d = np.maximum(dist.astype(np.float32), np.float32(1.0))
    large = max_exact + (np.log(d / np.float32(max_exact)) / np.float32(math.log(MAX_DISTANCE / max_exact))
                         * np.float32(N_BUCKETS - max_exact)).astype(np.int32)
    large = np.minimum(large, N_BUCKETS - 1)
    bucket = np.where(dist < max_exact, dist, large)
    band = (step >= 0) & (step <= n)
    return np.where(band, bucket, -1).astype(np.int32)


def _build_bias(tbl_ref, bkt_ref, bias_ref, g):
    bk = bkt_ref[...]
    for h in range(HEADS):
        acc = jnp.full(bk.shape, NEG, F32)
        for b in range(N_BUCKETS):
            acc = jnp.where(bk == b, tbl_ref[b, HEADS * g + h], acc)
        bias_ref[h] = acc


def _to_classes(a, dil):
    if dil == 1:
        return a
    t, n = a.shape
    return a.reshape(t // dil, dil, n).transpose(1, 0, 2).reshape(t, n)


def _from_classes(a, dil):
    if dil == 1:
        return a
    t, n = a.shape
    return a.reshape(dil, t // dil, n).transpose(1, 0, 2).reshape(t, n)


def _attn_fwd(zq, table, qw, kw, g, dil):
    T = zq.shape[0]
    nb = T // dil // BLK
    qb = _pick(nb, (4, 2, 1))
    nt = nb // qb
    bkt = jnp.asarray(_bucket_matrix(dil))

    def zspec(part, prev):
        if prev:
            return _bs((BLK, GW), lambda c, i: (c * nb + jnp.maximum(i * qb - 1, 0), part))
        return _bs((qb * BLK, GW), lambda c, i: (c * nt + i, part))

    def body(tbl_ref, bkt_ref, qw_ref, kw_ref, q_ref, kp_ref, kc_ref, vp_ref, vc_ref, o_ref, lse_ref, bias_ref):
        c, i = pl.program_id(0), pl.program_id(1)

        @pl.when((c == 0) & (i == 0))
        def _():
            _build_bias(tbl_ref, bkt_ref, bias_ref, g)

        kj = lax.broadcasted_iota(jnp.int32, (BLK, 2 * BLK), 1)
        no_prev = jnp.logical_and(i == 0, kj < BLK)
        for h in range(HEADS):
            sl = slice(h * HEAD_DIM, (h + 1) * HEAD_DIM)
            qn, _ = _rmsn(q_ref[:, sl], qw_ref[...])
            kn, _ = _rmsn(jnp.concatenate([kp_ref[:, sl], kc_ref[:, sl]], axis=0), kw_ref[...])
            v = jnp.concatenate([vp_ref[:, sl], vc_ref[:, sl]], axis=0)
            for j in range(qb):
                rows = slice(j * BLK, (j + 1) * BLK)
                keys = slice(j * BLK, (j + 2) * BLK)
                s = _dot_nt(qn[rows], kn[keys]) * SCALE + bias_ref[h]
                if j == 0:
                    s = jnp.where(no_prev, NEG, s)
                m = jnp.max(s, axis=-1, keepdims=True)
                p = jnp.exp(s - m)
                l = jnp.sum(p, axis=-1, keepdims=True)
                o_ref[rows, sl] = _dot(p, v[keys]) / l
                lse_ref[rows, sl] = jnp.broadcast_to(m + jnp.log(l), (BLK, HEAD_DIM))

    ospec = _bs((qb * BLK, GW), lambda c, i: (c * nt + i, 0))
    vec = _bs((1, HEAD_DIM), lambda c, i: (0, 0))
    return pl.pallas_call(
        body, grid=(dil, nt),
        in_specs=[pl.BlockSpec(memory_space=pltpu.SMEM), _bs((BLK, 2 * BLK), lambda c, i: (0, 0)), vec, vec,
                  zspec(0, False), zspec(1, True), zspec(1, False), zspec(2, True), zspec(2, False)],
        out_specs=[ospec, ospec],
        out_shape=[S((T, GW), F32), S((T, GW), F32)],
        scratch_shapes=[pltpu.VMEM((HEADS, BLK, 2 * BLK), F32)],
        name=f"attn_fwd_g{g}", compiler_params=_cparams("arbitrary", "arbitrary"))(table, bkt, qw, kw, zq, zq, zq, zq, zq)


def _attn_merge(os_, lses):
    T = os_[0].shape[0]
    tm = min(512, T)

    def body(o0, o1, o2, l0, l1, l2, a_ref, lse_ref):
        ls = [l0[...], l1[...], l2[...]]
        mx = jnp.maximum(jnp.maximum(ls[0], ls[1]), ls[2])
        tot = mx + jnp.log(jnp.exp(ls[0] - mx) + jnp.exp(ls[1] - mx) + jnp.exp(ls[2] - mx))
        a_ref[...] = (jnp.exp(ls[0] - tot) * o0[...] + jnp.exp(ls[1] - tot) * o1[...] + jnp.exp(ls[2] - tot) * o2[...])
        lse_ref[...] = tot

    row = _bs((tm, GW), lambda i: (i, 0))
    return pl.pallas_call(
        body, grid=(T // tm,), in_specs=[row] * 6, out_specs=[row, row],
        out_shape=[S((T, GW), F32), S((T, GW), F32)], name="attn_merge",
        compiler_params=_cparams("parallel"))(*os_, *lses)


def _attn_bwd(zq, table, qw, kw, d_attn, attn, lse, g, dil):
    T = zq.shape[0]
    nb = T // dil // BLK
    qb = _pick(nb, (4, 2, 1))
    nt = nb // qb
    bkt = jnp.asarray(_bucket_matrix(dil))

    def body(tbl_ref, bkt_ref, qw_ref, kw_ref, q_ref, k_ref, v_ref, kp_ref, vp_ref, qx_ref, da_ref, at_ref, lse_ref,
             dax_ref, atx_ref, lsex_ref, dz_ref, dqw_ref, dkw_ref, dtab_ref, bias_ref, dbias_ref):
        c, i = pl.program_id(0), pl.program_id(1)

        @pl.when((c == 0) & (i == 0))
        def _():
            _build_bias(tbl_ref, bkt_ref, bias_ref, g)
            dbias_ref[...] = jnp.zeros_like(dbias_ref)
            dqw_ref[...] = jnp.zeros_like(dqw_ref)
            dkw_ref[...] = jnp.zeros_like(dkw_ref)

        kj = lax.broadcasted_iota(jnp.int32, (BLK, 2 * BLK), 1)
        no_prev = jnp.logical_and(i == 0, kj < BLK)
        has_next = i < nt - 1
        last = slice((qb - 1) * BLK, qb * BLK)
        dqw_acc = jnp.zeros((1, HEAD_DIM), F32)
        dkw_acc = jnp.zeros((1, HEAD_DIM), F32)

        def add(parts, t, val):
            parts[t] = val if parts[t] is None else parts[t] + val

        for h in range(HEADS):
            lo = h * HEAD_DIM
            sl = slice(lo, lo + HEAD_DIM)
            q, k = q_ref[:, sl], k_ref[:, sl]
            qn, rq = _rmsn(q, qw_ref[...])
            kn, rk = _rmsn(k, kw_ref[...])
            kpn, _ = _rmsn(kp_ref[:, sl], kw_ref[...])
            kn_ext = jnp.concatenate([kpn, kn], axis=0)
            v_ext = jnp.concatenate([vp_ref[:, sl], v_ref[:, sl]], axis=0)
            dqn, dkn, dv = [None] * qb, [None] * qb, [None] * qb
            for j in range(qb):
                rows = slice(j * BLK, (j + 1) * BLK)
                keys = slice(j * BLK, (j + 2) * BLK)
                s = _dot_nt(qn[rows], kn_ext[keys]) * SCALE + bias_ref[h]
                if j == 0:
                    s = jnp.where(no_prev, NEG, s)
                p = jnp.exp(s - lse_ref[rows, lo:lo + 1])
                do = da_ref[rows, sl]
                delta = jnp.sum(do * at_ref[rows, sl], axis=-1, keepdims=True)
                ds = p * (_dot_nt(do, v_ext[keys]) - delta)
                dbias_ref[h] += ds
                dqn[j] = _dot(ds, kn_ext[keys]) * SCALE
                dv2 = _dot_tn(p, do)
                dk2 = _dot_tn(ds, qn[rows]) * SCALE
                if j >= 1:
                    add(dv, j - 1, dv2[:BLK])
                    add(dkn, j - 1, dk2[:BLK])
                add(dv, j, dv2[BLK:])
                add(dkn, j, dk2[BLK:])
            qxn, _ = _rmsn(qx_ref[:, sl], qw_ref[...])
            sx = _dot_nt(qxn, kn[last]) * SCALE + bias_ref[h, :, 0:BLK]
            px = jnp.where(has_next, jnp.exp(sx - lsex_ref[:, lo:lo + 1]), 0.0)
            dox = dax_ref[:, sl]
            dsx = px * (_dot_nt(dox, v_ref[last, sl]) - jnp.sum(dox * atx_ref[:, sl], axis=-1, keepdims=True))
            add(dv, qb - 1, _dot_tn(px, dox))
            add(dkn, qb - 1, _dot_tn(dsx, qxn) * SCALE)
            dq, dqw = _rmsn_bwd(q, rq, qw_ref[...], jnp.concatenate(dqn, axis=0))
            dk, dkw = _rmsn_bwd(k, rk, kw_ref[...], jnp.concatenate(dkn, axis=0))
            dqw_acc += dqw
            dkw_acc += dkw
            dz_ref[:, lo:lo + HEAD_DIM] = dq.astype(dz_ref.dtype)
            dz_ref[:, GW + lo:GW + lo + HEAD_DIM] = dk.astype(dz_ref.dtype)
            dz_ref[:, 2 * GW + lo:2 * GW + lo + HEAD_DIM] = jnp.concatenate(dv, axis=0).astype(dz_ref.dtype)
        dqw_ref[...] += dqw_acc
        dkw_ref[...] += dkw_acc

        @pl.when((c == dil - 1) & (i == nt - 1))
        def _():
            bk = bkt_ref[...]
            rows = lax.broadcasted_iota(jnp.int32, (N_BUCKETS, HEAD_DIM), 0)
            lanes = lax.broadcasted_iota(jnp.int32, (N_BUCKETS, HEAD_DIM), 1)
            out = jnp.zeros((N_BUCKETS, HEAD_DIM), F32)
            for h in range(HEADS):
                acc = dbias_ref[h]
                for b in range(N_BUCKETS):
                    val = jnp.sum(jnp.where(bk == b, acc, 0.0))
                    out = jnp.where((rows == b) & (lanes == h), val, out)
            dtab_ref[...] = out

    tile = lambda part: _bs((qb * BLK, GW), lambda c, i: (c * nt + i, part))
    before = lambda part: _bs((BLK, GW), lambda c, i: (c * nb + jnp.maximum(i * qb - 1, 0), part))
    after = lambda part: _bs((BLK, GW), lambda c, i: (c * nb + jnp.minimum((i + 1) * qb, nb - 1), part))
    vec = _bs((1, HEAD_DIM), lambda c, i: (0, 0))
    tabs = _bs((N_BUCKETS, HEAD_DIM), lambda c, i: (0, 0))
    dzq, dqw, dkw, dtab = pl.pallas_call(
        body, grid=(dil, nt),
        in_specs=[pl.BlockSpec(memory_space=pltpu.SMEM), _bs((BLK, 2 * BLK), lambda c, i: (0, 0)), vec, vec,
                  tile(0), tile(1), tile(2), before(1), before(2), after(0), tile(0), tile(0), tile(0),
                  after(0), after(0), after(0)],
        out_specs=[_bs((qb * BLK, 3 * GW), lambda c, i: (c * nt + i, 0)), vec, vec, tabs],
        out_shape=[S((T, 3 * GW), MXU)] + [S((1, HEAD_DIM), F32)] * 2 + [S((N_BUCKETS, HEAD_DIM), F32)],
        scratch_shapes=[pltpu.VMEM((HEADS, BLK, 2 * BLK), F32), pltpu.VMEM((HEADS, BLK, 2 * BLK), F32)],
        name=f"attn_bwd_g{g}", compiler_params=_cparams("arbitrary", "arbitrary"))(
            table, bkt, qw, kw, zq, zq, zq, zq, zq, zq, d_attn, attn, lse, d_attn, attn, lse)
    return dzq, dqw, dkw, dtab[:, :HEADS]


def _mem_fwd(mem, mem_norm_w, w_mem_kv, xk_w):
    def body(mem_ref, nw_ref, w_ref, xk_ref, mk_ref, mv_ref):
        mn, _ = _rmsn(mem_ref[...], nw_ref[...])
        kv = _dot(mn, w_ref[...])
        for h in range(HEADS):
            sl = slice(h * HEAD_DIM, (h + 1) * HEAD_DIM)
            kn, _ = _rmsn(kv[:, sl], xk_ref[...])
            mk_ref[:, sl] = kn.astype(mk_ref.dtype)
        mv_ref[...] = kv[:, GW:].astype(mv_ref.dtype)

    return pl.pallas_call(body, out_shape=[S((N_MEM, GW), MXU), S((N_MEM, GW), MXU)], name="mem_fwd",
                          compiler_params=_cparams())(mem, mem_norm_w, w_mem_kv, xk_w)


def _mem_bwd(mem, mem_norm_w, w_mem_kv, xk_w, dmk, dmv):
    def body(mem_ref, nw_ref, w_ref, xk_ref, dmk_ref, dmv_ref, dw_ref, dnw_ref, dxk_ref):
        memv = mem_ref[...]
        mn, r = _rmsn(memv, nw_ref[...])
        kv = _dot(mn, w_ref[...])
        dxk = jnp.zeros((1, HEAD_DIM), F32)
        parts = []
        for h in range(HEADS):
            sl = slice(h * HEAD_DIM, (h + 1) * HEAD_DIM)
            kh = kv[:, sl]
            _, rk = _rmsn(kh, xk_ref[...])
            dk, dw = _rmsn_bwd(kh, rk, xk_ref[...], dmk_ref[:, sl])
            dxk += dw
            parts.append(dk)
        dkv = jnp.concatenate(parts + [dmv_ref[...]], axis=1)
        dw_ref[...] = _dot_tn(mn, dkv)
        dmn = _dot_nt(dkv, w_ref[...])
        dnw_ref[...] = jnp.sum(dmn * memv * r, axis=0, keepdims=True)
        dxk_ref[...] = dxk

    return pl.pallas_call(
        body, out_shape=[S((D_MODEL, 2 * GW), F32), S((1, D_MODEL), F32), S((1, HEAD_DIM), F32)], name="mem_bwd",
        compiler_params=_cparams())(mem, mem_norm_w, w_mem_kv, xk_w, dmk, dmv)


def _cross_fwd(z, mk, mv, xq_w):
    T = z.shape[0]
    tm = min(512, T)

    def body(q_ref, mk_ref, mv_ref, w_ref, o_ref):
        for h in range(HEADS):
            sl = slice(h * HEAD_DIM, (h + 1) * HEAD_DIM)
            qn, _ = _rmsn(q_ref[:, sl], w_ref[...])
            s = _dot_nt(qn, mk_ref[:, sl]) * SCALE
            e = jnp.exp(s - jnp.max(s, axis=-1, keepdims=True))
            p = e / jnp.sum(e, axis=-1, keepdims=True)
            o_ref[:, sl] = _dot(p, mv_ref[:, sl]).astype(o_ref.dtype)

    full = _bs((N_MEM, GW), lambda i: (0, 0))
    return pl.pallas_call(
        body, grid=(T // tm,),
        in_specs=[_bs((tm, GW), lambda i: (i, 2)), full, full, _bs((1, HEAD_DIM), lambda i: (0, 0))],
        out_specs=_bs((tm, GW), lambda i: (i, 0)), out_shape=S((T, GW), MXU), name="cross_fwd",
        compiler_params=_cparams("parallel"))(z, mk, mv, xq_w)


def _cross_bwd(z, mk, mv, xq_w, d_cross):
    T = z.shape[0]
    tm = min(512, T)

    def body(q_ref, mk_ref, mv_ref, w_ref, do_ref, dq_ref, dmk_ref, dmv_ref, dw_ref):
        first = pl.program_id(0) == 0
        dw_acc = jnp.zeros((1, HEAD_DIM), F32)
        dmk_parts, dmv_parts = [], []
        for h in range(HEADS):
            sl = slice(h * HEAD_DIM, (h + 1) * HEAD_DIM)
            qh = q_ref[:, sl]
            qn, r = _rmsn(qh, w_ref[...])
            s = _dot_nt(qn, mk_ref[:, sl]) * SCALE
            e = jnp.exp(s - jnp.max(s, axis=-1, keepdims=True))
            p = e / jnp.sum(e, axis=-1, keepdims=True)
            do = do_ref[:, sl]
            dp = _dot_nt(do, mv_ref[:, sl])
            ds = p * (dp - jnp.sum(dp * p, axis=-1, keepdims=True)) * SCALE
            dmv_parts.append(_dot_tn(p, do))
            dmk_parts.append(_dot_tn(ds, qn))
            dq, dw = _rmsn_bwd(qh, r, w_ref[...], _dot(ds, mk_ref[:, sl]))
            dw_acc += dw
            dq_ref[:, sl] = dq.astype(dq_ref.dtype)
        _acc_out(dmk_ref, jnp.concatenate(dmk_parts, axis=1), first)
        _acc_out(dmv_ref, jnp.concatenate(dmv_parts, axis=1), first)
        _acc_out(dw_ref, dw_acc, first)

    full = _bs((N_MEM, GW), lambda i: (0, 0))
    vec = _bs((1, HEAD_DIM), lambda i: (0, 0))
    row = _bs((tm, GW), lambda i: (i, 0))
    return pl.pallas_call(
        body, grid=(T // tm,),
        in_specs=[_bs((tm, GW), lambda i: (i, 2)), full, full, vec, row],
        out_specs=[row, full, full, vec],
        out_shape=[S((T, GW), MXU), S((N_MEM, GW), F32), S((N_MEM, GW), F32), S((1, HEAD_DIM), F32)],
        name="cross_bwd", compiler_params=_cparams("arbitrary"))(z, mk, mv, xq_w, d_cross)


SUBLANES = 8


def _row_windows(ext, first, count, rows, shift_ref=None):
    for b in range(SUBLANES):
        js = [j for j in range(count) if (first + j) % SUBLANES == b]
        if not js:
            continue
        span = max(first + j for j in js) - b + rows
        shifted = ext[b:b + span, :]
        if shift_ref is not None:
            shift_ref[b, 0:span, :] = shifted
        for j in js:
            a = first + j - b
            yield j, (shifted[a:a + rows, :] if shift_ref is None else shift_ref[b, a:a + rows, :])


def _taps(ext, w_ref, width, base, rows, shift_ref=None):
    acc = None
    for k, win in _row_windows(ext, base - (width - 1), width, rows, shift_ref):
        term = win * w_ref[k:k + 1, :]
        acc = term if acc is None else acc + term
    return acc


def _taps_bwd(d_ext, x, w_ref, width, rows, shift_ref=None):
    acc = None
    dw = [None] * width
    for j, win in _row_windows(d_ext, 0, width, rows, shift_ref):
        k = width - 1 - j
        term = win * w_ref[k:k + 1, :]
        acc = term if acc is None else acc + term
        dw[k] = jnp.sum(win * x, axis=0, keepdims=True)
    return acc, jnp.concatenate(dw, axis=0)


def _conv_fwd(z, cw, cb, lw, lb):
    T = z.shape[0]
    tm = min(512, T)
    hb = tm // CONV_HALO

    def body(val_ref, gate_ref, hval_ref, hgate_ref, cw_ref, cb_ref, lw_ref, lb_ref, o_ref, shift_ref):
        i = pl.program_id(0)
        halo = hval_ref[...] * _sigmoid(hgate_ref[...])
        halo = jnp.where(i == 0, 0.0, halo)
        ext = jnp.concatenate([halo, val_ref[...] * _sigmoid(gate_ref[...])], axis=0)
        y = _taps(ext, cw_ref, CONV_WIDTH, CONV_HALO, tm, shift_ref) + cb_ref[...]
        xc = y - jnp.mean(y, axis=-1, keepdims=True)
        a = xc * lax.rsqrt(jnp.mean(xc * xc, axis=-1, keepdims=True) + LN_EPS) * lw_ref[...] + lb_ref[...]
        o_ref[...] = (a * _sigmoid(a)).astype(o_ref.dtype)

    vec = _bs((1, GW), lambda i: (0, 0))
    halo_spec = lambda col: _bs((CONV_HALO, GW), lambda i: (jnp.maximum(i * hb - 1, 0), col))
    return pl.pallas_call(
        body, grid=(T // tm,),
        in_specs=[_bs((tm, GW), lambda i: (i, 0)), _bs((tm, GW), lambda i: (i, 1)), halo_spec(0), halo_spec(1),
                  _bs((CONV_WIDTH, GW), lambda i: (0, 0)), vec, vec, vec],
        out_specs=_bs((tm, GW), lambda i: (i, 0)), out_shape=S((T, GW), MXU),
        scratch_shapes=[pltpu.VMEM((SUBLANES, tm + CONV_HALO, GW), F32)], name="conv_fwd",
        compiler_params=_cparams("parallel"))(z, z, z, z, cw, cb, lw, lb)


def _conv_bwd(z, cw, cb, lw, lb, d_u):
    T = z.shape[0]
    tm = min(512, T)
    hb = tm // CONV_HALO
    nt = T // tm
    H = CONV_HALO

    def body(val_ref, gate_ref, pval_ref, pgate_ref, nval_ref, ngate_ref, du_ref, ndu_ref, cw_ref, cb_ref, lw_ref,
             lb_ref, dval_ref, dgate_ref, dcw_ref, dcb_ref, dlw_ref, dlb_ref, shift_ref):
        i = pl.program_id(0)
        first = i == 0
        val = jnp.concatenate([pval_ref[...] * jnp.where(first, 0.0, 1.0), val_ref[...], nval_ref[...]], axis=0)
        sg = _sigmoid(jnp.concatenate([pgate_ref[...], gate_ref[...], ngate_ref[...]], axis=0))
        u0 = val * sg
        y = _taps(u0, cw_ref, CONV_WIDTH, H, tm + H, shift_ref) + cb_ref[...]
        xc = y - jnp.mean(y, axis=-1, keepdims=True)
        rs = lax.rsqrt(jnp.mean(xc * xc, axis=-1, keepdims=True) + LN_EPS)
        nh = xc * rs
        a = nh * lw_ref[...] + lb_ref[...]
        sa = _sigmoid(a)
        du = jnp.concatenate([du_ref[...], ndu_ref[...] * jnp.where(i == nt - 1, 0.0, 1.0)], axis=0)
        da = du * (sa * (1.0 + a * (1.0 - sa)))
        dn = da * lw_ref[...]
        dy = rs * (dn - jnp.mean(dn, axis=-1, keepdims=True) - nh * jnp.mean(dn * nh, axis=-1, keepdims=True))
        du0, dcw = _taps_bwd(dy, u0[H:H + tm], cw_ref, CONV_WIDTH, tm, shift_ref)
        v0, s0 = val[H:H + tm], sg[H:H + tm]
        dval_ref[...] = (du0 * s0).astype(dval_ref.dtype)
        dgate_ref[...] = (du0 * v0 * s0 * (1.0 - s0)).astype(dgate_ref.dtype)
        dy0 = dy[:tm]
        _acc_out(dcw_ref, dcw, first)
        _acc_out(dcb_ref, jnp.sum(dy0, axis=0, keepdims=True), first)
        _acc_out(dlw_ref, jnp.sum(da[:tm] * nh[:tm], axis=0, keepdims=True), first)
        _acc_out(dlb_ref, jnp.sum(da[:tm], axis=0, keepdims=True), first)

    vec = _bs((1, GW), lambda i: (0, 0))
    cwspec = _bs((CONV_WIDTH, GW), lambda i: (0, 0))
    prev = lambda col: _bs((H, GW), lambda i: (jnp.maximum(i * hb - 1, 0), col))
    nxt = lambda col: _bs((H, GW), lambda i: (jnp.minimum((i + 1) * hb, nt * hb - 1), col))
    row = _bs((tm, GW), lambda i: (i, 0))
    return pl.pallas_call(
        body, grid=(nt,),
        in_specs=[_bs((tm, GW), lambda i: (i, 0)), _bs((tm, GW), lambda i: (i, 1)), prev(0), prev(1), nxt(0), nxt(1),
                  row, nxt(0), cwspec, vec, vec, vec],
        out_specs=[row, row, cwspec, vec, vec, vec],
        out_shape=[S((T, GW), MXU), S((T, GW), MXU), S((CONV_WIDTH, GW), F32)] + [S((1, GW), F32)] * 3,
        scratch_shapes=[pltpu.VMEM((SUBLANES, tm + 2 * H, GW), F32)], name="conv_bwd", compiler_params=_cparams("arbitrary"))(z, z, z, z, z, z, d_u, d_u, cw, cb, lw, lb)


def _ffn_act_fwd(up0, fw, fb):
    T = up0.shape[0]
    tm = min(256, T)
    hb = tm // FFN_HALO
    H = FFN_HALO

    def body(a_ref, g_ref, pa_ref, pg_ref, wa_ref, wg_ref, ba_ref, bg_ref, o_ref, up_ref):
        i = pl.program_id(0)
        keep = jnp.where(i == 0, 0.0, 1.0)
        ea = jnp.concatenate([pa_ref[...] * keep, a_ref[...]], axis=0)
        eg = jnp.concatenate([pg_ref[...] * keep, g_ref[...]], axis=0)
        av = _taps(ea, wa_ref, FFN_CONV_WIDTH, H, tm) + ba_ref[...]
        gv = _taps(eg, wg_ref, FFN_CONV_WIDTH, H, tm) + bg_ref[...]
        o_ref[...] = (gv * _sigmoid(gv) * av).astype(o_ref.dtype)
        up_ref[:, :D_FF] = av
        up_ref[:, D_FF:] = gv

    col = lambda j: _bs((tm, D_FF), lambda i: (i, j))
    prev = lambda j: _bs((H, D_FF), lambda i: (jnp.maximum(i * hb - 1, 0), j))
    wspec = lambda j: _bs((FFN_CONV_WIDTH, D_FF), lambda i: (0, j))
    bspec = lambda j: _bs((1, D_FF), lambda i: (0, j))
    return pl.pallas_call(
        body, grid=(T // tm,),
        in_specs=[col(0), col(1), prev(0), prev(1), wspec(0), wspec(1), bspec(0), bspec(1)],
        out_specs=[_bs((tm, D_FF), lambda i: (i, 0)), _bs((tm, 2 * D_FF), lambda i: (i, 0))],
        out_shape=[S((T, D_FF), MXU), S((T, 2 * D_FF), F32)], name="ffn_act_fwd",
        compiler_params=_cparams("parallel"))(up0, up0, up0, up0, fw, fw, fb, fb)


def _ffn_act_bwd(up0, up, fw, d_f):
    T = up0.shape[0]
    tm = min(128, T)
    hb = tm // FFN_HALO
    nt = T // tm
    H = FFN_HALO
    W = FFN_CONV_WIDTH

    def body(a_ref, g_ref, av_ref, gv_ref, nav_ref, ngv_ref, df_ref, ndf_ref, wa_ref, wg_ref, dup_ref, dw_ref, db_ref):
        i = pl.program_id(0)
        first = i == 0
        av = jnp.concatenate([av_ref[...], nav_ref[...]], axis=0)
        gv = jnp.concatenate([gv_ref[...], ngv_ref[...]], axis=0)
        df = jnp.concatenate([df_ref[...], ndf_ref[...] * jnp.where(i == nt - 1, 0.0, 1.0)], axis=0)
        sg = _sigmoid(gv)
        d_av = df * gv * sg
        d_gv = df * av * (sg * (1.0 + gv * (1.0 - sg)))
        dua, dwa = _taps_bwd(d_av, a_ref[...], wa_ref, W, tm)
        dug, dwg = _taps_bwd(d_gv, g_ref[...], wg_ref, W, tm)
        dup_ref[:, :D_FF] = dua.astype(dup_ref.dtype)
        dup_ref[:, D_FF:] = dug.astype(dup_ref.dtype)
        dw = jnp.concatenate([dwa, dwg], axis=1)
        db = jnp.concatenate([jnp.sum(d_av[:tm], axis=0, keepdims=True), jnp.sum(d_gv[:tm], axis=0, keepdims=True)], axis=1)
        _acc_out(dw_ref, dw, first)
        _acc_out(db_ref, db, first)

    col = lambda j: _bs((tm, D_FF), lambda i: (i, j))
    nxt = lambda j: _bs((H, D_FF), lambda i: (jnp.minimum((i + 1) * hb, nt * hb - 1), j))
    wspec = lambda j: _bs((W, D_FF), lambda i: (0, j))
    return pl.pallas_call(
        body, grid=(nt,),
        in_specs=[col(0), col(1), col(0), col(1), nxt(0), nxt(1), col(0), nxt(0), wspec(0), wspec(1)],
        out_specs=[_bs((tm, 2 * D_FF), lambda i: (i, 0)), _bs((W, 2 * D_FF), lambda i: (0, 0)),
                   _bs((1, 2 * D_FF), lambda i: (0, 0))],
        out_shape=[S((T, 2 * D_FF), MXU), S((W, 2 * D_FF), F32), S((1, 2 * D_FF), F32)],
        name="ffn_act_bwd", compiler_params=_cparams("arbitrary"))(
            up0, up0, up, up, up, up, d_f, d_f, fw, fw)


def _branch_fwd(attn, u, cross, z, b_gate, wa, wc, wx):
    T = z.shape[0]
    tm = min(512, T)

    def body(a_ref, u_ref, x_ref, g0_ref, g1_ref, g2_ref, b_ref, wa_ref, wc_ref, wx_ref, o_ref):
        acc = None
        for j, (act, g_ref, w_ref) in enumerate(((a_ref, g0_ref, wa_ref), (u_ref, g1_ref, wc_ref), (x_ref, g2_ref, wx_ref))):
            gate = _sigmoid(g_ref[...] + b_ref[:, j * D_MODEL:(j + 1) * D_MODEL])
            term = gate * _dot(act[...], w_ref[...])
            acc = term if acc is None else acc + term
        o_ref[...] = acc.astype(o_ref.dtype)

    act = _bs((tm, GW), lambda i: (i, 0))
    gcol = lambda j: _bs((tm, D_MODEL), lambda i: (i, j))
    wfull = _bs((GW, D_MODEL), lambda i: (0, 0))
    return pl.pallas_call(
        body, grid=(T // tm,),
        in_specs=[act, act, act, gcol(0), gcol(1), gcol(2), _bs((1, 3 * D_MODEL), lambda i: (0, 0)), wfull, wfull, wfull],
        out_specs=_bs((tm, D_MODEL), lambda i: (i, 0)), out_shape=S((T, D_MODEL), MXU), name="branch_fwd",
        compiler_params=_cparams("parallel"))(attn, u, cross, z, z, z, b_gate, wa, wc, wx)


def _branch_bwd(d_merged, attn, u, cross, z, b_gate, wa, wc, wx):
    T = z.shape[0]
    tm = min(512, T)

    def body(dm_ref, a_ref, u_ref, x_ref, g0_ref, g1_ref, g2_ref, b_ref, wa_ref, wc_ref, wx_ref,
             dzg_ref, da_ref, du_ref, dx_ref, dwa_ref, dwc_ref, dwx_ref, db_ref):
        first = pl.program_id(0) == 0
        dm = dm_ref[...]
        dbs = []
        for j, (act, g_ref, w_ref, dact_ref, dw_ref) in enumerate((
                (a_ref, g0_ref, wa_ref, da_ref, dwa_ref), (u_ref, g1_ref, wc_ref, du_ref, dwc_ref),
                (x_ref, g2_ref, wx_ref, dx_ref, dwx_ref))):
            av = act[...]
            gate = _sigmoid(g_ref[...] + b_ref[:, j * D_MODEL:(j + 1) * D_MODEL])
            y = _dot(av, w_ref[...])
            dzg = dm * y * gate * (1.0 - gate)
            dzg_ref[:, j * D_MODEL:(j + 1) * D_MODEL] = dzg.astype(dzg_ref.dtype)
            dbs.append(jnp.sum(dzg, axis=0, keepdims=True))
            dy = (gate * dm).astype(MXU)
            dact_ref[...] = _dot_nt(dy, w_ref[...])
            _acc_out(dw_ref, _dot_tn(av, dy), first)
        _acc_out(db_ref, jnp.concatenate(dbs, axis=1), first)

    act = _bs((tm, GW), lambda i: (i, 0))
    gcol = lambda j: _bs((tm, D_MODEL), lambda i: (i, j))
    wfull = _bs((GW, D_MODEL), lambda i: (0, 0))
    bvec = _bs((1, 3 * D_MODEL), lambda i: (0, 0))
    return pl.pallas_call(
        body, grid=(T // tm,),
        in_specs=[_bs((tm, D_MODEL), lambda i: (i, 0)), act, act, act, gcol(0), gcol(1), gcol(2), bvec, wfull, wfull, wfull],
        out_specs=[_bs((tm, 3 * D_MODEL), lambda i: (i, 0)), act, act, act, wfull, wfull, wfull, bvec],
        out_shape=[S((T, 3 * D_MODEL), MXU)] + [S((T, GW), F32)] * 3 + [S((GW, D_MODEL), F32)] * 3 + [S((1, 3 * D_MODEL), F32)],
        name="branch_bwd", compiler_params=_cparams("arbitrary"))(d_merged, attn, u, cross, z, z, z, b_gate, wa, wc, wx)


def _loss_head(y, target):
    T, D = y.shape
    tm = min(512, T)

    def body(y_ref, t_ref, dy_ref, l_ref):
        e = y_ref[...] - t_ref[...]
        dy_ref[...] = e * (1.0 / D)
        part = jnp.full((8, 128), jnp.sum(e * e), F32)
        _acc_out(l_ref, part, pl.program_id(0) == 0)

    row = _bs((tm, D), lambda i: (i, 0))
    return pl.pallas_call(
        body, grid=(T // tm,), in_specs=[row, row], out_specs=[row, _bs((8, 128), lambda i: (0, 0))],
        out_shape=[S((T, D), F32), S((8, 128), F32)], name="loss_head", compiler_params=_cparams("arbitrary"))(y, target)


def _peer(mask):
    x, y, c = lax.axis_index("x"), lax.axis_index("y"), lax.axis_index("c")
    px = 1 - x if mask & 4 else x
    py = 1 - y if mask & 2 else y
    pc = 1 - c if mask & 1 else c
    return (px, py, pc), 4 * px + 2 * py + pc


def _exchange(arrs, scatter, name):
    n = len(arrs)
    outs_shape = [S(a.shape if scatter else (N_DEV,) + a.shape, a.dtype) for a in arrs]

    def body(*refs):
        ins, outs = refs[:n], refs[n:2 * n]
        send_sems, recv_sems, local_sems = refs[2 * n:]
        me = 4 * lax.axis_index("x") + 2 * lax.axis_index("y") + lax.axis_index("c")
        copies = []
        for w in range(n):
            src = ins[w].at[me] if scatter else ins[w]
            cp = pltpu.make_async_copy(src, outs[w].at[me], local_sems.at[w])
            cp.start()
            copies.append(cp)
        for k in range(1, N_DEV):
            peer, pidx = _peer(k)
            for w in range(n):
                src = ins[w].at[pidx] if scatter else ins[w]
                cp = pltpu.make_async_remote_copy(
                    src_ref=src, dst_ref=outs[w].at[me], send_sem=send_sems.at[w, k - 1], recv_sem=recv_sems.at[w, k - 1],
                    device_id=peer, device_id_type=pl.DeviceIdType.MESH)
                cp.start()
                copies.append(cp)
        for cp in copies:
            cp.wait()

    hbm = pl.BlockSpec(memory_space=pl.ANY)
    return pl.pallas_call(
        body, in_specs=[hbm] * n, out_specs=[hbm] * n, out_shape=outs_shape,
        scratch_shapes=[pltpu.SemaphoreType.DMA((n, N_DEV - 1)), pltpu.SemaphoreType.DMA((n, N_DEV - 1)),
                        pltpu.SemaphoreType.DMA((n,))],
        name=name)(*arrs)


def _exchange_copies(ins, lands, send_sems, recv_sems, local_sems, scatter):
    n = len(ins)
    me = 4 * lax.axis_index("x") + 2 * lax.axis_index("y") + lax.axis_index("c")
    copies = []
    for w in range(n):
        src = ins[w].at[me] if scatter else ins[w]
        copies.append(pltpu.make_async_copy(src, lands[w].at[me], local_sems.at[w]))
    for k in range(1, N_DEV):
        peer, pidx = _peer(k)
        for w in range(n):
            src = ins[w].at[pidx] if scatter else ins[w]
            copies.append(pltpu.make_async_remote_copy(
                src_ref=src, dst_ref=lands[w].at[me], send_sem=send_sems.at[w * (N_DEV - 1) + k - 1],
                recv_sem=recv_sems.at[w * (N_DEV - 1) + k - 1],
                device_id=peer, device_id_type=pl.DeviceIdType.MESH))
    return copies


_HBM_SPEC = pl.BlockSpec(memory_space=pltpu.HBM)
_SEM_SPEC = pl.BlockSpec(memory_space=pltpu.SEMAPHORE)
_DATAFLOW = pltpu.SideEffectType.DATAFLOW_SIDE_EFFECTING


def _exchange_start(arrs, scatter, name, after=None):
    n = len(arrs)
    land_shapes = [a.shape if scatter else (N_DEV,) + a.shape for a in arrs]
    n_in = 2 * n + (after is not None)

    def body(*refs):
        ins, lands = refs[:n], refs[n:2 * n]
        send_sems, recv_sems, local_sems = refs[n_in:n_in + 3]
        token = refs[-1]
        for cp in _exchange_copies(ins, lands, send_sems, recv_sems, local_sems, scatter):
            cp.start()
        token[...] = jnp.zeros_like(token)

    out_shape = ([pltpu.SemaphoreType.DMA((n * (N_DEV - 1),)), pltpu.SemaphoreType.DMA((n * (N_DEV - 1),)),
                  pltpu.SemaphoreType.DMA((n,))]
                 + [pltpu.HBM(a.shape, a.dtype) for a in arrs]
                 + [pltpu.HBM(s, a.dtype) for s, a in zip(land_shapes, arrs)]
                 + [S((8, 128), F32)])
    args = ([pltpu.with_memory_space_constraint(a, pltpu.HBM) for a in arrs]
            + [pltpu.with_memory_space_constraint(lax.empty(s, a.dtype), pltpu.HBM) for s, a in zip(land_shapes, arrs)])
    if after is not None:
        args.append(after)
    outs = pl.pallas_call(
        body, in_specs=[_HBM_SPEC] * (2 * n) + [pl.BlockSpec(memory_space=pl.ANY)] * (after is not None),
        out_specs=[_SEM_SPEC] * 3 + [_HBM_SPEC] * (2 * n) + [pl.BlockSpec(memory_space=pltpu.VMEM)],
        out_shape=out_shape, input_output_aliases={j: 3 + j for j in range(2 * n)},
        name=name, compiler_params=pltpu.CompilerParams(has_side_effects=_DATAFLOW))(*args)
    return (n, scatter, outs[:3], outs[3:3 + n], outs[3 + n:3 + 2 * n]), outs[-1]


def _exchange_wait(state, after, name):
    n, scatter, sems, ins, lands = state

    def body(*refs):
        ins_r, lands_r = refs[:n], refs[n:2 * n]
        send_sems, recv_sems, local_sems = refs[2 * n:2 * n + 3]
        for cp in _exchange_copies(ins_r, lands_r, send_sems, recv_sems, local_sems, scatter):
            cp.wait()

    outs = pl.pallas_call(
        body, in_specs=[_HBM_SPEC] * (2 * n) + [_SEM_SPEC] * 3 + [pl.BlockSpec(memory_space=pl.ANY)],
        out_specs=[_HBM_SPEC] * (2 * n),
        out_shape=[pltpu.HBM(a.shape, a.dtype) for a in ins] + [pltpu.HBM(a.shape, a.dtype) for a in lands],
        input_output_aliases={j: j for j in range(2 * n)},
        name=name, compiler_params=pltpu.CompilerParams(has_side_effects=_DATAFLOW))(*ins, *lands, *sems, after)
    return list(outs[n:])


def _adamw(w, m, v, parts, name):
    R, C = w.shape
    P = parts.shape[0]
    tr = _pick(R, tuple(t for t in (256, 176, 128, 64, 32, 16, 8) if P * t * C * 4 <= ADAMW_BLOCK_BYTES))
    c1 = 1.0 / (1.0 - ADAM_B1 ** ADAM_STEP)
    c2 = 1.0 / (1.0 - ADAM_B2 ** ADAM_STEP)

    def body(w_ref, m_ref, v_ref, p_ref, g_ref, d_ref, nm_ref, nv_ref):
        g = p_ref[0].astype(F32)
        for j in range(1, P):
            g = g + p_ref[j].astype(F32)
        m2 = ADAM_B1 * m_ref[...] + (1.0 - ADAM_B1) * g
        v2 = ADAM_B2 * v_ref[...] + (1.0 - ADAM_B2) * (g * g)
        g_ref[...] = g
        nm_ref[...] = m2
        nv_ref[...] = v2
        d_ref[...] = -ADAM_LR * ((m2 * c1) / (jnp.sqrt(v2 * c2) + ADAM_EPS) + ADAM_WD * w_ref[...])

    row = _bs((tr, C), lambda i: (i, 0))
    return pl.pallas_call(
        body, grid=(R // tr,), in_specs=[row, row, row, _bs((P, tr, C), lambda i: (0, i, 0))], out_specs=[row] * 4,
        out_shape=[S((R, C), F32)] * 4, name=name, compiler_params=_cparams("parallel"))(w, m, v, parts)


def _sum_parts(parts, name):
    P, R, C = parts.shape

    def body(p_ref, o_ref):
        g = p_ref[0]
        for j in range(1, P):
            g = g + p_ref[j]
        o_ref[...] = g

    return pl.pallas_call(body, out_shape=S((R, C), F32), name=name, compiler_params=_cparams())(parts)


def _pack(arrs):
    flat = jnp.concatenate([a.reshape(-1) for a in arrs])
    rows = -(-flat.shape[0] // 1024) * 8
    return jnp.pad(flat, (0, rows * 128 - flat.shape[0])).reshape(rows, 128)


def _unpack(packed, shapes):
    flat = packed.reshape(-1)
    out, off = [], 0
    for s in shapes:
        n = int(np.prod(s))
        out.append(flat[off:off + n].reshape(s))
        off += n
    return out


def _behind(a, token):
    return a if token is None else a + token[0, 0]


def _local_step(x, mem, target, p, comm=None):
    table = p["rel_bias_table"]
    xn = _rms_fwd(x, p["attn_norm_w"], "attn_norm_fwd")
    dils = [dil for _, dil in ATTN_GROUPS]
    xn_c = [_to_classes(xn, dil) for dil in dils]
    w_in = p["w_in"]
    qkv_w = 3 * N_GROUPS * GW
    wq = [jnp.concatenate([w_in[:, (N_GROUPS * part + g) * GW:(N_GROUPS * part + g + 1) * GW] for part in range(3)], axis=1)
          for g in range(N_GROUPS)]
    wc = w_in[:, qkv_w:qkv_w + 3 * GW]
    wg = w_in[:, qkv_w + 3 * GW:]
    zq = [_matmul(xn_c[g], wq[g], name=f"mm_in_qkv{g}") for g in range(N_GROUPS)]
    zc = _matmul(xn, wc, name="mm_in_c")
    zg = _matmul(xn, wg, name="mm_in_g")
    os_, lses = [], []
    for g, dil in enumerate(dils):
        o, l = _attn_fwd(zq[g], table, p["q_norm_w"][g:g + 1], p["k_norm_w"][g:g + 1], g, dil)
        os_.append(_from_classes(o, dil))
        lses.append(_from_classes(l, dil))
    attn, lse = _attn_merge(os_, lses)
    u = _conv_fwd(zc, p["conv_dw_w"], p["conv_dw_b"], p["conv_ln_w"], p["conv_ln_b"])
    if comm is not None:
        p = {**p, **comm.late_weights(after=u)}
    mk, mv = _mem_fwd(mem, p["mem_norm_w"], p["w_mem_kv"], p["xk_norm_w"])
    cross = _cross_fwd(zc, mk, mv, p["xq_norm_w"])
    merged = _branch_fwd(attn, u, cross, zg, p["b_gate"], p["w_attn_o"], p["w_conv_o"], p["w_cross_o"])
    h1 = _matmul(merged, p["w_out"], residual=x, name="mm_out")
    hn = _rms_fwd(h1, p["ffn_norm_w"], "ffn_norm_fwd")
    up0 = _matmul(hn, p["w_up"], name="mm_up")
    f, up = _ffn_act_fwd(up0, p["ffn_conv_w"], p["ffn_conv_b"])
    h2 = _matmul(f, p["w_down"], residual=h1, name="mm_down")
    dh2, lsum = _loss_head(h2, target)
    g = {}
    d_f = _matmul(dh2, p["w_down"], tb=True, name="mm_down_dx")
    g["w_down"] = _matmul(f, dh2, ta=True, name="mm_down_dw")
    d_up0, g["ffn_conv_w"], g["ffn_conv_b"] = _ffn_act_bwd(up0, up, p["ffn_conv_w"], d_f)
    dhn = _matmul(d_up0, p["w_up"], tb=True, name="mm_up_dx")
    g["w_up"] = _matmul(hn, d_up0, ta=True, name="mm_up_dw")
    dh1, g["ffn_norm_w"] = _rms_bwd(h1, p["ffn_norm_w"], [dhn], dh2, "ffn_norm_bwd")
    d_merged = _matmul(dh1, p["w_out"], tb=True, name="mm_out_dx")
    g["w_out"] = _matmul(merged, dh1, ta=True, name="mm_out_dw")
    (d_zg, d_attn, d_u, d_cross, g["w_attn_o"], g["w_conv_o"], g["w_cross_o"], g["b_gate"]) = _branch_bwd(
        d_merged, attn, u, cross, zg, p["b_gate"], p["w_attn_o"], p["w_conv_o"], p["w_cross_o"])
    d_xq, dmk, dmv, g["xq_norm_w"] = _cross_bwd(zc, mk, mv, p["xq_norm_w"], d_cross)
    g["w_mem_kv"], g["mem_norm_w"], g["xk_norm_w"] = _mem_bwd(mem, p["mem_norm_w"], p["w_mem_kv"], p["xk_norm_w"], dmk, dmv)
    tok = comm.start_early_grads(g) if comm is not None else None
    d_val, d_gate, g["conv_dw_w"], g["conv_dw_b"], g["conv_ln_w"], g["conv_ln_b"] = _conv_bwd(
        zc, p["conv_dw_w"], _behind(p["conv_dw_b"], tok), p["conv_ln_w"], p["conv_ln_b"], d_u)
    dzq, dqw, dkw, dtab = [], [], [], []
    for gi, dil in enumerate(dils):
        r = _attn_bwd(zq[gi], table, p["q_norm_w"][gi:gi + 1], p["k_norm_w"][gi:gi + 1], _to_classes(d_attn, dil),
                      _to_classes(attn, dil), _to_classes(lse, dil), gi, dil)
        for lst, val in zip((dzq, dqw, dkw, dtab), r):
            lst.append(val)
    g["q_norm_w"] = jnp.concatenate(dqw, axis=0)
    g["k_norm_w"] = jnp.concatenate(dkw, axis=0)
    g["rel_bias_table"] = jnp.concatenate(dtab, axis=1)
    d_zc = jnp.concatenate([d_val, d_gate, d_xq], axis=1)
    gq = [_matmul(xn_c[gi], dzq[gi], ta=True, name=f"mm_in_qkv{gi}_dw") for gi in range(N_GROUPS)]
    gc = _matmul(xn, d_zc, ta=True, name="mm_in_c_dw")
    gg = _matmul(xn, d_zg, ta=True, name="mm_in_g_dw")
    g["w_in"] = jnp.concatenate(
        [gq[gi][:, part * GW:(part + 1) * GW] for part in range(3) for gi in range(N_GROUPS)] + [gc, gg], axis=1)
    tok = comm.start_w_in_grad(g["w_in"]) if comm is not None else None
    dxn = _matmul(d_zg, wg, tb=True, after=tok, name="mm_in_g_dx")
    dxn = _matmul(d_zc, wc, tb=True, residual=dxn, name="mm_in_c_dx")
    dxn = _matmul(dzq[0], wq[0], tb=True, residual=dxn, name="mm_in_qkv0_dx")
    dxs = [dxn] + [_from_classes(_matmul(dzq[gi], wq[gi], tb=True, name=f"mm_in_qkv{gi}_dx"), dils[gi])
                   for gi in range(1, N_GROUPS)]
    grad_x, g["attn_norm_w"] = _rms_bwd(x, p["attn_norm_w"], dxs, dh1, "attn_norm_bwd")
    return lsum[0, 0], grad_x, g


WEIGHT_NAMES = ["rel_bias_table", "attn_norm_w", "w_in", "b_gate", "q_norm_w", "k_norm_w", "w_attn_o", "conv_dw_w",
                "conv_dw_b", "conv_ln_w", "conv_ln_b", "w_conv_o", "mem_norm_w", "w_mem_kv", "xq_norm_w", "xk_norm_w",
                "w_cross_o", "w_out", "ffn_norm_w", "w_up", "ffn_conv_w", "ffn_conv_b", "w_down"]
COL_SHARDED = ("w_in", "w_attn_o", "w_conv_o", "w_cross_o", "w_up")
ROW_SHARDED = ("w_mem_kv", "w_out", "w_down")
SMALL_COL_SHARDED = ("conv_dw_w", "ffn_conv_w")
BIG = COL_SHARDED + ROW_SHARDED


def _cols_to_blocks(a):
    k, n8 = a.shape
    return a.reshape(k, N_DEV, n8 // N_DEV).transpose(1, 0, 2)


def _blocks_to_cols(a):
    return a.transpose(1, 0, 2).reshape(a.shape[1], N_DEV * a.shape[2])


def _step(x, mem, target, w, m, v):
    me = 4 * lax.axis_index("x") + 2 * lax.axis_index("y") + lax.axis_index("c")

    def to_full(n, blocks):
        return _blocks_to_cols(blocks) if n in COL_SHARDED + SMALL_COL_SHARDED else blocks.reshape(-1, blocks.shape[-1])

    def to_blocks(n, grad):
        blocks = _cols_to_blocks(grad) if n in COL_SHARDED else grad.reshape(N_DEV, -1, grad.shape[-1])
        return blocks.astype(MXU)

    first = ("w_in",) + SMALL_COL_SHARDED
    late = tuple(n for n in BIG if n != "w_in")
    cast = lambda n: w[n].astype(MXU) if n in BIG else w[n]
    first_state, _ = _exchange_start([cast(n) for n in first], False, "gather_first_start")
    late_state, late_token = _exchange_start([cast(n) for n in late], False, "gather_late_start")
    got = _exchange_wait(first_state, late_token, "gather_first_wait")
    p = {n: w[n] for n in WEIGHT_NAMES if n not in BIG + SMALL_COL_SHARDED}
    p.update({n: to_full(n, b) for n, b in zip(first, got)})

    class Comm:
        def late_weights(self, after):
            return {n: to_full(n, b) for n, b in zip(late, _exchange_wait(late_state, after, "gather_late_wait"))}

        def start_early_grads(self, g):
            self.early_state, token = _exchange_start([to_blocks(n, g[n]) for n in late], True, "scatter_early_start")
            return token

        def start_w_in_grad(self, grad):
            self.w_in_state, token = _exchange_start([to_blocks("w_in", grad)], True, "scatter_w_in_start")
            return token

    comm = Comm()
    lsum, grad_x, g = _local_step(x, mem, target, p, comm)
    small_names = [n for n in WEIGHT_NAMES if n not in BIG]
    small_shapes = [g[n].shape for n in small_names]
    small_parts = _exchange([_pack([g[n] for n in small_names])], False, "gather_small_grads")[0]
    gsmall = dict(zip(small_names, _unpack(_sum_parts(small_parts, "sum_small_grads"), small_shapes)))
    for n in SMALL_COL_SHARDED:
        width = w[n].shape[-1]
        gsmall[n] = lax.dynamic_slice_in_dim(gsmall[n], me * width, width, axis=1)
    res = {}
    parts = dict(zip(late, _exchange_wait(comm.early_state, grad_x, "scatter_early_wait")))
    for n in late:
        res[n] = _adamw(w[n], m[n], v[n], parts[n], "adamw_" + n)
    w_in_parts = _exchange_wait(comm.w_in_state, res[late[-1]][1], "scatter_w_in_wait")[0]
    res["w_in"] = _adamw(w["w_in"], m["w_in"], v["w_in"], w_in_parts, "adamw_w_in")
    shapes = [w[n].shape for n in small_names]
    packed = [_pack([d[n] for n in small_names]) for d in (w, m, v, gsmall)]
    outs = _adamw(packed[0], packed[1], packed[2], packed[3][None], "adamw_small")
    unpacked = [_unpack(o, shapes) for o in outs]
    for j, n in enumerate(small_names):
        res[n] = tuple(unpacked[q][j] for q in range(4))
    return lsum, grad_x, res


def kernel(x, mem, rel_bias_table, attn_norm_w, w_in, b_gate, q_norm_w, k_norm_w, w_attn_o, conv_dw_w, conv_dw_b, conv_ln_w, conv_ln_b, w_conv_o, mem_norm_w, w_mem_kv, xq_norm_w, xk_norm_w, w_cross_o, w_out, ffn_norm_w, w_up, ffn_conv_w, ffn_conv_b, w_down, loss_target, m_rel_bias_table, m_attn_norm_w, m_w_in, m_b_gate, m_q_norm_w, m_k_norm_w, m_w_attn_o, m_conv_dw_w, m_conv_dw_b, m_conv_ln_w, m_conv_ln_b, m_w_conv_o, m_mem_norm_w, m_w_mem_kv, m_xq_norm_w, m_xk_norm_w, m_w_cross_o, m_w_out, m_ffn_norm_w, m_w_up, m_ffn_conv_w, m_ffn_conv_b, m_w_down, v_rel_bias_table, v_attn_norm_w, v_w_in, v_b_gate, v_q_norm_w, v_k_norm_w, v_w_attn_o, v_conv_dw_w, v_conv_dw_b, v_conv_ln_w, v_conv_ln_b, v_w_conv_o, v_mem_norm_w, v_w_mem_kv, v_xq_norm_w, v_xk_norm_w, v_w_cross_o, v_w_out, v_ffn_norm_w, v_w_up, v_ffn_conv_w, v_ffn_conv_b, v_w_down):
    ws = dict(zip(WEIGHT_NAMES, (rel_bias_table, attn_norm_w, w_in, b_gate, q_norm_w, k_norm_w, w_attn_o, conv_dw_w, conv_dw_b, conv_ln_w, conv_ln_b, w_conv_o, mem_norm_w, w_mem_kv, xq_norm_w, xk_norm_w, w_cross_o, w_out, ffn_norm_w, w_up, ffn_conv_w, ffn_conv_b, w_down)))
    ms = dict(zip(WEIGHT_NAMES, (m_rel_bias_table, m_attn_norm_w, m_w_in, m_b_gate, m_q_norm_w, m_k_norm_w, m_w_attn_o, m_conv_dw_w, m_conv_dw_b, m_conv_ln_w, m_conv_ln_b, m_w_conv_o, m_mem_norm_w, m_w_mem_kv, m_xq_norm_w, m_xk_norm_w, m_w_cross_o, m_w_out, m_ffn_norm_w, m_w_up, m_ffn_conv_w, m_ffn_conv_b, m_w_down)))
    vs = dict(zip(WEIGHT_NAMES, (v_rel_bias_table, v_attn_norm_w, v_w_in, v_b_gate, v_q_norm_w, v_k_norm_w, v_w_attn_o, v_conv_dw_w, v_conv_dw_b, v_conv_ln_w, v_conv_ln_b, v_w_conv_o, v_mem_norm_w, v_w_mem_kv, v_xq_norm_w, v_xk_norm_w, v_w_cross_o, v_w_out, v_ffn_norm_w, v_w_up, v_ffn_conv_w, v_ffn_conv_b, v_w_down)))
    full_shapes = {n: ws[n].shape for n in WEIGHT_NAMES}

    def squeeze(d):
        return {n: (a if n == "rel_bias_table" else a[0]) for n, a in d.items()}

    w, m, v = squeeze(ws), squeeze(ms), squeeze(vs)
    for d in (w, m, v):
        for n in WEIGHT_NAMES:
            if d[n].ndim == 1:
                d[n] = d[n][None]
    lsum, grad_x, res = _step(x[0], mem[0], loss_target[0], w, m, v)
    loss = lax.psum(0.5 / D_MODEL * lsum, ("x", "y", "c"))
    outs = [loss, grad_x[None]]
    for q in range(4):
        outs += [res[n][q].reshape(full_shapes[n]) for n in WEIGHT_NAMES]
    return tuple(outs)
```

```python
import functools
import math

import numpy as np
import jax
import jax.numpy as jnp
from jax import lax
from jax.experimental import pallas as pl
from jax.experimental.pallas import tpu as pltpu

F32 = jnp.float32
MXU = jnp.bfloat16
S = jax.ShapeDtypeStruct

D_MODEL = 1024
HEAD_DIM = 128
ATTN_GROUPS = ((128, 1), (512, 4), (2048, 16))
N_GROUPS = 3
HEADS = 4
GW = HEADS * HEAD_DIM
CONV_WIDTH = 31
N_MEM = 256
D_FF = 2816
FFN_CONV_WIDTH = 3
N_BUCKETS = 32
MAX_DISTANCE = 2048
RMS_EPS = 1e-6
LN_EPS = 1e-5
N_IN = 9216
NCB = N_IN // GW
BLK = 128
SCALE = HEAD_DIM ** -0.5
NEG = -1e30
N_DEV = 8

ADAM_LR, ADAM_B1, ADAM_B2, ADAM_EPS, ADAM_WD, ADAM_STEP = 0.001, 0.9, 0.999, 1e-08, 0.01, 10

VMEM_LIMIT = 48 * 1024 * 1024
CONV_HALO = 32
FFN_HALO = 8
ADAMW_BLOCK_BYTES = 4 * 1024 * 1024


def _cparams(*sem):
    return pltpu.CompilerParams(dimension_semantics=sem or None, vmem_limit_bytes=VMEM_LIMIT)


def _bs(shape, imap):
    return pl.BlockSpec(shape, imap)


def _dot(a, b):
    return lax.dot_general(a.astype(MXU), b.astype(MXU), (((1,), (0,)), ((), ())), preferred_element_type=F32)


def _dot_nt(a, b):
    return lax.dot_general(a.astype(MXU), b.astype(MXU), (((1,), (1,)), ((), ())), preferred_element_type=F32)


def _dot_tn(a, b):
    return lax.dot_general(a.astype(MXU), b.astype(MXU), (((0,), (0,)), ((), ())), preferred_element_type=F32)


def _sigmoid(x):
    return 0.5 * jnp.tanh(0.5 * x) + 0.5


def _rmsn(x, w):
    r = lax.rsqrt(jnp.mean(x * x, axis=-1, keepdims=True) + RMS_EPS)
    return x * r * w, r


def _rmsn_bwd(x, r, w, dy):
    g = dy * w
    dx = r * g - x * (r * r * r) * jnp.mean(x * g, axis=-1, keepdims=True)
    dw = jnp.sum(dy * x * r, axis=0, keepdims=True)
    return dx, dw


def _acc_out(ref, val, first):
    @pl.when(first)
    def _():
        ref[...] = val

    @pl.when(jnp.logical_not(first))
    def _():
        ref[...] += val


def _rms_fwd(x, w, name):
    T, D = x.shape
    tm = min(512, T)

    def body(x_ref, w_ref, o_ref):
        y, _ = _rmsn(x_ref[...], w_ref[...])
        o_ref[...] = y.astype(o_ref.dtype)

    return pl.pallas_call(
        body, grid=(T // tm,),
        in_specs=[_bs((tm, D), lambda i: (i, 0)), _bs((1, D), lambda i: (0, 0))],
        out_specs=_bs((tm, D), lambda i: (i, 0)),
        out_shape=S((T, D), MXU), name=name, compiler_params=_cparams("parallel"))(x, w)


def _rms_bwd(x, w, dys, resid, name):
    T, D = x.shape
    tm = min(512, T)
    n = len(dys)

    def body(*refs):
        x_ref, w_ref, res_ref = refs[0], refs[1], refs[2 + n]
        dx_ref, dw_ref = refs[3 + n], refs[4 + n]
        xv = x_ref[...]
        dy = refs[2][...]
        for dy_ref in refs[3:2 + n]:
            dy = dy + dy_ref[...]
        _, r = _rmsn(xv, w_ref[...])
        dx, dw = _rmsn_bwd(xv, r, w_ref[...], dy)
        dx_ref[...] = res_ref[...] + dx
        _acc_out(dw_ref, dw, pl.program_id(0) == 0)

    row = _bs((tm, D), lambda i: (i, 0))
    vec = _bs((1, D), lambda i: (0, 0))
    return pl.pallas_call(
        body, grid=(T // tm,), in_specs=[row, vec] + [row] * (n + 1), out_specs=[row, vec],
        out_shape=[S((T, D), F32), S((1, D), F32)], name=name, compiler_params=_cparams("arbitrary"))(x, w, *dys, resid)


def _pick(n, cands):
    for c in cands:
        if n % c == 0:
            return c
    return n


MM_VMEM_BUDGET = 36 * 1024 * 1024


def _mm_tiles(tm, N, K, a_bytes, b_bytes, o_bytes, has_res):
    best = None
    for tn in (1536, 1024, 1408, 512, 256, 128):
        for tk in (3072, 1536, 1024, 1408, 512, 256, 128):
            if N % tn or K % tk:
                continue
            nk = K // tk
            need = 2 * (tm * tk * a_bytes + tk * tn * b_bytes + tm * tn * (o_bytes + 4 * has_res)) + (nk > 1) * tm * tn * 4
            if need > MM_VMEM_BUDGET:
                continue
            key = ((N // tn) * nk, nk)
            if best is None or key < best[0]:
                best = (key, tn, tk)
    if best is None:
        return _pick(N, (128,)), _pick(K, (128,))
    return best[1], best[2]


def _matmul(a, b, *, ta=False, tb=False, out_dtype=F32, residual=None, after=None, tm=None, tn=None, tk=None, name):
    M, K = (a.shape[1], a.shape[0]) if ta else a.shape
    N = b.shape[0] if tb else b.shape[1]
    tm = tm or _pick(M, (1024, 1408, 512, 256, 128))
    if tn is None or tk is None:
        tn, tk = _mm_tiles(tm, N, K, a.dtype.itemsize, b.dtype.itemsize, jnp.dtype(out_dtype).itemsize, residual is not None)
    nk = K // tk
    dn = (((0 if ta else 1,), (1 if tb else 0,)), ((), ()))
    has_res = residual is not None
    n_in = 2 + has_res + (after is not None)

    def body(*refs):
        a_ref, b_ref = refs[0], refs[1]
        res_ref = refs[2] if has_res else None
        o_ref = refs[n_in]
        p = lax.dot_general(a_ref[...].astype(MXU), b_ref[...].astype(MXU), dn, preferred_element_type=F32)

        def finish(acc):
            if has_res:
                acc = acc + res_ref[...]
            o_ref[...] = acc.astype(o_ref.dtype)

        if nk == 1:
            finish(p)
        else:
            acc_ref = refs[-1]
            k = pl.program_id(2)

            @pl.when(k == 0)
            def _():
                acc_ref[...] = p

            @pl.when(k > 0)
            def _():
                acc_ref[...] += p

            @pl.when(k == nk - 1)
            def _():
                finish(acc_ref[...])

    a_spec = _bs((tk, tm), lambda i, j, k: (k, i)) if ta else _bs((tm, tk), lambda i, j, k: (i, k))
    b_spec = _bs((tn, tk), lambda i, j, k: (j, k)) if tb else _bs((tk, tn), lambda i, j, k: (k, j))
    o_spec = _bs((tm, tn), lambda i, j, k: (i, j))
    in_specs = [a_spec, b_spec] + ([o_spec] if has_res else [])
    args = (a, b) + ((residual,) if has_res else ())
    if after is not None:
        in_specs.append(_bs((8, 128), lambda i, j, k: (0, 0)))
        args += (after,)
    return pl.pallas_call(
        body, grid=(M // tm, N // tn, nk), in_specs=in_specs, out_specs=o_spec,
        out_shape=S((M, N), out_dtype), scratch_shapes=[pltpu.VMEM((tm, tn), F32)] if nk > 1 else [],
        name=name, compiler_params=_cparams("parallel", "parallel", "arbitrary"))(*args)


def _bucket_matrix(dilation):
    n = BLK
    qi = np.arange(n)[:, None]
    kj = np.arange(2 * n)[None, :]
    step = qi + n - kj
    dist = np.clip(step, 0, None) * dilation
    max_exact = N_BUCKETS // 2
    d = np.maximum(dist.astype(np.float32), np.float32(1.0))
    large = max_exact + (np.log(d / np.float32(max_exact)) / np.float32(math.log(MAX_DISTANCE / max_exact))
                         * np.float32(N_BUCKETS - max_exact)).astype(np.int32)
    large = np.minimum(large, N_BUCKETS - 1)
    bucket = np.where(dist < max_exact, dist, large)
    band = (step >= 0) & (step <= n)
    return np.where(band, bucket, -1).astype(np.int32)


def _build_bias(tbl_ref, bkt_ref, bias_ref, g):
    bk = bkt_ref[...]
    for h in range(HEADS):
        acc = jnp.full(bk.shape, NEG, F32)
        for b in range(N_BUCKETS):
            acc = jnp.where(bk == b, tbl_ref[b, HEADS * g + h], acc)
        bias_ref[h] = acc


def _to_classes(a, dil):
    if dil == 1:
        return a
    t, n = a.shape
    return a.reshape(t // dil, dil, n).transpose(1, 0, 2).reshape(t, n)


def _from_classes(a, dil):
    if dil == 1:
        return a
    t, n = a.shape
    return a.reshape(dil, t // dil, n).transpose(1, 0, 2).reshape(t, n)


def _attn_fwd(zq, table, qw, kw, g, dil):
    T = zq.shape[0]
    nb = T // dil // BLK
    qb = _pick(nb, (4, 2, 1))
    nt = nb // qb
    bkt = jnp.asarray(_bucket_matrix(dil))

    def zspec(part, prev):
        if prev:
            return _bs((BLK, GW), lambda c, i: (c * nb + jnp.maximum(i * qb - 1, 0), part))
        return _bs((qb * BLK, GW), lambda c, i: (c * nt + i, part))

    def body(tbl_ref, bkt_ref, qw_ref, kw_ref, q_ref, kp_ref, kc_ref, vp_ref, vc_ref, o_ref, lse_ref, bias_ref):
        c, i = pl.program_id(0), pl.program_id(1)

        @pl.when((c == 0) & (i == 0))
        def _():
            _build_bias(tbl_ref, bkt_ref, bias_ref, g)

        kj = lax.broadcasted_iota(jnp.int32, (BLK, 2 * BLK), 1)
        no_prev = jnp.logical_and(i == 0, kj < BLK)
        for h in range(HEADS):
            sl = slice(h * HEAD_DIM, (h + 1) * HEAD_DIM)
            qn, _ = _rmsn(q_ref[:, sl], qw_ref[...])
            kn, _ = _rmsn(jnp.concatenate([kp_ref[:, sl], kc_ref[:, sl]], axis=0), kw_ref[...])
            v = jnp.concatenate([vp_ref[:, sl], vc_ref[:, sl]], axis=0)
            for j in range(qb):
                rows = slice(j * BLK, (j + 1) * BLK)
                keys = slice(j * BLK, (j + 2) * BLK)
                s = _dot_nt(qn[rows], kn[keys]) * SCALE + bias_ref[h]
                if j == 0:
                    s = jnp.where(no_prev, NEG, s)
                m = jnp.max(s, axis=-1, keepdims=True)
                p = jnp.exp(s - m)
                l = jnp.sum(p, axis=-1, keepdims=True)
                o_ref[rows, sl] = _dot(p, v[keys]) / l
                lse_ref[rows, sl] = jnp.broadcast_to(m + jnp.log(l), (BLK, HEAD_DIM))

    ospec = _bs((qb * BLK, GW), lambda c, i: (c * nt + i, 0))
    vec = _bs((1, HEAD_DIM), lambda c, i: (0, 0))
    return pl.pallas_call(
        body, grid=(dil, nt),
        in_specs=[pl.BlockSpec(memory_space=pltpu.SMEM), _bs((BLK, 2 * BLK), lambda c, i: (0, 0)), vec, vec,
                  zspec(0, False), zspec(1, True), zspec(1, False), zspec(2, True), zspec(2, False)],
        out_specs=[ospec, ospec],
        out_shape=[S((T, GW), F32), S((T, GW), F32)],
        scratch_shapes=[pltpu.VMEM((HEADS, BLK, 2 * BLK), F32)],
        name=f"attn_fwd_g{g}", compiler_params=_cparams("arbitrary", "arbitrary"))(table, bkt, qw, kw, zq, zq, zq, zq, zq)


def _classes_to_tokens(src_ref, dst_ref, dil, rows):
    for c in range(dil):
        for h in range(HEADS):
            dst_ref[h, pl.ds(c, rows // dil, stride=dil), :] = src_ref[c, :, h * HEAD_DIM:(h + 1) * HEAD_DIM]


def _tokens_to_classes(src_ref, dst_ref, dil, rows):
    for c in range(dil):
        for h in range(HEADS):
            dst_ref[c, :, h * HEAD_DIM:(h + 1) * HEAD_DIM] = src_ref[h, pl.ds(c, rows // dil, stride=dil), :]


def _class_view(a, dil):
    return a.reshape(dil, a.shape[0] // dil, a.shape[1])


def _attn_merge(os_, lses, dils):
    T = os_[0].shape[0]
    tm = min(512, T)
    assert dils[0] == 1 and len(dils) == 3

    def body(o0, l0, o1, l1, o2, l2, a_ref, lse_ref, a1_ref, lse1_ref, a2_ref, lse2_ref,
             no1, nl1, no2, nl2, ra, rl):
        for src, dst, d in ((o1, no1, dils[1]), (l1, nl1, dils[1]), (o2, no2, dils[2]), (l2, nl2, dils[2])):
            _classes_to_tokens(src, dst, d, tm)
        for h in range(HEADS):
            sl = slice(h * HEAD_DIM, (h + 1) * HEAD_DIM)
            ls = [l0[:, sl], nl1[h], nl2[h]]
            os3 = [o0[:, sl], no1[h], no2[h]]
            mx = jnp.maximum(jnp.maximum(ls[0], ls[1]), ls[2])
            tot = mx + jnp.log(jnp.exp(ls[0] - mx) + jnp.exp(ls[1] - mx) + jnp.exp(ls[2] - mx))
            att = jnp.exp(ls[0] - tot) * os3[0] + jnp.exp(ls[1] - tot) * os3[1] + jnp.exp(ls[2] - tot) * os3[2]
            a_ref[:, sl] = att
            lse_ref[:, sl] = tot
            ra[h] = att
            rl[h] = tot
        for src, dst, d in ((ra, a1_ref, dils[1]), (rl, lse1_ref, dils[1]), (ra, a2_ref, dils[2]), (rl, lse2_ref, dils[2])):
            _tokens_to_classes(src, dst, d, tm)

    row = _bs((tm, GW), lambda i: (i, 0))
    cls = lambda d: _bs((d, tm // d, GW), lambda i: (0, i, 0))
    cshape = lambda d: S((d, T // d, GW), F32)
    slab = pltpu.VMEM((HEADS, tm, HEAD_DIM), F32)
    d1, d2 = dils[1], dils[2]
    attn, lse, a1, l1, a2, l2 = pl.pallas_call(
        body, grid=(T // tm,), in_specs=[row, row, cls(d1), cls(d1), cls(d2), cls(d2)],
        out_specs=[row, row, cls(d1), cls(d1), cls(d2), cls(d2)],
        out_shape=[S((T, GW), F32), S((T, GW), F32), cshape(d1), cshape(d1), cshape(d2), cshape(d2)],
        scratch_shapes=[slab] * 6, name="attn_merge", compiler_params=_cparams("parallel"))(
            os_[0], lses[0], _class_view(os_[1], d1), _class_view(lses[1], d1), _class_view(os_[2], d2),
            _class_view(lses[2], d2))
    flat = lambda a: a.reshape(T, GW)
    return attn, lse, [attn, flat(a1), flat(a2)], [lse, flat(l1), flat(l2)]


def _attn_bwd(zq, table, qw, kw, d_attn, attn, lse, g, dil):
    T = zq.shape[0]
    nb = T // dil // BLK
    qb = _pick(nb, (4, 2, 1))
    nt = nb // qb
    bkt = jnp.asarray(_bucket_matrix(dil))

    def body(tbl_ref, bkt_ref, qw_ref, kw_ref, q_ref, k_ref, v_ref, kp_ref, vp_ref, qx_ref, da_ref, at_ref, lse_ref,
             dax_ref, atx_ref, lsex_ref, dz_ref, dqw_ref, dkw_ref, dtab_ref, bias_ref, dbias_ref):
        c, i = pl.program_id(0), pl.program_id(1)

        @pl.when((c == 0) & (i == 0))
        def _():
            _build_bias(tbl_ref, bkt_ref, bias_ref, g)
            dbias_ref[...] = jnp.zeros_like(dbias_ref)
            dqw_ref[...] = jnp.zeros_like(dqw_ref)
            dkw_ref[...] = jnp.zeros_like(dkw_ref)

        kj = lax.broadcasted_iota(jnp.int32, (BLK, 2 * BLK), 1)
        no_prev = jnp.logical_and(i == 0, kj < BLK)
        has_next = i < nt - 1
        last = slice((qb - 1) * BLK, qb * BLK)
        dqw_acc = jnp.zeros((1, HEAD_DIM), F32)
        dkw_acc = jnp.zeros((1, HEAD_DIM), F32)

        def add(parts, t, val):
            parts[t] = val if parts[t] is None else parts[t] + val

        for h in range(HEADS):
            lo = h * HEAD_DIM
            sl = slice(lo, lo + HEAD_DIM)
            q, k = q_ref[:, sl], k_ref[:, sl]
            qn, rq = _rmsn(q, qw_ref[...])
            kn, rk = _rmsn(k, kw_ref[...])
            kpn, _ = _rmsn(kp_ref[:, sl], kw_ref[...])
            kn_ext = jnp.concatenate([kpn, kn], axis=0)
            v_ext = jnp.concatenate([vp_ref[:, sl], v_ref[:, sl]], axis=0)
            dqn, dkn, dv = [None] * qb, [None] * qb, [None] * qb
            for j in range(qb):
                rows = slice(j * BLK, (j + 1) * BLK)
                keys = slice(j * BLK, (j + 2) * BLK)
                s = _dot_nt(qn[rows], kn_ext[keys]) * SCALE + bias_ref[h]
                if j == 0:
                    s = jnp.where(no_prev, NEG, s)
                p = jnp.exp(s - lse_ref[rows, lo:lo + 1])
                do = da_ref[rows, sl]
                delta = jnp.sum(do * at_ref[rows, sl], axis=-1, keepdims=True)
                ds = p * (_dot_nt(do, v_ext[keys]) - delta)
                dbias_ref[h] += ds
                dqn[j] = _dot(ds, kn_ext[keys]) * SCALE
                dv2 = _dot_tn(p, do)
                dk2 = _dot_tn(ds, qn[rows]) * SCALE
                if j >= 1:
                    add(dv, j - 1, dv2[:BLK])
                    add(dkn, j - 1, dk2[:BLK])
                add(dv, j, dv2[BLK:])
                add(dkn, j, dk2[BLK:])
            qxn, _ = _rmsn(qx_ref[:, sl], qw_ref[...])
            sx = _dot_nt(qxn, kn[last]) * SCALE + bias_ref[h, :, 0:BLK]
            px = jnp.where(has_next, jnp.exp(sx - lsex_ref[:, lo:lo + 1]), 0.0)
            dox = dax_ref[:, sl]
            dsx = px * (_dot_nt(dox, v_ref[last, sl]) - jnp.sum(dox * atx_ref[:, sl], axis=-1, keepdims=True))
            add(dv, qb - 1, _dot_tn(px, dox))
            add(dkn, qb - 1, _dot_tn(dsx, qxn) * SCALE)
            dq, dqw = _rmsn_bwd(q, rq, qw_ref[...], jnp.concatenate(dqn, axis=0))
            dk, dkw = _rmsn_bwd(k, rk, kw_ref[...], jnp.concatenate(dkn, axis=0))
            dqw_acc += dqw
            dkw_acc += dkw
            dz_ref[:, lo:lo + HEAD_DIM] = dq.astype(dz_ref.dtype)
            dz_ref[:, GW + lo:GW + lo + HEAD_DIM] = dk.astype(dz_ref.dtype)
            dz_ref[:, 2 * GW + lo:2 * GW + lo + HEAD_DIM] = jnp.concatenate(dv, axis=0).astype(dz_ref.dtype)
        dqw_ref[...] += dqw_acc
        dkw_ref[...] += dkw_acc

        @pl.when((c == dil - 1) & (i == nt - 1))
        def _():
            bk = bkt_ref[...]
            rows = lax.broadcasted_iota(jnp.int32, (N_BUCKETS, HEAD_DIM), 0)
            lanes = lax.broadcasted_iota(jnp.int32, (N_BUCKETS, HEAD_DIM), 1)
            out = jnp.zeros((N_BUCKETS, HEAD_DIM), F32)
            for h in range(HEADS):
                acc = dbias_ref[h]
                for b in range(N_BUCKETS):
                    val = jnp.sum(jnp.where(bk == b, acc, 0.0))
                    out = jnp.where((rows == b) & (lanes == h), val, out)
            dtab_ref[...] = out

    tile = lambda part: _bs((qb * BLK, GW), lambda c, i: (c * nt + i, part))
    before = lambda part: _bs((BLK, GW), lambda c, i: (c * nb + jnp.maximum(i * qb - 1, 0), part))
    after = lambda part: _bs((BLK, GW), lambda c, i: (c * nb + jnp.minimum((i + 1) * qb, nb - 1), part))
    vec = _bs((1, HEAD_DIM), lambda c, i: (0, 0))
    tabs = _bs((N_BUCKETS, HEAD_DIM), lambda c, i: (0, 0))
    dzq, dqw, dkw, dtab = pl.pallas_call(
        body, grid=(dil, nt),
        in_specs=[pl.BlockSpec(memory_space=pltpu.SMEM), _bs((BLK, 2 * BLK), lambda c, i: (0, 0)), vec, vec,
                  tile(0), tile(1), tile(2), before(1), before(2), after(0), tile(0), tile(0), tile(0),
                  after(0), after(0), after(0)],
        out_specs=[_bs((qb * BLK, 3 * GW), lambda c, i: (c * nt + i, 0)), vec, vec, tabs],
        out_shape=[S((T, 3 * GW), MXU)] + [S((1, HEAD_DIM), F32)] * 2 + [S((N_BUCKETS, HEAD_DIM), F32)],
        scratch_shapes=[pltpu.VMEM((HEADS, BLK, 2 * BLK), F32), pltpu.VMEM((HEADS, BLK, 2 * BLK), F32)],
        name=f"attn_bwd_g{g}", compiler_params=_cparams("arbitrary", "arbitrary"))(
            table, bkt, qw, kw, zq, zq, zq, zq, zq, zq, d_attn, attn, lse, d_attn, attn, lse)
    return dzq, dqw, dkw, dtab[:, :HEADS]


def _mem_fwd(mem, mem_norm_w, w_mem_kv, xk_w):
    def body(mem_ref, nw_ref, w_ref, xk_ref, mk_ref, mv_ref):
        mn, _ = _rmsn(mem_ref[...], nw_ref[...])
        kv = _dot(mn, w_ref[...])
        for h in range(HEADS):
            sl = slice(h * HEAD_DIM, (h + 1) * HEAD_DIM)
            kn, _ = _rmsn(kv[:, sl], xk_ref[...])
            mk_ref[:, sl] = kn.astype(mk_ref.dtype)
        mv_ref[...] = kv[:, GW:].astype(mv_ref.dtype)

    return pl.pallas_call(body, out_shape=[S((N_MEM, GW), MXU), S((N_MEM, GW), MXU)], name="mem_fwd",
                          compiler_params=_cparams())(mem, mem_norm_w, w_mem_kv, xk_w)


def _mem_bwd(mem, mem_norm_w, w_mem_kv, xk_w, dmk, dmv):
    def body(mem_ref, nw_ref, w_ref, xk_ref, dmk_ref, dmv_ref, dw_ref, dnw_ref, dxk_ref):
        memv = mem_ref[...]
        mn, r = _rmsn(memv, nw_ref[...])
        kv = _dot(mn, w_ref[...])
        dxk = jnp.zeros((1, HEAD_DIM), F32)
        parts = []
        for h in range(HEADS):
            sl = slice(h * HEAD_DIM, (h + 1) * HEAD_DIM)
            kh = kv[:, sl]
            _, rk = _rmsn(kh, xk_ref[...])
            dk, dw = _rmsn_bwd(kh, rk, xk_ref[...], dmk_ref[:, sl])
            dxk += dw
            parts.append(dk)
        dkv = jnp.concatenate(parts + [dmv_ref[...]], axis=1)
        dw_ref[...] = _dot_tn(mn, dkv)
        dmn = _dot_nt(dkv, w_ref[...])
        dnw_ref[...] = jnp.sum(dmn * memv * r, axis=0, keepdims=True)
        dxk_ref[...] = dxk

    return pl.pallas_call(
        body, out_shape=[S((D_MODEL, 2 * GW), F32), S((1, D_MODEL), F32), S((1, HEAD_DIM), F32)], name="mem_bwd",
        compiler_params=_cparams())(mem, mem_norm_w, w_mem_kv, xk_w, dmk, dmv)


def _cross_fwd(z, mk, mv, xq_w):
    T = z.shape[0]
    tm = min(512, T)

    def body(q_ref, mk_ref, mv_ref, w_ref, o_ref):
        for h in range(HEADS):
            sl = slice(h * HEAD_DIM, (h + 1) * HEAD_DIM)
            qn, _ = _rmsn(q_ref[:, sl], w_ref[...])
            s = _dot_nt(qn, mk_ref[:, sl]) * SCALE
            e = jnp.exp(s - jnp.max(s, axis=-1, keepdims=True))
            p = e / jnp.sum(e, axis=-1, keepdims=True)
            o_ref[:, sl] = _dot(p, mv_ref[:, sl]).astype(o_ref.dtype)

    full = _bs((N_MEM, GW), lambda i: (0, 0))
    return pl.pallas_call(
        body, grid=(T // tm,),
        in_specs=[_bs((tm, GW), lambda i: (i, 2)), full, full, _bs((1, HEAD_DIM), lambda i: (0, 0))],
        out_specs=_bs((tm, GW), lambda i: (i, 0)), out_shape=S((T, GW), MXU), name="cross_fwd",
        compiler_params=_cparams("parallel"))(z, mk, mv, xq_w)


def _cross_bwd(z, mk, mv, xq_w, d_cross):
    T = z.shape[0]
    tm = min(512, T)

    def body(q_ref, mk_ref, mv_ref, w_ref, do_ref, dq_ref, dmk_ref, dmv_ref, dw_ref):
        first = pl.program_id(0) == 0
        dw_acc = jnp.zeros((1, HEAD_DIM), F32)
        dmk_parts, dmv_parts = [], []
        for h in range(HEADS):
            sl = slice(h * HEAD_DIM, (h + 1) * HEAD_DIM)
            qh = q_ref[:, sl]
            qn, r = _rmsn(qh, w_ref[...])
            s = _dot_nt(qn, mk_ref[:, sl]) * SCALE
            e = jnp.exp(s - jnp.max(s, axis=-1, keepdims=True))
            p = e / jnp.sum(e, axis=-1, keepdims=True)
            do = do_ref[:, sl]
            dp = _dot_nt(do, mv_ref[:, sl])
            ds = p * (dp - jnp.sum(dp * p, axis=-1, keepdims=True)) * SCALE
            dmv_parts.append(_dot_tn(p, do))
            dmk_parts.append(_dot_tn(ds, qn))
            dq, dw = _rmsn_bwd(qh, r, w_ref[...], _dot(ds, mk_ref[:, sl]))
            dw_acc += dw
            dq_ref[:, sl] = dq.astype(dq_ref.dtype)
        _acc_out(dmk_ref, jnp.concatenate(dmk_parts, axis=1), first)
        _acc_out(dmv_ref, jnp.concatenate(dmv_parts, axis=1), first)
        _acc_out(dw_ref, dw_acc, first)

    full = _bs((N_MEM, GW), lambda i: (0, 0))
    vec = _bs((1, HEAD_DIM), lambda i: (0, 0))
    row = _bs((tm, GW), lambda i: (i, 0))
    return pl.pallas_call(
        body, grid=(T // tm,),
        in_specs=[_bs((tm, GW), lambda i: (i, 2)), full, full, vec, row],
        out_specs=[row, full, full, vec],
        out_shape=[S((T, GW), MXU), S((N_MEM, GW), F32), S((N_MEM, GW), F32), S((1, HEAD_DIM), F32)],
        name="cross_bwd", compiler_params=_cparams("arbitrary"))(z, mk, mv, xq_w, d_cross)


SUBLANES = 8


def _row_windows(ext, first, count, rows, shift_ref=None):
    for b in range(SUBLANES):
        js = [j for j in range(count) if (first + j) % SUBLANES == b]
        if not js:
            continue
        span = max(first + j for j in js) - b + rows
        shifted = ext[b:b + span, :]
        if shift_ref is not None:
            shift_ref[b, 0:span, :] = shifted
        for j in js:
            a = first + j - b
            yield j, (shifted[a:a + rows, :] if shift_ref is None else shift_ref[b, a:a + rows, :])


def _taps(ext, w_ref, width, base, rows, shift_ref=None):
    acc = None
    for k, win in _row_windows(ext, base - (width - 1), width, rows, shift_ref):
        term = win * w_ref[k:k + 1, :]
        acc = term if acc is None else acc + term
    return acc


def _taps_bwd(d_ext, x, w_ref, width, rows, shift_ref=None):
    acc = None
    dw = [None] * width
    for j, win in _row_windows(d_ext, 0, width, rows, shift_ref):
        k = width - 1 - j
        term = win * w_ref[k:k + 1, :]
        acc = term if acc is None else acc + term
        dw[k] = jnp.sum(win * x, axis=0, keepdims=True)
    return acc, jnp.concatenate(dw, axis=0)


def _conv_fwd(z, cw, cb, lw, lb):
    T = z.shape[0]
    tm = min(512, T)
    hb = tm // CONV_HALO

    def body(val_ref, gate_ref, hval_ref, hgate_ref, cw_ref, cb_ref, lw_ref, lb_ref, o_ref, shift_ref):
        i = pl.program_id(0)
        halo = hval_ref[...] * _sigmoid(hgate_ref[...])
        halo = jnp.where(i == 0, 0.0, halo)
        ext = jnp.concatenate([halo, val_ref[...] * _sigmoid(gate_ref[...])], axis=0)
        y = _taps(ext, cw_ref, CONV_WIDTH, CONV_HALO, tm, shift_ref) + cb_ref[...]
        xc = y - jnp.mean(y, axis=-1, keepdims=True)
        a = xc * lax.rsqrt(jnp.mean(xc * xc, axis=-1, keepdims=True) + LN_EPS) * lw_ref[...] + lb_ref[...]
        o_ref[...] = (a * _sigmoid(a)).astype(o_ref.dtype)

    vec = _bs((1, GW), lambda i: (0, 0))
    halo_spec = lambda col: _bs((CONV_HALO, GW), lambda i: (jnp.maximum(i * hb - 1, 0), col))
    return pl.pallas_call(
        body, grid=(T // tm,),
        in_specs=[_bs((tm, GW), lambda i: (i, 0)), _bs((tm, GW), lambda i: (i, 1)), halo_spec(0), halo_spec(1),
                  _bs((CONV_WIDTH, GW), lambda i: (0, 0)), vec, vec, vec],
        out_specs=_bs((tm, GW), lambda i: (i, 0)), out_shape=S((T, GW), MXU),
        scratch_shapes=[pltpu.VMEM((SUBLANES, tm + CONV_HALO, GW), F32)], name="conv_fwd",
        compiler_params=_cparams("parallel"))(z, z, z, z, cw, cb, lw, lb)


def _conv_bwd(z, cw, cb, lw, lb, d_u):
    T = z.shape[0]
    tm = min(512, T)
    hb = tm // CONV_HALO
    nt = T // tm
    H = CONV_HALO

    def body(val_ref, gate_ref, pval_ref, pgate_ref, nval_ref, ngate_ref, du_ref, ndu_ref, cw_ref, cb_ref, lw_ref,
             lb_ref, dval_ref, dgate_ref, dcw_ref, dcb_ref, dlw_ref, dlb_ref, shift_ref):
        i = pl.program_id(0)
        first = i == 0
        val = jnp.concatenate([pval_ref[...] * jnp.where(first, 0.0, 1.0), val_ref[...], nval_ref[...]], axis=0)
        sg = _sigmoid(jnp.concatenate([pgate_ref[...], gate_ref[...], ngate_ref[...]], axis=0))
        u0 = val * sg
        y = _taps(u0, cw_ref, CONV_WIDTH, H, tm + H, shift_ref) + cb_ref[...]
        xc = y - jnp.mean(y, axis=-1, keepdims=True)
        rs = lax.rsqrt(jnp.mean(xc * xc, axis=-1, keepdims=True) + LN_EPS)
        nh = xc * rs
        a = nh * lw_ref[...] + lb_ref[...]
        sa = _sigmoid(a)
        du = jnp.concatenate([du_ref[...], ndu_ref[...] * jnp.where(i == nt - 1, 0.0, 1.0)], axis=0)
        da = du * (sa * (1.0 + a * (1.0 - sa)))
        dn = da * lw_ref[...]
        dy = rs * (dn - jnp.mean(dn, axis=-1, keepdims=True) - nh * jnp.mean(dn * nh, axis=-1, keepdims=True))
        du0, dcw = _taps_bwd(dy, u0[H:H + tm], cw_ref, CONV_WIDTH, tm, shift_ref)
        v0, s0 = val[H:H + tm], sg[H:H + tm]
        dval_ref[...] = (du0 * s0).astype(dval_ref.dtype)
        dgate_ref[...] = (du0 * v0 * s0 * (1.0 - s0)).astype(dgate_ref.dtype)
        dy0 = dy[:tm]
        _acc_out(dcw_ref, dcw, first)
        _acc_out(dcb_ref, jnp.sum(dy0, axis=0, keepdims=True), first)
        _acc_out(dlw_ref, jnp.sum(da[:tm] * nh[:tm], axis=0, keepdims=True), first)
        _acc_out(dlb_ref, jnp.sum(da[:tm], axis=0, keepdims=True), first)

    vec = _bs((1, GW), lambda i: (0, 0))
    cwspec = _bs((CONV_WIDTH, GW), lambda i: (0, 0))
    prev = lambda col: _bs((H, GW), lambda i: (jnp.maximum(i * hb - 1, 0), col))
    nxt = lambda col: _bs((H, GW), lambda i: (jnp.minimum((i + 1) * hb, nt * hb - 1), col))
    row = _bs((tm, GW), lambda i: (i, 0))
    return pl.pallas_call(
        body, grid=(nt,),
        in_specs=[_bs((tm, GW), lambda i: (i, 0)), _bs((tm, GW), lambda i: (i, 1)), prev(0), prev(1), nxt(0), nxt(1),
                  row, nxt(0), cwspec, vec, vec, vec],
        out_specs=[row, row, cwspec, vec, vec, vec],
        out_shape=[S((T, GW), MXU), S((T, GW), MXU), S((CONV_WIDTH, GW), F32)] + [S((1, GW), F32)] * 3,
        scratch_shapes=[pltpu.VMEM((SUBLANES, tm + 2 * H, GW), F32)], name="conv_bwd", compiler_params=_cparams("arbitrary"))(z, z, z, z, z, z, d_u, d_u, cw, cb, lw, lb)


def _ffn_act_fwd(up0, fw, fb):
    T = up0.shape[0]
    tm = min(256, T)
    hb = tm // FFN_HALO
    H = FFN_HALO

    def body(a_ref, g_ref, pa_ref, pg_ref, wa_ref, wg_ref, ba_ref, bg_ref, o_ref, up_ref):
        i = pl.program_id(0)
        keep = jnp.where(i == 0, 0.0, 1.0)
        ea = jnp.concatenate([pa_ref[...] * keep, a_ref[...]], axis=0)
        eg = jnp.concatenate([pg_ref[...] * keep, g_ref[...]], axis=0)
        av = _taps(ea, wa_ref, FFN_CONV_WIDTH, H, tm) + ba_ref[...]
        gv = _taps(eg, wg_ref, FFN_CONV_WIDTH, H, tm) + bg_ref[...]
        o_ref[...] = (gv * _sigmoid(gv) * av).astype(o_ref.dtype)
        up_ref[:, :D_FF] = av
        up_ref[:, D_FF:] = gv

    col = lambda j: _bs((tm, D_FF), lambda i: (i, j))
    prev = lambda j: _bs((H, D_FF), lambda i: (jnp.maximum(i * hb - 1, 0), j))
    wspec = lambda j: _bs((FFN_CONV_WIDTH, D_FF), lambda i: (0, j))
    bspec = lambda j: _bs((1, D_FF), lambda i: (0, j))
    return pl.pallas_call(
        body, grid=(T // tm,),
        in_specs=[col(0), col(1), prev(0), prev(1), wspec(0), wspec(1), bspec(0), bspec(1)],
        out_specs=[_bs((tm, D_FF), lambda i: (i, 0)), _bs((tm, 2 * D_FF), lambda i: (i, 0))],
        out_shape=[S((T, D_FF), MXU), S((T, 2 * D_FF), F32)], name="ffn_act_fwd",
        compiler_params=_cparams("parallel"))(up0, up0, up0, up0, fw, fw, fb, fb)


def _ffn_act_bwd(up0, up, fw, d_f):
    T = up0.shape[0]
    tm = min(128, T)
    hb = tm // FFN_HALO
    nt = T // tm
    H = FFN_HALO
    W = FFN_CONV_WIDTH

    def body(a_ref, g_ref, av_ref, gv_ref, nav_ref, ngv_ref, df_ref, ndf_ref, wa_ref, wg_ref, dup_ref, dw_ref, db_ref):
        i = pl.program_id(0)
        first = i == 0
        av = jnp.concatenate([av_ref[...], nav_ref[...]], axis=0)
        gv = jnp.concatenate([gv_ref[...], ngv_ref[...]], axis=0)
        df = jnp.concatenate([df_ref[...], ndf_ref[...] * jnp.where(i == nt - 1, 0.0, 1.0)], axis=0)
        sg = _sigmoid(gv)
        d_av = df * gv * sg
        d_gv = df * av * (sg * (1.0 + gv * (1.0 - sg)))
        dua, dwa = _taps_bwd(d_av, a_ref[...], wa_ref, W, tm)
        dug, dwg = _taps_bwd(d_gv, g_ref[...], wg_ref, W, tm)
        dup_ref[:, :D_FF] = dua.astype(dup_ref.dtype)
        dup_ref[:, D_FF:] = dug.astype(dup_ref.dtype)
        dw = jnp.concatenate([dwa, dwg], axis=1)
        db = jnp.concatenate([jnp.sum(d_av[:tm], axis=0, keepdims=True), jnp.sum(d_gv[:tm], axis=0, keepdims=True)], axis=1)
        _acc_out(dw_ref, dw, first)
        _acc_out(db_ref, db, first)

    col = lambda j: _bs((tm, D_FF), lambda i: (i, j))
    nxt = lambda j: _bs((H, D_FF), lambda i: (jnp.minimum((i + 1) * hb, nt * hb - 1), j))
    wspec = lambda j: _bs((W, D_FF), lambda i: (0, j))
    return pl.pallas_call(
        body, grid=(nt,),
        in_specs=[col(0), col(1), col(0), col(1), nxt(0), nxt(1), col(0), nxt(0), wspec(0), wspec(1)],
        out_specs=[_bs((tm, 2 * D_FF), lambda i: (i, 0)), _bs((W, 2 * D_FF), lambda i: (0, 0)),
                   _bs((1, 2 * D_FF), lambda i: (0, 0))],
        out_shape=[S((T, 2 * D_FF), MXU), S((W, 2 * D_FF), F32), S((1, 2 * D_FF), F32)],
        name="ffn_act_bwd", compiler_params=_cparams("arbitrary"))(
            up0, up0, up, up, up, up, d_f, d_f, fw, fw)


def _branch_fwd(attn, u, cross, z, b_gate, wa, wc, wx):
    T = z.shape[0]
    tm = min(512, T)

    def body(a_ref, u_ref, x_ref, g0_ref, g1_ref, g2_ref, b_ref, wa_ref, wc_ref, wx_ref, o_ref):
        acc = None
        for j, (act, g_ref, w_ref) in enumerate(((a_ref, g0_ref, wa_ref), (u_ref, g1_ref, wc_ref), (x_ref, g2_ref, wx_ref))):
            gate = _sigmoid(g_ref[...] + b_ref[:, j * D_MODEL:(j + 1) * D_MODEL])
            term = gate * _dot(act[...], w_ref[...])
            acc = term if acc is None else acc + term
        o_ref[...] = acc.astype(o_ref.dtype)

    act = _bs((tm, GW), lambda i: (i, 0))
    gcol = lambda j: _bs((tm, D_MODEL), lambda i: (i, j))
    wfull = _bs((GW, D_MODEL), lambda i: (0, 0))
    return pl.pallas_call(
        body, grid=(T // tm,),
        in_specs=[act, act, act, gcol(0), gcol(1), gcol(2), _bs((1, 3 * D_MODEL), lambda i: (0, 0)), wfull, wfull, wfull],
        out_specs=_bs((tm, D_MODEL), lambda i: (i, 0)), out_shape=S((T, D_MODEL), MXU), name="branch_fwd",
        compiler_params=_cparams("parallel"))(attn, u, cross, z, z, z, b_gate, wa, wc, wx)


def _branch_bwd(d_merged, attn, u, cross, z, b_gate, wa, wc, wx, dils):
    T = z.shape[0]
    tm = min(512, T)
    d1, d2 = dils[1], dils[2]

    def body(dm_ref, a_ref, u_ref, x_ref, g0_ref, g1_ref, g2_ref, b_ref, wa_ref, wc_ref, wx_ref,
             dzg_ref, da_ref, du_ref, dx_ref, dwa_ref, dwc_ref, dwx_ref, db_ref, da1_ref, da2_ref, slab_ref):
        first = pl.program_id(0) == 0
        dm = dm_ref[...]
        dbs = []
        for j, (act, g_ref, w_ref, dact_ref, dw_ref) in enumerate((
                (a_ref, g0_ref, wa_ref, da_ref, dwa_ref), (u_ref, g1_ref, wc_ref, du_ref, dwc_ref),
                (x_ref, g2_ref, wx_ref, dx_ref, dwx_ref))):
            av = act[...]
            gate = _sigmoid(g_ref[...] + b_ref[:, j * D_MODEL:(j + 1) * D_MODEL])
            y = _dot(av, w_ref[...])
            dzg = dm * y * gate * (1.0 - gate)
            dzg_ref[:, j * D_MODEL:(j + 1) * D_MODEL] = dzg.astype(dzg_ref.dtype)
            dbs.append(jnp.sum(dzg, axis=0, keepdims=True))
            dy = (gate * dm).astype(MXU)
            dact = _dot_nt(dy, w_ref[...])
            dact_ref[...] = dact
            if j == 0:
                for h in range(HEADS):
                    slab_ref[h] = dact[:, h * HEAD_DIM:(h + 1) * HEAD_DIM]
                _tokens_to_classes(slab_ref, da1_ref, d1, tm)
                _tokens_to_classes(slab_ref, da2_ref, d2, tm)
            _acc_out(dw_ref, _dot_tn(av, dy), first)
        _acc_out(db_ref, jnp.concatenate(dbs, axis=1), first)

    act = _bs((tm, GW), lambda i: (i, 0))
    gcol = lambda j: _bs((tm, D_MODEL), lambda i: (i, j))
    wfull = _bs((GW, D_MODEL), lambda i: (0, 0))
    bvec = _bs((1, 3 * D_MODEL), lambda i: (0, 0))
    cls = lambda d: _bs((d, tm // d, GW), lambda i: (0, i, 0))
    outs = pl.pallas_call(
        body, grid=(T // tm,),
        in_specs=[_bs((tm, D_MODEL), lambda i: (i, 0)), act, act, act, gcol(0), gcol(1), gcol(2), bvec, wfull, wfull, wfull],
        out_specs=[_bs((tm, 3 * D_MODEL), lambda i: (i, 0)), act, act, act, wfull, wfull, wfull, bvec, cls(d1), cls(d2)],
        out_shape=[S((T, 3 * D_MODEL), MXU)] + [S((T, GW), F32)] * 3 + [S((GW, D_MODEL), F32)] * 3 + [S((1, 3 * D_MODEL), F32)]
        + [S((d1, T // d1, GW), F32), S((d2, T // d2, GW), F32)],
        scratch_shapes=[pltpu.VMEM((HEADS, tm, HEAD_DIM), F32)],
        name="branch_bwd", compiler_params=_cparams("arbitrary"))(d_merged, attn, u, cross, z, z, z, b_gate, wa, wc, wx)
    d_zg, d_attn, d_u, d_cross, dwa, dwc, dwx, db, da1, da2 = outs
    return d_zg, [d_attn, da1.reshape(T, GW), da2.reshape(T, GW)], d_u, d_cross, dwa, dwc, dwx, db


def _loss_head(y, target):
    T, D = y.shape
    tm = min(512, T)

    def body(y_ref, t_ref, dy_ref, l_ref):
        e = y_ref[...] - t_ref[...]
        dy_ref[...] = e * (1.0 / D)
        part = jnp.full((8, 128), jnp.sum(e * e), F32)
        _acc_out(l_ref, part, pl.program_id(0) == 0)

    row = _bs((tm, D), lambda i: (i, 0))
    return pl.pallas_call(
        body, grid=(T // tm,), in_specs=[row, row], out_specs=[row, _bs((8, 128), lambda i: (0, 0))],
        out_shape=[S((T, D), F32), S((8, 128), F32)], name="loss_head", compiler_params=_cparams("arbitrary"))(y, target)


def _peer(mask):
    x, y, c = lax.axis_index("x"), lax.axis_index("y"), lax.axis_index("c")
    px = 1 - x if mask & 4 else x
    py = 1 - y if mask & 2 else y
    pc = 1 - c if mask & 1 else c
    return (px, py, pc), 4 * px + 2 * py + pc


def _exchange(arrs, scatter, name):
    n = len(arrs)
    outs_shape = [S(a.shape if scatter else (N_DEV,) + a.shape, a.dtype) for a in arrs]

    def body(*refs):
        ins, outs = refs[:n], refs[n:2 * n]
        send_sems, recv_sems, local_sems = refs[2 * n:]
        me = 4 * lax.axis_index("x") + 2 * lax.axis_index("y") + lax.axis_index("c")
        copies = []
        for w in range(n):
            src = ins[w].at[me] if scatter else ins[w]
            cp = pltpu.make_async_copy(src, outs[w].at[me], local_sems.at[w])
            cp.start()
            copies.append(cp)
        for k in range(1, N_DEV):
            peer, pidx = _peer(k)
            for w in range(n):
                src = ins[w].at[pidx] if scatter else ins[w]
                cp = pltpu.make_async_remote_copy(
                    src_ref=src, dst_ref=outs[w].at[me], send_sem=send_sems.at[w, k - 1], recv_sem=recv_sems.at[w, k - 1],
                    device_id=peer, device_id_type=pl.DeviceIdType.MESH)
                cp.start()
                copies.append(cp)
        for cp in copies:
            cp.wait()

    hbm = pl.BlockSpec(memory_space=pl.ANY)
    return pl.pallas_call(
        body, in_specs=[hbm] * n, out_specs=[hbm] * n, out_shape=outs_shape,
        scratch_shapes=[pltpu.SemaphoreType.DMA((n, N_DEV - 1)), pltpu.SemaphoreType.DMA((n, N_DEV - 1)),
                        pltpu.SemaphoreType.DMA((n,))],
        name=name)(*arrs)


def _exchange_copies(ins, lands, send_sems, recv_sems, local_sems, scatter):
    n = len(ins)
    me = 4 * lax.axis_index("x") + 2 * lax.axis_index("y") + lax.axis_index("c")
    copies = []
    for w in range(n):
        src = ins[w].at[me] if scatter else ins[w]
        copies.append(pltpu.make_async_copy(src, lands[w].at[me], local_sems.at[w]))
    for k in range(1, N_DEV):
        peer, pidx = _peer(k)
        for w in range(n):
            src = ins[w].at[pidx] if scatter else ins[w]
            copies.append(pltpu.make_async_remote_copy(
                src_ref=src, dst_ref=lands[w].at[me], send_sem=send_sems.at[w * (N_DEV - 1) + k - 1],
                recv_sem=recv_sems.at[w * (N_DEV - 1) + k - 1],
                device_id=peer, device_id_type=pl.DeviceIdType.MESH))
    return copies


_HBM_SPEC = pl.BlockSpec(memory_space=pltpu.HBM)
_SEM_SPEC = pl.BlockSpec(memory_space=pltpu.SEMAPHORE)
_DATAFLOW = pltpu.SideEffectType.DATAFLOW_SIDE_EFFECTING


def _exchange_start(arrs, scatter, name, after=None):
    n = len(arrs)
    land_shapes = [a.shape if scatter else (N_DEV,) + a.shape for a in arrs]
    n_in = 2 * n + (after is not None)

    def body(*refs):
        ins, lands = refs[:n], refs[n:2 * n]
        send_sems, recv_sems, local_sems = refs[n_in:n_in + 3]
        token = refs[-1]
        for cp in _exchange_copies(ins, lands, send_sems, recv_sems, local_sems, scatter):
            cp.start()
        token[...] = jnp.zeros_like(token)

    out_shape = ([pltpu.SemaphoreType.DMA((n * (N_DEV - 1),)), pltpu.SemaphoreType.DMA((n * (N_DEV - 1),)),
                  pltpu.SemaphoreType.DMA((n,))]
                 + [pltpu.HBM(a.shape, a.dtype) for a in arrs]
                 + [pltpu.HBM(s, a.dtype) for s, a in zip(land_shapes, arrs)]
                 + [S((8, 128), F32)])
    args = ([pltpu.with_memory_space_constraint(a, pltpu.HBM) for a in arrs]
            + [pltpu.with_memory_space_constraint(lax.empty(s, a.dtype), pltpu.HBM) for s, a in zip(land_shapes, arrs)])
    if after is not None:
        args.append(after)
    outs = pl.pallas_call(
        body, in_specs=[_HBM_SPEC] * (2 * n) + [pl.BlockSpec(memory_space=pl.ANY)] * (after is not None),
        out_specs=[_SEM_SPEC] * 3 + [_HBM_SPEC] * (2 * n) + [pl.BlockSpec(memory_space=pltpu.VMEM)],
        out_shape=out_shape, input_output_aliases={j: 3 + j for j in range(2 * n)},
        name=name, compiler_params=pltpu.CompilerParams(has_side_effects=_DATAFLOW))(*args)
    return (n, scatter, outs[:3], outs[3:3 + n], outs[3 + n:3 + 2 * n]), outs[-1]


def _exchange_wait(state, after, name):
    n, scatter, sems, ins, lands = state

    def body(*refs):
        ins_r, lands_r = refs[:n], refs[n:2 * n]
        send_sems, recv_sems, local_sems = refs[2 * n:2 * n + 3]
        for cp in _exchange_copies(ins_r, lands_r, send_sems, recv_sems, local_sems, scatter):
            cp.wait()

    outs = pl.pallas_call(
        body, in_specs=[_HBM_SPEC] * (2 * n) + [_SEM_SPEC] * 3 + [pl.BlockSpec(memory_space=pl.ANY)],
        out_specs=[_HBM_SPEC] * (2 * n),
        out_shape=[pltpu.HBM(a.shape, a.dtype) for a in ins] + [pltpu.HBM(a.shape, a.dtype) for a in lands],
        input_output_aliases={j: j for j in range(2 * n)},
        name=name, compiler_params=pltpu.CompilerParams(has_side_effects=_DATAFLOW))(*ins, *lands, *sems, after)
    return list(outs[n:])


def _adamw(w, m, v, parts, name):
    R, C = w.shape
    P = parts.shape[0]
    tr = _pick(R, tuple(t for t in (256, 176, 128, 64, 32, 16, 8) if P * t * C * 4 <= ADAMW_BLOCK_BYTES))
    c1 = 1.0 / (1.0 - ADAM_B1 ** ADAM_STEP)
    c2 = 1.0 / (1.0 - ADAM_B2 ** ADAM_STEP)

    def body(w_ref, m_ref, v_ref, p_ref, g_ref, d_ref, nm_ref, nv_ref):
        g = p_ref[0].astype(F32)
        for j in range(1, P):
            g = g + p_ref[j].astype(F32)
        m2 = ADAM_B1 * m_ref[...] + (1.0 - ADAM_B1) * g
        v2 = ADAM_B2 * v_ref[...] + (1.0 - ADAM_B2) * (g * g)
        g_ref[...] = g
        nm_ref[...] = m2
        nv_ref[...] = v2
        d_ref[...] = -ADAM_LR * ((m2 * c1) / (jnp.sqrt(v2 * c2) + ADAM_EPS) + ADAM_WD * w_ref[...])

    row = _bs((tr, C), lambda i: (i, 0))
    return pl.pallas_call(
        body, grid=(R // tr,), in_specs=[row, row, row, _bs((P, tr, C), lambda i: (0, i, 0))], out_specs=[row] * 4,
        out_shape=[S((R, C), F32)] * 4, name=name, compiler_params=_cparams("parallel"))(w, m, v, parts)


def _sum_parts(parts, name):
    P, R, C = parts.shape

    def body(p_ref, o_ref):
        g = p_ref[0]
        for j in range(1, P):
            g = g + p_ref[j]
        o_ref[...] = g

    return pl.pallas_call(body, out_shape=S((R, C), F32), name=name, compiler_params=_cparams())(parts)


def _pack(arrs):
    flat = jnp.concatenate([a.reshape(-1) for a in arrs])
    rows = -(-flat.shape[0] // 1024) * 8
    return jnp.pad(flat, (0, rows * 128 - flat.shape[0])).reshape(rows, 128)


def _unpack(packed, shapes):
    flat = packed.reshape(-1)
    out, off = [], 0
    for s in shapes:
        n = int(np.prod(s))
        out.append(flat[off:off + n].reshape(s))
        off += n
    return out


def _behind(a, token):
    return a if token is None else a + token[0, 0]


def _local_step(x, mem, target, p, comm=None):
    table = p["rel_bias_table"]
    xn = _rms_fwd(x, p["attn_norm_w"], "attn_norm_fwd")
    dils = [dil for _, dil in ATTN_GROUPS]
    xn_c = [_to_classes(xn, dil) for dil in dils]
    w_in = p["w_in"]
    qkv_w = 3 * N_GROUPS * GW
    wq = [jnp.concatenate([w_in[:, (N_GROUPS * part + g) * GW:(N_GROUPS * part + g + 1) * GW] for part in range(3)], axis=1)
          for g in range(N_GROUPS)]
    wc = w_in[:, qkv_w:qkv_w + 3 * GW]
    wg = w_in[:, qkv_w + 3 * GW:]
    zq = [_matmul(xn_c[g], wq[g], name=f"mm_in_qkv{g}") for g in range(N_GROUPS)]
    zc = _matmul(xn, wc, name="mm_in_c")
    zg = _matmul(xn, wg, name="mm_in_g")
    os_, lses = [], []
    for g, dil in enumerate(dils):
        o, l = _attn_fwd(zq[g], table, p["q_norm_w"][g:g + 1], p["k_norm_w"][g:g + 1], g, dil)
        os_.append(o)
        lses.append(l)
    attn, lse, attn_c, lse_c = _attn_merge(os_, lses, dils)
    u = _conv_fwd(zc, p["conv_dw_w"], p["conv_dw_b"], p["conv_ln_w"], p["conv_ln_b"])
    if comm is not None:
        p = {**p, **comm.late_weights(after=u)}
    mk, mv = _mem_fwd(mem, p["mem_norm_w"], p["w_mem_kv"], p["xk_norm_w"])
    cross = _cross_fwd(zc, mk, mv, p["xq_norm_w"])
    merged = _branch_fwd(attn, u, cross, zg, p["b_gate"], p["w_attn_o"], p["w_conv_o"], p["w_cross_o"])
    h1 = _matmul(merged, p["w_out"], residual=x, name="mm_out")
    hn = _rms_fwd(h1, p["ffn_norm_w"], "ffn_norm_fwd")
    up0 = _matmul(hn, p["w_up"], name="mm_up")
    f, up = _ffn_act_fwd(up0, p["ffn_conv_w"], p["ffn_conv_b"])
    h2 = _matmul(f, p["w_down"], residual=h1, name="mm_down")
    dh2, lsum = _loss_head(h2, target)
    g = {}
    d_f = _matmul(dh2, p["w_down"], tb=True, name="mm_down_dx")
    g["w_down"] = _matmul(f, dh2, ta=True, name="mm_down_dw")
    d_up0, g["ffn_conv_w"], g["ffn_conv_b"] = _ffn_act_bwd(up0, up, p["ffn_conv_w"], d_f)
    dhn = _matmul(d_up0, p["w_up"], tb=True, name="mm_up_dx")
    g["w_up"] = _matmul(hn, d_up0, ta=True, name="mm_up_dw")
    dh1, g["ffn_norm_w"] = _rms_bwd(h1, p["ffn_norm_w"], [dhn], dh2, "ffn_norm_bwd")
    d_merged = _matmul(dh1, p["w_out"], tb=True, name="mm_out_dx")
    g["w_out"] = _matmul(merged, dh1, ta=True, name="mm_out_dw")
    (d_zg, d_attn_c, d_u, d_cross, g["w_attn_o"], g["w_conv_o"], g["w_cross_o"], g["b_gate"]) = _branch_bwd(
        d_merged, attn, u, cross, zg, p["b_gate"], p["w_attn_o"], p["w_conv_o"], p["w_cross_o"], dils)
    d_xq, dmk, dmv, g["xq_norm_w"] = _cross_bwd(zc, mk, mv, p["xq_norm_w"], d_cross)
    g["w_mem_kv"], g["mem_norm_w"], g["xk_norm_w"] = _mem_bwd(mem, p["mem_norm_w"], p["w_mem_kv"], p["xk_norm_w"], dmk, dmv)
    tok = comm.start_early_grads(g) if comm is not None else None
    d_val, d_gate, g["conv_dw_w"], g["conv_dw_b"], g["conv_ln_w"], g["conv_ln_b"] = _conv_bwd(
        zc, p["conv_dw_w"], _behind(p["conv_dw_b"], tok), p["conv_ln_w"], p["conv_ln_b"], d_u)
    dzq, dqw, dkw, dtab = [], [], [], []
    for gi, dil in enumerate(dils):
        r = _attn_bwd(zq[gi], table, p["q_norm_w"][gi:gi + 1], p["k_norm_w"][gi:gi + 1], d_attn_c[gi], attn_c[gi],
                      lse_c[gi], gi, dil)
        for lst, val in zip((dzq, dqw, dkw, dtab), r):
            lst.append(val)
    g["q_norm_w"] = jnp.concatenate(dqw, axis=0)
    g["k_norm_w"] = jnp.concatenate(dkw, axis=0)
    g["rel_bias_table"] = jnp.concatenate(dtab, axis=1)
    d_zc = jnp.concatenate([d_val, d_gate, d_xq], axis=1)
    gq = [_matmul(xn_c[gi], dzq[gi], ta=True, name=f"mm_in_qkv{gi}_dw") for gi in range(N_GROUPS)]
    gc = _matmul(xn, d_zc, ta=True, name="mm_in_c_dw")
    gg = _matmul(xn, d_zg, ta=True, name="mm_in_g_dw")
    g["w_in"] = jnp.concatenate(
        [gq[gi][:, part * GW:(part + 1) * GW] for part in range(3) for gi in range(N_GROUPS)] + [gc, gg], axis=1)
    tok = comm.start_w_in_grad(g["w_in"]) if comm is not None else None
    dxn = _matmul(d_zg, wg, tb=True, after=tok, name="mm_in_g_dx")
    dxn = _matmul(d_zc, wc, tb=True, residual=dxn, name="mm_in_c_dx")
    dxn = _matmul(dzq[0], wq[0], tb=True, residual=dxn, name="mm_in_qkv0_dx")
    dxs = [dxn] + [_from_classes(_matmul(dzq[gi], wq[gi], tb=True, name=f"mm_in_qkv{gi}_dx"), dils[gi])
                   for gi in range(1, N_GROUPS)]
    grad_x, g["attn_norm_w"] = _rms_bwd(x, p["attn_norm_w"], dxs, dh1, "attn_norm_bwd")
    return lsum[0, 0], grad_x, g


WEIGHT_NAMES = ["rel_bias_table", "attn_norm_w", "w_in", "b_gate", "q_norm_w", "k_norm_w", "w_attn_o", "conv_dw_w",
                "conv_dw_b", "conv_ln_w", "conv_ln_b", "w_conv_o", "mem_norm_w", "w_mem_kv", "xq_norm_w", "xk_norm_w",
                "w_cross_o", "w_out", "ffn_norm_w", "w_up", "ffn_conv_w", "ffn_conv_b", "w_down"]
COL_SHARDED = ("w_in", "w_attn_o", "w_conv_o", "w_cross_o", "w_up")
ROW_SHARDED = ("w_mem_kv", "w_out", "w_down")
SMALL_COL_SHARDED = ("conv_dw_w", "ffn_conv_w")
BIG = COL_SHARDED + ROW_SHARDED


def _cols_to_blocks(a):
    k, n8 = a.shape
    return a.reshape(k, N_DEV, n8 // N_DEV).transpose(1, 0, 2)


def _blocks_to_cols(a):
    return a.transpose(1, 0, 2).reshape(a.shape[1], N_DEV * a.shape[2])


def _step(x, mem, target, w, m, v):
    me = 4 * lax.axis_index("x") + 2 * lax.axis_index("y") + lax.axis_index("c")

    def to_full(n, blocks):
        return _blocks_to_cols(blocks) if n in COL_SHARDED + SMALL_COL_SHARDED else blocks.reshape(-1, blocks.shape[-1])

    def to_blocks(n, grad):
        blocks = _cols_to_blocks(grad) if n in COL_SHARDED else grad.reshape(N_DEV, -1, grad.shape[-1])
        return blocks.astype(MXU)

    first = ("w_in",) + SMALL_COL_SHARDED
    late = tuple(n for n in BIG if n != "w_in")
    cast = lambda n: w[n].astype(MXU) if n in BIG else w[n]
    first_state, _ = _exchange_start([cast(n) for n in first], False, "gather_first_start")
    late_state, late_token = _exchange_start([cast(n) for n in late], False, "gather_late_start")
    got = _exchange_wait(first_state, late_token, "gather_first_wait")
    p = {n: w[n] for n in WEIGHT_NAMES if n not in BIG + SMALL_COL_SHARDED}
    p.update({n: to_full(n, b) for n, b in zip(first, got)})

    class Comm:
        def late_weights(self, after):
            return {n: to_full(n, b) for n, b in zip(late, _exchange_wait(late_state, after, "gather_late_wait"))}

        def start_early_grads(self, g):
            self.early_state, token = _exchange_start([to_blocks(n, g[n]) for n in late], True, "scatter_early_start")
            return token

        def start_w_in_grad(self, grad):
            self.w_in_state, token = _exchange_start([to_blocks("w_in", grad)], True, "scatter_w_in_start")
            return token

    comm = Comm()
    lsum, grad_x, g = _local_step(x, mem, target, p, comm)
    small_names = [n for n in WEIGHT_NAMES if n not in BIG]
    small_shapes = [g[n].shape for n in small_names]
    small_parts = _exchange([_pack([g[n] for n in small_names])], False, "gather_small_grads")[0]
    gsmall = dict(zip(small_names, _unpack(_sum_parts(small_parts, "sum_small_grads"), small_shapes)))
    for n in SMALL_COL_SHARDED:
        width = w[n].shape[-1]
        gsmall[n] = lax.dynamic_slice_in_dim(gsmall[n], me * width, width, axis=1)
    res = {}
    parts = dict(zip(late, _exchange_wait(comm.early_state, grad_x, "scatter_early_wait")))
    for n in late:
        res[n] = _adamw(w[n], m[n], v[n], parts[n], "adamw_" + n)
    w_in_parts = _exchange_wait(comm.w_in_state, res[late[-1]][1], "scatter_w_in_wait")[0]
    res["w_in"] = _adamw(w["w_in"], m["w_in"], v["w_in"], w_in_parts, "adamw_w_in")
    shapes = [w[n].shape for n in small_names]
    packed = [_pack([d[n] for n in small_names]) for d in (w, m, v, gsmall)]
    outs = _adamw(packed[0], packed[1], packed[2], packed[3][None], "adamw_small")
    unpacked = [_unpack(o, shapes) for o in outs]
    for j, n in enumerate(small_names):
        res[n] = tuple(unpacked[q][j] for q in range(4))
    return lsum, grad_x, res


def kernel(x, mem, rel_bias_table, attn_norm_w, w_in, b_gate, q_norm_w, k_norm_w, w_attn_o, conv_dw_w, conv_dw_b, conv_ln_w, conv_ln_b, w_conv_o, mem_norm_w, w_mem_kv, xq_norm_w, xk_norm_w, w_cross_o, w_out, ffn_norm_w, w_up, ffn_conv_w, ffn_conv_b, w_down, loss_target, m_rel_bias_table, m_attn_norm_w, m_w_in, m_b_gate, m_q_norm_w, m_k_norm_w, m_w_attn_o, m_conv_dw_w, m_conv_dw_b, m_conv_ln_w, m_conv_ln_b, m_w_conv_o, m_mem_norm_w, m_w_mem_kv, m_xq_norm_w, m_xk_norm_w, m_w_cross_o, m_w_out, m_ffn_norm_w, m_w_up, m_ffn_conv_w, m_ffn_conv_b, m_w_down, v_rel_bias_table, v_attn_norm_w, v_w_in, v_b_gate, v_q_norm_w, v_k_norm_w, v_w_attn_o, v_conv_dw_w, v_conv_dw_b, v_conv_ln_w, v_conv_ln_b, v_w_conv_o, v_mem_norm_w, v_w_mem_kv, v_xq_norm_w, v_xk_norm_w, v_w_cross_o, v_w_out, v_ffn_norm_w, v_w_up, v_ffn_conv_w, v_ffn_conv_b, v_w_down):
    ws = dict(zip(WEIGHT_NAMES, (rel_bias_table, attn_norm_w, w_in, b_gate, q_norm_w, k_norm_w, w_attn_o, conv_dw_w, conv_dw_b, conv_ln_w, conv_ln_b, w_conv_o, mem_norm_w, w_mem_kv, xq_norm_w, xk_norm_w, w_cross_o, w_out, ffn_norm_w, w_up, ffn_conv_w, ffn_conv_b, w_down)))
    ms = dict(zip(WEIGHT_NAMES, (m_rel_bias_table, m_attn_norm_w, m_w_in, m_b_gate, m_q_norm_w, m_k_norm_w, m_w_attn_o, m_conv_dw_w, m_conv_dw_b, m_conv_ln_w, m_conv_ln_b, m_w_conv_o, m_mem_norm_w, m_w_mem_kv, m_xq_norm_w, m_xk_norm_w, m_w_cross_o, m_w_out, m_ffn_norm_w, m_w_up, m_ffn_conv_w, m_ffn_conv_b, m_w_down)))
    vs = dict(zip(WEIGHT_NAMES, (v_rel_bias_table, v_attn_norm_w, v_w_in, v_b_gate, v_q_norm_w, v_k_norm_w, v_w_attn_o, v_conv_dw_w, v_conv_dw_b, v_conv_ln_w, v_conv_ln_b, v_w_conv_o, v_mem_norm_w, v_w_mem_kv, v_xq_norm_w, v_xk_norm_w, v_w_cross_o, v_w_out, v_ffn_norm_w, v_w_up, v_ffn_conv_w, v_ffn_conv_b, v_w_down)))
    full_shapes = {n: ws[n].shape for n in WEIGHT_NAMES}

    def squeeze(d):
        return {n: (a if n == "rel_bias_table" else a[0]) for n, a in d.items()}

    w, m, v = squeeze(ws), squeeze(ms), squeeze(vs)
    for d in (w, m, v):
        for n in WEIGHT_NAMES:
            if d[n].ndim == 1:
                d[n] = d[n][None]
    lsum, grad_x, res = _step(x[0], mem[0], loss_target[0], w, m, v)
    loss = lax.psum(0.5 / D_MODEL * lsum, ("x", "y", "c"))
    outs = [loss, grad_x[None]]
    for q in range(4):
        outs += [res[n][q].reshape(full_shapes[n]) for n in WEIGHT_NAMES]
    return tuple(outs)
```

```python
import functools
import math

import numpy as np
import jax
import jax.numpy as jnp
from jax import lax
from jax.experimental import pallas as pl
from jax.experimental.pallas import tpu as pltpu

F32 = jnp.float32
MXU = jnp.bfloat16
S = jax.ShapeDtypeStruct

D_MODEL = 1024
HEAD_DIM = 128
ATTN_GROUPS = ((128, 1), (512, 4), (2048, 16))
N_GROUPS = 3
HEADS = 4
GW = HEADS * HEAD_DIM
CONV_WIDTH = 31
N_MEM = 256
D_FF = 2816
FFN_CONV_WIDTH = 3
N_BUCKETS = 32
MAX_DISTANCE = 2048
RMS_EPS = 1e-6
LN_EPS = 1e-5
BLK = 128
SCALE = HEAD_DIM ** -0.5
NEG = -1e30
N_DEV = 8

ADAM_LR, ADAM_B1, ADAM_B2, ADAM_EPS, ADAM_WD, ADAM_STEP = 0.001, 0.9, 0.999, 1e-08, 0.01, 10

VMEM_LIMIT = 48 * 1024 * 1024
CONV_HALO = 32
FFN_HALO = 8
ADAMW_BLOCK_BYTES = 4 * 1024 * 1024


def _cparams(*sem):
    return pltpu.CompilerParams(dimension_semantics=sem or None, vmem_limit_bytes=VMEM_LIMIT)


def _bs(shape, imap):
    return pl.BlockSpec(shape, imap)


def _dot(a, b):
    return lax.dot_general(a.astype(MXU), b.astype(MXU), (((1,), (0,)), ((), ())), preferred_element_type=F32)


def _dot_nt(a, b):
    return lax.dot_general(a.astype(MXU), b.astype(MXU), (((1,), (1,)), ((), ())), preferred_element_type=F32)


def _dot_tn(a, b):
    return lax.dot_general(a.astype(MXU), b.astype(MXU), (((0,), (0,)), ((), ())), preferred_element_type=F32)


def _sigmoid(x):
    return 0.5 * jnp.tanh(0.5 * x) + 0.5


def _rmsn(x, w):
    r = lax.rsqrt(jnp.mean(x * x, axis=-1, keepdims=True) + RMS_EPS)
    return x * r * w, r


def _rmsn_bwd(x, r, w, dy):
    g = dy * w
    dx = r * g - x * (r * r * r) * jnp.mean(x * g, axis=-1, keepdims=True)
    dw = jnp.sum(dy * x * r, axis=0, keepdims=True)
    return dx, dw


def _acc_out(ref, val, first):
    @pl.when(first)
    def _():
        ref[...] = val

    @pl.when(jnp.logical_not(first))
    def _():
        ref[...] += val


def _rms_fwd(x, w, name):
    T, D = x.shape
    tm = min(512, T)

    def body(x_ref, w_ref, o_ref):
        y, _ = _rmsn(x_ref[...], w_ref[...])
        o_ref[...] = y.astype(o_ref.dtype)

    return pl.pallas_call(
        body, grid=(T // tm,),
        in_specs=[_bs((tm, D), lambda i: (i, 0)), _bs((1, D), lambda i: (0, 0))],
        out_specs=_bs((tm, D), lambda i: (i, 0)),
        out_shape=S((T, D), MXU), name=name, compiler_params=_cparams("parallel"))(x, w)


def _rms_fwd_classes(x, w, dils, name):
    T, D = x.shape
    tm = min(512, T)
    slabs = D // HEAD_DIM

    def body(x_ref, w_ref, o_ref, *rest):
        outs, slab_ref = rest[:-1], rest[-1]
        y, _ = _rmsn(x_ref[...], w_ref[...])
        o_ref[...] = y.astype(o_ref.dtype)
        for j in range(slabs):
            slab_ref[j] = y[:, j * HEAD_DIM:(j + 1) * HEAD_DIM]
        for out_ref, d in zip(outs, dils):
            _tokens_to_classes(slab_ref, out_ref, d, tm)

    outs = pl.pallas_call(
        body, grid=(T // tm,),
        in_specs=[_bs((tm, D), lambda i: (i, 0)), _bs((1, D), lambda i: (0, 0))],
        out_specs=[_bs((tm, D), lambda i: (i, 0))] + [_bs((d, tm // d, D), lambda i: (0, i, 0)) for d in dils],
        out_shape=[S((T, D), MXU)] + [S((d, T // d, D), MXU) for d in dils],
        scratch_shapes=[pltpu.VMEM((slabs, tm, HEAD_DIM), F32)], name=name, compiler_params=_cparams("parallel"))(x, w)
    return outs[0], [a.reshape(T, D) for a in outs[1:]]


def _rms_bwd(x, w, dys, resid, name, class_dys=()):
    T, D = x.shape
    tm = min(512, T)
    n = len(dys)
    nc = len(class_dys)
    slabs = D // HEAD_DIM

    def body(*refs):
        x_ref, w_ref, res_ref = refs[0], refs[1], refs[2 + n + nc]
        dx_ref, dw_ref = refs[3 + n + nc], refs[4 + n + nc]
        xv = x_ref[...]
        dy = refs[2][...]
        for dy_ref in refs[3:2 + n]:
            dy = dy + dy_ref[...]
        for dy_ref, (_, d) in zip(refs[2 + n:2 + n + nc], class_dys):
            slab_ref = refs[-1]
            _classes_to_tokens(dy_ref, slab_ref, d, tm)
            dy = dy + jnp.concatenate([slab_ref[j] for j in range(slabs)], axis=1)
        _, r = _rmsn(xv, w_ref[...])
        dx, dw = _rmsn_bwd(xv, r, w_ref[...], dy)
        dx_ref[...] = res_ref[...] + dx
        _acc_out(dw_ref, dw, pl.program_id(0) == 0)

    row = _bs((tm, D), lambda i: (i, 0))
    vec = _bs((1, D), lambda i: (0, 0))
    return pl.pallas_call(
        body, grid=(T // tm,),
        in_specs=[row, vec] + [row] * n + [_bs((d, tm // d, D), lambda i: (0, i, 0)) for _, d in class_dys] + [row],
        out_specs=[row, vec], out_shape=[S((T, D), F32), S((1, D), F32)],
        scratch_shapes=[pltpu.VMEM((slabs, tm, HEAD_DIM), F32)] if nc else [],
        name=name, compiler_params=_cparams("arbitrary"))(
            x, w, *dys, *[a.reshape(d, T // d, D) for a, d in class_dys], resid)


def _pick(n, cands):
    for c in cands:
        if n % c == 0:
            return c
    return n


MM_VMEM_BUDGET = 36 * 1024 * 1024


def _mm_tiles(tm, N, K, a_bytes, b_bytes, o_bytes, has_res):
    best = None
    for tn in (1536, 1024, 1408, 512, 256, 128):
        for tk in (3072, 1536, 1024, 1408, 512, 256, 128):
            if N % tn or K % tk:
                continue
            nk = K // tk
            need = 2 * (tm * tk * a_bytes + tk * tn * b_bytes + tm * tn * (o_bytes + 4 * has_res)) + (nk > 1) * tm * tn * 4
            if need > MM_VMEM_BUDGET:
                continue
            key = ((N // tn) * nk, nk)
            if best is None or key < best[0]:
                best = (key, tn, tk)
    if best is None:
        return _pick(N, (128,)), _pick(K, (128,))
    return best[1], best[2]


def _matmul(a, b, *, ta=False, tb=False, out_dtype=F32, residual=None, after=None, tm=None, tn=None, tk=None, name):
    M, K = (a.shape[1], a.shape[0]) if ta else a.shape
    N = b.shape[0] if tb else b.shape[1]
    tm = tm or _pick(M, (1024, 1408, 512, 256, 128))
    if tn is None or tk is None:
        tn, tk = _mm_tiles(tm, N, K, a.dtype.itemsize, b.dtype.itemsize, jnp.dtype(out_dtype).itemsize, residual is not None)
    nk = K // tk
    dn = (((0 if ta else 1,), (1 if tb else 0,)), ((), ()))
    has_res = residual is not None
    n_in = 2 + has_res + (after is not None)

    def body(*refs):
        a_ref, b_ref = refs[0], refs[1]
        res_ref = refs[2] if has_res else None
        o_ref = refs[n_in]
        p = lax.dot_general(a_ref[...].astype(MXU), b_ref[...].astype(MXU), dn, preferred_element_type=F32)

        def finish(acc):
            if has_res:
                acc = acc + res_ref[...]
            o_ref[...] = acc.astype(o_ref.dtype)

        if nk == 1:
            finish(p)
        else:
            acc_ref = refs[-1]
            k = pl.program_id(2)

            @pl.when(k == 0)
            def _():
                acc_ref[...] = p

            @pl.when(k > 0)
            def _():
                acc_ref[...] += p

            @pl.when(k == nk - 1)
            def _():
                finish(acc_ref[...])

    a_spec = _bs((tk, tm), lambda i, j, k: (k, i)) if ta else _bs((tm, tk), lambda i, j, k: (i, k))
    b_spec = _bs((tn, tk), lambda i, j, k: (j, k)) if tb else _bs((tk, tn), lambda i, j, k: (k, j))
    o_spec = _bs((tm, tn), lambda i, j, k: (i, j))
    in_specs = [a_spec, b_spec] + ([o_spec] if has_res else [])
    args = (a, b) + ((residual,) if has_res else ())
    if after is not None:
        in_specs.append(_bs((8, 128), lambda i, j, k: (0, 0)))
        args += (after,)
    return pl.pallas_call(
        body, grid=(M // tm, N // tn, nk), in_specs=in_specs, out_specs=o_spec,
        out_shape=S((M, N), out_dtype), scratch_shapes=[pltpu.VMEM((tm, tn), F32)] if nk > 1 else [],
        name=name, compiler_params=_cparams("parallel", "parallel", "arbitrary"))(*args)


def _bucket_matrix(dilation):
    n = BLK
    qi = np.arange(n)[:, None]
    kj = np.arange(2 * n)[None, :]
    step = qi + n - kj
    dist = np.clip(step, 0, None) * dilation
    max_exact = N_BUCKETS // 2
    d = np.maximum(dist.astype(np.float32), np.float32(1.0))
    large = max_exact + (np.log(d / np.float32(max_exact)) / np.float32(math.log(MAX_DISTANCE / max_exact))
                         * np.float32(N_BUCKETS - max_exact)).astype(np.int32)
    large = np.minimum(large, N_BUCKETS - 1)
    bucket = np.where(dist < max_exact, dist, large)
    band = (step >= 0) & (step <= n)
    return np.where(band, bucket, -1).astype(np.int32)


def _build_bias(tbl_ref, bkt_ref, bias_ref, g):
    bk = bkt_ref[...]
    for h in range(HEADS):
        acc = jnp.full(bk.shape, NEG, F32)
        for b in range(N_BUCKETS):
            acc = jnp.where(bk == b, tbl_ref[b, HEADS * g + h], acc)
        bias_ref[h] = acc


def _attn_fwd(zq, table, qw, kw, g, dil):
    T = zq.shape[0]
    nb = T // dil // BLK
    qb = _pick(nb, (4, 2, 1))
    nt = nb // qb
    bkt = jnp.asarray(_bucket_matrix(dil))

    def zspec(part, prev):
        if prev:
            return _bs((BLK, GW), lambda c, i: (c * nb + jnp.maximum(i * qb - 1, 0), part))
        return _bs((qb * BLK, GW), lambda c, i: (c * nt + i, part))

    def body(tbl_ref, bkt_ref, qw_ref, kw_ref, q_ref, kp_ref, kc_ref, vp_ref, vc_ref, o_ref, lse_ref, bias_ref):
        c, i = pl.program_id(0), pl.program_id(1)

        @pl.when((c == 0) & (i == 0))
        def _():
            _build_bias(tbl_ref, bkt_ref, bias_ref, g)

        kj = lax.broadcasted_iota(jnp.int32, (BLK, 2 * BLK), 1)
        no_prev = jnp.logical_and(i == 0, kj < BLK)
        for h in range(HEADS):
            sl = slice(h * HEAD_DIM, (h + 1) * HEAD_DIM)
            qn, _ = _rmsn(q_ref[:, sl], qw_ref[...])
            kn, _ = _rmsn(jnp.concatenate([kp_ref[:, sl], kc_ref[:, sl]], axis=0), kw_ref[...])
            v = jnp.concatenate([vp_ref[:, sl], vc_ref[:, sl]], axis=0)
            for j in range(qb):
                rows = slice(j * BLK, (j + 1) * BLK)
                keys = slice(j * BLK, (j + 2) * BLK)
                s = _dot_nt(qn[rows], kn[keys]) * SCALE + bias_ref[h]
                if j == 0:
                    s = jnp.where(no_prev, NEG, s)
                m = jnp.max(s, axis=-1, keepdims=True)
                p = jnp.exp(s - m)
                l = jnp.sum(p, axis=-1, keepdims=True)
                o_ref[rows, sl] = _dot(p, v[keys]) / l
                lse_ref[rows, sl] = jnp.broadcast_to(m + jnp.log(l), (BLK, HEAD_DIM))

    ospec = _bs((qb * BLK, GW), lambda c, i: (c * nt + i, 0))
    vec = _bs((1, HEAD_DIM), lambda c, i: (0, 0))
    return pl.pallas_call(
        body, grid=(dil, nt),
        in_specs=[pl.BlockSpec(memory_space=pltpu.SMEM), _bs((BLK, 2 * BLK), lambda c, i: (0, 0)), vec, vec,
                  zspec(0, False), zspec(1, True), zspec(1, False), zspec(2, True), zspec(2, False)],
        out_specs=[ospec, ospec],
        out_shape=[S((T, GW), F32), S((T, GW), F32)],
        scratch_shapes=[pltpu.VMEM((HEADS, BLK, 2 * BLK), F32)],
        name=f"attn_fwd_g{g}", compiler_params=_cparams("arbitrary", "arbitrary"))(table, bkt, qw, kw, zq, zq, zq, zq, zq)


def _classes_to_tokens(src_ref, dst_ref, dil, rows):
    for c in range(dil):
        for h in range(dst_ref.shape[0]):
            dst_ref[h, pl.ds(c, rows // dil, stride=dil), :] = src_ref[c, :, h * HEAD_DIM:(h + 1) * HEAD_DIM]


def _tokens_to_classes(src_ref, dst_ref, dil, rows):
    for c in range(dil):
        for h in range(src_ref.shape[0]):
            dst_ref[c, :, h * HEAD_DIM:(h + 1) * HEAD_DIM] = src_ref[h, pl.ds(c, rows // dil, stride=dil), :].astype(dst_ref.dtype)


def _class_view(a, dil):
    return a.reshape(dil, a.shape[0] // dil, a.shape[1])


def _attn_merge(os_, lses, dils):
    T = os_[0].shape[0]
    tm = min(512, T)
    assert dils[0] == 1 and len(dils) == 3

    def body(o0, l0, o1, l1, o2, l2, a_ref, lse_ref, a1_ref, lse1_ref, a2_ref, lse2_ref,
             no1, nl1, no2, nl2, ra, rl):
        for src, dst, d in ((o1, no1, dils[1]), (l1, nl1, dils[1]), (o2, no2, dils[2]), (l2, nl2, dils[2])):
            _classes_to_tokens(src, dst, d, tm)
        for h in range(HEADS):
            sl = slice(h * HEAD_DIM, (h + 1) * HEAD_DIM)
            ls = [l0[:, sl], nl1[h], nl2[h]]
            os3 = [o0[:, sl], no1[h], no2[h]]
            mx = jnp.maximum(jnp.maximum(ls[0], ls[1]), ls[2])
            tot = mx + jnp.log(jnp.exp(ls[0] - mx) + jnp.exp(ls[1] - mx) + jnp.exp(ls[2] - mx))
            att = jnp.exp(ls[0] - tot) * os3[0] + jnp.exp(ls[1] - tot) * os3[1] + jnp.exp(ls[2] - tot) * os3[2]
            a_ref[:, sl] = att
            lse_ref[:, sl] = tot
            ra[h] = att
            rl[h] = tot
        for src, dst, d in ((ra, a1_ref, dils[1]), (rl, lse1_ref, dils[1]), (ra, a2_ref, dils[2]), (rl, lse2_ref, dils[2])):
            _tokens_to_classes(src, dst, d, tm)

    row = _bs((tm, GW), lambda i: (i, 0))
    cls = lambda d: _bs((d, tm // d, GW), lambda i: (0, i, 0))
    cshape = lambda d: S((d, T // d, GW), F32)
    slab = pltpu.VMEM((HEADS, tm, HEAD_DIM), F32)
    d1, d2 = dils[1], dils[2]
    attn, lse, a1, l1, a2, l2 = pl.pallas_call(
        body, grid=(T // tm,), in_specs=[row, row, cls(d1), cls(d1), cls(d2), cls(d2)],
        out_specs=[row, row, cls(d1), cls(d1), cls(d2), cls(d2)],
        out_shape=[S((T, GW), F32), S((T, GW), F32), cshape(d1), cshape(d1), cshape(d2), cshape(d2)],
        scratch_shapes=[slab] * 6, name="attn_merge", compiler_params=_cparams("parallel"))(
            os_[0], lses[0], _class_view(os_[1], d1), _class_view(lses[1], d1), _class_view(os_[2], d2),
            _class_view(lses[2], d2))
    flat = lambda a: a.reshape(T, GW)
    return attn, lse, [attn, flat(a1), flat(a2)], [lse, flat(l1), flat(l2)]


def _attn_bwd(zq, table, qw, kw, d_attn, attn, lse, g, dil):
    T = zq.shape[0]
    nb = T // dil // BLK
    qb = _pick(nb, (4, 2, 1))
    nt = nb // qb
    bkt = jnp.asarray(_bucket_matrix(dil))

    def body(tbl_ref, bkt_ref, qw_ref, kw_ref, q_ref, k_ref, v_ref, kp_ref, vp_ref, qx_ref, da_ref, at_ref, lse_ref,
             dax_ref, atx_ref, lsex_ref, dz_ref, dqw_ref, dkw_ref, dtab_ref, bias_ref, dbias_ref):
        c, i = pl.program_id(0), pl.program_id(1)

        @pl.when((c == 0) & (i == 0))
        def _():
            _build_bias(tbl_ref, bkt_ref, bias_ref, g)
            dbias_ref[...] = jnp.zeros_like(dbias_ref)
            dqw_ref[...] = jnp.zeros_like(dqw_ref)
            dkw_ref[...] = jnp.zeros_like(dkw_ref)

        kj = lax.broadcasted_iota(jnp.int32, (BLK, 2 * BLK), 1)
        no_prev = jnp.logical_and(i == 0, kj < BLK)
        has_next = i < nt - 1
        last = slice((qb - 1) * BLK, qb * BLK)
        dqw_acc = jnp.zeros((1, HEAD_DIM), F32)
        dkw_acc = jnp.zeros((1, HEAD_DIM), F32)

        def add(parts, t, val):
            parts[t] = val if parts[t] is None else parts[t] + val

        for h in range(HEADS):
            lo = h * HEAD_DIM
            sl = slice(lo, lo + HEAD_DIM)
            q, k = q_ref[:, sl], k_ref[:, sl]
            qn, rq = _rmsn(q, qw_ref[...])
            kn, rk = _rmsn(k, kw_ref[...])
            kpn, _ = _rmsn(kp_ref[:, sl], kw_ref[...])
            kn_ext = jnp.concatenate([kpn, kn], axis=0)
            v_ext = jnp.concatenate([vp_ref[:, sl], v_ref[:, sl]], axis=0)
            dqn, dkn, dv = [None] * qb, [None] * qb, [None] * qb
            for j in range(qb):
                rows = slice(j * BLK, (j + 1) * BLK)
                keys = slice(j * BLK, (j + 2) * BLK)
                s = _dot_nt(qn[rows], kn_ext[keys]) * SCALE + bias_ref[h]
                if j == 0:
                    s = jnp.where(no_prev, NEG, s)
                p = jnp.exp(s - lse_ref[rows, lo:lo + 1])
                do = da_ref[rows, sl]
                delta = jnp.sum(do * at_ref[rows, sl], axis=-1, keepdims=True)
                ds = p * (_dot_nt(do, v_ext[keys]) - delta)
                dbias_ref[h] += ds
                dqn[j] = _dot(ds, kn_ext[keys]) * SCALE
                dv2 = _dot_tn(p, do)
                dk2 = _dot_tn(ds, qn[rows]) * SCALE
                if j >= 1:
                    add(dv, j - 1, dv2[:BLK])
                    add(dkn, j - 1, dk2[:BLK])
                add(dv, j, dv2[BLK:])
                add(dkn, j, dk2[BLK:])
            qxn, _ = _rmsn(qx_ref[:, sl], qw_ref[...])
            sx = _dot_nt(qxn, kn[last]) * SCALE + bias_ref[h, :, 0:BLK]
            px = jnp.where(has_next, jnp.exp(sx - lsex_ref[:, lo:lo + 1]), 0.0)
            dox = dax_ref[:, sl]
            dsx = px * (_dot_nt(dox, v_ref[last, sl]) - jnp.sum(dox * atx_ref[:, sl], axis=-1, keepdims=True))
            add(dv, qb - 1, _dot_tn(px, dox))
            add(dkn, qb - 1, _dot_tn(dsx, qxn) * SCALE)
            dq, dqw = _rmsn_bwd(q, rq, qw_ref[...], jnp.concatenate(dqn, axis=0))
            dk, dkw = _rmsn_bwd(k, rk, kw_ref[...], jnp.concatenate(dkn, axis=0))
            dqw_acc += dqw
            dkw_acc += dkw
            dz_ref[:, lo:lo + HEAD_DIM] = dq.astype(dz_ref.dtype)
            dz_ref[:, GW + lo:GW + lo + HEAD_DIM] = dk.astype(dz_ref.dtype)
            dz_ref[:, 2 * GW + lo:2 * GW + lo + HEAD_DIM] = jnp.concatenate(dv, axis=0).astype(dz_ref.dtype)
        dqw_ref[...] += dqw_acc
        dkw_ref[...] += dkw_acc

        @pl.when((c == dil - 1) & (i == nt - 1))
        def _():
            bk = bkt_ref[...]
            rows = lax.broadcasted_iota(jnp.int32, (N_BUCKETS, HEAD_DIM), 0)
            lanes = lax.broadcasted_iota(jnp.int32, (N_BUCKETS, HEAD_DIM), 1)
            out = jnp.zeros((N_BUCKETS, HEAD_DIM), F32)
            for h in range(HEADS):
                acc = dbias_ref[h]
                for b in range(N_BUCKETS):
                    val = jnp.sum(jnp.where(bk == b, acc, 0.0))
                    out = jnp.where((rows == b) & (lanes == h), val, out)
            dtab_ref[...] = out

    tile = lambda part: _bs((qb * BLK, GW), lambda c, i: (c * nt + i, part))
    before = lambda part: _bs((BLK, GW), lambda c, i: (c * nb + jnp.maximum(i * qb - 1, 0), part))
    after = lambda part: _bs((BLK, GW), lambda c, i: (c * nb + jnp.minimum((i + 1) * qb, nb - 1), part))
    vec = _bs((1, HEAD_DIM), lambda c, i: (0, 0))
    tabs = _bs((N_BUCKETS, HEAD_DIM), lambda c, i: (0, 0))
    dzq, dqw, dkw, dtab = pl.pallas_call(
        body, grid=(dil, nt),
        in_specs=[pl.BlockSpec(memory_space=pltpu.SMEM), _bs((BLK, 2 * BLK), lambda c, i: (0, 0)), vec, vec,
                  tile(0), tile(1), tile(2), before(1), before(2), after(0), tile(0), tile(0), tile(0),
                  after(0), after(0), after(0)],
        out_specs=[_bs((qb * BLK, 3 * GW), lambda c, i: (c * nt + i, 0)), vec, vec, tabs],
        out_shape=[S((T, 3 * GW), MXU)] + [S((1, HEAD_DIM), F32)] * 2 + [S((N_BUCKETS, HEAD_DIM), F32)],
        scratch_shapes=[pltpu.VMEM((HEADS, BLK, 2 * BLK), F32), pltpu.VMEM((HEADS, BLK, 2 * BLK), F32)],
        name=f"attn_bwd_g{g}", compiler_params=_cparams("arbitrary", "arbitrary"))(
            table, bkt, qw, kw, zq, zq, zq, zq, zq, zq, d_attn, attn, lse, d_attn, attn, lse)
    return dzq, dqw, dkw, dtab[:, :HEADS]


def _mem_fwd(mem, mem_norm_w, w_mem_kv, xk_w):
    def body(mem_ref, nw_ref, w_ref, xk_ref, mk_ref, mv_ref):
        mn, _ = _rmsn(mem_ref[...], nw_ref[...])
        kv = _dot(mn, w_ref[...])
        for h in range(HEADS):
            sl = slice(h * HEAD_DIM, (h + 1) * HEAD_DIM)
            kn, _ = _rmsn(kv[:, sl], xk_ref[...])
            mk_ref[:, sl] = kn.astype(mk_ref.dtype)
        mv_ref[...] = kv[:, GW:].astype(mv_ref.dtype)

    return pl.pallas_call(body, out_shape=[S((N_MEM, GW), MXU), S((N_MEM, GW), MXU)], name="mem_fwd",
                          compiler_params=_cparams())(mem, mem_norm_w, w_mem_kv, xk_w)


def _mem_bwd(mem, mem_norm_w, w_mem_kv, xk_w, dmk, dmv):
    def body(mem_ref, nw_ref, w_ref, xk_ref, dmk_ref, dmv_ref, dw_ref, dnw_ref, dxk_ref):
        memv = mem_ref[...]
        mn, r = _rmsn(memv, nw_ref[...])
        kv = _dot(mn, w_ref[...])
        dxk = jnp.zeros((1, HEAD_DIM), F32)
        parts = []
        for h in range(HEADS):
            sl = slice(h * HEAD_DIM, (h + 1) * HEAD_DIM)
            kh = kv[:, sl]
            _, rk = _rmsn(kh, xk_ref[...])
            dk, dw = _rmsn_bwd(kh, rk, xk_ref[...], dmk_ref[:, sl])
            dxk += dw
            parts.append(dk)
        dkv = jnp.concatenate(parts + [dmv_ref[...]], axis=1)
        dw_ref[...] = _dot_tn(mn, dkv)
        dmn = _dot_nt(dkv, w_ref[...])
        dnw_ref[...] = jnp.sum(dmn * memv * r, axis=0, keepdims=True)
        dxk_ref[...] = dxk

    return pl.pallas_call(
        body, out_shape=[S((D_MODEL, 2 * GW), F32), S((1, D_MODEL), F32), S((1, HEAD_DIM), F32)], name="mem_bwd",
        compiler_params=_cparams())(mem, mem_norm_w, w_mem_kv, xk_w, dmk, dmv)


def _cross_fwd(z, mk, mv, xq_w):
    T = z.shape[0]
    tm = min(512, T)

    def body(q_ref, mk_ref, mv_ref, w_ref, o_ref):
        for h in range(HEADS):
            sl = slice(h * HEAD_DIM, (h + 1) * HEAD_DIM)
            qn, _ = _rmsn(q_ref[:, sl], w_ref[...])
            s = _dot_nt(qn, mk_ref[:, sl]) * SCALE
            e = jnp.exp(s - jnp.max(s, axis=-1, keepdims=True))
            p = e / jnp.sum(e, axis=-1, keepdims=True)
            o_ref[:, sl] = _dot(p, mv_ref[:, sl]).astype(o_ref.dtype)

    full = _bs((N_MEM, GW), lambda i: (0, 0))
    return pl.pallas_call(
        body, grid=(T // tm,),
        in_specs=[_bs((tm, GW), lambda i: (i, 2)), full, full, _bs((1, HEAD_DIM), lambda i: (0, 0))],
        out_specs=_bs((tm, GW), lambda i: (i, 0)), out_shape=S((T, GW), MXU), name="cross_fwd",
        compiler_params=_cparams("parallel"))(z, mk, mv, xq_w)


def _cross_bwd(z, mk, mv, xq_w, d_cross):
    T = z.shape[0]
    tm = min(512, T)

    def body(q_ref, mk_ref, mv_ref, w_ref, do_ref, dq_ref, dmk_ref, dmv_ref, dw_ref):
        first = pl.program_id(0) == 0
        dw_acc = jnp.zeros((1, HEAD_DIM), F32)
        dmk_parts, dmv_parts = [], []
        for h in range(HEADS):
            sl = slice(h * HEAD_DIM, (h + 1) * HEAD_DIM)
            qh = q_ref[:, sl]
            qn, r = _rmsn(qh, w_ref[...])
            s = _dot_nt(qn, mk_ref[:, sl]) * SCALE
            e = jnp.exp(s - jnp.max(s, axis=-1, keepdims=True))
            p = e / jnp.sum(e, axis=-1, keepdims=True)
            do = do_ref[:, sl]
            dp = _dot_nt(do, mv_ref[:, sl])
            ds = p * (dp - jnp.sum(dp * p, axis=-1, keepdims=True)) * SCALE
            dmv_parts.append(_dot_tn(p, do))
            dmk_parts.append(_dot_tn(ds, qn))
            dq, dw = _rmsn_bwd(qh, r, w_ref[...], _dot(ds, mk_ref[:, sl]))
            dw_acc += dw
            dq_ref[:, sl] = dq.astype(dq_ref.dtype)
        _acc_out(dmk_ref, jnp.concatenate(dmk_parts, axis=1), first)
        _acc_out(dmv_ref, jnp.concatenate(dmv_parts, axis=1), first)
        _acc_out(dw_ref, dw_acc, first)

    full = _bs((N_MEM, GW), lambda i: (0, 0))
    vec = _bs((1, HEAD_DIM), lambda i: (0, 0))
    row = _bs((tm, GW), lambda i: (i, 0))
    return pl.pallas_call(
        body, grid=(T // tm,),
        in_specs=[_bs((tm, GW), lambda i: (i, 2)), full, full, vec, row],
        out_specs=[row, full, full, vec],
        out_shape=[S((T, GW), MXU), S((N_MEM, GW), F32), S((N_MEM, GW), F32), S((1, HEAD_DIM), F32)],
        name="cross_bwd", compiler_params=_cparams("arbitrary"))(z, mk, mv, xq_w, d_cross)


SUBLANES = 8


def _row_windows(ext, first, count, rows, shift_ref=None):
    for b in range(SUBLANES):
        js = [j for j in range(count) if (first + j) % SUBLANES == b]
        if not js:
            continue
        span = max(first + j for j in js) - b + rows
        shifted = ext[b:b + span, :]
        if shift_ref is not None:
            shift_ref[b, 0:span, :] = shifted
        for j in js:
            a = first + j - b
            yield j, (shifted[a:a + rows, :] if shift_ref is None else shift_ref[b, a:a + rows, :])


def _taps(ext, w_ref, width, base, rows, shift_ref=None):
    acc = None
    for k, win in _row_windows(ext, base - (width - 1), width, rows, shift_ref):
        term = win * w_ref[k:k + 1, :]
        acc = term if acc is None else acc + term
    return acc


def _taps_bwd(d_ext, x, w_ref, width, rows, shift_ref=None):
    acc = None
    dw = [None] * width
    for j, win in _row_windows(d_ext, 0, width, rows, shift_ref):
        k = width - 1 - j
        term = win * w_ref[k:k + 1, :]
        acc = term if acc is None else acc + term
        dw[k] = jnp.sum(win * x, axis=0, keepdims=True)
    return acc, jnp.concatenate(dw, axis=0)


def _conv_fwd(z, cw, cb, lw, lb):
    T = z.shape[0]
    tm = min(512, T)
    hb = tm // CONV_HALO

    def body(val_ref, gate_ref, hval_ref, hgate_ref, cw_ref, cb_ref, lw_ref, lb_ref, o_ref, shift_ref):
        i = pl.program_id(0)
        halo = hval_ref[...] * _sigmoid(hgate_ref[...])
        halo = jnp.where(i == 0, 0.0, halo)
        ext = jnp.concatenate([halo, val_ref[...] * _sigmoid(gate_ref[...])], axis=0)
        y = _taps(ext, cw_ref, CONV_WIDTH, CONV_HALO, tm, shift_ref) + cb_ref[...]
        xc = y - jnp.mean(y, axis=-1, keepdims=True)
        a = xc * lax.rsqrt(jnp.mean(xc * xc, axis=-1, keepdims=True) + LN_EPS) * lw_ref[...] + lb_ref[...]
        o_ref[...] = (a * _sigmoid(a)).astype(o_ref.dtype)

    vec = _bs((1, GW), lambda i: (0, 0))
    halo_spec = lambda col: _bs((CONV_HALO, GW), lambda i: (jnp.maximum(i * hb - 1, 0), col))
    return pl.pallas_call(
        body, grid=(T // tm,),
        in_specs=[_bs((tm, GW), lambda i: (i, 0)), _bs((tm, GW), lambda i: (i, 1)), halo_spec(0), halo_spec(1),
                  _bs((CONV_WIDTH, GW), lambda i: (0, 0)), vec, vec, vec],
        out_specs=_bs((tm, GW), lambda i: (i, 0)), out_shape=S((T, GW), MXU),
        scratch_shapes=[pltpu.VMEM((SUBLANES, tm + CONV_HALO, GW), F32)], name="conv_fwd",
        compiler_params=_cparams("parallel"))(z, z, z, z, cw, cb, lw, lb)


def _conv_bwd(z, cw, cb, lw, lb, d_u):
    T = z.shape[0]
    tm = min(512, T)
    hb = tm // CONV_HALO
    nt = T // tm
    H = CONV_HALO

    def body(val_ref, gate_ref, pval_ref, pgate_ref, nval_ref, ngate_ref, du_ref, ndu_ref, cw_ref, cb_ref, lw_ref,
             lb_ref, dval_ref, dgate_ref, dcw_ref, dcb_ref, dlw_ref, dlb_ref, shift_ref):
        i = pl.program_id(0)
        first = i == 0
        val = jnp.concatenate([pval_ref[...] * jnp.where(first, 0.0, 1.0), val_ref[...], nval_ref[...]], axis=0)
        sg = _sigmoid(jnp.concatenate([pgate_ref[...], gate_ref[...], ngate_ref[...]], axis=0))
        u0 = val * sg
        y = _taps(u0, cw_ref, CONV_WIDTH, H, tm + H, shift_ref) + cb_ref[...]
        xc = y - jnp.mean(y, axis=-1, keepdims=True)
        rs = lax.rsqrt(jnp.mean(xc * xc, axis=-1, keepdims=True) + LN_EPS)
        nh = xc * rs
        a = nh * lw_ref[...] + lb_ref[...]
        sa = _sigmoid(a)
        du = jnp.concatenate([du_ref[...], ndu_ref[...] * jnp.where(i == nt - 1, 0.0, 1.0)], axis=0)
        da = du * (sa * (1.0 + a * (1.0 - sa)))
        dn = da * lw_ref[...]
        dy = rs * (dn - jnp.mean(dn, axis=-1, keepdims=True) - nh * jnp.mean(dn * nh, axis=-1, keepdims=True))
        du0, dcw = _taps_bwd(dy, u0[H:H + tm], cw_ref, CONV_WIDTH, tm, shift_ref)
        v0, s0 = val[H:H + tm], sg[H:H + tm]
        dval_ref[...] = (du0 * s0).astype(dval_ref.dtype)
        dgate_ref[...] = (du0 * v0 * s0 * (1.0 - s0)).astype(dgate_ref.dtype)
        dy0 = dy[:tm]
        _acc_out(dcw_ref, dcw, first)
        _acc_out(dcb_ref, jnp.sum(dy0, axis=0, keepdims=True), first)
        _acc_out(dlw_ref, jnp.sum(da[:tm] * nh[:tm], axis=0, keepdims=True), first)
        _acc_out(dlb_ref, jnp.sum(da[:tm], axis=0, keepdims=True), first)

    vec = _bs((1, GW), lambda i: (0, 0))
    cwspec = _bs((CONV_WIDTH, GW), lambda i: (0, 0))
    prev = lambda col: _bs((H, GW), lambda i: (jnp.maximum(i * hb - 1, 0), col))
    nxt = lambda col: _bs((H, GW), lambda i: (jnp.minimum((i + 1) * hb, nt * hb - 1), col))
    row = _bs((tm, GW), lambda i: (i, 0))
    return pl.pallas_call(
        body, grid=(nt,),
        in_specs=[_bs((tm, GW), lambda i: (i, 0)), _bs((tm, GW), lambda i: (i, 1)), prev(0), prev(1), nxt(0), nxt(1),
                  row, nxt(0), cwspec, vec, vec, vec],
        out_specs=[row, row, cwspec, vec, vec, vec],
        out_shape=[S((T, GW), MXU), S((T, GW), MXU), S((CONV_WIDTH, GW), F32)] + [S((1, GW), F32)] * 3,
        scratch_shapes=[pltpu.VMEM((SUBLANES, tm + 2 * H, GW), F32)], name="conv_bwd", compiler_params=_cparams("arbitrary"))(z, z, z, z, z, z, d_u, d_u, cw, cb, lw, lb)


def _ffn_act_fwd(up0, fw, fb):
    T = up0.shape[0]
    tm = min(256, T)
    hb = tm // FFN_HALO
    H = FFN_HALO

    def body(a_ref, g_ref, pa_ref, pg_ref, wa_ref, wg_ref, ba_ref, bg_ref, o_ref, up_ref):
        i = pl.program_id(0)
        keep = jnp.where(i == 0, 0.0, 1.0)
        ea = jnp.concatenate([pa_ref[...] * keep, a_ref[...]], axis=0)
        eg = jnp.concatenate([pg_ref[...] * keep, g_ref[...]], axis=0)
        av = _taps(ea, wa_ref, FFN_CONV_WIDTH, H, tm) + ba_ref[...]
        gv = _taps(eg, wg_ref, FFN_CONV_WIDTH, H, tm) + bg_ref[...]
        o_ref[...] = (gv * _sigmoid(gv) * av).astype(o_ref.dtype)
        up_ref[:, :D_FF] = av
        up_ref[:, D_FF:] = gv

    col = lambda j: _bs((tm, D_FF), lambda i: (i, j))
    prev = lambda j: _bs((H, D_FF), lambda i: (jnp.maximum(i * hb - 1, 0), j))
    wspec = lambda j: _bs((FFN_CONV_WIDTH, D_FF), lambda i: (0, j))
    bspec = lambda j: _bs((1, D_FF), lambda i: (0, j))
    return pl.pallas_call(
        body, grid=(T // tm,),
        in_specs=[col(0), col(1), prev(0), prev(1), wspec(0), wspec(1), bspec(0), bspec(1)],
        out_specs=[_bs((tm, D_FF), lambda i: (i, 0)), _bs((tm, 2 * D_FF), lambda i: (i, 0))],
        out_shape=[S((T, D_FF), MXU), S((T, 2 * D_FF), F32)], name="ffn_act_fwd",
        compiler_params=_cparams("parallel"))(up0, up0, up0, up0, fw, fw, fb, fb)


def _ffn_act_bwd(up0, up, fw, d_f):
    T = up0.shape[0]
    tm = min(128, T)
    hb = tm // FFN_HALO
    nt = T // tm
    H = FFN_HALO
    W = FFN_CONV_WIDTH

    def body(a_ref, g_ref, av_ref, gv_ref, nav_ref, ngv_ref, df_ref, ndf_ref, wa_ref, wg_ref, dup_ref, dw_ref, db_ref):
        i = pl.program_id(0)
        first = i == 0
        av = jnp.concatenate([av_ref[...], nav_ref[...]], axis=0)
        gv = jnp.concatenate([gv_ref[...], ngv_ref[...]], axis=0)
        df = jnp.concatenate([df_ref[...], ndf_ref[...] * jnp.where(i == nt - 1, 0.0, 1.0)], axis=0)
        sg = _sigmoid(gv)
        d_av = df * gv * sg
        d_gv = df * av * (sg * (1.0 + gv * (1.0 - sg)))
        dua, dwa = _taps_bwd(d_av, a_ref[...], wa_ref, W, tm)
        dug, dwg = _taps_bwd(d_gv, g_ref[...], wg_ref, W, tm)
        dup_ref[:, :D_FF] = dua.astype(dup_ref.dtype)
        dup_ref[:, D_FF:] = dug.astype(dup_ref.dtype)
        dw = jnp.concatenate([dwa, dwg], axis=1)
        db = jnp.concatenate([jnp.sum(d_av[:tm], axis=0, keepdims=True), jnp.sum(d_gv[:tm], axis=0, keepdims=True)], axis=1)
        _acc_out(dw_ref, dw, first)
        _acc_out(db_ref, db, first)

    col = lambda j: _bs((tm, D_FF), lambda i: (i, j))
    nxt = lambda j: _bs((H, D_FF), lambda i: (jnp.minimum((i + 1) * hb, nt * hb - 1), j))
    wspec = lambda j: _bs((W, D_FF), lambda i: (0, j))
    return pl.pallas_call(
        body, grid=(nt,),
        in_specs=[col(0), col(1), col(0), col(1), nxt(0), nxt(1), col(0), nxt(0), wspec(0), wspec(1)],
        out_specs=[_bs((tm, 2 * D_FF), lambda i: (i, 0)), _bs((W, 2 * D_FF), lambda i: (0, 0)),
                   _bs((1, 2 * D_FF), lambda i: (0, 0))],
        out_shape=[S((T, 2 * D_FF), MXU), S((W, 2 * D_FF), F32), S((1, 2 * D_FF), F32)],
        name="ffn_act_bwd", compiler_params=_cparams("arbitrary"))(
            up0, up0, up, up, up, up, d_f, d_f, fw, fw)


def _branch_fwd(attn, u, cross, z, b_gate, wa, wc, wx):
    T = z.shape[0]
    tm = min(512, T)

    def body(a_ref, u_ref, x_ref, g0_ref, g1_ref, g2_ref, b_ref, wa_ref, wc_ref, wx_ref, o_ref):
        acc = None
        for j, (act, g_ref, w_ref) in enumerate(((a_ref, g0_ref, wa_ref), (u_ref, g1_ref, wc_ref), (x_ref, g2_ref, wx_ref))):
            gate = _sigmoid(g_ref[...] + b_ref[:, j * D_MODEL:(j + 1) * D_MODEL])
            term = gate * _dot(act[...], w_ref[...])
            acc = term if acc is None else acc + term
        o_ref[...] = acc.astype(o_ref.dtype)

    act = _bs((tm, GW), lambda i: (i, 0))
    gcol = lambda j: _bs((tm, D_MODEL), lambda i: (i, j))
    wfull = _bs((GW, D_MODEL), lambda i: (0, 0))
    return pl.pallas_call(
        body, grid=(T // tm,),
        in_specs=[act, act, act, gcol(0), gcol(1), gcol(2), _bs((1, 3 * D_MODEL), lambda i: (0, 0)), wfull, wfull, wfull],
        out_specs=_bs((tm, D_MODEL), lambda i: (i, 0)), out_shape=S((T, D_MODEL), MXU), name="branch_fwd",
        compiler_params=_cparams("parallel"))(attn, u, cross, z, z, z, b_gate, wa, wc, wx)


def _branch_bwd(d_merged, attn, u, cross, z, b_gate, wa, wc, wx, dils):
    T = z.shape[0]
    tm = min(512, T)
    d1, d2 = dils[1], dils[2]

    def body(dm_ref, a_ref, u_ref, x_ref, g0_ref, g1_ref, g2_ref, b_ref, wa_ref, wc_ref, wx_ref,
             dzg_ref, da_ref, du_ref, dx_ref, dwa_ref, dwc_ref, dwx_ref, db_ref, da1_ref, da2_ref, slab_ref):
        first = pl.program_id(0) == 0
        dm = dm_ref[...]
        dbs = []
        for j, (act, g_ref, w_ref, dact_ref, dw_ref) in enumerate((
                (a_ref, g0_ref, wa_ref, da_ref, dwa_ref), (u_ref, g1_ref, wc_ref, du_ref, dwc_ref),
                (x_ref, g2_ref, wx_ref, dx_ref, dwx_ref))):
            av = act[...]
            gate = _sigmoid(g_ref[...] + b_ref[:, j * D_MODEL:(j + 1) * D_MODEL])
            y = _dot(av, w_ref[...])
            dzg = dm * y * gate * (1.0 - gate)
            dzg_ref[:, j * D_MODEL:(j + 1) * D_MODEL] = dzg.astype(dzg_ref.dtype)
            dbs.append(jnp.sum(dzg, axis=0, keepdims=True))
            dy = (gate * dm).astype(MXU)
            dact = _dot_nt(dy, w_ref[...])
            dact_ref[...] = dact
            if j == 0:
                for h in range(HEADS):
                    slab_ref[h] = dact[:, h * HEAD_DIM:(h + 1) * HEAD_DIM]
                _tokens_to_classes(slab_ref, da1_ref, d1, tm)
                _tokens_to_classes(slab_ref, da2_ref, d2, tm)
            _acc_out(dw_ref, _dot_tn(av, dy), first)
        _acc_out(db_ref, jnp.concatenate(dbs, axis=1), first)

    act = _bs((tm, GW), lambda i: (i, 0))
    gcol = lambda j: _bs((tm, D_MODEL), lambda i: (i, j))
    wfull = _bs((GW, D_MODEL), lambda i: (0, 0))
    bvec = _bs((1, 3 * D_MODEL), lambda i: (0, 0))
    cls = lambda d: _bs((d, tm // d, GW), lambda i: (0, i, 0))
    outs = pl.pallas_call(
        body, grid=(T // tm,),
        in_specs=[_bs((tm, D_MODEL), lambda i: (i, 0)), act, act, act, gcol(0), gcol(1), gcol(2), bvec, wfull, wfull, wfull],
        out_specs=[_bs((tm, 3 * D_MODEL), lambda i: (i, 0)), act, act, act, wfull, wfull, wfull, bvec, cls(d1), cls(d2)],
        out_shape=[S((T, 3 * D_MODEL), MXU)] + [S((T, GW), F32)] * 3 + [S((GW, D_MODEL), F32)] * 3 + [S((1, 3 * D_MODEL), F32)]
        + [S((d1, T // d1, GW), F32), S((d2, T // d2, GW), F32)],
        scratch_shapes=[pltpu.VMEM((HEADS, tm, HEAD_DIM), F32)],
        name="branch_bwd", compiler_params=_cparams("arbitrary"))(d_merged, attn, u, cross, z, z, z, b_gate, wa, wc, wx)
    d_zg, d_attn, d_u, d_cross, dwa, dwc, dwx, db, da1, da2 = outs
    return d_zg, [d_attn, da1.reshape(T, GW), da2.reshape(T, GW)], d_u, d_cross, dwa, dwc, dwx, db


def _loss_head(y, target):
    T, D = y.shape
    tm = min(512, T)

    def body(y_ref, t_ref, dy_ref, l_ref):
        e = y_ref[...] - t_ref[...]
        dy_ref[...] = e * (1.0 / D)
        part = jnp.full((8, 128), jnp.sum(e * e), F32)
        _acc_out(l_ref, part, pl.program_id(0) == 0)

    row = _bs((tm, D), lambda i: (i, 0))
    return pl.pallas_call(
        body, grid=(T // tm,), in_specs=[row, row], out_specs=[row, _bs((8, 128), lambda i: (0, 0))],
        out_shape=[S((T, D), F32), S((8, 128), F32)], name="loss_head", compiler_params=_cparams("arbitrary"))(y, target)


def _peer(mask):
    x, y, c = lax.axis_index("x"), lax.axis_index("y"), lax.axis_index("c")
    px = 1 - x if mask & 4 else x
    py = 1 - y if mask & 2 else y
    pc = 1 - c if mask & 1 else c
    return (px, py, pc), 4 * px + 2 * py + pc


def _exchange(arrs, scatter, name):
    n = len(arrs)
    outs_shape = [S(a.shape if scatter else (N_DEV,) + a.shape, a.dtype) for a in arrs]

    def body(*refs):
        ins, outs = refs[:n], refs[n:2 * n]
        send_sems, recv_sems, local_sems = refs[2 * n:]
        me = 4 * lax.axis_index("x") + 2 * lax.axis_index("y") + lax.axis_index("c")
        copies = []
        for w in range(n):
            src = ins[w].at[me] if scatter else ins[w]
            cp = pltpu.make_async_copy(src, outs[w].at[me], local_sems.at[w])
            cp.start()
            copies.append(cp)
        for k in range(1, N_DEV):
            peer, pidx = _peer(k)
            for w in range(n):
                src = ins[w].at[pidx] if scatter else ins[w]
                cp = pltpu.make_async_remote_copy(
                    src_ref=src, dst_ref=outs[w].at[me], send_sem=send_sems.at[w, k - 1], recv_sem=recv_sems.at[w, k - 1],
                    device_id=peer, device_id_type=pl.DeviceIdType.MESH)
                cp.start()
                copies.append(cp)
        for cp in copies:
            cp.wait()

    hbm = pl.BlockSpec(memory_space=pl.ANY)
    return pl.pallas_call(
        body, in_specs=[hbm] * n, out_specs=[hbm] * n, out_shape=outs_shape,
        scratch_shapes=[pltpu.SemaphoreType.DMA((n, N_DEV - 1)), pltpu.SemaphoreType.DMA((n, N_DEV - 1)),
                        pltpu.SemaphoreType.DMA((n,))],
        name=name)(*arrs)


def _exchange_copies(ins, lands, send_sems, recv_sems, local_sems, scatter):
    n = len(ins)
    me = 4 * lax.axis_index("x") + 2 * lax.axis_index("y") + lax.axis_index("c")
    copies = []
    for w in range(n):
        src = ins[w].at[me] if scatter else ins[w]
        copies.append(pltpu.make_async_copy(src, lands[w].at[me], local_sems.at[w]))
    for k in range(1, N_DEV):
        peer, pidx = _peer(k)
        for w in range(n):
            src = ins[w].at[pidx] if scatter else ins[w]
            copies.append(pltpu.make_async_remote_copy(
                src_ref=src, dst_ref=lands[w].at[me], send_sem=send_sems.at[w * (N_DEV - 1) + k - 1],
                recv_sem=recv_sems.at[w * (N_DEV - 1) + k - 1],
                device_id=peer, device_id_type=pl.DeviceIdType.MESH))
    return copies


_HBM_SPEC = pl.BlockSpec(memory_space=pltpu.HBM)
_SEM_SPEC = pl.BlockSpec(memory_space=pltpu.SEMAPHORE)
_DATAFLOW = pltpu.SideEffectType.DATAFLOW_SIDE_EFFECTING


def _exchange_start(arrs, scatter, name):
    n = len(arrs)
    land_shapes = [a.shape if scatter else (N_DEV,) + a.shape for a in arrs]

    def body(*refs):
        ins, lands = refs[:n], refs[n:2 * n]
        send_sems, recv_sems, local_sems = refs[2 * n:2 * n + 3]
        token = refs[-1]
        for cp in _exchange_copies(ins, lands, send_sems, recv_sems, local_sems, scatter):
            cp.start()
        token[...] = jnp.zeros_like(token)

    out_shape = ([pltpu.SemaphoreType.DMA((n * (N_DEV - 1),)), pltpu.SemaphoreType.DMA((n * (N_DEV - 1),)),
                  pltpu.SemaphoreType.DMA((n,))]
                 + [pltpu.HBM(a.shape, a.dtype) for a in arrs]
                 + [pltpu.HBM(s, a.dtype) for s, a in zip(land_shapes, arrs)]
                 + [S((8, 128), F32)])
    args = ([pltpu.with_memory_space_constraint(a, pltpu.HBM) for a in arrs]
            + [pltpu.with_memory_space_constraint(lax.empty(s, a.dtype), pltpu.HBM) for s, a in zip(land_shapes, arrs)])
    outs = pl.pallas_call(
        body, in_specs=[_HBM_SPEC] * (2 * n),
        out_specs=[_SEM_SPEC] * 3 + [_HBM_SPEC] * (2 * n) + [pl.BlockSpec(memory_space=pltpu.VMEM)],
        out_shape=out_shape, input_output_aliases={j: 3 + j for j in range(2 * n)},
        name=name, compiler_params=pltpu.CompilerParams(has_side_effects=_DATAFLOW))(*args)
    return (n, scatter, outs[:3], outs[3:3 + n], outs[3 + n:3 + 2 * n]), outs[-1]


def _exchange_wait(state, after, name):
    n, scatter, sems, ins, lands = state

    def body(*refs):
        ins_r, lands_r = refs[:n], refs[n:2 * n]
        send_sems, recv_sems, local_sems = refs[2 * n:2 * n + 3]
        for cp in _exchange_copies(ins_r, lands_r, send_sems, recv_sems, local_sems, scatter):
            cp.wait()

    outs = pl.pallas_call(
        body, in_specs=[_HBM_SPEC] * (2 * n) + [_SEM_SPEC] * 3 + [pl.BlockSpec(memory_space=pl.ANY)],
        out_specs=[_HBM_SPEC] * (2 * n),
        out_shape=[pltpu.HBM(a.shape, a.dtype) for a in ins] + [pltpu.HBM(a.shape, a.dtype) for a in lands],
        input_output_aliases={j: j for j in range(2 * n)},
        name=name, compiler_params=pltpu.CompilerParams(has_side_effects=_DATAFLOW))(*ins, *lands, *sems, after)
    return list(outs[n:])


def _adamw(w, m, v, parts, name):
    R, C = w.shape
    P = parts.shape[0]
    tr = _pick(R, tuple(t for t in (256, 176, 128, 64, 32, 16, 8) if P * t * C * 4 <= ADAMW_BLOCK_BYTES))
    c1 = 1.0 / (1.0 - ADAM_B1 ** ADAM_STEP)
    c2 = 1.0 / (1.0 - ADAM_B2 ** ADAM_STEP)

    def body(w_ref, m_ref, v_ref, p_ref, g_ref, d_ref, nm_ref, nv_ref):
        g = p_ref[0].astype(F32)
        for j in range(1, P):
            g = g + p_ref[j].astype(F32)
        m2 = ADAM_B1 * m_ref[...] + (1.0 - ADAM_B1) * g
        v2 = ADAM_B2 * v_ref[...] + (1.0 - ADAM_B2) * (g * g)
        g_ref[...] = g
        nm_ref[...] = m2
        nv_ref[...] = v2
        d_ref[...] = -ADAM_LR * ((m2 * c1) / (jnp.sqrt(v2 * c2) + ADAM_EPS) + ADAM_WD * w_ref[...])

    row = _bs((tr, C), lambda i: (i, 0))
    return pl.pallas_call(
        body, grid=(R // tr,), in_specs=[row, row, row, _bs((P, tr, C), lambda i: (0, i, 0))], out_specs=[row] * 4,
        out_shape=[S((R, C), F32)] * 4, name=name, compiler_params=_cparams("parallel"))(w, m, v, parts)


def _sum_parts(parts, name):
    P, R, C = parts.shape

    def body(p_ref, o_ref):
        g = p_ref[0]
        for j in range(1, P):
            g = g + p_ref[j]
        o_ref[...] = g

    return pl.pallas_call(body, out_shape=S((R, C), F32), name=name, compiler_params=_cparams())(parts)


def _pack(arrs):
    flat = jnp.concatenate([a.reshape(-1) for a in arrs])
    rows = -(-flat.shape[0] // 1024) * 8
    return jnp.pad(flat, (0, rows * 128 - flat.shape[0])).reshape(rows, 128)


def _unpack(packed, shapes):
    flat = packed.reshape(-1)
    out, off = [], 0
    for s in shapes:
        n = int(np.prod(s))
        out.append(flat[off:off + n].reshape(s))
        off += n
    return out


def _behind(a, token):
    return a if token is None else a + token[0, 0]


def _local_step(x, mem, target, p, comm=None):
    table = p["rel_bias_table"]
    dils = [dil for _, dil in ATTN_GROUPS]
    xn, xn_classes = _rms_fwd_classes(x, p["attn_norm_w"], dils[1:], "attn_norm_fwd")
    xn_c = [xn] + xn_classes
    w_in = p["w_in"]
    qkv_w = 3 * N_GROUPS * GW
    wq = [jnp.concatenate([w_in[:, (N_GROUPS * part + g) * GW:(N_GROUPS * part + g + 1) * GW] for part in range(3)], axis=1)
          for g in range(N_GROUPS)]
    wc = w_in[:, qkv_w:qkv_w + 3 * GW]
    wg = w_in[:, qkv_w + 3 * GW:]
    zq = [_matmul(xn_c[g], wq[g], name=f"mm_in_qkv{g}") for g in range(N_GROUPS)]
    zc = _matmul(xn, wc, name="mm_in_c")
    zg = _matmul(xn, wg, name="mm_in_g")
    os_, lses = [], []
    for g, dil in enumerate(dils):
        o, l = _attn_fwd(zq[g], table, p["q_norm_w"][g:g + 1], p["k_norm_w"][g:g + 1], g, dil)
        os_.append(o)
        lses.append(l)
    attn, lse, attn_c, lse_c = _attn_merge(os_, lses, dils)
    u = _conv_fwd(zc, p["conv_dw_w"], p["conv_dw_b"], p["conv_ln_w"], p["conv_ln_b"])
    if comm is not None:
        p = {**p, **comm.late_weights(after=u)}
    mk, mv = _mem_fwd(mem, p["mem_norm_w"], p["w_mem_kv"], p["xk_norm_w"])
    cross = _cross_fwd(zc, mk, mv, p["xq_norm_w"])
    merged = _branch_fwd(attn, u, cross, zg, p["b_gate"], p["w_attn_o"], p["w_conv_o"], p["w_cross_o"])
    h1 = _matmul(merged, p["w_out"], residual=x, name="mm_out")
    hn = _rms_fwd(h1, p["ffn_norm_w"], "ffn_norm_fwd")
    up0 = _matmul(hn, p["w_up"], name="mm_up")
    f, up = _ffn_act_fwd(up0, p["ffn_conv_w"], p["ffn_conv_b"])
    h2 = _matmul(f, p["w_down"], residual=h1, name="mm_down")
    dh2, lsum = _loss_head(h2, target)
    g = {}
    d_f = _matmul(dh2, p["w_down"], tb=True, name="mm_down_dx")
    g["w_down"] = _matmul(f, dh2, ta=True, name="mm_down_dw")
    d_up0, g["ffn_conv_w"], g["ffn_conv_b"] = _ffn_act_bwd(up0, up, p["ffn_conv_w"], d_f)
    dhn = _matmul(d_up0, p["w_up"], tb=True, name="mm_up_dx")
    g["w_up"] = _matmul(hn, d_up0, ta=True, name="mm_up_dw")
    dh1, g["ffn_norm_w"] = _rms_bwd(h1, p["ffn_norm_w"], [dhn], dh2, "ffn_norm_bwd")
    d_merged = _matmul(dh1, p["w_out"], tb=True, name="mm_out_dx")
    g["w_out"] = _matmul(merged, dh1, ta=True, name="mm_out_dw")
    (d_zg, d_attn_c, d_u, d_cross, g["w_attn_o"], g["w_conv_o"], g["w_cross_o"], g["b_gate"]) = _branch_bwd(
        d_merged, attn, u, cross, zg, p["b_gate"], p["w_attn_o"], p["w_conv_o"], p["w_cross_o"], dils)
    d_xq, dmk, dmv, g["xq_norm_w"] = _cross_bwd(zc, mk, mv, p["xq_norm_w"], d_cross)
    g["w_mem_kv"], g["mem_norm_w"], g["xk_norm_w"] = _mem_bwd(mem, p["mem_norm_w"], p["w_mem_kv"], p["xk_norm_w"], dmk, dmv)
    tok = comm.start_early_grads(g) if comm is not None else None
    d_val, d_gate, g["conv_dw_w"], g["conv_dw_b"], g["conv_ln_w"], g["conv_ln_b"] = _conv_bwd(
        zc, p["conv_dw_w"], _behind(p["conv_dw_b"], tok), p["conv_ln_w"], p["conv_ln_b"], d_u)
    dzq, dqw, dkw, dtab = [], [], [], []
    for gi, dil in enumerate(dils):
        r = _attn_bwd(zq[gi], table, p["q_norm_w"][gi:gi + 1], p["k_norm_w"][gi:gi + 1], d_attn_c[gi], attn_c[gi],
                      lse_c[gi], gi, dil)
        for lst, val in zip((dzq, dqw, dkw, dtab), r):
            lst.append(val)
    g["q_norm_w"] = jnp.concatenate(dqw, axis=0)
    g["k_norm_w"] = jnp.concatenate(dkw, axis=0)
    g["rel_bias_table"] = jnp.concatenate(dtab, axis=1)
    d_zc = jnp.concatenate([d_val, d_gate, d_xq], axis=1)
    gq = [_matmul(xn_c[gi], dzq[gi], ta=True, name=f"mm_in_qkv{gi}_dw") for gi in range(N_GROUPS)]
    gc = _matmul(xn, d_zc, ta=True, name="mm_in_c_dw")
    gg = _matmul(xn, d_zg, ta=True, name="mm_in_g_dw")
    g["w_in"] = jnp.concatenate(
        [gq[gi][:, part * GW:(part + 1) * GW] for part in range(3) for gi in range(N_GROUPS)] + [gc, gg], axis=1)
    tok = comm.start_w_in_grad(g["w_in"]) if comm is not None else None
    dxn = _matmul(d_zg, wg, tb=True, after=tok, name="mm_in_g_dx")
    dxn = _matmul(d_zc, wc, tb=True, residual=dxn, name="mm_in_c_dx")
    dxn = _matmul(dzq[0], wq[0], tb=True, residual=dxn, name="mm_in_qkv0_dx")
    dx_classes = [(_matmul(dzq[gi], wq[gi], tb=True, name=f"mm_in_qkv{gi}_dx"), dils[gi]) for gi in range(1, N_GROUPS)]
    grad_x, g["attn_norm_w"] = _rms_bwd(x, p["attn_norm_w"], [dxn], dh1, "attn_norm_bwd", class_dys=dx_classes)
    return lsum[0, 0], grad_x, g


WEIGHT_NAMES = ["rel_bias_table", "attn_norm_w", "w_in", "b_gate", "q_norm_w", "k_norm_w", "w_attn_o", "conv_dw_w",
                "conv_dw_b", "conv_ln_w", "conv_ln_b", "w_conv_o", "mem_norm_w", "w_mem_kv", "xq_norm_w", "xk_norm_w",
                "w_cross_o", "w_out", "ffn_norm_w", "w_up", "ffn_conv_w", "ffn_conv_b", "w_down"]
COL_SHARDED = ("w_in", "w_attn_o", "w_conv_o", "w_cross_o", "w_up")
ROW_SHARDED = ("w_mem_kv", "w_out", "w_down")
SMALL_COL_SHARDED = ("conv_dw_w", "ffn_conv_w")
BIG = COL_SHARDED + ROW_SHARDED


def _cols_to_blocks(a):
    k, n8 = a.shape
    return a.reshape(k, N_DEV, n8 // N_DEV).transpose(1, 0, 2)


def _blocks_to_cols(a):
    return a.transpose(1, 0, 2).reshape(a.shape[1], N_DEV * a.shape[2])


def _step(x, mem, target, w, m, v):
    me = 4 * lax.axis_index("x") + 2 * lax.axis_index("y") + lax.axis_index("c")

    def to_full(n, blocks):
        return _blocks_to_cols(blocks) if n in COL_SHARDED + SMALL_COL_SHARDED else blocks.reshape(-1, blocks.shape[-1])

    def to_blocks(n, grad):
        blocks = _cols_to_blocks(grad) if n in COL_SHARDED else grad.reshape(N_DEV, -1, grad.shape[-1])
        return blocks.astype(MXU)

    first = ("w_in",) + SMALL_COL_SHARDED
    late = tuple(n for n in BIG if n != "w_in")
    cast = lambda n: w[n].astype(MXU) if n in BIG else w[n]
    first_state, _ = _exchange_start([cast(n) for n in first], False, "gather_first_start")
    late_state, late_token = _exchange_start([cast(n) for n in late], False, "gather_late_start")
    got = _exchange_wait(first_state, late_token, "gather_first_wait")
    p = {n: w[n] for n in WEIGHT_NAMES if n not in BIG + SMALL_COL_SHARDED}
    p.update({n: to_full(n, b) for n, b in zip(first, got)})

    class Comm:
        def late_weights(self, after):
            return {n: to_full(n, b) for n, b in zip(late, _exchange_wait(late_state, after, "gather_late_wait"))}

        def start_early_grads(self, g):
            self.early_state, token = _exchange_start([to_blocks(n, g[n]) for n in late], True, "scatter_early_start")
            return token

        def start_w_in_grad(self, grad):
            self.w_in_state, token = _exchange_start([to_blocks("w_in", grad)], True, "scatter_w_in_start")
            return token

    comm = Comm()
    lsum, grad_x, g = _local_step(x, mem, target, p, comm)
    small_names = [n for n in WEIGHT_NAMES if n not in BIG]
    small_shapes = [g[n].shape for n in small_names]
    small_parts = _exchange([_pack([g[n] for n in small_names])], False, "gather_small_grads")[0]
    gsmall = dict(zip(small_names, _unpack(_sum_parts(small_parts, "sum_small_grads"), small_shapes)))
    for n in SMALL_COL_SHARDED:
        width = w[n].shape[-1]
        gsmall[n] = lax.dynamic_slice_in_dim(gsmall[n], me * width, width, axis=1)
    res = {}
    parts = dict(zip(late, _exchange_wait(comm.early_state, grad_x, "scatter_early_wait")))
    for n in late:
        res[n] = _adamw(w[n], m[n], v[n], parts[n], "adamw_" + n)
    w_in_parts = _exchange_wait(comm.w_in_state, res[late[-1]][1], "scatter_w_in_wait")[0]
    res["w_in"] = _adamw(w["w_in"], m["w_in"], v["w_in"], w_in_parts, "adamw_w_in")
    shapes = [w[n].shape for n in small_names]
    packed = [_pack([d[n] for n in small_names]) for d in (w, m, v, gsmall)]
    outs = _adamw(packed[0], packed[1], packed[2], packed[3][None], "adamw_small")
    unpacked = [_unpack(o, shapes) for o in outs]
    for j, n in enumerate(small_names):
        res[n] = tuple(unpacked[q][j] for q in range(4))
    return lsum, grad_x, res


def kernel(x, mem, rel_bias_table, attn_norm_w, w_in, b_gate, q_norm_w, k_norm_w, w_attn_o, conv_dw_w, conv_dw_b, conv_ln_w, conv_ln_b, w_conv_o, mem_norm_w, w_mem_kv, xq_norm_w, xk_norm_w, w_cross_o, w_out, ffn_norm_w, w_up, ffn_conv_w, ffn_conv_b, w_down, loss_target, m_rel_bias_table, m_attn_norm_w, m_w_in, m_b_gate, m_q_norm_w, m_k_norm_w, m_w_attn_o, m_conv_dw_w, m_conv_dw_b, m_conv_ln_w, m_conv_ln_b, m_w_conv_o, m_mem_norm_w, m_w_mem_kv, m_xq_norm_w, m_xk_norm_w, m_w_cross_o, m_w_out, m_ffn_norm_w, m_w_up, m_ffn_conv_w, m_ffn_conv_b, m_w_down, v_rel_bias_table, v_attn_norm_w, v_w_in, v_b_gate, v_q_norm_w, v_k_norm_w, v_w_attn_o, v_conv_dw_w, v_conv_dw_b, v_conv_ln_w, v_conv_ln_b, v_w_conv_o, v_mem_norm_w, v_w_mem_kv, v_xq_norm_w, v_xk_norm_w, v_w_cross_o, v_w_out, v_ffn_norm_w, v_w_up, v_ffn_conv_w, v_ffn_conv_b, v_w_down):
    ws = dict(zip(WEIGHT_NAMES, (rel_bias_table, attn_norm_w, w_in, b_gate, q_norm_w, k_norm_w, w_attn_o, conv_dw_w, conv_dw_b, conv_ln_w, conv_ln_b, w_conv_o, mem_norm_w, w_mem_kv, xq_norm_w, xk_norm_w, w_cross_o, w_out, ffn_norm_w, w_up, ffn_conv_w, ffn_conv_b, w_down)))
    ms = dict(zip(WEIGHT_NAMES, (m_rel_bias_table, m_attn_norm_w, m_w_in, m_b_gate, m_q_norm_w, m_k_norm_w, m_w_attn_o, m_conv_dw_w, m_conv_dw_b, m_conv_ln_w, m_conv_ln_b, m_w_conv_o, m_mem_norm_w, m_w_mem_kv, m_xq_norm_w, m_xk_norm_w, m_w_cross_o, m_w_out, m_ffn_norm_w, m_w_up, m_ffn_conv_w, m_ffn_conv_b, m_w_down)))
    vs = dict(zip(WEIGHT_NAMES, (v_rel_bias_table, v_attn_norm_w, v_w_in, v_b_gate, v_q_norm_w, v_k_norm_w, v_w_attn_o, v_conv_dw_w, v_conv_dw_b, v_conv_ln_w, v_conv_ln_b, v_w_conv_o, v_mem_norm_w, v_w_mem_kv, v_xq_norm_w, v_xk_norm_w, v_w_cross_o, v_w_out, v_ffn_norm_w, v_w_up, v_ffn_conv_w, v_ffn_conv_b, v_w_down)))
    full_shapes = {n: ws[n].shape for n in WEIGHT_NAMES}

    def squeeze(d):
        return {n: (a if n == "rel_bias_table" else a[0]) for n, a in d.items()}

    w, m, v = squeeze(ws), squeeze(ms), squeeze(vs)
    for d in (w, m, v):
        for n in WEIGHT_NAMES:
            if d[n].ndim == 1:
                d[n] = d[n][None]
    lsum, grad_x, res = _step(x[0], mem[0], loss_target[0], w, m, v)
    loss = lax.psum(0.5 / D_MODEL * lsum, ("x", "y", "c"))
    outs = [loss, grad_x[None]]
    for q in range(4):
        outs += [res[n][q].reshape(full_shapes[n]) for n in WEIGHT_NAMES]
    return tuple(outs)
```

```python
import functools
import math

import numpy as np
import jax
import jax.numpy as jnp
from jax import lax
from jax.experimental import pallas as pl
from jax.experimental.pallas import tpu as pltpu

F32 = jnp.float32
MXU = jnp.bfloat16
S = jax.ShapeDtypeStruct

D_MODEL = 1024
HEAD_DIM = 128
ATTN_GROUPS = ((128, 1), (512, 4), (2048, 16))
N_GROUPS = 3
HEADS = 4
GW = HEADS * HEAD_DIM
CONV_WIDTH = 31
N_MEM = 256
D_FF = 2816
FFN_CONV_WIDTH = 3
N_BUCKETS = 32
MAX_DISTANCE = 2048
RMS_EPS = 1e-6
LN_EPS = 1e-5
BLK = 128
SCALE = HEAD_DIM ** -0.5
NEG = -1e30
N_DEV = 8

ADAM_LR, ADAM_B1, ADAM_B2, ADAM_EPS, ADAM_WD, ADAM_STEP = 0.001, 0.9, 0.999, 1e-08, 0.01, 10

VMEM_LIMIT = 48 * 1024 * 1024
CONV_HALO = 32
FFN_HALO = 8
ADAMW_BLOCK_BYTES = 4 * 1024 * 1024


def _cparams(*sem):
    return pltpu.CompilerParams(dimension_semantics=sem or None, vmem_limit_bytes=VMEM_LIMIT)


def _bs(shape, imap):
    return pl.BlockSpec(shape, imap)


def _dot(a, b):
    return lax.dot_general(a.astype(MXU), b.astype(MXU), (((1,), (0,)), ((), ())), preferred_element_type=F32)


def _dot_nt(a, b):
    return lax.dot_general(a.astype(MXU), b.astype(MXU), (((1,), (1,)), ((), ())), preferred_element_type=F32)


def _dot_tn(a, b):
    return lax.dot_general(a.astype(MXU), b.astype(MXU), (((0,), (0,)), ((), ())), preferred_element_type=F32)


def _sigmoid(x):
    return 0.5 * jnp.tanh(0.5 * x) + 0.5


def _rmsn(x, w):
    r = lax.rsqrt(jnp.mean(x * x, axis=-1, keepdims=True) + RMS_EPS)
    return x * r * w, r


def _rmsn_bwd(x, r, w, dy):
    g = dy * w
    dx = r * g - x * (r * r * r) * jnp.mean(x * g, axis=-1, keepdims=True)
    dw = jnp.sum(dy * x * r, axis=0, keepdims=True)
    return dx, dw


def _acc_out(ref, val, first):
    @pl.when(first)
    def _():
        ref[...] = val

    @pl.when(jnp.logical_not(first))
    def _():
        ref[...] += val


def _rms_fwd(x, w, name):
    T, D = x.shape
    tm = min(512, T)

    def body(x_ref, w_ref, o_ref):
        y, _ = _rmsn(x_ref[...], w_ref[...])
        o_ref[...] = y.astype(o_ref.dtype)

    return pl.pallas_call(
        body, grid=(T // tm,),
        in_specs=[_bs((tm, D), lambda i: (i, 0)), _bs((1, D), lambda i: (0, 0))],
        out_specs=_bs((tm, D), lambda i: (i, 0)),
        out_shape=S((T, D), MXU), name=name, compiler_params=_cparams("parallel"))(x, w)


def _rms_fwd_classes(x, w, dils, name):
    T, D = x.shape
    tm = min(512, T)
    slabs = D // HEAD_DIM

    def body(x_ref, w_ref, o_ref, *rest):
        outs, slab_ref = rest[:-1], rest[-1]
        y, _ = _rmsn(x_ref[...], w_ref[...])
        o_ref[...] = y.astype(o_ref.dtype)
        for j in range(slabs):
            slab_ref[j] = y[:, j * HEAD_DIM:(j + 1) * HEAD_DIM]
        for out_ref, d in zip(outs, dils):
            _tokens_to_classes(slab_ref, out_ref, d, tm)

    outs = pl.pallas_call(
        body, grid=(T // tm,),
        in_specs=[_bs((tm, D), lambda i: (i, 0)), _bs((1, D), lambda i: (0, 0))],
        out_specs=[_bs((tm, D), lambda i: (i, 0))] + [_bs((d, tm // d, D), lambda i: (0, i, 0)) for d in dils],
        out_shape=[S((T, D), MXU)] + [S((d, T // d, D), MXU) for d in dils],
        scratch_shapes=[pltpu.VMEM((slabs, tm, HEAD_DIM), F32)], name=name, compiler_params=_cparams("parallel"))(x, w)
    return outs[0], [a.reshape(T, D) for a in outs[1:]]


def _rms_bwd(x, w, dys, resid, name, class_dys=()):
    T, D = x.shape
    tm = min(512, T)
    n = len(dys)
    nc = len(class_dys)
    slabs = D // HEAD_DIM

    def body(*refs):
        x_ref, w_ref, res_ref = refs[0], refs[1], refs[2 + n + nc]
        dx_ref, dw_ref = refs[3 + n + nc], refs[4 + n + nc]
        xv = x_ref[...]
        dy = refs[2][...]
        for dy_ref in refs[3:2 + n]:
            dy = dy + dy_ref[...]
        for dy_ref, (_, d) in zip(refs[2 + n:2 + n + nc], class_dys):
            slab_ref = refs[-1]
            _classes_to_tokens(dy_ref, slab_ref, d, tm)
            dy = dy + jnp.concatenate([slab_ref[j] for j in range(slabs)], axis=1)
        _, r = _rmsn(xv, w_ref[...])
        dx, dw = _rmsn_bwd(xv, r, w_ref[...], dy)
        dx_ref[...] = res_ref[...] + dx
        _acc_out(dw_ref, dw, pl.program_id(0) == 0)

    row = _bs((tm, D), lambda i: (i, 0))
    vec = _bs((1, D), lambda i: (0, 0))
    return pl.pallas_call(
        body, grid=(T // tm,),
        in_specs=[row, vec] + [row] * n + [_bs((d, tm // d, D), lambda i: (0, i, 0)) for _, d in class_dys] + [row],
        out_specs=[row, vec], out_shape=[S((T, D), F32), S((1, D), F32)],
        scratch_shapes=[pltpu.VMEM((slabs, tm, HEAD_DIM), F32)] if nc else [],
        name=name, compiler_params=_cparams("arbitrary"))(
            x, w, *dys, *[a.reshape(d, T // d, D) for a, d in class_dys], resid)


def _pick(n, cands):
    for c in cands:
        if n % c == 0:
            return c
    return n


MM_VMEM_BUDGET = 36 * 1024 * 1024


def _mm_tiles(tm, N, K, a_bytes, b_bytes, o_bytes, has_res):
    best = None
    for tn in (1536, 1024, 1408, 512, 256, 128):
        for tk in (3072, 1536, 1024, 1408, 512, 256, 128):
            if N % tn or K % tk:
                continue
            nk = K // tk
            need = 2 * (tm * tk * a_bytes + tk * tn * b_bytes + tm * tn * (o_bytes + 4 * has_res)) + (nk > 1) * tm * tn * 4
            if need > MM_VMEM_BUDGET:
                continue
            key = ((N // tn) * nk, nk)
            if best is None or key < best[0]:
                best = (key, tn, tk)
    if best is None:
        return _pick(N, (128,)), _pick(K, (128,))
    return best[1], best[2]


def _matmul(a, b, *, ta=False, tb=False, out_dtype=F32, residual=None, after=None, tm=None, tn=None, tk=None, name):
    M, K = (a.shape[1], a.shape[0]) if ta else a.shape
    N = b.shape[0] if tb else b.shape[1]
    tm = tm or _pick(M, (1024, 1408, 512, 256, 128))
    if tn is None or tk is None:
        tn, tk = _mm_tiles(tm, N, K, a.dtype.itemsize, b.dtype.itemsize, jnp.dtype(out_dtype).itemsize, residual is not None)
    nk = K // tk
    dn = (((0 if ta else 1,), (1 if tb else 0,)), ((), ()))
    has_res = residual is not None
    n_in = 2 + has_res + (after is not None)

    def body(*refs):
        a_ref, b_ref = refs[0], refs[1]
        res_ref = refs[2] if has_res else None
        o_ref = refs[n_in]
        p = lax.dot_general(a_ref[...].astype(MXU), b_ref[...].astype(MXU), dn, preferred_element_type=F32)

        def finish(acc):
            if has_res:
                acc = acc + res_ref[...]
            o_ref[...] = acc.astype(o_ref.dtype)

        if nk == 1:
            finish(p)
        else:
            acc_ref = refs[-1]
            k = pl.program_id(2)

            @pl.when(k == 0)
            def _():
                acc_ref[...] = p

            @pl.when(k > 0)
            def _():
                acc_ref[...] += p

            @pl.when(k == nk - 1)
            def _():
                finish(acc_ref[...])

    a_spec = _bs((tk, tm), lambda i, j, k: (k, i)) if ta else _bs((tm, tk), lambda i, j, k: (i, k))
    b_spec = _bs((tn, tk), lambda i, j, k: (j, k)) if tb else _bs((tk, tn), lambda i, j, k: (k, j))
    o_spec = _bs((tm, tn), lambda i, j, k: (i, j))
    in_specs = [a_spec, b_spec] + ([o_spec] if has_res else [])
    args = (a, b) + ((residual,) if has_res else ())
    if after is not None:
        in_specs.append(_bs((8, 128), lambda i, j, k: (0, 0)))
        args += (after,)
    return pl.pallas_call(
        body, grid=(M // tm, N // tn, nk), in_specs=in_specs, out_specs=o_spec,
        out_shape=S((M, N), out_dtype), scratch_shapes=[pltpu.VMEM((tm, tn), F32)] if nk > 1 else [],
        name=name, compiler_params=_cparams("parallel", "parallel", "arbitrary"))(*args)


def _bucket_matrix(dilation):
    n = BLK
    qi = np.arange(n)[:, None]
    kj = np.arange(2 * n)[None, :]
    step = qi + n - kj
    dist = np.clip(step, 0, None) * dilation
    max_exact = N_BUCKETS // 2
    d = np.maximum(dist.astype(np.float32), np.float32(1.0))
    large = max_exact + (np.log(d / np.float32(max_exact)) / np.float32(math.log(MAX_DISTANCE / max_exact))
                         * np.float32(N_BUCKETS - max_exact)).astype(np.int32)
    large = np.minimum(large, N_BUCKETS - 1)
    bucket = np.where(dist < max_exact, dist, large)
    band = (step >= 0) & (step <= n)
    return np.where(band, bucket, -1).astype(np.int32)


def _build_bias(tbl_ref, bkt_ref, bias_ref, g):
    bk = bkt_ref[...]
    for h in range(HEADS):
        acc = jnp.full(bk.shape, NEG, F32)
        for b in range(N_BUCKETS):
            acc = jnp.where(bk == b, tbl_ref[b, HEADS * g + h], acc)
        bias_ref[h] = acc


def _attn_fwd(zq, table, qw, kw, g, dil):
    T = zq.shape[0]
    nb = T // dil // BLK
    qb = _pick(nb, (4, 2, 1))
    nt = nb // qb
    bkt = jnp.asarray(_bucket_matrix(dil))

    def zspec(part, prev):
        if prev:
            return _bs((BLK, GW), lambda c, i: (c * nb + jnp.maximum(i * qb - 1, 0), part))
        return _bs((qb * BLK, GW), lambda c, i: (c * nt + i, part))

    def body(tbl_ref, bkt_ref, qw_ref, kw_ref, q_ref, kp_ref, kc_ref, vp_ref, vc_ref, o_ref, lse_ref, bias_ref):
        c, i = pl.program_id(0), pl.program_id(1)

        @pl.when((c == 0) & (i == 0))
        def _():
            _build_bias(tbl_ref, bkt_ref, bias_ref, g)

        kj = lax.broadcasted_iota(jnp.int32, (BLK, 2 * BLK), 1)
        no_prev = jnp.logical_and(i == 0, kj < BLK)
        for h in range(HEADS):
            sl = slice(h * HEAD_DIM, (h + 1) * HEAD_DIM)
            qn, _ = _rmsn(q_ref[:, sl], qw_ref[...])
            kn, _ = _rmsn(jnp.concatenate([kp_ref[:, sl], kc_ref[:, sl]], axis=0), kw_ref[...])
            v = jnp.concatenate([vp_ref[:, sl], vc_ref[:, sl]], axis=0)
            for j in range(qb):
                rows = slice(j * BLK, (j + 1) * BLK)
                keys = slice(j * BLK, (j + 2) * BLK)
                s = _dot_nt(qn[rows], kn[keys]) * SCALE + bias_ref[h]
                if j == 0:
                    s = jnp.where(no_prev, NEG, s)
                m = jnp.max(s, axis=-1, keepdims=True)
                p = jnp.exp(s - m)
                l = jnp.sum(p, axis=-1, keepdims=True)
                o_ref[rows, sl] = _dot(p, v[keys]) / l
                lse_ref[rows, sl] = jnp.broadcast_to(m + jnp.log(l), (BLK, HEAD_DIM))

    ospec = _bs((qb * BLK, GW), lambda c, i: (c * nt + i, 0))
    vec = _bs((1, HEAD_DIM), lambda c, i: (0, 0))
    return pl.pallas_call(
        body, grid=(dil, nt),
        in_specs=[pl.BlockSpec(memory_space=pltpu.SMEM), _bs((BLK, 2 * BLK), lambda c, i: (0, 0)), vec, vec,
                  zspec(0, False), zspec(1, True), zspec(1, False), zspec(2, True), zspec(2, False)],
        out_specs=[ospec, ospec],
        out_shape=[S((T, GW), F32), S((T, GW), F32)],
        scratch_shapes=[pltpu.VMEM((HEADS, BLK, 2 * BLK), F32)],
        name=f"attn_fwd_g{g}", compiler_params=_cparams("arbitrary", "arbitrary"))(table, bkt, qw, kw, zq, zq, zq, zq, zq)


def _classes_to_tokens(src_ref, dst_ref, dil, rows):
    for c in range(dil):
        for h in range(dst_ref.shape[0]):
            dst_ref[h, pl.ds(c, rows // dil, stride=dil), :] = src_ref[c, :, h * HEAD_DIM:(h + 1) * HEAD_DIM]


def _tokens_to_classes(src_ref, dst_ref, dil, rows):
    for c in range(dil):
        for h in range(src_ref.shape[0]):
            dst_ref[c, :, h * HEAD_DIM:(h + 1) * HEAD_DIM] = src_ref[h, pl.ds(c, rows // dil, stride=dil), :].astype(dst_ref.dtype)


def _class_view(a, dil):
    return a.reshape(dil, a.shape[0] // dil, a.shape[1])


def _attn_merge(os_, lses, dils):
    T = os_[0].shape[0]
    tm = min(512, T)
    assert dils[0] == 1 and len(dils) == 3

    def body(o0, l0, o1, l1, o2, l2, a_ref, lse_ref, a1_ref, lse1_ref, a2_ref, lse2_ref,
             no1, nl1, no2, nl2, ra, rl):
        for src, dst, d in ((o1, no1, dils[1]), (l1, nl1, dils[1]), (o2, no2, dils[2]), (l2, nl2, dils[2])):
            _classes_to_tokens(src, dst, d, tm)
        for h in range(HEADS):
            sl = slice(h * HEAD_DIM, (h + 1) * HEAD_DIM)
            ls = [l0[:, sl], nl1[h], nl2[h]]
            os3 = [o0[:, sl], no1[h], no2[h]]
            mx = jnp.maximum(jnp.maximum(ls[0], ls[1]), ls[2])
            tot = mx + jnp.log(jnp.exp(ls[0] - mx) + jnp.exp(ls[1] - mx) + jnp.exp(ls[2] - mx))
            att = jnp.exp(ls[0] - tot) * os3[0] + jnp.exp(ls[1] - tot) * os3[1] + jnp.exp(ls[2] - tot) * os3[2]
            a_ref[:, sl] = att
            lse_ref[:, sl] = tot
            ra[h] = att
            rl[h] = tot
        for src, dst, d in ((ra, a1_ref, dils[1]), (rl, lse1_ref, dils[1]), (ra, a2_ref, dils[2]), (rl, lse2_ref, dils[2])):
            _tokens_to_classes(src, dst, d, tm)

    row = _bs((tm, GW), lambda i: (i, 0))
    cls = lambda d: _bs((d, tm // d, GW), lambda i: (0, i, 0))
    cshape = lambda d: S((d, T // d, GW), F32)
    slab = pltpu.VMEM((HEADS, tm, HEAD_DIM), F32)
    d1, d2 = dils[1], dils[2]
    attn, lse, a1, l1, a2, l2 = pl.pallas_call(
        body, grid=(T // tm,), in_specs=[row, row, cls(d1), cls(d1), cls(d2), cls(d2)],
        out_specs=[row, row, cls(d1), cls(d1), cls(d2), cls(d2)],
        out_shape=[S((T, GW), F32), S((T, GW), F32), cshape(d1), cshape(d1), cshape(d2), cshape(d2)],
        scratch_shapes=[slab] * 6, name="attn_merge", compiler_params=_cparams("parallel"))(
            os_[0], lses[0], _class_view(os_[1], d1), _class_view(lses[1], d1), _class_view(os_[2], d2),
            _class_view(lses[2], d2))
    flat = lambda a: a.reshape(T, GW)
    return attn, lse, [attn, flat(a1), flat(a2)], [lse, flat(l1), flat(l2)]


def _attn_bwd(zq, table, qw, kw, d_attn, attn, lse, g, dil):
    T = zq.shape[0]
    nb = T // dil // BLK
    qb = _pick(nb, (4, 2, 1))
    nt = nb // qb
    bkt = jnp.asarray(_bucket_matrix(dil))

    def body(tbl_ref, bkt_ref, qw_ref, kw_ref, q_ref, k_ref, v_ref, kp_ref, vp_ref, qx_ref, da_ref, at_ref, lse_ref,
             dax_ref, atx_ref, lsex_ref, dz_ref, dqw_ref, dkw_ref, dtab_ref, bias_ref, dbias_ref):
        c, i = pl.program_id(0), pl.program_id(1)

        @pl.when((c == 0) & (i == 0))
        def _():
            _build_bias(tbl_ref, bkt_ref, bias_ref, g)
            dbias_ref[...] = jnp.zeros_like(dbias_ref)
            dqw_ref[...] = jnp.zeros_like(dqw_ref)
            dkw_ref[...] = jnp.zeros_like(dkw_ref)

        kj = lax.broadcasted_iota(jnp.int32, (BLK, 2 * BLK), 1)
        no_prev = jnp.logical_and(i == 0, kj < BLK)
        has_next = i < nt - 1
        last = slice((qb - 1) * BLK, qb * BLK)
        dqw_acc = jnp.zeros((1, HEAD_DIM), F32)
        dkw_acc = jnp.zeros((1, HEAD_DIM), F32)

        def add(parts, t, val):
            parts[t] = val if parts[t] is None else parts[t] + val

        for h in range(HEADS):
            lo = h * HEAD_DIM
            sl = slice(lo, lo + HEAD_DIM)
            q, k = q_ref[:, sl], k_ref[:, sl]
            qn, rq = _rmsn(q, qw_ref[...])
            kn, rk = _rmsn(k, kw_ref[...])
            kpn, _ = _rmsn(kp_ref[:, sl], kw_ref[...])
            kn_ext = jnp.concatenate([kpn, kn], axis=0)
            v_ext = jnp.concatenate([vp_ref[:, sl], v_ref[:, sl]], axis=0)
            dqn, dkn, dv = [None] * qb, [None] * qb, [None] * qb
            for j in range(qb):
                rows = slice(j * BLK, (j + 1) * BLK)
                keys = slice(j * BLK, (j + 2) * BLK)
                s = _dot_nt(qn[rows], kn_ext[keys]) * SCALE + bias_ref[h]
                if j == 0:
                    s = jnp.where(no_prev, NEG, s)
                p = jnp.exp(s - lse_ref[rows, lo:lo + 1])
                do = da_ref[rows, sl]
                delta = jnp.sum(do * at_ref[rows, sl], axis=-1, keepdims=True)
                ds = p * (_dot_nt(do, v_ext[keys]) - delta)
                dbias_ref[h] += ds
                dqn[j] = _dot(ds, kn_ext[keys]) * SCALE
                dv2 = _dot_tn(p, do)
                dk2 = _dot_tn(ds, qn[rows]) * SCALE
                if j >= 1:
                    add(dv, j - 1, dv2[:BLK])
                    add(dkn, j - 1, dk2[:BLK])
                add(dv, j, dv2[BLK:])
                add(dkn, j, dk2[BLK:])
            qxn, _ = _rmsn(qx_ref[:, sl], qw_ref[...])
            sx = _dot_nt(qxn, kn[last]) * SCALE + bias_ref[h, :, 0:BLK]
            px = jnp.where(has_next, jnp.exp(sx - lsex_ref[:, lo:lo + 1]), 0.0)
            dox = dax_ref[:, sl]
            dsx = px * (_dot_nt(dox, v_ref[last, sl]) - jnp.sum(dox * atx_ref[:, sl], axis=-1, keepdims=True))
            add(dv, qb - 1, _dot_tn(px, dox))
            add(dkn, qb - 1, _dot_tn(dsx, qxn) * SCALE)
            dq, dqw = _rmsn_bwd(q, rq, qw_ref[...], jnp.concatenate(dqn, axis=0))
            dk, dkw = _rmsn_bwd(k, rk, kw_ref[...], jnp.concatenate(dkn, axis=0))
            dqw_acc += dqw
            dkw_acc += dkw
            dz_ref[:, lo:lo + HEAD_DIM] = dq.astype(dz_ref.dtype)
            dz_ref[:, GW + lo:GW + lo + HEAD_DIM] = dk.astype(dz_ref.dtype)
            dz_ref[:, 2 * GW + lo:2 * GW + lo + HEAD_DIM] = jnp.concatenate(dv, axis=0).astype(dz_ref.dtype)
        dqw_ref[...] += dqw_acc
        dkw_ref[...] += dkw_acc

        @pl.when((c == dil - 1) & (i == nt - 1))
        def _():
            bk = bkt_ref[...]
            rows = lax.broadcasted_iota(jnp.int32, (N_BUCKETS, HEAD_DIM), 0)
            lanes = lax.broadcasted_iota(jnp.int32, (N_BUCKETS, HEAD_DIM), 1)
            out = jnp.zeros((N_BUCKETS, HEAD_DIM), F32)
            for h in range(HEADS):
                acc = dbias_ref[h]
                for b in range(N_BUCKETS):
                    val = jnp.sum(jnp.where(bk == b, acc, 0.0))
                    out = jnp.where((rows == b) & (lanes == h), val, out)
            dtab_ref[...] = out

    tile = lambda part: _bs((qb * BLK, GW), lambda c, i: (c * nt + i, part))
    before = lambda part: _bs((BLK, GW), lambda c, i: (c * nb + jnp.maximum(i * qb - 1, 0), part))
    after = lambda part: _bs((BLK, GW), lambda c, i: (c * nb + jnp.minimum((i + 1) * qb, nb - 1), part))
    vec = _bs((1, HEAD_DIM), lambda c, i: (0, 0))
    tabs = _bs((N_BUCKETS, HEAD_DIM), lambda c, i: (0, 0))
    dzq, dqw, dkw, dtab = pl.pallas_call(
        body, grid=(dil, nt),
        in_specs=[pl.BlockSpec(memory_space=pltpu.SMEM), _bs((BLK, 2 * BLK), lambda c, i: (0, 0)), vec, vec,
                  tile(0), tile(1), tile(2), before(1), before(2), after(0), tile(0), tile(0), tile(0),
                  after(0), after(0), after(0)],
        out_specs=[_bs((qb * BLK, 3 * GW), lambda c, i: (c * nt + i, 0)), vec, vec, tabs],
        out_shape=[S((T, 3 * GW), MXU)] + [S((1, HEAD_DIM), F32)] * 2 + [S((N_BUCKETS, HEAD_DIM), F32)],
        scratch_shapes=[pltpu.VMEM((HEADS, BLK, 2 * BLK), F32), pltpu.VMEM((HEADS, BLK, 2 * BLK), F32)],
        name=f"attn_bwd_g{g}", compiler_params=_cparams("arbitrary", "arbitrary"))(
            table, bkt, qw, kw, zq, zq, zq, zq, zq, zq, d_attn, attn, lse, d_attn, attn, lse)
    return dzq, dqw, dkw, dtab[:, :HEADS]


def _mem_fwd(mem, mem_norm_w, w_mem_kv, xk_w):
    def body(mem_ref, nw_ref, w_ref, xk_ref, mk_ref, mv_ref):
        mn, _ = _rmsn(mem_ref[...], nw_ref[...])
        kv = _dot(mn, w_ref[...])
        for h in range(HEADS):
            sl = slice(h * HEAD_DIM, (h + 1) * HEAD_DIM)
            kn, _ = _rmsn(kv[:, sl], xk_ref[...])
            mk_ref[:, sl] = kn.astype(mk_ref.dtype)
        mv_ref[...] = kv[:, GW:].astype(mv_ref.dtype)

    return pl.pallas_call(body, out_shape=[S((N_MEM, GW), MXU), S((N_MEM, GW), MXU)], name="mem_fwd",
                          compiler_params=_cparams())(mem, mem_norm_w, w_mem_kv, xk_w)


def _mem_bwd(mem, mem_norm_w, w_mem_kv, xk_w, dmk, dmv):
    def body(mem_ref, nw_ref, w_ref, xk_ref, dmk_ref, dmv_ref, dw_ref, dnw_ref, dxk_ref):
        memv = mem_ref[...]
        mn, r = _rmsn(memv, nw_ref[...])
        kv = _dot(mn, w_ref[...])
        dxk = jnp.zeros((1, HEAD_DIM), F32)
        parts = []
        for h in range(HEADS):
            sl = slice(h * HEAD_DIM, (h + 1) * HEAD_DIM)
            kh = kv[:, sl]
            _, rk = _rmsn(kh, xk_ref[...])
            dk, dw = _rmsn_bwd(kh, rk, xk_ref[...], dmk_ref[:, sl])
            dxk += dw
            parts.append(dk)
        dkv = jnp.concatenate(parts + [dmv_ref[...]], axis=1)
        dw_ref[...] = _dot_tn(mn, dkv)
        dmn = _dot_nt(dkv, w_ref[...])
        dnw_ref[...] = jnp.sum(dmn * memv * r, axis=0, keepdims=True)
        dxk_ref[...] = dxk

    return pl.pallas_call(
        body, out_shape=[S((D_MODEL, 2 * GW), F32), S((1, D_MODEL), F32), S((1, HEAD_DIM), F32)], name="mem_bwd",
        compiler_params=_cparams())(mem, mem_norm_w, w_mem_kv, xk_w, dmk, dmv)


def _cross_fwd(z, mk, mv, xq_w):
    T = z.shape[0]
    tm = min(512, T)

    def body(q_ref, mk_ref, mv_ref, w_ref, o_ref):
        for h in range(HEADS):
            sl = slice(h * HEAD_DIM, (h + 1) * HEAD_DIM)
            qn, _ = _rmsn(q_ref[:, sl], w_ref[...])
            s = _dot_nt(qn, mk_ref[:, sl]) * SCALE
            e = jnp.exp(s - jnp.max(s, axis=-1, keepdims=True))
            p = e / jnp.sum(e, axis=-1, keepdims=True)
            o_ref[:, sl] = _dot(p, mv_ref[:, sl]).astype(o_ref.dtype)

    full = _bs((N_MEM, GW), lambda i: (0, 0))
    return pl.pallas_call(
        body, grid=(T // tm,),
        in_specs=[_bs((tm, GW), lambda i: (i, 2)), full, full, _bs((1, HEAD_DIM), lambda i: (0, 0))],
        out_specs=_bs((tm, GW), lambda i: (i, 0)), out_shape=S((T, GW), MXU), name="cross_fwd",
        compiler_params=_cparams("parallel"))(z, mk, mv, xq_w)


def _cross_bwd(z, mk, mv, xq_w, d_cross):
    T = z.shape[0]
    tm = min(512, T)

    def body(q_ref, mk_ref, mv_ref, w_ref, do_ref, dq_ref, dmk_ref, dmv_ref, dw_ref):
        first = pl.program_id(0) == 0
        dw_acc = jnp.zeros((1, HEAD_DIM), F32)
        dmk_parts, dmv_parts = [], []
        for h in range(HEADS):
            sl = slice(h * HEAD_DIM, (h + 1) * HEAD_DIM)
            qh = q_ref[:, sl]
            qn, r = _rmsn(qh, w_ref[...])
            s = _dot_nt(qn, mk_ref[:, sl]) * SCALE
            e = jnp.exp(s - jnp.max(s, axis=-1, keepdims=True))
            p = e / jnp.sum(e, axis=-1, keepdims=True)
            do = do_ref[:, sl]
            dp = _dot_nt(do, mv_ref[:, sl])
            ds = p * (dp - jnp.sum(dp * p, axis=-1, keepdims=True)) * SCALE
            dmv_parts.append(_dot_tn(p, do))
            dmk_parts.append(_dot_tn(ds, qn))
            dq, dw = _rmsn_bwd(qh, r, w_ref[...], _dot(ds, mk_ref[:, sl]))
            dw_acc += dw
            dq_ref[:, sl] = dq.astype(dq_ref.dtype)
        _acc_out(dmk_ref, jnp.concatenate(dmk_parts, axis=1), first)
        _acc_out(dmv_ref, jnp.concatenate(dmv_parts, axis=1), first)
        _acc_out(dw_ref, dw_acc, first)

    full = _bs((N_MEM, GW), lambda i: (0, 0))
    vec = _bs((1, HEAD_DIM), lambda i: (0, 0))
    row = _bs((tm, GW), lambda i: (i, 0))
    return pl.pallas_call(
        body, grid=(T // tm,),
        in_specs=[_bs((tm, GW), lambda i: (i, 2)), full, full, vec, row],
        out_specs=[row, full, full, vec],
        out_shape=[S((T, GW), MXU), S((N_MEM, GW), F32), S((N_MEM, GW), F32), S((1, HEAD_DIM), F32)],
        name="cross_bwd", compiler_params=_cparams("arbitrary"))(z, mk, mv, xq_w, d_cross)


SUBLANES = 8


def _row_windows(ext, first, count, rows, shift_ref=None):
    for b in range(SUBLANES):
        js = [j for j in range(count) if (first + j) % SUBLANES == b]
        if not js:
            continue
        span = max(first + j for j in js) - b + rows
        shifted = ext[b:b + span, :]
        if shift_ref is not None:
            shift_ref[b, 0:span, :] = shifted
        for j in js:
            a = first + j - b
            yield j, (shifted[a:a + rows, :] if shift_ref is None else shift_ref[b, a:a + rows, :])


def _taps(ext, w_ref, width, base, rows, shift_ref=None):
    acc = None
    for k, win in _row_windows(ext, base - (width - 1), width, rows, shift_ref):
        term = win * w_ref[k:k + 1, :]
        acc = term if acc is None else acc + term
    return acc


def _taps_bwd(d_ext, x, w_ref, width, rows, shift_ref=None):
    acc = None
    dw = [None] * width
    for j, win in _row_windows(d_ext, 0, width, rows, shift_ref):
        k = width - 1 - j
        term = win * w_ref[k:k + 1, :]
        acc = term if acc is None else acc + term
        dw[k] = jnp.sum(win * x, axis=0, keepdims=True)
    return acc, jnp.concatenate(dw, axis=0)


def _conv_fwd(z, cw, cb, lw, lb):
    T = z.shape[0]
    tm = min(512, T)
    hb = tm // CONV_HALO

    def body(val_ref, gate_ref, hval_ref, hgate_ref, cw_ref, cb_ref, lw_ref, lb_ref, o_ref, shift_ref):
        i = pl.program_id(0)
        halo = hval_ref[...] * _sigmoid(hgate_ref[...])
        halo = jnp.where(i == 0, 0.0, halo)
        ext = jnp.concatenate([halo, val_ref[...] * _sigmoid(gate_ref[...])], axis=0)
        y = _taps(ext, cw_ref, CONV_WIDTH, CONV_HALO, tm, shift_ref) + cb_ref[...]
        xc = y - jnp.mean(y, axis=-1, keepdims=True)
        a = xc * lax.rsqrt(jnp.mean(xc * xc, axis=-1, keepdims=True) + LN_EPS) * lw_ref[...] + lb_ref[...]
        o_ref[...] = (a * _sigmoid(a)).astype(o_ref.dtype)

    vec = _bs((1, GW), lambda i: (0, 0))
    halo_spec = lambda col: _bs((CONV_HALO, GW), lambda i: (jnp.maximum(i * hb - 1, 0), col))
    return pl.pallas_call(
        body, grid=(T // tm,),
        in_specs=[_bs((tm, GW), lambda i: (i, 0)), _bs((tm, GW), lambda i: (i, 1)), halo_spec(0), halo_spec(1),
                  _bs((CONV_WIDTH, GW), lambda i: (0, 0)), vec, vec, vec],
        out_specs=_bs((tm, GW), lambda i: (i, 0)), out_shape=S((T, GW), MXU),
        scratch_shapes=[pltpu.VMEM((SUBLANES, tm + CONV_HALO, GW), F32)], name="conv_fwd",
        compiler_params=_cparams("parallel"))(z, z, z, z, cw, cb, lw, lb)


def _conv_bwd(z, cw, cb, lw, lb, d_u):
    T = z.shape[0]
    tm = min(512, T)
    hb = tm // CONV_HALO
    nt = T // tm
    H = CONV_HALO

    def body(val_ref, gate_ref, pval_ref, pgate_ref, nval_ref, ngate_ref, du_ref, ndu_ref, cw_ref, cb_ref, lw_ref,
             lb_ref, dval_ref, dgate_ref, dcw_ref, dcb_ref, dlw_ref, dlb_ref, shift_ref):
        i = pl.program_id(0)
        first = i == 0
        val = jnp.concatenate([pval_ref[...] * jnp.where(first, 0.0, 1.0), val_ref[...], nval_ref[...]], axis=0)
        sg = _sigmoid(jnp.concatenate([pgate_ref[...], gate_ref[...], ngate_ref[...]], axis=0))
        u0 = val * sg
        y = _taps(u0, cw_ref, CONV_WIDTH, H, tm + H, shift_ref) + cb_ref[...]
        xc = y - jnp.mean(y, axis=-1, keepdims=True)
        rs = lax.rsqrt(jnp.mean(xc * xc, axis=-1, keepdims=True) + LN_EPS)
        nh = xc * rs
        a = nh * lw_ref[...] + lb_ref[...]
        sa = _sigmoid(a)
        du = jnp.concatenate([du_ref[...], ndu_ref[...] * jnp.where(i == nt - 1, 0.0, 1.0)], axis=0)
        da = du * (sa * (1.0 + a * (1.0 - sa)))
        dn = da * lw_ref[...]
        dy = rs * (dn - jnp.mean(dn, axis=-1, keepdims=True) - nh * jnp.mean(dn * nh, axis=-1, keepdims=True))
        du0, dcw = _taps_bwd(dy, u0[H:H + tm], cw_ref, CONV_WIDTH, tm, shift_ref)
        v0, s0 = val[H:H + tm], sg[H:H + tm]
        dval_ref[...] = (du0 * s0).astype(dval_ref.dtype)
        dgate_ref[...] = (du0 * v0 * s0 * (1.0 - s0)).astype(dgate_ref.dtype)
        dy0 = dy[:tm]
        _acc_out(dcw_ref, dcw, first)
        _acc_out(dcb_ref, jnp.sum(dy0, axis=0, keepdims=True), first)
        _acc_out(dlw_ref, jnp.sum(da[:tm] * nh[:tm], axis=0, keepdims=True), first)
        _acc_out(dlb_ref, jnp.sum(da[:tm], axis=0, keepdims=True), first)

    vec = _bs((1, GW), lambda i: (0, 0))
    cwspec = _bs((CONV_WIDTH, GW), lambda i: (0, 0))
    prev = lambda col: _bs((H, GW), lambda i: (jnp.maximum(i * hb - 1, 0), col))
    nxt = lambda col: _bs((H, GW), lambda i: (jnp.minimum((i + 1) * hb, nt * hb - 1), col))
    row = _bs((tm, GW), lambda i: (i, 0))
    return pl.pallas_call(
        body, grid=(nt,),
        in_specs=[_bs((tm, GW), lambda i: (i, 0)), _bs((tm, GW), lambda i: (i, 1)), prev(0), prev(1), nxt(0), nxt(1),
                  row, nxt(0), cwspec, vec, vec, vec],
        out_specs=[row, row, cwspec, vec, vec, vec],
        out_shape=[S((T, GW), MXU), S((T, GW), MXU), S((CONV_WIDTH, GW), F32)] + [S((1, GW), F32)] * 3,
        scratch_shapes=[pltpu.VMEM((SUBLANES, tm + 2 * H, GW), F32)], name="conv_bwd", compiler_params=_cparams("arbitrary"))(z, z, z, z, z, z, d_u, d_u, cw, cb, lw, lb)


def _ffn_act_fwd(up0, fw, fb):
    T = up0.shape[0]
    tm = min(256, T)
    hb = tm // FFN_HALO
    H = FFN_HALO

    def body(a_ref, g_ref, pa_ref, pg_ref, wa_ref, wg_ref, ba_ref, bg_ref, o_ref, up_ref):
        i = pl.program_id(0)
        keep = jnp.where(i == 0, 0.0, 1.0)
        ea = jnp.concatenate([pa_ref[...] * keep, a_ref[...]], axis=0)
        eg = jnp.concatenate([pg_ref[...] * keep, g_ref[...]], axis=0)
        av = _taps(ea, wa_ref, FFN_CONV_WIDTH, H, tm) + ba_ref[...]
        gv = _taps(eg, wg_ref, FFN_CONV_WIDTH, H, tm) + bg_ref[...]
        o_ref[...] = (gv * _sigmoid(gv) * av).astype(o_ref.dtype)
        up_ref[:, :D_FF] = av
        up_ref[:, D_FF:] = gv

    col = lambda j: _bs((tm, D_FF), lambda i: (i, j))
    prev = lambda j: _bs((H, D_FF), lambda i: (jnp.maximum(i * hb - 1, 0), j))
    wspec = lambda j: _bs((FFN_CONV_WIDTH, D_FF), lambda i: (0, j))
    bspec = lambda j: _bs((1, D_FF), lambda i: (0, j))
    return pl.pallas_call(
        body, grid=(T // tm,),
        in_specs=[col(0), col(1), prev(0), prev(1), wspec(0), wspec(1), bspec(0), bspec(1)],
        out_specs=[_bs((tm, D_FF), lambda i: (i, 0)), _bs((tm, 2 * D_FF), lambda i: (i, 0))],
        out_shape=[S((T, D_FF), MXU), S((T, 2 * D_FF), F32)], name="ffn_act_fwd",
        compiler_params=_cparams("parallel"))(up0, up0, up0, up0, fw, fw, fb, fb)


def _ffn_act_bwd(up0, up, fw, d_f):
    T = up0.shape[0]
    tm = min(128, T)
    hb = tm // FFN_HALO
    nt = T // tm
    H = FFN_HALO
    W = FFN_CONV_WIDTH

    def body(a_ref, g_ref, av_ref, gv_ref, nav_ref, ngv_ref, df_ref, ndf_ref, wa_ref, wg_ref, dup_ref, dw_ref, db_ref):
        i = pl.program_id(0)
        first = i == 0
        av = jnp.concatenate([av_ref[...], nav_ref[...]], axis=0)
        gv = jnp.concatenate([gv_ref[...], ngv_ref[...]], axis=0)
        df = jnp.concatenate([df_ref[...], ndf_ref[...] * jnp.where(i == nt - 1, 0.0, 1.0)], axis=0)
        sg = _sigmoid(gv)
        d_av = df * gv * sg
        d_gv = df * av * (sg * (1.0 + gv * (1.0 - sg)))
        dua, dwa = _taps_bwd(d_av, a_ref[...], wa_ref, W, tm)
        dug, dwg = _taps_bwd(d_gv, g_ref[...], wg_ref, W, tm)
        dup_ref[:, :D_FF] = dua.astype(dup_ref.dtype)
        dup_ref[:, D_FF:] = dug.astype(dup_ref.dtype)
        dw = jnp.concatenate([dwa, dwg], axis=1)
        db = jnp.concatenate([jnp.sum(d_av[:tm], axis=0, keepdims=True), jnp.sum(d_gv[:tm], axis=0, keepdims=True)], axis=1)
        _acc_out(dw_ref, dw, first)
        _acc_out(db_ref, db, first)

    col = lambda j: _bs((tm, D_FF), lambda i: (i, j))
    nxt = lambda j: _bs((H, D_FF), lambda i: (jnp.minimum((i + 1) * hb, nt * hb - 1), j))
    wspec = lambda j: _bs((W, D_FF), lambda i: (0, j))
    return pl.pallas_call(
        body, grid=(nt,),
        in_specs=[col(0), col(1), col(0), col(1), nxt(0), nxt(1), col(0), nxt(0), wspec(0), wspec(1)],
        out_specs=[_bs((tm, 2 * D_FF), lambda i: (i, 0)), _bs((W, 2 * D_FF), lambda i: (0, 0)),
                   _bs((1, 2 * D_FF), lambda i: (0, 0))],
        out_shape=[S((T, 2 * D_FF), MXU), S((W, 2 * D_FF), F32), S((1, 2 * D_FF), F32)],
        name="ffn_act_bwd", compiler_params=_cparams("arbitrary"))(
            up0, up0, up, up, up, up, d_f, d_f, fw, fw)


def _branch_fwd(attn, u, cross, z, b_gate, wa, wc, wx):
    T = z.shape[0]
    tm = min(512, T)

    def body(a_ref, u_ref, x_ref, g0_ref, g1_ref, g2_ref, b_ref, wa_ref, wc_ref, wx_ref, o_ref):
        acc = None
        for j, (act, g_ref, w_ref) in enumerate(((a_ref, g0_ref, wa_ref), (u_ref, g1_ref, wc_ref), (x_ref, g2_ref, wx_ref))):
            gate = _sigmoid(g_ref[...] + b_ref[:, j * D_MODEL:(j + 1) * D_MODEL])
            term = gate * _dot(act[...], w_ref[...])
            acc = term if acc is None else acc + term
        o_ref[...] = acc.astype(o_ref.dtype)

    act = _bs((tm, GW), lambda i: (i, 0))
    gcol = lambda j: _bs((tm, D_MODEL), lambda i: (i, j))
    wfull = _bs((GW, D_MODEL), lambda i: (0, 0))
    return pl.pallas_call(
        body, grid=(T // tm,),
        in_specs=[act, act, act, gcol(0), gcol(1), gcol(2), _bs((1, 3 * D_MODEL), lambda i: (0, 0)), wfull, wfull, wfull],
        out_specs=_bs((tm, D_MODEL), lambda i: (i, 0)), out_shape=S((T, D_MODEL), MXU), name="branch_fwd",
        compiler_params=_cparams("parallel"))(attn, u, cross, z, z, z, b_gate, wa, wc, wx)


def _branch_bwd(d_merged, attn, u, cross, z, b_gate, wa, wc, wx, dils):
    T = z.shape[0]
    tm = min(512, T)
    d1, d2 = dils[1], dils[2]

    def body(dm_ref, a_ref, u_ref, x_ref, g0_ref, g1_ref, g2_ref, b_ref, wa_ref, wc_ref, wx_ref,
             dzg_ref, da_ref, du_ref, dx_ref, dwa_ref, dwc_ref, dwx_ref, db_ref, da1_ref, da2_ref, slab_ref):
        first = pl.program_id(0) == 0
        dm = dm_ref[...]
        dbs = []
        for j, (act, g_ref, w_ref, dact_ref, dw_ref) in enumerate((
                (a_ref, g0_ref, wa_ref, da_ref, dwa_ref), (u_ref, g1_ref, wc_ref, du_ref, dwc_ref),
                (x_ref, g2_ref, wx_ref, dx_ref, dwx_ref))):
            av = act[...]
            gate = _sigmoid(g_ref[...] + b_ref[:, j * D_MODEL:(j + 1) * D_MODEL])
            y = _dot(av, w_ref[...])
            dzg = dm * y * gate * (1.0 - gate)
            dzg_ref[:, j * D_MODEL:(j + 1) * D_MODEL] = dzg.astype(dzg_ref.dtype)
            dbs.append(jnp.sum(dzg, axis=0, keepdims=True))
            dy = (gate * dm).astype(MXU)
            dact = _dot_nt(dy, w_ref[...])
            dact_ref[...] = dact
            if j == 0:
                for h in range(HEADS):
                    slab_ref[h] = dact[:, h * HEAD_DIM:(h + 1) * HEAD_DIM]
                _tokens_to_classes(slab_ref, da1_ref, d1, tm)
                _tokens_to_classes(slab_ref, da2_ref, d2, tm)
            _acc_out(dw_ref, _dot_tn(av, dy), first)
        _acc_out(db_ref, jnp.concatenate(dbs, axis=1), first)

    act = _bs((tm, GW), lambda i: (i, 0))
    gcol = lambda j: _bs((tm, D_MODEL), lambda i: (i, j))
    wfull = _bs((GW, D_MODEL), lambda i: (0, 0))
    bvec = _bs((1, 3 * D_MODEL), lambda i: (0, 0))
    cls = lambda d: _bs((d, tm // d, GW), lambda i: (0, i, 0))
    outs = pl.pallas_call(
        body, grid=(T // tm,),
        in_specs=[_bs((tm, D_MODEL), lambda i: (i, 0)), act, act, act, gcol(0), gcol(1), gcol(2), bvec, wfull, wfull, wfull],
        out_specs=[_bs((tm, 3 * D_MODEL), lambda i: (i, 0)), act, act, act, wfull, wfull, wfull, bvec, cls(d1), cls(d2)],
        out_shape=[S((T, 3 * D_MODEL), MXU)] + [S((T, GW), F32)] * 3 + [S((GW, D_MODEL), F32)] * 3 + [S((1, 3 * D_MODEL), F32)]
        + [S((d1, T // d1, GW), F32), S((d2, T // d2, GW), F32)],
        scratch_shapes=[pltpu.VMEM((HEADS, tm, HEAD_DIM), F32)],
        name="branch_bwd", compiler_params=_cparams("arbitrary"))(d_merged, attn, u, cross, z, z, z, b_gate, wa, wc, wx)
    d_zg, d_attn, d_u, d_cross, dwa, dwc, dwx, db, da1, da2 = outs
    return d_zg, [d_attn, da1.reshape(T, GW), da2.reshape(T, GW)], d_u, d_cross, dwa, dwc, dwx, db


def _loss_head(y, target):
    T, D = y.shape
    tm = min(512, T)

    def body(y_ref, t_ref, dy_ref, l_ref):
        e = y_ref[...] - t_ref[...]
        dy_ref[...] = e * (1.0 / D)
        part = jnp.full((8, 128), jnp.sum(e * e), F32)
        _acc_out(l_ref, part, pl.program_id(0) == 0)

    row = _bs((tm, D), lambda i: (i, 0))
    return pl.pallas_call(
        body, grid=(T // tm,), in_specs=[row, row], out_specs=[row, _bs((8, 128), lambda i: (0, 0))],
        out_shape=[S((T, D), F32), S((8, 128), F32)], name="loss_head", compiler_params=_cparams("arbitrary"))(y, target)


def _peer(mask):
    x, y, c = lax.axis_index("x"), lax.axis_index("y"), lax.axis_index("c")
    px = 1 - x if mask & 4 else x
    py = 1 - y if mask & 2 else y
    pc = 1 - c if mask & 1 else c
    return (px, py, pc), 4 * px + 2 * py + pc


def _exchange(arrs, scatter, name):
    n = len(arrs)
    outs_shape = [S(a.shape if scatter else (N_DEV,) + a.shape, a.dtype) for a in arrs]

    def body(*refs):
        ins, outs = refs[:n], refs[n:2 * n]
        send_sems, recv_sems, local_sems = refs[2 * n:]
        me = 4 * lax.axis_index("x") + 2 * lax.axis_index("y") + lax.axis_index("c")
        copies = []
        for w in range(n):
            src = ins[w].at[me] if scatter else ins[w]
            cp = pltpu.make_async_copy(src, outs[w].at[me], local_sems.at[w])
            cp.start()
            copies.append(cp)
        for k in range(1, N_DEV):
            peer, pidx = _peer(k)
            for w in range(n):
                src = ins[w].at[pidx] if scatter else ins[w]
                cp = pltpu.make_async_remote_copy(
                    src_ref=src, dst_ref=outs[w].at[me], send_sem=send_sems.at[w, k - 1], recv_sem=recv_sems.at[w, k - 1],
                    device_id=peer, device_id_type=pl.DeviceIdType.MESH)
                cp.start()
                copies.append(cp)
        for cp in copies:
            cp.wait()

    hbm = pl.BlockSpec(memory_space=pl.ANY)
    return pl.pallas_call(
        body, in_specs=[hbm] * n, out_specs=[hbm] * n, out_shape=outs_shape,
        scratch_shapes=[pltpu.SemaphoreType.DMA((n, N_DEV - 1)), pltpu.SemaphoreType.DMA((n, N_DEV - 1)),
                        pltpu.SemaphoreType.DMA((n,))],
        name=name)(*arrs)


def _exchange_copies(ins, lands, send_sems, recv_sems, local_sems, scatter):
    n = len(ins)
    me = 4 * lax.axis_index("x") + 2 * lax.axis_index("y") + lax.axis_index("c")
    copies = []
    for w in range(n):
        src = ins[w].at[me] if scatter else ins[w]
        copies.append(pltpu.make_async_copy(src, lands[w].at[me], local_sems.at[w]))
    for k in range(1, N_DEV):
        peer, pidx = _peer(k)
        for w in range(n):
            src = ins[w].at[pidx] if scatter else ins[w]
            copies.append(pltpu.make_async_remote_copy(
                src_ref=src, dst_ref=lands[w].at[me], send_sem=send_sems.at[w * (N_DEV - 1) + k - 1],
                recv_sem=recv_sems.at[w * (N_DEV - 1) + k - 1],
                device_id=peer, device_id_type=pl.DeviceIdType.MESH))
    return copies


_HBM_SPEC = pl.BlockSpec(memory_space=pltpu.HBM)
_SEM_SPEC = pl.BlockSpec(memory_space=pltpu.SEMAPHORE)
_DATAFLOW = pltpu.SideEffectType.DATAFLOW_SIDE_EFFECTING


def _exchange_start(arrs, scatter, name):
    n = len(arrs)
    land_shapes = [a.shape if scatter else (N_DEV,) + a.shape for a in arrs]

    def body(*refs):
        ins, lands = refs[:n], refs[n:2 * n]
        send_sems, recv_sems, local_sems = refs[2 * n:2 * n + 3]
        token = refs[-1]
        for cp in _exchange_copies(ins, lands, send_sems, recv_sems, local_sems, scatter):
            cp.start()
        token[...] = jnp.zeros_like(token)

    out_shape = ([pltpu.SemaphoreType.DMA((n * (N_DEV - 1),)), pltpu.SemaphoreType.DMA((n * (N_DEV - 1),)),
                  pltpu.SemaphoreType.DMA((n,))]
                 + [pltpu.HBM(a.shape, a.dtype) for a in arrs]
                 + [pltpu.HBM(s, a.dtype) for s, a in zip(land_shapes, arrs)]
                 + [S((8, 128), F32)])
    args = ([pltpu.with_memory_space_constraint(a, pltpu.HBM) for a in arrs]
            + [pltpu.with_memory_space_constraint(lax.empty(s, a.dtype), pltpu.HBM) for s, a in zip(land_shapes, arrs)])
    outs = pl.pallas_call(
        body, in_specs=[_HBM_SPEC] * (2 * n),
        out_specs=[_SEM_SPEC] * 3 + [_HBM_SPEC] * (2 * n) + [pl.BlockSpec(memory_space=pltpu.VMEM)],
        out_shape=out_shape, input_output_aliases={j: 3 + j for j in range(2 * n)},
        name=name, compiler_params=pltpu.CompilerParams(has_side_effects=_DATAFLOW))(*args)
    return (n, scatter, outs[:3], outs[3:3 + n], outs[3 + n:3 + 2 * n]), outs[-1]


def _exchange_wait(state, after, name):
    n, scatter, sems, ins, lands = state

    def body(*refs):
        ins_r, lands_r = refs[:n], refs[n:2 * n]
        send_sems, recv_sems, local_sems = refs[2 * n:2 * n + 3]
        for cp in _exchange_copies(ins_r, lands_r, send_sems, recv_sems, local_sems, scatter):
            cp.wait()

    outs = pl.pallas_call(
        body, in_specs=[_HBM_SPEC] * (2 * n) + [_SEM_SPEC] * 3 + [pl.BlockSpec(memory_space=pl.ANY)],
        out_specs=[_HBM_SPEC] * (2 * n),
        out_shape=[pltpu.HBM(a.shape, a.dtype) for a in ins] + [pltpu.HBM(a.shape, a.dtype) for a in lands],
        input_output_aliases={j: j for j in range(2 * n)},
        name=name, compiler_params=pltpu.CompilerParams(has_side_effects=_DATAFLOW))(*ins, *lands, *sems, after)
    return list(outs[n:])


def _adamw(w, m, v, parts, name):
    R, C = w.shape
    P = parts.shape[0]
    tr = _pick(R, tuple(t for t in (256, 176, 128, 64, 32, 16, 8) if P * t * C * 4 <= ADAMW_BLOCK_BYTES))
    c1 = 1.0 / (1.0 - ADAM_B1 ** ADAM_STEP)
    c2 = 1.0 / (1.0 - ADAM_B2 ** ADAM_STEP)

    def body(w_ref, m_ref, v_ref, p_ref, g_ref, d_ref, nm_ref, nv_ref):
        g = p_ref[0].astype(F32)
        for j in range(1, P):
            g = g + p_ref[j].astype(F32)
        m2 = ADAM_B1 * m_ref[...] + (1.0 - ADAM_B1) * g
        v2 = ADAM_B2 * v_ref[...] + (1.0 - ADAM_B2) * (g * g)
        g_ref[...] = g
        nm_ref[...] = m2
        nv_ref[...] = v2
        d_ref[...] = -ADAM_LR * ((m2 * c1) / (jnp.sqrt(v2 * c2) + ADAM_EPS) + ADAM_WD * w_ref[...])

    row = _bs((tr, C), lambda i: (i, 0))
    return pl.pallas_call(
        body, grid=(R // tr,), in_specs=[row, row, row, _bs((P, tr, C), lambda i: (0, i, 0))], out_specs=[row] * 4,
        out_shape=[S((R, C), F32)] * 4, name=name, compiler_params=_cparams("parallel"))(w, m, v, parts)


def _sum_parts(parts, name):
    P, R, C = parts.shape

    def body(p_ref, o_ref):
        g = p_ref[0]
        for j in range(1, P):
            g = g + p_ref[j]
        o_ref[...] = g

    return pl.pallas_call(body, out_shape=S((R, C), F32), name=name, compiler_params=_cparams())(parts)


def _pack(arrs):
    flat = jnp.concatenate([a.reshape(-1) for a in arrs])
    rows = -(-flat.shape[0] // 1024) * 8
    return jnp.pad(flat, (0, rows * 128 - flat.shape[0])).reshape(rows, 128)


def _unpack(packed, shapes):
    flat = packed.reshape(-1)
    out, off = [], 0
    for s in shapes:
        n = int(np.prod(s))
        out.append(flat[off:off + n].reshape(s))
        off += n
    return out


def _behind(a, token):
    return a if token is None else a + token[0, 0]


def _local_step(x, mem, target, p, comm=None):
    table = p["rel_bias_table"]
    dils = [dil for _, dil in ATTN_GROUPS]
    xn, xn_classes = _rms_fwd_classes(x, p["attn_norm_w"], dils[1:], "attn_norm_fwd")
    if comm is not None:
        p = {**p, **comm.first_weights(after=xn)}
    xn_c = [xn] + xn_classes
    w_in = p["w_in"]
    qkv_w = 3 * N_GROUPS * GW
    wq = [jnp.concatenate([w_in[:, (N_GROUPS * part + g) * GW:(N_GROUPS * part + g + 1) * GW] for part in range(3)], axis=1)
          for g in range(N_GROUPS)]
    wc = w_in[:, qkv_w:qkv_w + 3 * GW]
    wg = w_in[:, qkv_w + 3 * GW:]
    zq = [_matmul(xn_c[g], wq[g], name=f"mm_in_qkv{g}") for g in range(N_GROUPS)]
    zc = _matmul(xn, wc, name="mm_in_c")
    zg = _matmul(xn, wg, name="mm_in_g")
    os_, lses = [], []
    for g, dil in enumerate(dils):
        o, l = _attn_fwd(zq[g], table, p["q_norm_w"][g:g + 1], p["k_norm_w"][g:g + 1], g, dil)
        os_.append(o)
        lses.append(l)
    attn, lse, attn_c, lse_c = _attn_merge(os_, lses, dils)
    u = _conv_fwd(zc, p["conv_dw_w"], p["conv_dw_b"], p["conv_ln_w"], p["conv_ln_b"])
    if comm is not None:
        p = {**p, **comm.late_weights(after=u)}
    mk, mv = _mem_fwd(mem, p["mem_norm_w"], p["w_mem_kv"], p["xk_norm_w"])
    cross = _cross_fwd(zc, mk, mv, p["xq_norm_w"])
    merged = _branch_fwd(attn, u, cross, zg, p["b_gate"], p["w_attn_o"], p["w_conv_o"], p["w_cross_o"])
    h1 = _matmul(merged, p["w_out"], residual=x, name="mm_out")
    hn = _rms_fwd(h1, p["ffn_norm_w"], "ffn_norm_fwd")
    up0 = _matmul(hn, p["w_up"], name="mm_up")
    f, up = _ffn_act_fwd(up0, p["ffn_conv_w"], p["ffn_conv_b"])
    h2 = _matmul(f, p["w_down"], residual=h1, name="mm_down")
    dh2, lsum = _loss_head(h2, target)
    g = {}
    d_f = _matmul(dh2, p["w_down"], tb=True, name="mm_down_dx")
    g["w_down"] = _matmul(f, dh2, ta=True, name="mm_down_dw")
    d_up0, g["ffn_conv_w"], g["ffn_conv_b"] = _ffn_act_bwd(up0, up, p["ffn_conv_w"], d_f)
    dhn = _matmul(d_up0, p["w_up"], tb=True, name="mm_up_dx")
    g["w_up"] = _matmul(hn, d_up0, ta=True, name="mm_up_dw")
    dh1, g["ffn_norm_w"] = _rms_bwd(h1, p["ffn_norm_w"], [dhn], dh2, "ffn_norm_bwd")
    d_merged = _matmul(dh1, p["w_out"], tb=True, name="mm_out_dx")
    g["w_out"] = _matmul(merged, dh1, ta=True, name="mm_out_dw")
    (d_zg, d_attn_c, d_u, d_cross, g["w_attn_o"], g["w_conv_o"], g["w_cross_o"], g["b_gate"]) = _branch_bwd(
        d_merged, attn, u, cross, zg, p["b_gate"], p["w_attn_o"], p["w_conv_o"], p["w_cross_o"], dils)
    d_xq, dmk, dmv, g["xq_norm_w"] = _cross_bwd(zc, mk, mv, p["xq_norm_w"], d_cross)
    g["w_mem_kv"], g["mem_norm_w"], g["xk_norm_w"] = _mem_bwd(mem, p["mem_norm_w"], p["w_mem_kv"], p["xk_norm_w"], dmk, dmv)
    tok = comm.start_early_grads(g) if comm is not None else None
    d_val, d_gate, g["conv_dw_w"], g["conv_dw_b"], g["conv_ln_w"], g["conv_ln_b"] = _conv_bwd(
        zc, p["conv_dw_w"], _behind(p["conv_dw_b"], tok), p["conv_ln_w"], p["conv_ln_b"], d_u)
    dzq, dqw, dkw, dtab = [], [], [], []
    for gi, dil in enumerate(dils):
        r = _attn_bwd(zq[gi], table, p["q_norm_w"][gi:gi + 1], p["k_norm_w"][gi:gi + 1], d_attn_c[gi], attn_c[gi],
                      lse_c[gi], gi, dil)
        for lst, val in zip((dzq, dqw, dkw, dtab), r):
            lst.append(val)
    g["q_norm_w"] = jnp.concatenate(dqw, axis=0)
    g["k_norm_w"] = jnp.concatenate(dkw, axis=0)
    g["rel_bias_table"] = jnp.concatenate(dtab, axis=1)
    d_zc = jnp.concatenate([d_val, d_gate, d_xq], axis=1)
    gq = [_matmul(xn_c[gi], dzq[gi], ta=True, name=f"mm_in_qkv{gi}_dw") for gi in range(N_GROUPS)]
    gc = _matmul(xn, d_zc, ta=True, name="mm_in_c_dw")
    gg = _matmul(xn, d_zg, ta=True, name="mm_in_g_dw")
    g["w_in"] = jnp.concatenate(
        [gq[gi][:, part * GW:(part + 1) * GW] for part in range(3) for gi in range(N_GROUPS)] + [gc, gg], axis=1)
    tok = comm.start_w_in_grad(g["w_in"]) if comm is not None else None
    dxn = _matmul(d_zg, wg, tb=True, after=tok, name="mm_in_g_dx")
    dxn = _matmul(d_zc, wc, tb=True, residual=dxn, name="mm_in_c_dx")
    dxn = _matmul(dzq[0], wq[0], tb=True, residual=dxn, name="mm_in_qkv0_dx")
    dx_classes = [(_matmul(dzq[gi], wq[gi], tb=True, name=f"mm_in_qkv{gi}_dx"), dils[gi]) for gi in range(1, N_GROUPS)]
    grad_x, g["attn_norm_w"] = _rms_bwd(x, p["attn_norm_w"], [dxn], dh1, "attn_norm_bwd", class_dys=dx_classes)
    return lsum[0, 0], grad_x, g


WEIGHT_NAMES = ["rel_bias_table", "attn_norm_w", "w_in", "b_gate", "q_norm_w", "k_norm_w", "w_attn_o", "conv_dw_w",
                "conv_dw_b", "conv_ln_w", "conv_ln_b", "w_conv_o", "mem_norm_w", "w_mem_kv", "xq_norm_w", "xk_norm_w",
                "w_cross_o", "w_out", "ffn_norm_w", "w_up", "ffn_conv_w", "ffn_conv_b", "w_down"]
COL_SHARDED = ("w_in", "w_attn_o", "w_conv_o", "w_cross_o", "w_up")
ROW_SHARDED = ("w_mem_kv", "w_out", "w_down")
SMALL_COL_SHARDED = ("conv_dw_w", "ffn_conv_w")
BIG = COL_SHARDED + ROW_SHARDED


def _cols_to_blocks(a):
    k, n8 = a.shape
    return a.reshape(k, N_DEV, n8 // N_DEV).transpose(1, 0, 2)


def _blocks_to_cols(a):
    return a.transpose(1, 0, 2).reshape(a.shape[1], N_DEV * a.shape[2])


def _step(x, mem, target, w, m, v):
    me = 4 * lax.axis_index("x") + 2 * lax.axis_index("y") + lax.axis_index("c")

    def to_full(n, blocks):
        return _blocks_to_cols(blocks) if n in COL_SHARDED + SMALL_COL_SHARDED else blocks.reshape(-1, blocks.shape[-1])

    def to_blocks(n, grad):
        blocks = _cols_to_blocks(grad) if n in COL_SHARDED else grad.reshape(N_DEV, -1, grad.shape[-1])
        return blocks.astype(MXU)

    first = ("w_in",) + SMALL_COL_SHARDED
    late = tuple(n for n in BIG if n != "w_in")
    cast = lambda n: w[n].astype(MXU) if n in BIG else w[n]
    first_state, _ = _exchange_start([cast(n) for n in first], False, "gather_first_start")
    late_state, late_token = _exchange_start([cast(n) for n in late], False, "gather_late_start")
    p = {n: w[n] for n in WEIGHT_NAMES if n not in BIG + SMALL_COL_SHARDED}
    p["attn_norm_w"] = _behind(p["attn_norm_w"], late_token)

    class Comm:
        def first_weights(self, after):
            return {n: to_full(n, b) for n, b in zip(first, _exchange_wait(first_state, after, "gather_first_wait"))}

        def late_weights(self, after):
            return {n: to_full(n, b) for n, b in zip(late, _exchange_wait(late_state, after, "gather_late_wait"))}

        def start_early_grads(self, g):
            self.early_state, token = _exchange_start([to_blocks(n, g[n]) for n in late], True, "scatter_early_start")
            return token

        def start_w_in_grad(self, grad):
            self.w_in_state, token = _exchange_start([to_blocks("w_in", grad)], True, "scatter_w_in_start")
            return token

    comm = Comm()
    lsum, grad_x, g = _local_step(x, mem, target, p, comm)
    small_names = [n for n in WEIGHT_NAMES if n not in BIG]
    small_shapes = [g[n].shape for n in small_names]
    small_parts = _exchange([_pack([g[n] for n in small_names])], False, "gather_small_grads")[0]
    gsmall = dict(zip(small_names, _unpack(_sum_parts(small_parts, "sum_small_grads"), small_shapes)))
    for n in SMALL_COL_SHARDED:
        width = w[n].shape[-1]
        gsmall[n] = lax.dynamic_slice_in_dim(gsmall[n], me * width, width, axis=1)
    res = {}
    parts = dict(zip(late, _exchange_wait(comm.early_state, grad_x, "scatter_early_wait")))
    for n in late:
        res[n] = _adamw(w[n], m[n], v[n], parts[n], "adamw_" + n)
    w_in_parts = _exchange_wait(comm.w_in_state, res[late[-1]][1], "scatter_w_in_wait")[0]
    res["w_in"] = _adamw(w["w_in"], m["w_in"], v["w_in"], w_in_parts, "adamw_w_in")
    shapes = [w[n].shape for n in small_names]
    packed = [_pack([d[n] for n in small_names]) for d in (w, m, v, gsmall)]
    outs = _adamw(packed[0], packed[1], packed[2], packed[3][None], "adamw_small")
    unpacked = [_unpack(o, shapes) for o in outs]
    for j, n in enumerate(small_names):
        res[n] = tuple(unpacked[q][j] for q in range(4))
    return lsum, grad_x, res


def kernel(x, mem, rel_bias_table, attn_norm_w, w_in, b_gate, q_norm_w, k_norm_w, w_attn_o, conv_dw_w, conv_dw_b, conv_ln_w, conv_ln_b, w_conv_o, mem_norm_w, w_mem_kv, xq_norm_w, xk_norm_w, w_cross_o, w_out, ffn_norm_w, w_up, ffn_conv_w, ffn_conv_b, w_down, loss_target, m_rel_bias_table, m_attn_norm_w, m_w_in, m_b_gate, m_q_norm_w, m_k_norm_w, m_w_attn_o, m_conv_dw_w, m_conv_dw_b, m_conv_ln_w, m_conv_ln_b, m_w_conv_o, m_mem_norm_w, m_w_mem_kv, m_xq_norm_w, m_xk_norm_w, m_w_cross_o, m_w_out, m_ffn_norm_w, m_w_up, m_ffn_conv_w, m_ffn_conv_b, m_w_down, v_rel_bias_table, v_attn_norm_w, v_w_in, v_b_gate, v_q_norm_w, v_k_norm_w, v_w_attn_o, v_conv_dw_w, v_conv_dw_b, v_conv_ln_w, v_conv_ln_b, v_w_conv_o, v_mem_norm_w, v_w_mem_kv, v_xq_norm_w, v_xk_norm_w, v_w_cross_o, v_w_out, v_ffn_norm_w, v_w_up, v_ffn_conv_w, v_ffn_conv_b, v_w_down):
    ws = dict(zip(WEIGHT_NAMES, (rel_bias_table, attn_norm_w, w_in, b_gate, q_norm_w, k_norm_w, w_attn_o, conv_dw_w, conv_dw_b, conv_ln_w, conv_ln_b, w_conv_o, mem_norm_w, w_mem_kv, xq_norm_w, xk_norm_w, w_cross_o, w_out, ffn_norm_w, w_up, ffn_conv_w, ffn_conv_b, w_down)))
    ms = dict(zip(WEIGHT_NAMES, (m_rel_bias_table, m_attn_norm_w, m_w_in, m_b_gate, m_q_norm_w, m_k_norm_w, m_w_attn_o, m_conv_dw_w, m_conv_dw_b, m_conv_ln_w, m_conv_ln_b, m_w_conv_o, m_mem_norm_w, m_w_mem_kv, m_xq_norm_w, m_xk_norm_w, m_w_cross_o, m_w_out, m_ffn_norm_w, m_w_up, m_ffn_conv_w, m_ffn_conv_b, m_w_down)))
    vs = dict(zip(WEIGHT_NAMES, (v_rel_bias_table, v_attn_norm_w, v_w_in, v_b_gate, v_q_norm_w, v_k_norm_w, v_w_attn_o, v_conv_dw_w, v_conv_dw_b, v_conv_ln_w, v_conv_ln_b, v_w_conv_o, v_mem_norm_w, v_w_mem_kv, v_xq_norm_w, v_xk_norm_w, v_w_cross_o, v_w_out, v_ffn_norm_w, v_w_up, v_ffn_conv_w, v_ffn_conv_b, v_w_down)))
    full_shapes = {n: ws[n].shape for n in WEIGHT_NAMES}

    def squeeze(d):
        return {n: (a if n == "rel_bias_table" else a[0]) for n, a in d.items()}

    w, m, v = squeeze(ws), squeeze(ms), squeeze(vs)
    for d in (w, m, v):
        for n in WEIGHT_NAMES:
            if d[n].ndim == 1:
                d[n] = d[n][None]
    lsum, grad_x, res = _step(x[0], mem[0], loss_target[0], w, m, v)
    loss = lax.psum(0.5 / D_MODEL * lsum, ("x", "y", "c"))
    outs = [loss, grad_x[None]]
    for q in range(4):
        outs += [res[n][q].reshape(full_shapes[n]) for n in WEIGHT_NAMES]
    return tuple(outs)
```

```python
import functools
import math

import numpy as np
import jax
import jax.numpy as jnp
from jax import lax
from jax.experimental import pallas as pl
from jax.experimental.pallas import tpu as pltpu

F32 = jnp.float32
MXU = jnp.bfloat16
S = jax.ShapeDtypeStruct

D_MODEL = 1024
HEAD_DIM = 128
ATTN_GROUPS = ((128, 1), (512, 4), (2048, 16))
N_GROUPS = 3
HEADS = 4
GW = HEADS * HEAD_DIM
CONV_WIDTH = 31
N_MEM = 256
D_FF = 2816
FFN_CONV_WIDTH = 3
N_BUCKETS = 32
MAX_DISTANCE = 2048
RMS_EPS = 1e-6
LN_EPS = 1e-5
BLK = 128
SCALE = HEAD_DIM ** -0.5
NEG = -1e30
N_DEV = 8

ADAM_LR, ADAM_B1, ADAM_B2, ADAM_EPS, ADAM_WD, ADAM_STEP = 0.001, 0.9, 0.999, 1e-08, 0.01, 10

VMEM_LIMIT = 48 * 1024 * 1024
CONV_HALO = 32
FFN_HALO = 8
ADAMW_BLOCK_BYTES = 4 * 1024 * 1024


def _cparams(*sem):
    return pltpu.CompilerParams(dimension_semantics=sem or None, vmem_limit_bytes=VMEM_LIMIT)


def _bs(shape, imap):
    return pl.BlockSpec(shape, imap)


def _dot(a, b):
    return lax.dot_general(a.astype(MXU), b.astype(MXU), (((1,), (0,)), ((), ())), preferred_element_type=F32)


def _dot_nt(a, b):
    return lax.dot_general(a.astype(MXU), b.astype(MXU), (((1,), (1,)), ((), ())), preferred_element_type=F32)


def _dot_tn(a, b):
    return lax.dot_general(a.astype(MXU), b.astype(MXU), (((0,), (0,)), ((), ())), preferred_element_type=F32)


def _sigmoid(x):
    return 0.5 * jnp.tanh(0.5 * x) + 0.5


def _rmsn(x, w):
    r = lax.rsqrt(jnp.mean(x * x, axis=-1, keepdims=True) + RMS_EPS)
    return x * r * w, r


def _rmsn_bwd(x, r, w, dy):
    g = dy * w
    dx = r * g - x * (r * r * r) * jnp.mean(x * g, axis=-1, keepdims=True)
    dw = jnp.sum(dy * x * r, axis=0, keepdims=True)
    return dx, dw


def _acc_out(ref, val, first):
    @pl.when(first)
    def _():
        ref[...] = val

    @pl.when(jnp.logical_not(first))
    def _():
        ref[...] += val


def _rms_fwd(x, w, name):
    T, D = x.shape
    tm = min(512, T)

    def body(x_ref, w_ref, o_ref):
        y, _ = _rmsn(x_ref[...], w_ref[...])
        o_ref[...] = y.astype(o_ref.dtype)

    return pl.pallas_call(
        body, grid=(T // tm,),
        in_specs=[_bs((tm, D), lambda i: (i, 0)), _bs((1, D), lambda i: (0, 0))],
        out_specs=_bs((tm, D), lambda i: (i, 0)),
        out_shape=S((T, D), MXU), name=name, compiler_params=_cparams("parallel"))(x, w)


def _rms_fwd_classes(x, w, dils, name):
    T, D = x.shape
    tm = min(512, T)
    slabs = D // HEAD_DIM

    def body(x_ref, w_ref, o_ref, *rest):
        outs, slab_ref = rest[:-1], rest[-1]
        y, _ = _rmsn(x_ref[...], w_ref[...])
        o_ref[...] = y.astype(o_ref.dtype)
        for j in range(slabs):
            slab_ref[j] = y[:, j * HEAD_DIM:(j + 1) * HEAD_DIM]
        for out_ref, d in zip(outs, dils):
            _tokens_to_classes(slab_ref, out_ref, d, tm)

    outs = pl.pallas_call(
        body, grid=(T // tm,),
        in_specs=[_bs((tm, D), lambda i: (i, 0)), _bs((1, D), lambda i: (0, 0))],
        out_specs=[_bs((tm, D), lambda i: (i, 0))] + [_bs((d, tm // d, D), lambda i: (0, i, 0)) for d in dils],
        out_shape=[S((T, D), MXU)] + [S((d, T // d, D), MXU) for d in dils],
        scratch_shapes=[pltpu.VMEM((slabs, tm, HEAD_DIM), F32)], name=name, compiler_params=_cparams("parallel"))(x, w)
    return outs[0], [a.reshape(T, D) for a in outs[1:]]


def _rms_bwd(x, w, dys, resid, name, class_dys=()):
    T, D = x.shape
    tm = min(512, T)
    n = len(dys)
    nc = len(class_dys)
    slabs = D // HEAD_DIM

    def body(*refs):
        x_ref, w_ref, res_ref = refs[0], refs[1], refs[2 + n + nc]
        dx_ref, dw_ref = refs[3 + n + nc], refs[4 + n + nc]
        xv = x_ref[...]
        dy = refs[2][...]
        for dy_ref in refs[3:2 + n]:
            dy = dy + dy_ref[...]
        for dy_ref, (_, d) in zip(refs[2 + n:2 + n + nc], class_dys):
            slab_ref = refs[-1]
            _classes_to_tokens(dy_ref, slab_ref, d, tm)
            dy = dy + jnp.concatenate([slab_ref[j] for j in range(slabs)], axis=1)
        _, r = _rmsn(xv, w_ref[...])
        dx, dw = _rmsn_bwd(xv, r, w_ref[...], dy)
        dx_ref[...] = res_ref[...] + dx
        _acc_out(dw_ref, dw, pl.program_id(0) == 0)

    row = _bs((tm, D), lambda i: (i, 0))
    vec = _bs((1, D), lambda i: (0, 0))
    return pl.pallas_call(
        body, grid=(T // tm,),
        in_specs=[row, vec] + [row] * n + [_bs((d, tm // d, D), lambda i: (0, i, 0)) for _, d in class_dys] + [row],
        out_specs=[row, vec], out_shape=[S((T, D), F32), S((1, D), F32)],
        scratch_shapes=[pltpu.VMEM((slabs, tm, HEAD_DIM), F32)] if nc else [],
        name=name, compiler_params=_cparams("arbitrary"))(
            x, w, *dys, *[a.reshape(d, T // d, D) for a, d in class_dys], resid)


def _pick(n, cands):
    for c in cands:
        if n % c == 0:
            return c
    return n


MM_VMEM_BUDGET = 36 * 1024 * 1024


def _mm_tiles(tm, N, K, a_bytes, b_bytes, o_bytes, has_res):
    best = None
    for tn in (1536, 1024, 1408, 512, 256, 128):
        for tk in (3072, 1536, 1024, 1408, 512, 256, 128):
            if N % tn or K % tk:
                continue
            nk = K // tk
            need = 2 * (tm * tk * a_bytes + tk * tn * b_bytes + tm * tn * (o_bytes + 4 * has_res)) + (nk > 1) * tm * tn * 4
            if need > MM_VMEM_BUDGET:
                continue
            key = ((N // tn) * nk, nk)
            if best is None or key < best[0]:
                best = (key, tn, tk)
    if best is None:
        return _pick(N, (128,)), _pick(K, (128,))
    return best[1], best[2]


def _matmul(a, b, *, ta=False, tb=False, out_dtype=F32, residual=None, after=None, tm=None, tn=None, tk=None, name):
    M, K = (a.shape[1], a.shape[0]) if ta else a.shape
    N = b.shape[0] if tb else b.shape[1]
    tm = tm or _pick(M, (1024, 1408, 512, 256, 128))
    if tn is None or tk is None:
        tn, tk = _mm_tiles(tm, N, K, a.dtype.itemsize, b.dtype.itemsize, jnp.dtype(out_dtype).itemsize, residual is not None)
    nk = K // tk
    dn = (((0 if ta else 1,), (1 if tb else 0,)), ((), ()))
    has_res = residual is not None
    n_in = 2 + has_res + (after is not None)

    def body(*refs):
        a_ref, b_ref = refs[0], refs[1]
        res_ref = refs[2] if has_res else None
        o_ref = refs[n_in]
        p = lax.dot_general(a_ref[...].astype(MXU), b_ref[...].astype(MXU), dn, preferred_element_type=F32)

        def finish(acc):
            if has_res:
                acc = acc + res_ref[...]
            o_ref[...] = acc.astype(o_ref.dtype)

        if nk == 1:
            finish(p)
        else:
            acc_ref = refs[-1]
            k = pl.program_id(2)

            @pl.when(k == 0)
            def _():
                acc_ref[...] = p

            @pl.when(k > 0)
            def _():
                acc_ref[...] += p

            @pl.when(k == nk - 1)
            def _():
                finish(acc_ref[...])

    a_spec = _bs((tk, tm), lambda i, j, k: (k, i)) if ta else _bs((tm, tk), lambda i, j, k: (i, k))
    b_spec = _bs((tn, tk), lambda i, j, k: (j, k)) if tb else _bs((tk, tn), lambda i, j, k: (k, j))
    o_spec = _bs((tm, tn), lambda i, j, k: (i, j))
    in_specs = [a_spec, b_spec] + ([o_spec] if has_res else [])
    args = (a, b) + ((residual,) if has_res else ())
    if after is not None:
        in_specs.append(_bs((8, 128), lambda i, j, k: (0, 0)))
        args += (after,)
    return pl.pallas_call(
        body, grid=(M // tm, N // tn, nk), in_specs=in_specs, out_specs=o_spec,
        out_shape=S((M, N), out_dtype), scratch_shapes=[pltpu.VMEM((tm, tn), F32)] if nk > 1 else [],
        name=name, compiler_params=_cparams("parallel", "parallel", "arbitrary"))(*args)


def _bucket_matrix(dilation):
    n = BLK
    qi = np.arange(n)[:, None]
    kj = np.arange(2 * n)[None, :]
    step = qi + n - kj
    dist = np.clip(step, 0, None) * dilation
    max_exact = N_BUCKETS // 2
    d = np.maximum(dist.astype(np.float32), np.float32(1.0))
    large = max_exact + (np.log(d / np.float32(max_exact)) / np.float32(math.log(MAX_DISTANCE / max_exact))
                         * np.float32(N_BUCKETS - max_exact)).astype(np.int32)
    large = np.minimum(large, N_BUCKETS - 1)
    bucket = np.where(dist < max_exact, dist, large)
    band = (step >= 0) & (step <= n)
    return np.where(band, bucket, -1).astype(np.int32)


def _build_bias(tbl_ref, bkt_ref, bias_ref, g):
    bk = bkt_ref[...]
    for h in range(HEADS):
        acc = jnp.full(bk.shape, NEG, F32)
        for b in range(N_BUCKETS):
            acc = jnp.where(bk == b, tbl_ref[b, HEADS * g + h], acc)
        bias_ref[h] = acc


def _attn_fwd(zq, table, qw, kw, g, dil):
    T = zq.shape[0]
    nb = T // dil // BLK
    qb = _pick(nb, (4, 2, 1))
    nt = nb // qb
    bkt = jnp.asarray(_bucket_matrix(dil))

    def zspec(part, prev):
        if prev:
            return _bs((BLK, GW), lambda c, i: (c * nb + jnp.maximum(i * qb - 1, 0), part))
        return _bs((qb * BLK, GW), lambda c, i: (c * nt + i, part))

    def body(tbl_ref, bkt_ref, qw_ref, kw_ref, q_ref, kp_ref, kc_ref, vp_ref, vc_ref, o_ref, lse_ref, bias_ref):
        c, i = pl.program_id(0), pl.program_id(1)

        @pl.when((c == 0) & (i == 0))
        def _():
            _build_bias(tbl_ref, bkt_ref, bias_ref, g)

        kj = lax.broadcasted_iota(jnp.int32, (BLK, 2 * BLK), 1)
        no_prev = jnp.logical_and(i == 0, kj < BLK)
        for h in range(HEADS):
            sl = slice(h * HEAD_DIM, (h + 1) * HEAD_DIM)
            qn, _ = _rmsn(q_ref[:, sl], qw_ref[...])
            kn, _ = _rmsn(jnp.concatenate([kp_ref[:, sl], kc_ref[:, sl]], axis=0), kw_ref[...])
            v = jnp.concatenate([vp_ref[:, sl], vc_ref[:, sl]], axis=0)
            for j in range(qb):
                rows = slice(j * BLK, (j + 1) * BLK)
                keys = slice(j * BLK, (j + 2) * BLK)
                s = _dot_nt(qn[rows], kn[keys]) * SCALE + bias_ref[h]
                if j == 0:
                    s = jnp.where(no_prev, NEG, s)
                m = jnp.max(s, axis=-1, keepdims=True)
                p = jnp.exp(s - m)
                l = jnp.sum(p, axis=-1, keepdims=True)
                o_ref[rows, sl] = _dot(p, v[keys]) / l
                lse_ref[rows, sl] = jnp.broadcast_to(m + jnp.log(l), (BLK, HEAD_DIM))

    ospec = _bs((qb * BLK, GW), lambda c, i: (c * nt + i, 0))
    vec = _bs((1, HEAD_DIM), lambda c, i: (0, 0))
    return pl.pallas_call(
        body, grid=(dil, nt),
        in_specs=[pl.BlockSpec(memory_space=pltpu.SMEM), _bs((BLK, 2 * BLK), lambda c, i: (0, 0)), vec, vec,
                  zspec(0, False), zspec(1, True), zspec(1, False), zspec(2, True), zspec(2, False)],
        out_specs=[ospec, ospec],
        out_shape=[S((T, GW), F32), S((T, GW), F32)],
        scratch_shapes=[pltpu.VMEM((HEADS, BLK, 2 * BLK), F32)],
        name=f"attn_fwd_g{g}", compiler_params=_cparams("arbitrary", "arbitrary"))(table, bkt, qw, kw, zq, zq, zq, zq, zq)


def _classes_to_tokens(src_ref, dst_ref, dil, rows):
    for c in range(dil):
        for h in range(dst_ref.shape[0]):
            dst_ref[h, pl.ds(c, rows // dil, stride=dil), :] = src_ref[c, :, h * HEAD_DIM:(h + 1) * HEAD_DIM]


def _tokens_to_classes(src_ref, dst_ref, dil, rows):
    for c in range(dil):
        for h in range(src_ref.shape[0]):
            dst_ref[c, :, h * HEAD_DIM:(h + 1) * HEAD_DIM] = src_ref[h, pl.ds(c, rows // dil, stride=dil), :].astype(dst_ref.dtype)


def _class_view(a, dil):
    return a.reshape(dil, a.shape[0] // dil, a.shape[1])


def _attn_merge(os_, lses, dils):
    T = os_[0].shape[0]
    tm = min(512, T)
    assert dils[0] == 1 and len(dils) == 3

    def body(o0, l0, o1, l1, o2, l2, a_ref, lse_ref, a1_ref, lse1_ref, a2_ref, lse2_ref,
             no1, nl1, no2, nl2, ra, rl):
        for src, dst, d in ((o1, no1, dils[1]), (l1, nl1, dils[1]), (o2, no2, dils[2]), (l2, nl2, dils[2])):
            _classes_to_tokens(src, dst, d, tm)
        for h in range(HEADS):
            sl = slice(h * HEAD_DIM, (h + 1) * HEAD_DIM)
            ls = [l0[:, sl], nl1[h], nl2[h]]
            os3 = [o0[:, sl], no1[h], no2[h]]
            mx = jnp.maximum(jnp.maximum(ls[0], ls[1]), ls[2])
            tot = mx + jnp.log(jnp.exp(ls[0] - mx) + jnp.exp(ls[1] - mx) + jnp.exp(ls[2] - mx))
            att = jnp.exp(ls[0] - tot) * os3[0] + jnp.exp(ls[1] - tot) * os3[1] + jnp.exp(ls[2] - tot) * os3[2]
            a_ref[:, sl] = att
            lse_ref[:, sl] = tot
            ra[h] = att
            rl[h] = tot
        for src, dst, d in ((ra, a1_ref, dils[1]), (rl, lse1_ref, dils[1]), (ra, a2_ref, dils[2]), (rl, lse2_ref, dils[2])):
            _tokens_to_classes(src, dst, d, tm)

    row = _bs((tm, GW), lambda i: (i, 0))
    cls = lambda d: _bs((d, tm // d, GW), lambda i: (0, i, 0))
    cshape = lambda d: S((d, T // d, GW), F32)
    slab = pltpu.VMEM((HEADS, tm, HEAD_DIM), F32)
    d1, d2 = dils[1], dils[2]
    attn, lse, a1, l1, a2, l2 = pl.pallas_call(
        body, grid=(T // tm,), in_specs=[row, row, cls(d1), cls(d1), cls(d2), cls(d2)],
        out_specs=[row, row, cls(d1), cls(d1), cls(d2), cls(d2)],
        out_shape=[S((T, GW), F32), S((T, GW), F32), cshape(d1), cshape(d1), cshape(d2), cshape(d2)],
        scratch_shapes=[slab] * 6, name="attn_merge", compiler_params=_cparams("parallel"))(
            os_[0], lses[0], _class_view(os_[1], d1), _class_view(lses[1], d1), _class_view(os_[2], d2),
            _class_view(lses[2], d2))
    flat = lambda a: a.reshape(T, GW)
    return attn, lse, [attn, flat(a1), flat(a2)], [lse, flat(l1), flat(l2)]


def _attn_bwd(zq, table, qw, kw, d_attn, attn, lse, g, dil):
    T = zq.shape[0]
    nb = T // dil // BLK
    qb = _pick(nb, (4, 2, 1))
    nt = nb // qb
    bkt = jnp.asarray(_bucket_matrix(dil))

    def body(tbl_ref, bkt_ref, qw_ref, kw_ref, q_ref, k_ref, v_ref, kp_ref, vp_ref, qx_ref, da_ref, at_ref, lse_ref,
             dax_ref, atx_ref, lsex_ref, dz_ref, dqw_ref, dkw_ref, dtab_ref, bias_ref, dbias_ref):
        c, i = pl.program_id(0), pl.program_id(1)

        @pl.when((c == 0) & (i == 0))
        def _():
            _build_bias(tbl_ref, bkt_ref, bias_ref, g)
            dbias_ref[...] = jnp.zeros_like(dbias_ref)
            dqw_ref[...] = jnp.zeros_like(dqw_ref)
            dkw_ref[...] = jnp.zeros_like(dkw_ref)

        kj = lax.broadcasted_iota(jnp.int32, (BLK, 2 * BLK), 1)
        no_prev = jnp.logical_and(i == 0, kj < BLK)
        has_next = i < nt - 1
        last = slice((qb - 1) * BLK, qb * BLK)
        dqw_acc = jnp.zeros((1, HEAD_DIM), F32)
        dkw_acc = jnp.zeros((1, HEAD_DIM), F32)

        def add(parts, t, val):
            parts[t] = val if parts[t] is None else parts[t] + val

        for h in range(HEADS):
            lo = h * HEAD_DIM
            sl = slice(lo, lo + HEAD_DIM)
            q, k = q_ref[:, sl], k_ref[:, sl]
            qn, rq = _rmsn(q, qw_ref[...])
            kn, rk = _rmsn(k, kw_ref[...])
            kpn, _ = _rmsn(kp_ref[:, sl], kw_ref[...])
            kn_ext = jnp.concatenate([kpn, kn], axis=0)
            v_ext = jnp.concatenate([vp_ref[:, sl], v_ref[:, sl]], axis=0)
            dqn, dkn, dv = [None] * qb, [None] * qb, [None] * qb
            for j in range(qb):
                rows = slice(j * BLK, (j + 1) * BLK)
                keys = slice(j * BLK, (j + 2) * BLK)
                s = _dot_nt(qn[rows], kn_ext[keys]) * SCALE + bias_ref[h]
                if j == 0:
                    s = jnp.where(no_prev, NEG, s)
                p = jnp.exp(s - lse_ref[rows, lo:lo + 1])
                do = da_ref[rows, sl]
                delta = jnp.sum(do * at_ref[rows, sl], axis=-1, keepdims=True)
                ds = p * (_dot_nt(do, v_ext[keys]) - delta)
                dbias_ref[h] += ds
                dqn[j] = _dot(ds, kn_ext[keys]) * SCALE
                dv2 = _dot_tn(p, do)
                dk2 = _dot_tn(ds, qn[rows]) * SCALE
                if j >= 1:
                    add(dv, j - 1, dv2[:BLK])
                    add(dkn, j - 1, dk2[:BLK])
                add(dv, j, dv2[BLK:])
                add(dkn, j, dk2[BLK:])
            qxn, _ = _rmsn(qx_ref[:, sl], qw_ref[...])
            sx = _dot_nt(qxn, kn[last]) * SCALE + bias_ref[h, :, 0:BLK]
            px = jnp.where(has_next, jnp.exp(sx - lsex_ref[:, lo:lo + 1]), 0.0)
            dox = dax_ref[:, sl]
            dsx = px * (_dot_nt(dox, v_ref[last, sl]) - jnp.sum(dox * atx_ref[:, sl], axis=-1, keepdims=True))
            add(dv, qb - 1, _dot_tn(px, dox))
            add(dkn, qb - 1, _dot_tn(dsx, qxn) * SCALE)
            dq, dqw = _rmsn_bwd(q, rq, qw_ref[...], jnp.concatenate(dqn, axis=0))
            dk, dkw = _rmsn_bwd(k, rk, kw_ref[...], jnp.concatenate(dkn, axis=0))
            dqw_acc += dqw
            dkw_acc += dkw
            dz_ref[:, lo:lo + HEAD_DIM] = dq.astype(dz_ref.dtype)
            dz_ref[:, GW + lo:GW + lo + HEAD_DIM] = dk.astype(dz_ref.dtype)
            dz_ref[:, 2 * GW + lo:2 * GW + lo + HEAD_DIM] = jnp.concatenate(dv, axis=0).astype(dz_ref.dtype)
        dqw_ref[...] += dqw_acc
        dkw_ref[...] += dkw_acc

        @pl.when((c == dil - 1) & (i == nt - 1))
        def _():
            bk = bkt_ref[...]
            rows = lax.broadcasted_iota(jnp.int32, (N_BUCKETS, HEAD_DIM), 0)
            lanes = lax.broadcasted_iota(jnp.int32, (N_BUCKETS, HEAD_DIM), 1)
            out = jnp.zeros((N_BUCKETS, HEAD_DIM), F32)
            for h in range(HEADS):
                acc = dbias_ref[h]
                for b in range(N_BUCKETS):
                    val = jnp.sum(jnp.where(bk == b, acc, 0.0))
                    out = jnp.where((rows == b) & (lanes == h), val, out)
            dtab_ref[...] = out

    tile = lambda part: _bs((qb * BLK, GW), lambda c, i: (c * nt + i, part))
    before = lambda part: _bs((BLK, GW), lambda c, i: (c * nb + jnp.maximum(i * qb - 1, 0), part))
    after = lambda part: _bs((BLK, GW), lambda c, i: (c * nb + jnp.minimum((i + 1) * qb, nb - 1), part))
    vec = _bs((1, HEAD_DIM), lambda c, i: (0, 0))
    tabs = _bs((N_BUCKETS, HEAD_DIM), lambda c, i: (0, 0))
    dzq, dqw, dkw, dtab = pl.pallas_call(
        body, grid=(dil, nt),
        in_specs=[pl.BlockSpec(memory_space=pltpu.SMEM), _bs((BLK, 2 * BLK), lambda c, i: (0, 0)), vec, vec,
                  tile(0), tile(1), tile(2), before(1), before(2), after(0), tile(0), tile(0), tile(0),
                  after(0), after(0), after(0)],
        out_specs=[_bs((qb * BLK, 3 * GW), lambda c, i: (c * nt + i, 0)), vec, vec, tabs],
        out_shape=[S((T, 3 * GW), MXU)] + [S((1, HEAD_DIM), F32)] * 2 + [S((N_BUCKETS, HEAD_DIM), F32)],
        scratch_shapes=[pltpu.VMEM((HEADS, BLK, 2 * BLK), F32), pltpu.VMEM((HEADS, BLK, 2 * BLK), F32)],
        name=f"attn_bwd_g{g}", compiler_params=_cparams("arbitrary", "arbitrary"))(
            table, bkt, qw, kw, zq, zq, zq, zq, zq, zq, d_attn, attn, lse, d_attn, attn, lse)
    return dzq, dqw, dkw, dtab[:, :HEADS]


def _mem_fwd(mem, mem_norm_w, w_mem_kv, xk_w):
    def body(mem_ref, nw_ref, w_ref, xk_ref, mk_ref, mv_ref):
        mn, _ = _rmsn(mem_ref[...], nw_ref[...])
        kv = _dot(mn, w_ref[...])
        for h in range(HEADS):
            sl = slice(h * HEAD_DIM, (h + 1) * HEAD_DIM)
            kn, _ = _rmsn(kv[:, sl], xk_ref[...])
            mk_ref[:, sl] = kn.astype(mk_ref.dtype)
        mv_ref[...] = kv[:, GW:].astype(mv_ref.dtype)

    return pl.pallas_call(body, out_shape=[S((N_MEM, GW), MXU), S((N_MEM, GW), MXU)], name="mem_fwd",
                          compiler_params=_cparams())(mem, mem_norm_w, w_mem_kv, xk_w)


def _mem_bwd(mem, mem_norm_w, w_mem_kv, xk_w, dmk, dmv):
    def body(mem_ref, nw_ref, w_ref, xk_ref, dmk_ref, dmv_ref, dw_ref, dnw_ref, dxk_ref):
        memv = mem_ref[...]
        mn, r = _rmsn(memv, nw_ref[...])
        kv = _dot(mn, w_ref[...])
        dxk = jnp.zeros((1, HEAD_DIM), F32)
        parts = []
        for h in range(HEADS):
            sl = slice(h * HEAD_DIM, (h + 1) * HEAD_DIM)
            kh = kv[:, sl]
            _, rk = _rmsn(kh, xk_ref[...])
            dk, dw = _rmsn_bwd(kh, rk, xk_ref[...], dmk_ref[:, sl])
            dxk += dw
            parts.append(dk)
        dkv = jnp.concatenate(parts + [dmv_ref[...]], axis=1)
        dw_ref[...] = _dot_tn(mn, dkv)
        dmn = _dot_nt(dkv, w_ref[...])
        dnw_ref[...] = jnp.sum(dmn * memv * r, axis=0, keepdims=True)
        dxk_ref[...] = dxk

    return pl.pallas_call(
        body, out_shape=[S((D_MODEL, 2 * GW), F32), S((1, D_MODEL), F32), S((1, HEAD_DIM), F32)], name="mem_bwd",
        compiler_params=_cparams())(mem, mem_norm_w, w_mem_kv, xk_w, dmk, dmv)


def _cross_fwd(z, mk, mv, xq_w):
    T = z.shape[0]
    tm = min(512, T)

    def body(q_ref, mk_ref, mv_ref, w_ref, o_ref):
        for h in range(HEADS):
            sl = slice(h * HEAD_DIM, (h + 1) * HEAD_DIM)
            qn, _ = _rmsn(q_ref[:, sl], w_ref[...])
            s = _dot_nt(qn, mk_ref[:, sl]) * SCALE
            e = jnp.exp(s - jnp.max(s, axis=-1, keepdims=True))
            p = e / jnp.sum(e, axis=-1, keepdims=True)
            o_ref[:, sl] = _dot(p, mv_ref[:, sl]).astype(o_ref.dtype)

    full = _bs((N_MEM, GW), lambda i: (0, 0))
    return pl.pallas_call(
        body, grid=(T // tm,),
        in_specs=[_bs((tm, GW), lambda i: (i, 2)), full, full, _bs((1, HEAD_DIM), lambda i: (0, 0))],
        out_specs=_bs((tm, GW), lambda i: (i, 0)), out_shape=S((T, GW), MXU), name="cross_fwd",
        compiler_params=_cparams("parallel"))(z, mk, mv, xq_w)


def _cross_bwd(z, mk, mv, xq_w, d_cross):
    T = z.shape[0]
    tm = min(512, T)

    def body(q_ref, mk_ref, mv_ref, w_ref, do_ref, dq_ref, dmk_ref, dmv_ref, dw_ref):
        first = pl.program_id(0) == 0
        dw_acc = jnp.zeros((1, HEAD_DIM), F32)
        dmk_parts, dmv_parts = [], []
        for h in range(HEADS):
            sl = slice(h * HEAD_DIM, (h + 1) * HEAD_DIM)
            qh = q_ref[:, sl]
            qn, r = _rmsn(qh, w_ref[...])
            s = _dot_nt(qn, mk_ref[:, sl]) * SCALE
            e = jnp.exp(s - jnp.max(s, axis=-1, keepdims=True))
            p = e / jnp.sum(e, axis=-1, keepdims=True)
            do = do_ref[:, sl]
            dp = _dot_nt(do, mv_ref[:, sl])
            ds = p * (dp - jnp.sum(dp * p, axis=-1, keepdims=True)) * SCALE
            dmv_parts.append(_dot_tn(p, do))
            dmk_parts.append(_dot_tn(ds, qn))
            dq, dw = _rmsn_bwd(qh, r, w_ref[...], _dot(ds, mk_ref[:, sl]))
            dw_acc += dw
            dq_ref[:, sl] = dq.astype(dq_ref.dtype)
        _acc_out(dmk_ref, jnp.concatenate(dmk_parts, axis=1), first)
        _acc_out(dmv_ref, jnp.concatenate(dmv_parts, axis=1), first)
        _acc_out(dw_ref, dw_acc, first)

    full = _bs((N_MEM, GW), lambda i: (0, 0))
    vec = _bs((1, HEAD_DIM), lambda i: (0, 0))
    row = _bs((tm, GW), lambda i: (i, 0))
    return pl.pallas_call(
        body, grid=(T // tm,),
        in_specs=[_bs((tm, GW), lambda i: (i, 2)), full, full, vec, row],
        out_specs=[row, full, full, vec],
        out_shape=[S((T, GW), MXU), S((N_MEM, GW), F32), S((N_MEM, GW), F32), S((1, HEAD_DIM), F32)],
        name="cross_bwd", compiler_params=_cparams("arbitrary"))(z, mk, mv, xq_w, d_cross)


SUBLANES = 8


def _row_windows(ext, first, count, rows, shift_ref=None):
    for b in range(SUBLANES):
        js = [j for j in range(count) if (first + j) % SUBLANES == b]
        if not js:
            continue
        span = max(first + j for j in js) - b + rows
        shifted = ext[b:b + span, :]
        if shift_ref is not None:
            shift_ref[b, 0:span, :] = shifted
        for j in js:
            a = first + j - b
            yield j, (shifted[a:a + rows, :] if shift_ref is None else shift_ref[b, a:a + rows, :])


def _taps(ext, w_ref, width, base, rows, shift_ref=None):
    acc = None
    for k, win in _row_windows(ext, base - (width - 1), width, rows, shift_ref):
        term = win * w_ref[k:k + 1, :]
        acc = term if acc is None else acc + term
    return acc


def _taps_bwd(d_ext, x, w_ref, width, rows, shift_ref=None):
    acc = None
    dw = [None] * width
    for j, win in _row_windows(d_ext, 0, width, rows, shift_ref):
        k = width - 1 - j
        term = win * w_ref[k:k + 1, :]
        acc = term if acc is None else acc + term
        dw[k] = jnp.sum(win * x, axis=0, keepdims=True)
    return acc, jnp.concatenate(dw, axis=0)


def _conv_fwd(z, cw, cb, lw, lb):
    T = z.shape[0]
    tm = min(512, T)
    hb = tm // CONV_HALO

    def body(val_ref, gate_ref, hval_ref, hgate_ref, cw_ref, cb_ref, lw_ref, lb_ref, o_ref, shift_ref):
        i = pl.program_id(0)
        halo = hval_ref[...] * _sigmoid(hgate_ref[...])
        halo = jnp.where(i == 0, 0.0, halo)
        ext = jnp.concatenate([halo, val_ref[...] * _sigmoid(gate_ref[...])], axis=0)
        y = _taps(ext, cw_ref, CONV_WIDTH, CONV_HALO, tm, shift_ref) + cb_ref[...]
        xc = y - jnp.mean(y, axis=-1, keepdims=True)
        a = xc * lax.rsqrt(jnp.mean(xc * xc, axis=-1, keepdims=True) + LN_EPS) * lw_ref[...] + lb_ref[...]
        o_ref[...] = (a * _sigmoid(a)).astype(o_ref.dtype)

    vec = _bs((1, GW), lambda i: (0, 0))
    halo_spec = lambda col: _bs((CONV_HALO, GW), lambda i: (jnp.maximum(i * hb - 1, 0), col))
    return pl.pallas_call(
        body, grid=(T // tm,),
        in_specs=[_bs((tm, GW), lambda i: (i, 0)), _bs((tm, GW), lambda i: (i, 1)), halo_spec(0), halo_spec(1),
                  _bs((CONV_WIDTH, GW), lambda i: (0, 0)), vec, vec, vec],
        out_specs=_bs((tm, GW), lambda i: (i, 0)), out_shape=S((T, GW), MXU),
        scratch_shapes=[pltpu.VMEM((SUBLANES, tm + CONV_HALO, GW), F32)], name="conv_fwd",
        compiler_params=_cparams("parallel"))(z, z, z, z, cw, cb, lw, lb)


def _conv_bwd(z, cw, cb, lw, lb, d_u):
    T = z.shape[0]
    tm = min(512, T)
    hb = tm // CONV_HALO
    nt = T // tm
    H = CONV_HALO

    def body(val_ref, gate_ref, pval_ref, pgate_ref, nval_ref, ngate_ref, du_ref, ndu_ref, cw_ref, cb_ref, lw_ref,
             lb_ref, dval_ref, dgate_ref, dcw_ref, dcb_ref, dlw_ref, dlb_ref, shift_ref):
        i = pl.program_id(0)
        first = i == 0
        val = jnp.concatenate([pval_ref[...] * jnp.where(first, 0.0, 1.0), val_ref[...], nval_ref[...]], axis=0)
        sg = _sigmoid(jnp.concatenate([pgate_ref[...], gate_ref[...], ngate_ref[...]], axis=0))
        u0 = val * sg
        y = _taps(u0, cw_ref, CONV_WIDTH, H, tm + H, shift_ref) + cb_ref[...]
        xc = y - jnp.mean(y, axis=-1, keepdims=True)
        rs = lax.rsqrt(jnp.mean(xc * xc, axis=-1, keepdims=True) + LN_EPS)
        nh = xc * rs
        a = nh * lw_ref[...] + lb_ref[...]
        sa = _sigmoid(a)
        du = jnp.concatenate([du_ref[...], ndu_ref[...] * jnp.where(i == nt - 1, 0.0, 1.0)], axis=0)
        da = du * (sa * (1.0 + a * (1.0 - sa)))
        dn = da * lw_ref[...]
        dy = rs * (dn - jnp.mean(dn, axis=-1, keepdims=True) - nh * jnp.mean(dn * nh, axis=-1, keepdims=True))
        du0, dcw = _taps_bwd(dy, u0[H:H + tm], cw_ref, CONV_WIDTH, tm, shift_ref)
        v0, s0 = val[H:H + tm], sg[H:H + tm]
        dval_ref[...] = (du0 * s0).astype(dval_ref.dtype)
        dgate_ref[...] = (du0 * v0 * s0 * (1.0 - s0)).astype(dgate_ref.dtype)
        dy0 = dy[:tm]
        _acc_out(dcw_ref, dcw, first)
        _acc_out(dcb_ref, jnp.sum(dy0, axis=0, keepdims=True), first)
        _acc_out(dlw_ref, jnp.sum(da[:tm] * nh[:tm], axis=0, keepdims=True), first)
        _acc_out(dlb_ref, jnp.sum(da[:tm], axis=0, keepdims=True), first)

    vec = _bs((1, GW), lambda i: (0, 0))
    cwspec = _bs((CONV_WIDTH, GW), lambda i: (0, 0))
    prev = lambda col: _bs((H, GW), lambda i: (jnp.maximum(i * hb - 1, 0), col))
    nxt = lambda col: _bs((H, GW), lambda i: (jnp.minimum((i + 1) * hb, nt * hb - 1), col))
    row = _bs((tm, GW), lambda i: (i, 0))
    return pl.pallas_call(
        body, grid=(nt,),
        in_specs=[_bs((tm, GW), lambda i: (i, 0)), _bs((tm, GW), lambda i: (i, 1)), prev(0), prev(1), nxt(0), nxt(1),
                  row, nxt(0), cwspec, vec, vec, vec],
        out_specs=[row, row, cwspec, vec, vec, vec],
        out_shape=[S((T, GW), MXU), S((T, GW), MXU), S((CONV_WIDTH, GW), F32)] + [S((1, GW), F32)] * 3,
        scratch_shapes=[pltpu.VMEM((SUBLANES, tm + 2 * H, GW), F32)], name="conv_bwd", compiler_params=_cparams("arbitrary"))(z, z, z, z, z, z, d_u, d_u, cw, cb, lw, lb)


def _ffn_act_fwd(up0, fw, fb):
    T = up0.shape[0]
    tm = min(256, T)
    hb = tm // FFN_HALO
    H = FFN_HALO

    def body(a_ref, g_ref, pa_ref, pg_ref, wa_ref, wg_ref, ba_ref, bg_ref, o_ref, up_ref):
        i = pl.program_id(0)
        keep = jnp.where(i == 0, 0.0, 1.0)
        ea = jnp.concatenate([pa_ref[...] * keep, a_ref[...]], axis=0)
        eg = jnp.concatenate([pg_ref[...] * keep, g_ref[...]], axis=0)
        av = _taps(ea, wa_ref, FFN_CONV_WIDTH, H, tm) + ba_ref[...]
        gv = _taps(eg, wg_ref, FFN_CONV_WIDTH, H, tm) + bg_ref[...]
        o_ref[...] = (gv * _sigmoid(gv) * av).astype(o_ref.dtype)
        up_ref[:, :D_FF] = av
        up_ref[:, D_FF:] = gv

    col = lambda j: _bs((tm, D_FF), lambda i: (i, j))
    prev = lambda j: _bs((H, D_FF), lambda i: (jnp.maximum(i * hb - 1, 0), j))
    wspec = lambda j: _bs((FFN_CONV_WIDTH, D_FF), lambda i: (0, j))
    bspec = lambda j: _bs((1, D_FF), lambda i: (0, j))
    return pl.pallas_call(
        body, grid=(T // tm,),
        in_specs=[col(0), col(1), prev(0), prev(1), wspec(0), wspec(1), bspec(0), bspec(1)],
        out_specs=[_bs((tm, D_FF), lambda i: (i, 0)), _bs((tm, 2 * D_FF), lambda i: (i, 0))],
        out_shape=[S((T, D_FF), MXU), S((T, 2 * D_FF), F32)], name="ffn_act_fwd",
        compiler_params=_cparams("parallel"))(up0, up0, up0, up0, fw, fw, fb, fb)


def _ffn_act_bwd(up0, up, fw, d_f):
    T = up0.shape[0]
    tm = min(128, T)
    hb = tm // FFN_HALO
    nt = T // tm
    H = FFN_HALO
    W = FFN_CONV_WIDTH

    def body(a_ref, g_ref, av_ref, gv_ref, nav_ref, ngv_ref, df_ref, ndf_ref, wa_ref, wg_ref, dup_ref, dw_ref, db_ref):
        i = pl.program_id(0)
        first = i == 0
        av = jnp.concatenate([av_ref[...], nav_ref[...]], axis=0)
        gv = jnp.concatenate([gv_ref[...], ngv_ref[...]], axis=0)
        df = jnp.concatenate([df_ref[...], ndf_ref[...] * jnp.where(i == nt - 1, 0.0, 1.0)], axis=0)
        sg = _sigmoid(gv)
        d_av = df * gv * sg
        d_gv = df * av * (sg * (1.0 + gv * (1.0 - sg)))
        dua, dwa = _taps_bwd(d_av, a_ref[...], wa_ref, W, tm)
        dug, dwg = _taps_bwd(d_gv, g_ref[...], wg_ref, W, tm)
        dup_ref[:, :D_FF] = dua.astype(dup_ref.dtype)
        dup_ref[:, D_FF:] = dug.astype(dup_ref.dtype)
        dw = jnp.concatenate([dwa, dwg], axis=1)
        db = jnp.concatenate([jnp.sum(d_av[:tm], axis=0, keepdims=True), jnp.sum(d_gv[:tm], axis=0, keepdims=True)], axis=1)
        _acc_out(dw_ref, dw, first)
        _acc_out(db_ref, db, first)

    col = lambda j: _bs((tm, D_FF), lambda i: (i, j))
    nxt = lambda j: _bs((H, D_FF), lambda i: (jnp.minimum((i + 1) * hb, nt * hb - 1), j))
    wspec = lambda j: _bs((W, D_FF), lambda i: (0, j))
    return pl.pallas_call(
        body, grid=(nt,),
        in_specs=[col(0), col(1), col(0), col(1), nxt(0), nxt(1), col(0), nxt(0), wspec(0), wspec(1)],
        out_specs=[_bs((tm, 2 * D_FF), lambda i: (i, 0)), _bs((W, 2 * D_FF), lambda i: (0, 0)),
                   _bs((1, 2 * D_FF), lambda i: (0, 0))],
        out_shape=[S((T, 2 * D_FF), MXU), S((W, 2 * D_FF), F32), S((1, 2 * D_FF), F32)],
        name="ffn_act_bwd", compiler_params=_cparams("arbitrary"))(
            up0, up0, up, up, up, up, d_f, d_f, fw, fw)


def _branch_fwd(attn, u, cross, z, b_gate, wa, wc, wx):
    T = z.shape[0]
    tm = min(512, T)

    def body(a_ref, u_ref, x_ref, g0_ref, g1_ref, g2_ref, b_ref, wa_ref, wc_ref, wx_ref, o_ref):
        acc = None
        for j, (act, g_ref, w_ref) in enumerate(((a_ref, g0_ref, wa_ref), (u_ref, g1_ref, wc_ref), (x_ref, g2_ref, wx_ref))):
            gate = _sigmoid(g_ref[...] + b_ref[:, j * D_MODEL:(j + 1) * D_MODEL])
            term = gate * _dot(act[...], w_ref[...])
            acc = term if acc is None else acc + term
        o_ref[...] = acc.astype(o_ref.dtype)

    act = _bs((tm, GW), lambda i: (i, 0))
    gcol = lambda j: _bs((tm, D_MODEL), lambda i: (i, j))
    wfull = _bs((GW, D_MODEL), lambda i: (0, 0))
    return pl.pallas_call(
        body, grid=(T // tm,),
        in_specs=[act, act, act, gcol(0), gcol(1), gcol(2), _bs((1, 3 * D_MODEL), lambda i: (0, 0)), wfull, wfull, wfull],
        out_specs=_bs((tm, D_MODEL), lambda i: (i, 0)), out_shape=S((T, D_MODEL), MXU), name="branch_fwd",
        compiler_params=_cparams("parallel"))(attn, u, cross, z, z, z, b_gate, wa, wc, wx)


def _branch_bwd(d_merged, attn, u, cross, z, b_gate, wa, wc, wx, dils):
    T = z.shape[0]
    tm = min(512, T)
    d1, d2 = dils[1], dils[2]

    def body(dm_ref, a_ref, u_ref, x_ref, g0_ref, g1_ref, g2_ref, b_ref, wa_ref, wc_ref, wx_ref,
             dzg_ref, da_ref, du_ref, dx_ref, dwa_ref, dwc_ref, dwx_ref, db_ref, da1_ref, da2_ref, slab_ref):
        first = pl.program_id(0) == 0
        dm = dm_ref[...]
        dbs = []
        for j, (act, g_ref, w_ref, dact_ref, dw_ref) in enumerate((
                (a_ref, g0_ref, wa_ref, da_ref, dwa_ref), (u_ref, g1_ref, wc_ref, du_ref, dwc_ref),
                (x_ref, g2_ref, wx_ref, dx_ref, dwx_ref))):
            av = act[...]
            gate = _sigmoid(g_ref[...] + b_ref[:, j * D_MODEL:(j + 1) * D_MODEL])
            y = _dot(av, w_ref[...])
            dzg = dm * y * gate * (1.0 - gate)
            dzg_ref[:, j * D_MODEL:(j + 1) * D_MODEL] = dzg.astype(dzg_ref.dtype)
            dbs.append(jnp.sum(dzg, axis=0, keepdims=True))
            dy = (gate * dm).astype(MXU)
            dact = _dot_nt(dy, w_ref[...])
            dact_ref[...] = dact
            if j == 0:
                for h in range(HEADS):
                    slab_ref[h] = dact[:, h * HEAD_DIM:(h + 1) * HEAD_DIM]
                _tokens_to_classes(slab_ref, da1_ref, d1, tm)
                _tokens_to_classes(slab_ref, da2_ref, d2, tm)
            _acc_out(dw_ref, _dot_tn(av, dy), first)
        _acc_out(db_ref, jnp.concatenate(dbs, axis=1), first)

    act = _bs((tm, GW), lambda i: (i, 0))
    gcol = lambda j: _bs((tm, D_MODEL), lambda i: (i, j))
    wfull = _bs((GW, D_MODEL), lambda i: (0, 0))
    bvec = _bs((1, 3 * D_MODEL), lambda i: (0, 0))
    cls = lambda d: _bs((d, tm // d, GW), lambda i: (0, i, 0))
    outs = pl.pallas_call(
        body, grid=(T // tm,),
        in_specs=[_bs((tm, D_MODEL), lambda i: (i, 0)), act, act, act, gcol(0), gcol(1), gcol(2), bvec, wfull, wfull, wfull],
        out_specs=[_bs((tm, 3 * D_MODEL), lambda i: (i, 0)), act, act, act, wfull, wfull, wfull, bvec, cls(d1), cls(d2)],
        out_shape=[S((T, 3 * D_MODEL), MXU)] + [S((T, GW), F32)] * 3 + [S((GW, D_MODEL), F32)] * 3 + [S((1, 3 * D_MODEL), F32)]
        + [S((d1, T // d1, GW), F32), S((d2, T // d2, GW), F32)],
        scratch_shapes=[pltpu.VMEM((HEADS, tm, HEAD_DIM), F32)],
        name="branch_bwd", compiler_params=_cparams("arbitrary"))(d_merged, attn, u, cross, z, z, z, b_gate, wa, wc, wx)
    d_zg, d_attn, d_u, d_cross, dwa, dwc, dwx, db, da1, da2 = outs
    return d_zg, [d_attn, da1.reshape(T, GW), da2.reshape(T, GW)], d_u, d_cross, dwa, dwc, dwx, db


def _loss_head(y, target):
    T, D = y.shape
    tm = min(512, T)

    def body(y_ref, t_ref, dy_ref, l_ref):
        e = y_ref[...] - t_ref[...]
        dy_ref[...] = e * (1.0 / D)
        part = jnp.full((8, 128), jnp.sum(e * e), F32)
        _acc_out(l_ref, part, pl.program_id(0) == 0)

    row = _bs((tm, D), lambda i: (i, 0))
    return pl.pallas_call(
        body, grid=(T // tm,), in_specs=[row, row], out_specs=[row, _bs((8, 128), lambda i: (0, 0))],
        out_shape=[S((T, D), F32), S((8, 128), F32)], name="loss_head", compiler_params=_cparams("arbitrary"))(y, target)


def _peer(mask):
    x, y, c = lax.axis_index("x"), lax.axis_index("y"), lax.axis_index("c")
    px = 1 - x if mask & 4 else x
    py = 1 - y if mask & 2 else y
    pc = 1 - c if mask & 1 else c
    return (px, py, pc), 4 * px + 2 * py + pc


def _exchange(arrs, scatter, name):
    n = len(arrs)
    outs_shape = [S(a.shape if scatter else (N_DEV,) + a.shape, a.dtype) for a in arrs]

    def body(*refs):
        ins, outs = refs[:n], refs[n:2 * n]
        send_sems, recv_sems, local_sems = refs[2 * n:]
        me = 4 * lax.axis_index("x") + 2 * lax.axis_index("y") + lax.axis_index("c")
        copies = []
        for w in range(n):
            src = ins[w].at[me] if scatter else ins[w]
            cp = pltpu.make_async_copy(src, outs[w].at[me], local_sems.at[w])
            cp.start()
            copies.append(cp)
        for k in range(1, N_DEV):
            peer, pidx = _peer(k)
            for w in range(n):
                src = ins[w].at[pidx] if scatter else ins[w]
                cp = pltpu.make_async_remote_copy(
                    src_ref=src, dst_ref=outs[w].at[me], send_sem=send_sems.at[w, k - 1], recv_sem=recv_sems.at[w, k - 1],
                    device_id=peer, device_id_type=pl.DeviceIdType.MESH)
                cp.start()
                copies.append(cp)
        for cp in copies:
            cp.wait()

    hbm = pl.BlockSpec(memory_space=pl.ANY)
    return pl.pallas_call(
        body, in_specs=[hbm] * n, out_specs=[hbm] * n, out_shape=outs_shape,
        scratch_shapes=[pltpu.SemaphoreType.DMA((n, N_DEV - 1)), pltpu.SemaphoreType.DMA((n, N_DEV - 1)),
                        pltpu.SemaphoreType.DMA((n,))],
        name=name)(*arrs)


def _exchange_copies(ins, lands, send_sems, recv_sems, local_sems, scatter):
    n = len(ins)
    me = 4 * lax.axis_index("x") + 2 * lax.axis_index("y") + lax.axis_index("c")
    copies = []
    for w in range(n):
        src = ins[w].at[me] if scatter else ins[w]
        copies.append(pltpu.make_async_copy(src, lands[w].at[me], local_sems.at[w]))
    for k in range(1, N_DEV):
        peer, pidx = _peer(k)
        for w in range(n):
            src = ins[w].at[pidx] if scatter else ins[w]
            copies.append(pltpu.make_async_remote_copy(
                src_ref=src, dst_ref=lands[w].at[me], send_sem=send_sems.at[w * (N_DEV - 1) + k - 1],
                recv_sem=recv_sems.at[w * (N_DEV - 1) + k - 1],
                device_id=peer, device_id_type=pl.DeviceIdType.MESH))
    return copies


_HBM_SPEC = pl.BlockSpec(memory_space=pltpu.HBM)
_SEM_SPEC = pl.BlockSpec(memory_space=pltpu.SEMAPHORE)
_DATAFLOW = pltpu.SideEffectType.DATAFLOW_SIDE_EFFECTING


def _exchange_start(arrs, scatter, name):
    n = len(arrs)
    land_shapes = [a.shape if scatter else (N_DEV,) + a.shape for a in arrs]

    def body(*refs):
        ins, lands = refs[:n], refs[n:2 * n]
        send_sems, recv_sems, local_sems = refs[2 * n:2 * n + 3]
        token = refs[-1]
        for cp in _exchange_copies(ins, lands, send_sems, recv_sems, local_sems, scatter):
            cp.start()
        token[...] = jnp.zeros_like(token)

    out_shape = ([pltpu.SemaphoreType.DMA((n * (N_DEV - 1),)), pltpu.SemaphoreType.DMA((n * (N_DEV - 1),)),
                  pltpu.SemaphoreType.DMA((n,))]
                 + [pltpu.HBM(a.shape, a.dtype) for a in arrs]
                 + [pltpu.HBM(s, a.dtype) for s, a in zip(land_shapes, arrs)]
                 + [S((8, 128), F32)])
    args = ([pltpu.with_memory_space_constraint(a, pltpu.HBM) for a in arrs]
            + [pltpu.with_memory_space_constraint(lax.empty(s, a.dtype), pltpu.HBM) for s, a in zip(land_shapes, arrs)])
    outs = pl.pallas_call(
        body, in_specs=[_HBM_SPEC] * (2 * n),
        out_specs=[_SEM_SPEC] * 3 + [_HBM_SPEC] * (2 * n) + [pl.BlockSpec(memory_space=pltpu.VMEM)],
        out_shape=out_shape, input_output_aliases={j: 3 + j for j in range(2 * n)},
        name=name, compiler_params=pltpu.CompilerParams(has_side_effects=_DATAFLOW))(*args)
    return (n, scatter, outs[:3], outs[3:3 + n], outs[3 + n:3 + 2 * n]), outs[-1]


def _exchange_wait(state, after, name):
    n, scatter, sems, ins, lands = state

    def body(*refs):
        ins_r, lands_r = refs[:n], refs[n:2 * n]
        send_sems, recv_sems, local_sems = refs[2 * n:2 * n + 3]
        for cp in _exchange_copies(ins_r, lands_r, send_sems, recv_sems, local_sems, scatter):
            cp.wait()

    outs = pl.pallas_call(
        body, in_specs=[_HBM_SPEC] * (2 * n) + [_SEM_SPEC] * 3 + [pl.BlockSpec(memory_space=pl.ANY)],
        out_specs=[_HBM_SPEC] * (2 * n),
        out_shape=[pltpu.HBM(a.shape, a.dtype) for a in ins] + [pltpu.HBM(a.shape, a.dtype) for a in lands],
        input_output_aliases={j: j for j in range(2 * n)},
        name=name, compiler_params=pltpu.CompilerParams(has_side_effects=_DATAFLOW))(*ins, *lands, *sems, after)
    return list(outs[n:])


def _adamw(w, m, v, parts, name):
    R, C = w.shape
    P = parts.shape[0]
    tr = _pick(R, tuple(t for t in (256, 176, 128, 64, 32, 16, 8) if P * t * C * 4 <= ADAMW_BLOCK_BYTES))
    c1 = 1.0 / (1.0 - ADAM_B1 ** ADAM_STEP)
    c2 = 1.0 / (1.0 - ADAM_B2 ** ADAM_STEP)

    def body(w_ref, m_ref, v_ref, p_ref, g_ref, d_ref, nm_ref, nv_ref):
        g = p_ref[0].astype(F32)
        for j in range(1, P):
            g = g + p_ref[j].astype(F32)
        m2 = ADAM_B1 * m_ref[...] + (1.0 - ADAM_B1) * g
        v2 = ADAM_B2 * v_ref[...] + (1.0 - ADAM_B2) * (g * g)
        g_ref[...] = g
        nm_ref[...] = m2
        nv_ref[...] = v2
        d_ref[...] = -ADAM_LR * ((m2 * c1) / (jnp.sqrt(v2 * c2) + ADAM_EPS) + ADAM_WD * w_ref[...])

    row = _bs((tr, C), lambda i: (i, 0))
    return pl.pallas_call(
        body, grid=(R // tr,), in_specs=[row, row, row, _bs((P, tr, C), lambda i: (0, i, 0))], out_specs=[row] * 4,
        out_shape=[S((R, C), F32)] * 4, name=name, compiler_params=_cparams("parallel"))(w, m, v, parts)


def _sum_parts(parts, name):
    P, R, C = parts.shape

    def body(p_ref, o_ref):
        g = p_ref[0]
        for j in range(1, P):
            g = g + p_ref[j]
        o_ref[...] = g

    return pl.pallas_call(body, out_shape=S((R, C), F32), name=name, compiler_params=_cparams())(parts)


def _pack(arrs):
    flat = jnp.concatenate([a.reshape(-1) for a in arrs])
    rows = -(-flat.shape[0] // 1024) * 8
    return jnp.pad(flat, (0, rows * 128 - flat.shape[0])).reshape(rows, 128)


def _unpack(packed, shapes):
    flat = packed.reshape(-1)
    out, off = [], 0
    for s in shapes:
        n = int(np.prod(s))
        out.append(flat[off:off + n].reshape(s))
        off += n
    return out


def _behind(a, token):
    return a if token is None else a + token[0, 0]


def _local_step(x, mem, target, p, comm=None):
    table = p["rel_bias_table"]
    dils = [dil for _, dil in ATTN_GROUPS]
    xn, xn_classes = _rms_fwd_classes(x, p["attn_norm_w"], dils[1:], "attn_norm_fwd")
    xn_c = [xn] + xn_classes
    w_in = p["w_in"]
    qkv_w = 3 * N_GROUPS * GW
    wq = [jnp.concatenate([w_in[:, (N_GROUPS * part + g) * GW:(N_GROUPS * part + g + 1) * GW] for part in range(3)], axis=1)
          for g in range(N_GROUPS)]
    wc = w_in[:, qkv_w:qkv_w + 3 * GW]
    wg = w_in[:, qkv_w + 3 * GW:]
    zq = [_matmul(xn_c[g], wq[g], name=f"mm_in_qkv{g}") for g in range(N_GROUPS)]
    zc = _matmul(xn, wc, name="mm_in_c")
    zg = _matmul(xn, wg, name="mm_in_g")
    os_, lses = [], []
    for g, dil in enumerate(dils):
        o, l = _attn_fwd(zq[g], table, p["q_norm_w"][g:g + 1], p["k_norm_w"][g:g + 1], g, dil)
        os_.append(o)
        lses.append(l)
    attn, lse, attn_c, lse_c = _attn_merge(os_, lses, dils)
    u = _conv_fwd(zc, p["conv_dw_w"], p["conv_dw_b"], p["conv_ln_w"], p["conv_ln_b"])
    if comm is not None:
        p = {**p, **comm.late_weights(after=u)}
    mk, mv = _mem_fwd(mem, p["mem_norm_w"], p["w_mem_kv"], p["xk_norm_w"])
    cross = _cross_fwd(zc, mk, mv, p["xq_norm_w"])
    merged = _branch_fwd(attn, u, cross, zg, p["b_gate"], p["w_attn_o"], p["w_conv_o"], p["w_cross_o"])
    h1 = _matmul(merged, p["w_out"], residual=x, name="mm_out")
    hn = _rms_fwd(h1, p["ffn_norm_w"], "ffn_norm_fwd")
    if comm is not None:
        p = {**p, **comm.ffn_weights(after=hn)}
    up0 = _matmul(hn, p["w_up"], name="mm_up")
    f, up = _ffn_act_fwd(up0, p["ffn_conv_w"], p["ffn_conv_b"])
    h2 = _matmul(f, p["w_down"], residual=h1, name="mm_down")
    dh2, lsum = _loss_head(h2, target)
    g = {}
    d_f = _matmul(dh2, p["w_down"], tb=True, name="mm_down_dx")
    g["w_down"] = _matmul(f, dh2, ta=True, name="mm_down_dw")
    d_up0, g["ffn_conv_w"], g["ffn_conv_b"] = _ffn_act_bwd(up0, up, p["ffn_conv_w"], d_f)
    dhn = _matmul(d_up0, p["w_up"], tb=True, name="mm_up_dx")
    g["w_up"] = _matmul(hn, d_up0, ta=True, name="mm_up_dw")
    dh1, g["ffn_norm_w"] = _rms_bwd(h1, p["ffn_norm_w"], [dhn], dh2, "ffn_norm_bwd")
    d_merged = _matmul(dh1, p["w_out"], tb=True, name="mm_out_dx")
    g["w_out"] = _matmul(merged, dh1, ta=True, name="mm_out_dw")
    (d_zg, d_attn_c, d_u, d_cross, g["w_attn_o"], g["w_conv_o"], g["w_cross_o"], g["b_gate"]) = _branch_bwd(
        d_merged, attn, u, cross, zg, p["b_gate"], p["w_attn_o"], p["w_conv_o"], p["w_cross_o"], dils)
    d_xq, dmk, dmv, g["xq_norm_w"] = _cross_bwd(zc, mk, mv, p["xq_norm_w"], d_cross)
    g["w_mem_kv"], g["mem_norm_w"], g["xk_norm_w"] = _mem_bwd(mem, p["mem_norm_w"], p["w_mem_kv"], p["xk_norm_w"], dmk, dmv)
    tok = comm.start_early_grads(g) if comm is not None else None
    d_val, d_gate, g["conv_dw_w"], g["conv_dw_b"], g["conv_ln_w"], g["conv_ln_b"] = _conv_bwd(
        zc, p["conv_dw_w"], _behind(p["conv_dw_b"], tok), p["conv_ln_w"], p["conv_ln_b"], d_u)
    dzq, dqw, dkw, dtab = [], [], [], []
    for gi, dil in enumerate(dils):
        r = _attn_bwd(zq[gi], table, p["q_norm_w"][gi:gi + 1], p["k_norm_w"][gi:gi + 1], d_attn_c[gi], attn_c[gi],
                      lse_c[gi], gi, dil)
        for lst, val in zip((dzq, dqw, dkw, dtab), r):
            lst.append(val)
    g["q_norm_w"] = jnp.concatenate(dqw, axis=0)
    g["k_norm_w"] = jnp.concatenate(dkw, axis=0)
    g["rel_bias_table"] = jnp.concatenate(dtab, axis=1)
    d_zc = jnp.concatenate([d_val, d_gate, d_xq], axis=1)
    gq = [_matmul(xn_c[gi], dzq[gi], ta=True, name=f"mm_in_qkv{gi}_dw") for gi in range(N_GROUPS)]
    gc = _matmul(xn, d_zc, ta=True, name="mm_in_c_dw")
    gg = _matmul(xn, d_zg, ta=True, name="mm_in_g_dw")
    g["w_in"] = jnp.concatenate(
        [gq[gi][:, part * GW:(part + 1) * GW] for part in range(3) for gi in range(N_GROUPS)] + [gc, gg], axis=1)
    tok = comm.start_w_in_grad(g["w_in"]) if comm is not None else None
    dxn = _matmul(d_zg, wg, tb=True, after=tok, name="mm_in_g_dx")
    dxn = _matmul(d_zc, wc, tb=True, residual=dxn, name="mm_in_c_dx")
    dxn = _matmul(dzq[0], wq[0], tb=True, residual=dxn, name="mm_in_qkv0_dx")
    dx_classes = [(_matmul(dzq[gi], wq[gi], tb=True, name=f"mm_in_qkv{gi}_dx"), dils[gi]) for gi in range(1, N_GROUPS)]
    grad_x, g["attn_norm_w"] = _rms_bwd(x, p["attn_norm_w"], [dxn], dh1, "attn_norm_bwd", class_dys=dx_classes)
    return lsum[0, 0], grad_x, g


WEIGHT_NAMES = ["rel_bias_table", "attn_norm_w", "w_in", "b_gate", "q_norm_w", "k_norm_w", "w_attn_o", "conv_dw_w",
                "conv_dw_b", "conv_ln_w", "conv_ln_b", "w_conv_o", "mem_norm_w", "w_mem_kv", "xq_norm_w", "xk_norm_w",
                "w_cross_o", "w_out", "ffn_norm_w", "w_up", "ffn_conv_w", "ffn_conv_b", "w_down"]
COL_SHARDED = ("w_in", "w_attn_o", "w_conv_o", "w_cross_o", "w_up")
ROW_SHARDED = ("w_mem_kv", "w_out", "w_down")
SMALL_COL_SHARDED = ("conv_dw_w", "ffn_conv_w")
BIG = COL_SHARDED + ROW_SHARDED


def _cols_to_blocks(a):
    k, n8 = a.shape
    return a.reshape(k, N_DEV, n8 // N_DEV).transpose(1, 0, 2)


def _blocks_to_cols(a):
    return a.transpose(1, 0, 2).reshape(a.shape[1], N_DEV * a.shape[2])


def _step(x, mem, target, w, m, v):
    me = 4 * lax.axis_index("x") + 2 * lax.axis_index("y") + lax.axis_index("c")

    def to_full(n, blocks):
        return _blocks_to_cols(blocks) if n in COL_SHARDED + SMALL_COL_SHARDED else blocks.reshape(-1, blocks.shape[-1])

    def to_blocks(n, grad):
        blocks = _cols_to_blocks(grad) if n in COL_SHARDED else grad.reshape(N_DEV, -1, grad.shape[-1])
        return blocks.astype(MXU)

    first = ("w_in",) + SMALL_COL_SHARDED
    late = tuple(n for n in BIG if n != "w_in")
    cast = lambda n: w[n].astype(MXU) if n in BIG else w[n]
    first_state, _ = _exchange_start([cast(n) for n in first], False, "gather_first_start")
    ffn = ("w_up", "w_down")
    mixers = tuple(n for n in late if n not in ffn)
    late_state, _ = _exchange_start([cast(n) for n in mixers], False, "gather_late_start")
    ffn_state, late_token = _exchange_start([cast(n) for n in ffn], False, "gather_ffn_start")
    got = _exchange_wait(first_state, late_token, "gather_first_wait")
    p = {n: w[n] for n in WEIGHT_NAMES if n not in BIG + SMALL_COL_SHARDED}
    p.update({n: to_full(n, b) for n, b in zip(first, got)})

    class Comm:
        def late_weights(self, after):
            return {n: to_full(n, b) for n, b in zip(mixers, _exchange_wait(late_state, after, "gather_late_wait"))}

        def ffn_weights(self, after):
            return {n: to_full(n, b) for n, b in zip(ffn, _exchange_wait(ffn_state, after, "gather_ffn_wait"))}

        def start_early_grads(self, g):
            self.early_state, token = _exchange_start([to_blocks(n, g[n]) for n in late], True, "scatter_early_start")
            return token

        def start_w_in_grad(self, grad):
            self.w_in_state, token = _exchange_start([to_blocks("w_in", grad)], True, "scatter_w_in_start")
            return token

    comm = Comm()
    lsum, grad_x, g = _local_step(x, mem, target, p, comm)
    small_names = [n for n in WEIGHT_NAMES if n not in BIG]
    small_shapes = [g[n].shape for n in small_names]
    small_parts = _exchange([_pack([g[n] for n in small_names])], False, "gather_small_grads")[0]
    gsmall = dict(zip(small_names, _unpack(_sum_parts(small_parts, "sum_small_grads"), small_shapes)))
    for n in SMALL_COL_SHARDED:
        width = w[n].shape[-1]
        gsmall[n] = lax.dynamic_slice_in_dim(gsmall[n], me * width, width, axis=1)
    res = {}
    parts = dict(zip(late, _exchange_wait(comm.early_state, grad_x, "scatter_early_wait")))
    for n in late:
        res[n] = _adamw(w[n], m[n], v[n], parts[n], "adamw_" + n)
    w_in_parts = _exchange_wait(comm.w_in_state, res[late[-1]][1], "scatter_w_in_wait")[0]
    res["w_in"] = _adamw(w["w_in"], m["w_in"], v["w_in"], w_in_parts, "adamw_w_in")
    shapes = [w[n].shape for n in small_names]
    packed = [_pack([d[n] for n in small_names]) for d in (w, m, v, gsmall)]
    outs = _adamw(packed[0], packed[1], packed[2], packed[3][None], "adamw_small")
    unpacked = [_unpack(o, shapes) for o in outs]
    for j, n in enumerate(small_names):
        res[n] = tuple(unpacked[q][j] for q in range(4))
    return lsum, grad_x, res


def kernel(x, mem, rel_bias_table, attn_norm_w, w_in, b_gate, q_norm_w, k_norm_w, w_attn_o, conv_dw_w, conv_dw_b, conv_ln_w, conv_ln_b, w_conv_o, mem_norm_w, w_mem_kv, xq_norm_w, xk_norm_w, w_cross_o, w_out, ffn_norm_w, w_up, ffn_conv_w, ffn_conv_b, w_down, loss_target, m_rel_bias_table, m_attn_norm_w, m_w_in, m_b_gate, m_q_norm_w, m_k_norm_w, m_w_attn_o, m_conv_dw_w, m_conv_dw_b, m_conv_ln_w, m_conv_ln_b, m_w_conv_o, m_mem_norm_w, m_w_mem_kv, m_xq_norm_w, m_xk_norm_w, m_w_cross_o, m_w_out, m_ffn_norm_w, m_w_up, m_ffn_conv_w, m_ffn_conv_b, m_w_down, v_rel_bias_table, v_attn_norm_w, v_w_in, v_b_gate, v_q_norm_w, v_k_norm_w, v_w_attn_o, v_conv_dw_w, v_conv_dw_b, v_conv_ln_w, v_conv_ln_b, v_w_conv_o, v_mem_norm_w, v_w_mem_kv, v_xq_norm_w, v_xk_norm_w, v_w_cross_o, v_w_out, v_ffn_norm_w, v_w_up, v_ffn_conv_w, v_ffn_conv_b, v_w_down):
    ws = dict(zip(WEIGHT_NAMES, (rel_bias_table, attn_norm_w, w_in, b_gate, q_norm_w, k_norm_w, w_attn_o, conv_dw_w, conv_dw_b, conv_ln_w, conv_ln_b, w_conv_o, mem_norm_w, w_mem_kv, xq_norm_w, xk_norm_w, w_cross_o, w_out, ffn_norm_w, w_up, ffn_conv_w, ffn_conv_b, w_down)))
    ms = dict(zip(WEIGHT_NAMES, (m_rel_bias_table, m_attn_norm_w, m_w_in, m_b_gate, m_q_norm_w, m_k_norm_w, m_w_attn_o, m_conv_dw_w, m_conv_dw_b, m_conv_ln_w, m_conv_ln_b, m_w_conv_o, m_mem_norm_w, m_w_mem_kv, m_xq_norm_w, m_xk_norm_w, m_w_cross_o, m_w_out, m_ffn_norm_w, m_w_up, m_ffn_conv_w, m_ffn_conv_b, m_w_down)))
    vs = dict(zip(WEIGHT_NAMES, (v_rel_bias_table, v_attn_norm_w, v_w_in, v_b_gate, v_q_norm_w, v_k_norm_w, v_w_attn_o, v_conv_dw_w, v_conv_dw_b, v_conv_ln_w, v_conv_ln_b, v_w_conv_o, v_mem_norm_w, v_w_mem_kv, v_xq_norm_w, v_xk_norm_w, v_w_cross_o, v_w_out, v_ffn_norm_w, v_w_up, v_ffn_conv_w, v_ffn_conv_b, v_w_down)))
    full_shapes = {n: ws[n].shape for n in WEIGHT_NAMES}

    def squeeze(d):
        return {n: (a if n == "rel_bias_table" else a[0]) for n, a in d.items()}

    w, m, v = squeeze(ws), squeeze(ms), squeeze(vs)
    for d in (w, m, v):
        for n in WEIGHT_NAMES:
            if d[n].ndim == 1:
                d[n] = d[n][None]
    lsum, grad_x, res = _step(x[0], mem[0], loss_target[0], w, m, v)
    loss = lax.psum(0.5 / D_MODEL * lsum, ("x", "y", "c"))
    outs = [loss, grad_x[None]]
    for q in range(4):
        outs += [res[n][q].reshape(full_shapes[n]) for n in WEIGHT_NAMES]
    return tuple(outs)
```

```python
import functools
import math

import numpy as np
import jax
import jax.numpy as jnp
from jax import lax
from jax.experimental import pallas as pl
from jax.experimental.pallas import tpu as pltpu

F32 = jnp.float32
MXU = jnp.bfloat16
S = jax.ShapeDtypeStruct

D_MODEL = 1024
HEAD_DIM = 128
ATTN_GROUPS = ((128, 1), (512, 4), (2048, 16))
N_GROUPS = 3
HEADS = 4
GW = HEADS * HEAD_DIM
CONV_WIDTH = 31
N_MEM = 256
D_FF = 2816
FFN_CONV_WIDTH = 3
N_BUCKETS = 32
MAX_DISTANCE = 2048
RMS_EPS = 1e-6
LN_EPS = 1e-5
BLK = 128
SCALE = HEAD_DIM ** -0.5
NEG = -1e30
N_DEV = 8

ADAM_LR, ADAM_B1, ADAM_B2, ADAM_EPS, ADAM_WD, ADAM_STEP = 0.001, 0.9, 0.999, 1e-08, 0.01, 10

VMEM_LIMIT = 48 * 1024 * 1024
CONV_HALO = 32
FFN_HALO = 8
ADAMW_BLOCK_BYTES = 4 * 1024 * 1024


def _cparams(*sem):
    return pltpu.CompilerParams(dimension_semantics=sem or None, vmem_limit_bytes=VMEM_LIMIT)


def _bs(shape, imap):
    return pl.BlockSpec(shape, imap)


def _dot(a, b):
    return lax.dot_general(a.astype(MXU), b.astype(MXU), (((1,), (0,)), ((), ())), preferred_element_type=F32)


def _dot_nt(a, b):
    return lax.dot_general(a.astype(MXU), b.astype(MXU), (((1,), (1,)), ((), ())), preferred_element_type=F32)


def _dot_tn(a, b):
    return lax.dot_general(a.astype(MXU), b.astype(MXU), (((0,), (0,)), ((), ())), preferred_element_type=F32)


def _sigmoid(x):
    return 0.5 * jnp.tanh(0.5 * x) + 0.5


def _rmsn(x, w):
    r = lax.rsqrt(jnp.mean(x * x, axis=-1, keepdims=True) + RMS_EPS)
    return x * r * w, r


def _rmsn_bwd(x, r, w, dy):
    g = dy * w
    dx = r * g - x * (r * r * r) * jnp.mean(x * g, axis=-1, keepdims=True)
    dw = jnp.sum(dy * x * r, axis=0, keepdims=True)
    return dx, dw


def _acc_out(ref, val, first):
    @pl.when(first)
    def _():
        ref[...] = val

    @pl.when(jnp.logical_not(first))
    def _():
        ref[...] += val


def _rms_fwd(x, w, name):
    T, D = x.shape
    tm = min(512, T)

    def body(x_ref, w_ref, o_ref):
        y, _ = _rmsn(x_ref[...], w_ref[...])
        o_ref[...] = y.astype(o_ref.dtype)

    return pl.pallas_call(
        body, grid=(T // tm,),
        in_specs=[_bs((tm, D), lambda i: (i, 0)), _bs((1, D), lambda i: (0, 0))],
        out_specs=_bs((tm, D), lambda i: (i, 0)),
        out_shape=S((T, D), MXU), name=name, compiler_params=_cparams("parallel"))(x, w)


def _rms_fwd_classes(x, w, dils, name):
    T, D = x.shape
    tm = min(512, T)
    slabs = D // HEAD_DIM

    def body(x_ref, w_ref, o_ref, *rest):
        outs, slab_ref = rest[:-1], rest[-1]
        y, _ = _rmsn(x_ref[...], w_ref[...])
        o_ref[...] = y.astype(o_ref.dtype)
        for j in range(slabs):
            slab_ref[j] = y[:, j * HEAD_DIM:(j + 1) * HEAD_DIM]
        for out_ref, d in zip(outs, dils):
            _tokens_to_classes(slab_ref, out_ref, d, tm)

    outs = pl.pallas_call(
        body, grid=(T // tm,),
        in_specs=[_bs((tm, D), lambda i: (i, 0)), _bs((1, D), lambda i: (0, 0))],
        out_specs=[_bs((tm, D), lambda i: (i, 0))] + [_bs((d, tm // d, D), lambda i: (0, i, 0)) for d in dils],
        out_shape=[S((T, D), MXU)] + [S((d, T // d, D), MXU) for d in dils],
        scratch_shapes=[pltpu.VMEM((slabs, tm, HEAD_DIM), F32)], name=name, compiler_params=_cparams("parallel"))(x, w)
    return outs[0], [a.reshape(T, D) for a in outs[1:]]


def _rms_bwd(x, w, dys, resid, name, class_dys=()):
    T, D = x.shape
    tm = min(512, T)
    n = len(dys)
    nc = len(class_dys)
    slabs = D // HEAD_DIM

    def body(*refs):
        x_ref, w_ref, res_ref = refs[0], refs[1], refs[2 + n + nc]
        dx_ref, dw_ref = refs[3 + n + nc], refs[4 + n + nc]
        xv = x_ref[...]
        dy = refs[2][...]
        for dy_ref in refs[3:2 + n]:
            dy = dy + dy_ref[...]
        for dy_ref, (_, d) in zip(refs[2 + n:2 + n + nc], class_dys):
            slab_ref = refs[-1]
            _classes_to_tokens(dy_ref, slab_ref, d, tm)
            dy = dy + jnp.concatenate([slab_ref[j] for j in range(slabs)], axis=1)
        _, r = _rmsn(xv, w_ref[...])
        dx, dw = _rmsn_bwd(xv, r, w_ref[...], dy)
        dx_ref[...] = res_ref[...] + dx
        _acc_out(dw_ref, dw, pl.program_id(0) == 0)

    row = _bs((tm, D), lambda i: (i, 0))
    vec = _bs((1, D), lambda i: (0, 0))
    return pl.pallas_call(
        body, grid=(T // tm,),
        in_specs=[row, vec] + [row] * n + [_bs((d, tm // d, D), lambda i: (0, i, 0)) for _, d in class_dys] + [row],
        out_specs=[row, vec], out_shape=[S((T, D), F32), S((1, D), F32)],
        scratch_shapes=[pltpu.VMEM((slabs, tm, HEAD_DIM), F32)] if nc else [],
        name=name, compiler_params=_cparams("arbitrary"))(
            x, w, *dys, *[a.reshape(d, T // d, D) for a, d in class_dys], resid)


def _pick(n, cands):
    for c in cands:
        if n % c == 0:
            return c
    return n


MM_VMEM_BUDGET = 36 * 1024 * 1024


def _mm_tiles(tm, N, K, a_bytes, b_bytes, o_bytes, has_res):
    best = None
    for tn in (1536, 1024, 1408, 512, 256, 128):
        for tk in (3072, 1536, 1024, 1408, 512, 256, 128):
            if N % tn or K % tk:
                continue
            nk = K // tk
            need = 2 * (tm * tk * a_bytes + tk * tn * b_bytes + tm * tn * (o_bytes + 4 * has_res)) + (nk > 1) * tm * tn * 4
            if need > MM_VMEM_BUDGET:
                continue
            key = ((N // tn) * nk, nk)
            if best is None or key < best[0]:
                best = (key, tn, tk)
    if best is None:
        return _pick(N, (128,)), _pick(K, (128,))
    return best[1], best[2]


def _matmul(a, b, *, ta=False, tb=False, out_dtype=F32, residual=None, after=None, tm=None, tn=None, tk=None, name):
    M, K = (a.shape[1], a.shape[0]) if ta else a.shape
    N = b.shape[0] if tb else b.shape[1]
    tm = tm or _pick(M, (1024, 1408, 512, 256, 128))
    if tn is None or tk is None:
        tn, tk = _mm_tiles(tm, N, K, a.dtype.itemsize, b.dtype.itemsize, jnp.dtype(out_dtype).itemsize, residual is not None)
    nk = K // tk
    dn = (((0 if ta else 1,), (1 if tb else 0,)), ((), ()))
    has_res = residual is not None
    n_in = 2 + has_res + (after is not None)

    def body(*refs):
        a_ref, b_ref = refs[0], refs[1]
        res_ref = refs[2] if has_res else None
        o_ref = refs[n_in]
        p = lax.dot_general(a_ref[...].astype(MXU), b_ref[...].astype(MXU), dn, preferred_element_type=F32)

        def finish(acc):
            if has_res:
                acc = acc + res_ref[...]
            o_ref[...] = acc.astype(o_ref.dtype)

        if nk == 1:
            finish(p)
        else:
            acc_ref = refs[-1]
            k = pl.program_id(2)

            @pl.when(k == 0)
            def _():
                acc_ref[...] = p

            @pl.when(k > 0)
            def _():
                acc_ref[...] += p

            @pl.when(k == nk - 1)
            def _():
                finish(acc_ref[...])

    a_spec = _bs((tk, tm), lambda i, j, k: (k, i)) if ta else _bs((tm, tk), lambda i, j, k: (i, k))
    b_spec = _bs((tn, tk), lambda i, j, k: (j, k)) if tb else _bs((tk, tn), lambda i, j, k: (k, j))
    o_spec = _bs((tm, tn), lambda i, j, k: (i, j))
    in_specs = [a_spec, b_spec] + ([o_spec] if has_res else [])
    args = (a, b) + ((residual,) if has_res else ())
    if after is not None:
        in_specs.append(_bs((8, 128), lambda i, j, k: (0, 0)))
        args += (after,)
    return pl.pallas_call(
        body, grid=(M // tm, N // tn, nk), in_specs=in_specs, out_specs=o_spec,
        out_shape=S((M, N), out_dtype), scratch_shapes=[pltpu.VMEM((tm, tn), F32)] if nk > 1 else [],
        name=name, compiler_params=_cparams("parallel", "parallel", "arbitrary"))(*args)


def _bucket_matrix(dilation):
    n = BLK
    qi = np.arange(n)[:, None]
    kj = np.arange(2 * n)[None, :]
    step = qi + n - kj
    dist = np.clip(step, 0, None) * dilation
    max_exact = N_BUCKETS // 2
    d = np.maximum(dist.astype(np.float32), np.float32(1.0))
    large = max_exact + (np.log(d / np.float32(max_exact)) / np.float32(math.log(MAX_DISTANCE / max_exact))
                         * np.float32(N_BUCKETS - max_exact)).astype(np.int32)
    large = np.minimum(large, N_BUCKETS - 1)
    bucket = np.where(dist < max_exact, dist, large)
    band = (step >= 0) & (step <= n)
    return np.where(band, bucket, -1).astype(np.int32)


def _build_bias(tbl_ref, bkt_ref, bias_ref, g):
    bk = bkt_ref[...]
    for h in range(HEADS):
        acc = jnp.full(bk.shape, NEG, F32)
        for b in range(N_BUCKETS):
            acc = jnp.where(bk == b, tbl_ref[b, HEADS * g + h], acc)
        bias_ref[h] = acc


def _attn_fwd(zq, table, qw, kw, g, dil):
    T = zq.shape[0]
    nb = T // dil // BLK
    qb = _pick(nb, (4, 2, 1))
    nt = nb // qb
    bkt = jnp.asarray(_bucket_matrix(dil))

    def zspec(part, prev):
        if prev:
            return _bs((BLK, GW), lambda c, i: (c * nb + jnp.maximum(i * qb - 1, 0), part))
        return _bs((qb * BLK, GW), lambda c, i: (c * nt + i, part))

    def body(tbl_ref, bkt_ref, qw_ref, kw_ref, q_ref, kp_ref, kc_ref, vp_ref, vc_ref, o_ref, lse_ref, bias_ref):
        c, i = pl.program_id(0), pl.program_id(1)

        @pl.when((c == 0) & (i == 0))
        def _():
            _build_bias(tbl_ref, bkt_ref, bias_ref, g)

        kj = lax.broadcasted_iota(jnp.int32, (BLK, 2 * BLK), 1)
        no_prev = jnp.logical_and(i == 0, kj < BLK)
        for h in range(HEADS):
            sl = slice(h * HEAD_DIM, (h + 1) * HEAD_DIM)
            qn, _ = _rmsn(q_ref[:, sl].astype(F32), qw_ref[...])
            kn, _ = _rmsn(jnp.concatenate([kp_ref[:, sl], kc_ref[:, sl]], axis=0).astype(F32), kw_ref[...])
            v = jnp.concatenate([vp_ref[:, sl], vc_ref[:, sl]], axis=0)
            for j in range(qb):
                rows = slice(j * BLK, (j + 1) * BLK)
                keys = slice(j * BLK, (j + 2) * BLK)
                s = _dot_nt(qn[rows], kn[keys]) * SCALE + bias_ref[h]
                if j == 0:
                    s = jnp.where(no_prev, NEG, s)
                m = jnp.max(s, axis=-1, keepdims=True)
                p = jnp.exp(s - m)
                l = jnp.sum(p, axis=-1, keepdims=True)
                o_ref[rows, sl] = _dot(p, v[keys]) / l
                lse_ref[rows, sl] = jnp.broadcast_to(m + jnp.log(l), (BLK, HEAD_DIM))

    ospec = _bs((qb * BLK, GW), lambda c, i: (c * nt + i, 0))
    vec = _bs((1, HEAD_DIM), lambda c, i: (0, 0))
    return pl.pallas_call(
        body, grid=(dil, nt),
        in_specs=[pl.BlockSpec(memory_space=pltpu.SMEM), _bs((BLK, 2 * BLK), lambda c, i: (0, 0)), vec, vec,
                  zspec(0, False), zspec(1, True), zspec(1, False), zspec(2, True), zspec(2, False)],
        out_specs=[ospec, ospec],
        out_shape=[S((T, GW), F32), S((T, GW), F32)],
        scratch_shapes=[pltpu.VMEM((HEADS, BLK, 2 * BLK), F32)],
        name=f"attn_fwd_g{g}", compiler_params=_cparams("arbitrary", "arbitrary"))(table, bkt, qw, kw, zq, zq, zq, zq, zq)


def _classes_to_tokens(src_ref, dst_ref, dil, rows):
    for c in range(dil):
        for h in range(dst_ref.shape[0]):
            dst_ref[h, pl.ds(c, rows // dil, stride=dil), :] = src_ref[c, :, h * HEAD_DIM:(h + 1) * HEAD_DIM]


def _tokens_to_classes(src_ref, dst_ref, dil, rows):
    for c in range(dil):
        for h in range(src_ref.shape[0]):
            dst_ref[c, :, h * HEAD_DIM:(h + 1) * HEAD_DIM] = src_ref[h, pl.ds(c, rows // dil, stride=dil), :].astype(dst_ref.dtype)


def _class_view(a, dil):
    return a.reshape(dil, a.shape[0] // dil, a.shape[1])


def _attn_merge(os_, lses, dils):
    T = os_[0].shape[0]
    tm = min(512, T)
    assert dils[0] == 1 and len(dils) == 3

    def body(o0, l0, o1, l1, o2, l2, a_ref, lse_ref, a1_ref, lse1_ref, a2_ref, lse2_ref,
             no1, nl1, no2, nl2, ra, rl):
        for src, dst, d in ((o1, no1, dils[1]), (l1, nl1, dils[1]), (o2, no2, dils[2]), (l2, nl2, dils[2])):
            _classes_to_tokens(src, dst, d, tm)
        for h in range(HEADS):
            sl = slice(h * HEAD_DIM, (h + 1) * HEAD_DIM)
            ls = [l0[:, sl], nl1[h], nl2[h]]
            os3 = [o0[:, sl], no1[h], no2[h]]
            mx = jnp.maximum(jnp.maximum(ls[0], ls[1]), ls[2])
            tot = mx + jnp.log(jnp.exp(ls[0] - mx) + jnp.exp(ls[1] - mx) + jnp.exp(ls[2] - mx))
            att = jnp.exp(ls[0] - tot) * os3[0] + jnp.exp(ls[1] - tot) * os3[1] + jnp.exp(ls[2] - tot) * os3[2]
            a_ref[:, sl] = att
            lse_ref[:, sl] = tot
            ra[h] = att
            rl[h] = tot
        for src, dst, d in ((ra, a1_ref, dils[1]), (rl, lse1_ref, dils[1]), (ra, a2_ref, dils[2]), (rl, lse2_ref, dils[2])):
            _tokens_to_classes(src, dst, d, tm)

    row = _bs((tm, GW), lambda i: (i, 0))
    cls = lambda d: _bs((d, tm // d, GW), lambda i: (0, i, 0))
    cshape = lambda d: S((d, T // d, GW), F32)
    slab = pltpu.VMEM((HEADS, tm, HEAD_DIM), F32)
    d1, d2 = dils[1], dils[2]
    attn, lse, a1, l1, a2, l2 = pl.pallas_call(
        body, grid=(T // tm,), in_specs=[row, row, cls(d1), cls(d1), cls(d2), cls(d2)],
        out_specs=[row, row, cls(d1), cls(d1), cls(d2), cls(d2)],
        out_shape=[S((T, GW), F32), S((T, GW), F32), cshape(d1), cshape(d1), cshape(d2), cshape(d2)],
        scratch_shapes=[slab] * 6, name="attn_merge", compiler_params=_cparams("parallel"))(
            os_[0], lses[0], _class_view(os_[1], d1), _class_view(lses[1], d1), _class_view(os_[2], d2),
            _class_view(lses[2], d2))
    flat = lambda a: a.reshape(T, GW)
    return attn, lse, [attn, flat(a1), flat(a2)], [lse, flat(l1), flat(l2)]


def _attn_bwd(zq, table, qw, kw, d_attn, attn, lse, g, dil):
    T = zq.shape[0]
    nb = T // dil // BLK
    qb = _pick(nb, (4, 2, 1))
    nt = nb // qb
    bkt = jnp.asarray(_bucket_matrix(dil))

    def body(tbl_ref, bkt_ref, qw_ref, kw_ref, q_ref, k_ref, v_ref, kp_ref, vp_ref, qx_ref, da_ref, at_ref, lse_ref,
             dax_ref, atx_ref, lsex_ref, dz_ref, dqw_ref, dkw_ref, dtab_ref, bias_ref, dbias_ref):
        c, i = pl.program_id(0), pl.program_id(1)

        @pl.when((c == 0) & (i == 0))
        def _():
            _build_bias(tbl_ref, bkt_ref, bias_ref, g)
            dbias_ref[...] = jnp.zeros_like(dbias_ref)
            dqw_ref[...] = jnp.zeros_like(dqw_ref)
            dkw_ref[...] = jnp.zeros_like(dkw_ref)

        kj = lax.broadcasted_iota(jnp.int32, (BLK, 2 * BLK), 1)
        no_prev = jnp.logical_and(i == 0, kj < BLK)
        has_next = i < nt - 1
        last = slice((qb - 1) * BLK, qb * BLK)
        dqw_acc = jnp.zeros((1, HEAD_DIM), F32)
        dkw_acc = jnp.zeros((1, HEAD_DIM), F32)

        def add(parts, t, val):
            parts[t] = val if parts[t] is None else parts[t] + val

        for h in range(HEADS):
            lo = h * HEAD_DIM
            sl = slice(lo, lo + HEAD_DIM)
            q, k = q_ref[:, sl].astype(F32), k_ref[:, sl].astype(F32)
            qn, rq = _rmsn(q, qw_ref[...])
            kn, rk = _rmsn(k, kw_ref[...])
            kpn, _ = _rmsn(kp_ref[:, sl].astype(F32), kw_ref[...])
            kn_ext = jnp.concatenate([kpn, kn], axis=0)
            v_ext = jnp.concatenate([vp_ref[:, sl], v_ref[:, sl]], axis=0)
            dqn, dkn, dv = [None] * qb, [None] * qb, [None] * qb
            for j in range(qb):
                rows = slice(j * BLK, (j + 1) * BLK)
                keys = slice(j * BLK, (j + 2) * BLK)
                s = _dot_nt(qn[rows], kn_ext[keys]) * SCALE + bias_ref[h]
                if j == 0:
                    s = jnp.where(no_prev, NEG, s)
                p = jnp.exp(s - lse_ref[rows, lo:lo + 1])
                do = da_ref[rows, sl]
                delta = jnp.sum(do * at_ref[rows, sl], axis=-1, keepdims=True)
                ds = p * (_dot_nt(do, v_ext[keys]) - delta)
                dbias_ref[h] += ds
                dqn[j] = _dot(ds, kn_ext[keys]) * SCALE
                dv2 = _dot_tn(p, do)
                dk2 = _dot_tn(ds, qn[rows]) * SCALE
                if j >= 1:
                    add(dv, j - 1, dv2[:BLK])
                    add(dkn, j - 1, dk2[:BLK])
                add(dv, j, dv2[BLK:])
                add(dkn, j, dk2[BLK:])
            qxn, _ = _rmsn(qx_ref[:, sl].astype(F32), qw_ref[...])
            sx = _dot_nt(qxn, kn[last]) * SCALE + bias_ref[h, :, 0:BLK]
            px = jnp.where(has_next, jnp.exp(sx - lsex_ref[:, lo:lo + 1]), 0.0)
            dox = dax_ref[:, sl]
            dsx = px * (_dot_nt(dox, v_ref[last, sl]) - jnp.sum(dox * atx_ref[:, sl], axis=-1, keepdims=True))
            add(dv, qb - 1, _dot_tn(px, dox))
            add(dkn, qb - 1, _dot_tn(dsx, qxn) * SCALE)
            dq, dqw = _rmsn_bwd(q, rq, qw_ref[...], jnp.concatenate(dqn, axis=0))
            dk, dkw = _rmsn_bwd(k, rk, kw_ref[...], jnp.concatenate(dkn, axis=0))
            dqw_acc += dqw
            dkw_acc += dkw
            dz_ref[:, lo:lo + HEAD_DIM] = dq.astype(dz_ref.dtype)
            dz_ref[:, GW + lo:GW + lo + HEAD_DIM] = dk.astype(dz_ref.dtype)
            dz_ref[:, 2 * GW + lo:2 * GW + lo + HEAD_DIM] = jnp.concatenate(dv, axis=0).astype(dz_ref.dtype)
        dqw_ref[...] += dqw_acc
        dkw_ref[...] += dkw_acc

        @pl.when((c == dil - 1) & (i == nt - 1))
        def _():
            bk = bkt_ref[...]
            rows = lax.broadcasted_iota(jnp.int32, (N_BUCKETS, HEAD_DIM), 0)
            lanes = lax.broadcasted_iota(jnp.int32, (N_BUCKETS, HEAD_DIM), 1)
            out = jnp.zeros((N_BUCKETS, HEAD_DIM), F32)
            for h in range(HEADS):
                acc = dbias_ref[h]
                for b in range(N_BUCKETS):
                    val = jnp.sum(jnp.where(bk == b, acc, 0.0))
                    out = jnp.where((rows == b) & (lanes == h), val, out)
            dtab_ref[...] = out

    tile = lambda part: _bs((qb * BLK, GW), lambda c, i: (c * nt + i, part))
    before = lambda part: _bs((BLK, GW), lambda c, i: (c * nb + jnp.maximum(i * qb - 1, 0), part))
    after = lambda part: _bs((BLK, GW), lambda c, i: (c * nb + jnp.minimum((i + 1) * qb, nb - 1), part))
    vec = _bs((1, HEAD_DIM), lambda c, i: (0, 0))
    tabs = _bs((N_BUCKETS, HEAD_DIM), lambda c, i: (0, 0))
    dzq, dqw, dkw, dtab = pl.pallas_call(
        body, grid=(dil, nt),
        in_specs=[pl.BlockSpec(memory_space=pltpu.SMEM), _bs((BLK, 2 * BLK), lambda c, i: (0, 0)), vec, vec,
                  tile(0), tile(1), tile(2), before(1), before(2), after(0), tile(0), tile(0), tile(0),
                  after(0), after(0), after(0)],
        out_specs=[_bs((qb * BLK, 3 * GW), lambda c, i: (c * nt + i, 0)), vec, vec, tabs],
        out_shape=[S((T, 3 * GW), MXU)] + [S((1, HEAD_DIM), F32)] * 2 + [S((N_BUCKETS, HEAD_DIM), F32)],
        scratch_shapes=[pltpu.VMEM((HEADS, BLK, 2 * BLK), F32), pltpu.VMEM((HEADS, BLK, 2 * BLK), F32)],
        name=f"attn_bwd_g{g}", compiler_params=_cparams("arbitrary", "arbitrary"))(
            table, bkt, qw, kw, zq, zq, zq, zq, zq, zq, d_attn, attn, lse, d_attn, attn, lse)
    return dzq, dqw, dkw, dtab[:, :HEADS]


def _mem_fwd(mem, mem_norm_w, w_mem_kv, xk_w):
    def body(mem_ref, nw_ref, w_ref, xk_ref, mk_ref, mv_ref):
        mn, _ = _rmsn(mem_ref[...], nw_ref[...])
        kv = _dot(mn, w_ref[...])
        for h in range(HEADS):
            sl = slice(h * HEAD_DIM, (h + 1) * HEAD_DIM)
            kn, _ = _rmsn(kv[:, sl], xk_ref[...])
            mk_ref[:, sl] = kn.astype(mk_ref.dtype)
        mv_ref[...] = kv[:, GW:].astype(mv_ref.dtype)

    return pl.pallas_call(body, out_shape=[S((N_MEM, GW), MXU), S((N_MEM, GW), MXU)], name="mem_fwd",
                          compiler_params=_cparams())(mem, mem_norm_w, w_mem_kv, xk_w)


def _mem_bwd(mem, mem_norm_w, w_mem_kv, xk_w, dmk, dmv):
    def body(mem_ref, nw_ref, w_ref, xk_ref, dmk_ref, dmv_ref, dw_ref, dnw_ref, dxk_ref):
        memv = mem_ref[...]
        mn, r = _rmsn(memv, nw_ref[...])
        kv = _dot(mn, w_ref[...])
        dxk = jnp.zeros((1, HEAD_DIM), F32)
        parts = []
        for h in range(HEADS):
            sl = slice(h * HEAD_DIM, (h + 1) * HEAD_DIM)
            kh = kv[:, sl]
            _, rk = _rmsn(kh, xk_ref[...])
            dk, dw = _rmsn_bwd(kh, rk, xk_ref[...], dmk_ref[:, sl])
            dxk += dw
            parts.append(dk)
        dkv = jnp.concatenate(parts + [dmv_ref[...]], axis=1)
        dw_ref[...] = _dot_tn(mn, dkv)
        dmn = _dot_nt(dkv, w_ref[...])
        dnw_ref[...] = jnp.sum(dmn * memv * r, axis=0, keepdims=True)
        dxk_ref[...] = dxk

    return pl.pallas_call(
        body, out_shape=[S((D_MODEL, 2 * GW), F32), S((1, D_MODEL), F32), S((1, HEAD_DIM), F32)], name="mem_bwd",
        compiler_params=_cparams())(mem, mem_norm_w, w_mem_kv, xk_w, dmk, dmv)


def _cross_fwd(z, mk, mv, xq_w):
    T = z.shape[0]
    tm = min(512, T)

    def body(q_ref, mk_ref, mv_ref, w_ref, o_ref):
        for h in range(HEADS):
            sl = slice(h * HEAD_DIM, (h + 1) * HEAD_DIM)
            qn, _ = _rmsn(q_ref[:, sl], w_ref[...])
            s = _dot_nt(qn, mk_ref[:, sl]) * SCALE
            e = jnp.exp(s - jnp.max(s, axis=-1, keepdims=True))
            p = e / jnp.sum(e, axis=-1, keepdims=True)
            o_ref[:, sl] = _dot(p, mv_ref[:, sl]).astype(o_ref.dtype)

    full = _bs((N_MEM, GW), lambda i: (0, 0))
    return pl.pallas_call(
        body, grid=(T // tm,),
        in_specs=[_bs((tm, GW), lambda i: (i, 2)), full, full, _bs((1, HEAD_DIM), lambda i: (0, 0))],
        out_specs=_bs((tm, GW), lambda i: (i, 0)), out_shape=S((T, GW), MXU), name="cross_fwd",
        compiler_params=_cparams("parallel"))(z, mk, mv, xq_w)


def _cross_bwd(z, mk, mv, xq_w, d_cross):
    T = z.shape[0]
    tm = min(512, T)

    def body(q_ref, mk_ref, mv_ref, w_ref, do_ref, dq_ref, dmk_ref, dmv_ref, dw_ref):
        first = pl.program_id(0) == 0
        dw_acc = jnp.zeros((1, HEAD_DIM), F32)
        dmk_parts, dmv_parts = [], []
        for h in range(HEADS):
            sl = slice(h * HEAD_DIM, (h + 1) * HEAD_DIM)
            qh = q_ref[:, sl]
            qn, r = _rmsn(qh, w_ref[...])
            s = _dot_nt(qn, mk_ref[:, sl]) * SCALE
            e = jnp.exp(s - jnp.max(s, axis=-1, keepdims=True))
            p = e / jnp.sum(e, axis=-1, keepdims=True)
            do = do_ref[:, sl]
            dp = _dot_nt(do, mv_ref[:, sl])
            ds = p * (dp - jnp.sum(dp * p, axis=-1, keepdims=True)) * SCALE
            dmv_parts.append(_dot_tn(p, do))
            dmk_parts.append(_dot_tn(ds, qn))
            dq, dw = _rmsn_bwd(qh, r, w_ref[...], _dot(ds, mk_ref[:, sl]))
            dw_acc += dw
            dq_ref[:, sl] = dq.astype(dq_ref.dtype)
        _acc_out(dmk_ref, jnp.concatenate(dmk_parts, axis=1), first)
        _acc_out(dmv_ref, jnp.concatenate(dmv_parts, axis=1), first)
        _acc_out(dw_ref, dw_acc, first)

    full = _bs((N_MEM, GW), lambda i: (0, 0))
    vec = _bs((1, HEAD_DIM), lambda i: (0, 0))
    row = _bs((tm, GW), lambda i: (i, 0))
    return pl.pallas_call(
        body, grid=(T // tm,),
        in_specs=[_bs((tm, GW), lambda i: (i, 2)), full, full, vec, row],
        out_specs=[row, full, full, vec],
        out_shape=[S((T, GW), MXU), S((N_MEM, GW), F32), S((N_MEM, GW), F32), S((1, HEAD_DIM), F32)],
        name="cross_bwd", compiler_params=_cparams("arbitrary"))(z, mk, mv, xq_w, d_cross)


SUBLANES = 8


def _row_windows(ext, first, count, rows, shift_ref=None):
    for b in range(SUBLANES):
        js = [j for j in range(count) if (first + j) % SUBLANES == b]
        if not js:
            continue
        span = max(first + j for j in js) - b + rows
        shifted = ext[b:b + span, :]
        if shift_ref is not None:
            shift_ref[b, 0:span, :] = shifted
        for j in js:
            a = first + j - b
            yield j, (shifted[a:a + rows, :] if shift_ref is None else shift_ref[b, a:a + rows, :])


def _taps(ext, w_ref, width, base, rows, shift_ref=None):
    acc = None
    for k, win in _row_windows(ext, base - (width - 1), width, rows, shift_ref):
        term = win * w_ref[k:k + 1, :]
        acc = term if acc is None else acc + term
    return acc


def _taps_bwd(d_ext, x, w_ref, width, rows, shift_ref=None):
    acc = None
    dw = [None] * width
    for j, win in _row_windows(d_ext, 0, width, rows, shift_ref):
        k = width - 1 - j
        term = win * w_ref[k:k + 1, :]
        acc = term if acc is None else acc + term
        dw[k] = jnp.sum(win * x, axis=0, keepdims=True)
    return acc, jnp.concatenate(dw, axis=0)


def _conv_fwd(z, cw, cb, lw, lb):
    T = z.shape[0]
    tm = min(512, T)
    hb = tm // CONV_HALO

    def body(val_ref, gate_ref, hval_ref, hgate_ref, cw_ref, cb_ref, lw_ref, lb_ref, o_ref, shift_ref):
        i = pl.program_id(0)
        halo = hval_ref[...] * _sigmoid(hgate_ref[...])
        halo = jnp.where(i == 0, 0.0, halo)
        ext = jnp.concatenate([halo, val_ref[...] * _sigmoid(gate_ref[...])], axis=0)
        y = _taps(ext, cw_ref, CONV_WIDTH, CONV_HALO, tm, shift_ref) + cb_ref[...]
        xc = y - jnp.mean(y, axis=-1, keepdims=True)
        a = xc * lax.rsqrt(jnp.mean(xc * xc, axis=-1, keepdims=True) + LN_EPS) * lw_ref[...] + lb_ref[...]
        o_ref[...] = (a * _sigmoid(a)).astype(o_ref.dtype)

    vec = _bs((1, GW), lambda i: (0, 0))
    halo_spec = lambda col: _bs((CONV_HALO, GW), lambda i: (jnp.maximum(i * hb - 1, 0), col))
    return pl.pallas_call(
        body, grid=(T // tm,),
        in_specs=[_bs((tm, GW), lambda i: (i, 0)), _bs((tm, GW), lambda i: (i, 1)), halo_spec(0), halo_spec(1),
                  _bs((CONV_WIDTH, GW), lambda i: (0, 0)), vec, vec, vec],
        out_specs=_bs((tm, GW), lambda i: (i, 0)), out_shape=S((T, GW), MXU),
        scratch_shapes=[pltpu.VMEM((SUBLANES, tm + CONV_HALO, GW), F32)], name="conv_fwd",
        compiler_params=_cparams("parallel"))(z, z, z, z, cw, cb, lw, lb)


def _conv_bwd(z, cw, cb, lw, lb, d_u):
    T = z.shape[0]
    tm = min(512, T)
    hb = tm // CONV_HALO
    nt = T // tm
    H = CONV_HALO

    def body(val_ref, gate_ref, pval_ref, pgate_ref, nval_ref, ngate_ref, du_ref, ndu_ref, cw_ref, cb_ref, lw_ref,
             lb_ref, dval_ref, dgate_ref, dcw_ref, dcb_ref, dlw_ref, dlb_ref, shift_ref):
        i = pl.program_id(0)
        first = i == 0
        val = jnp.concatenate([pval_ref[...] * jnp.where(first, 0.0, 1.0), val_ref[...], nval_ref[...]], axis=0)
        sg = _sigmoid(jnp.concatenate([pgate_ref[...], gate_ref[...], ngate_ref[...]], axis=0))
        u0 = val * sg
        y = _taps(u0, cw_ref, CONV_WIDTH, H, tm + H, shift_ref) + cb_ref[...]
        xc = y - jnp.mean(y, axis=-1, keepdims=True)
        rs = lax.rsqrt(jnp.mean(xc * xc, axis=-1, keepdims=True) + LN_EPS)
        nh = xc * rs
        a = nh * lw_ref[...] + lb_ref[...]
        sa = _sigmoid(a)
        du = jnp.concatenate([du_ref[...], ndu_ref[...] * jnp.where(i == nt - 1, 0.0, 1.0)], axis=0)
        da = du * (sa * (1.0 + a * (1.0 - sa)))
        dn = da * lw_ref[...]
        dy = rs * (dn - jnp.mean(dn, axis=-1, keepdims=True) - nh * jnp.mean(dn * nh, axis=-1, keepdims=True))
        du0, dcw = _taps_bwd(dy, u0[H:H + tm], cw_ref, CONV_WIDTH, tm, shift_ref)
        v0, s0 = val[H:H + tm], sg[H:H + tm]
        dval_ref[...] = (du0 * s0).astype(dval_ref.dtype)
        dgate_ref[...] = (du0 * v0 * s0 * (1.0 - s0)).astype(dgate_ref.dtype)
        dy0 = dy[:tm]
        _acc_out(dcw_ref, dcw, first)
        _acc_out(dcb_ref, jnp.sum(dy0, axis=0, keepdims=True), first)
        _acc_out(dlw_ref, jnp.sum(da[:tm] * nh[:tm], axis=0, keepdims=True), first)
        _acc_out(dlb_ref, jnp.sum(da[:tm], axis=0, keepdims=True), first)

    vec = _bs((1, GW), lambda i: (0, 0))
    cwspec = _bs((CONV_WIDTH, GW), lambda i: (0, 0))
    prev = lambda col: _bs((H, GW), lambda i: (jnp.maximum(i * hb - 1, 0), col))
    nxt = lambda col: _bs((H, GW), lambda i: (jnp.minimum((i + 1) * hb, nt * hb - 1), col))
    row = _bs((tm, GW), lambda i: (i, 0))
    return pl.pallas_call(
        body, grid=(nt,),
        in_specs=[_bs((tm, GW), lambda i: (i, 0)), _bs((tm, GW), lambda i: (i, 1)), prev(0), prev(1), nxt(0), nxt(1),
                  row, nxt(0), cwspec, vec, vec, vec],
        out_specs=[row, row, cwspec, vec, vec, vec],
        out_shape=[S((T, GW), MXU), S((T, GW), MXU), S((CONV_WIDTH, GW), F32)] + [S((1, GW), F32)] * 3,
        scratch_shapes=[pltpu.VMEM((SUBLANES, tm + 2 * H, GW), F32)], name="conv_bwd", compiler_params=_cparams("arbitrary"))(z, z, z, z, z, z, d_u, d_u, cw, cb, lw, lb)


def _ffn_act_fwd(up0, fw, fb):
    T = up0.shape[0]
    tm = min(256, T)
    hb = tm // FFN_HALO
    H = FFN_HALO

    def body(a_ref, g_ref, pa_ref, pg_ref, wa_ref, wg_ref, ba_ref, bg_ref, o_ref, up_ref):
        i = pl.program_id(0)
        keep = jnp.where(i == 0, 0.0, 1.0)
        ea = jnp.concatenate([pa_ref[...] * keep, a_ref[...]], axis=0)
        eg = jnp.concatenate([pg_ref[...] * keep, g_ref[...]], axis=0)
        av = _taps(ea, wa_ref, FFN_CONV_WIDTH, H, tm) + ba_ref[...]
        gv = _taps(eg, wg_ref, FFN_CONV_WIDTH, H, tm) + bg_ref[...]
        o_ref[...] = (gv * _sigmoid(gv) * av).astype(o_ref.dtype)
        up_ref[:, :D_FF] = av
        up_ref[:, D_FF:] = gv

    col = lambda j: _bs((tm, D_FF), lambda i: (i, j))
    prev = lambda j: _bs((H, D_FF), lambda i: (jnp.maximum(i * hb - 1, 0), j))
    wspec = lambda j: _bs((FFN_CONV_WIDTH, D_FF), lambda i: (0, j))
    bspec = lambda j: _bs((1, D_FF), lambda i: (0, j))
    return pl.pallas_call(
        body, grid=(T // tm,),
        in_specs=[col(0), col(1), prev(0), prev(1), wspec(0), wspec(1), bspec(0), bspec(1)],
        out_specs=[_bs((tm, D_FF), lambda i: (i, 0)), _bs((tm, 2 * D_FF), lambda i: (i, 0))],
        out_shape=[S((T, D_FF), MXU), S((T, 2 * D_FF), F32)], name="ffn_act_fwd",
        compiler_params=_cparams("parallel"))(up0, up0, up0, up0, fw, fw, fb, fb)


def _ffn_act_bwd(up0, up, fw, d_f):
    T = up0.shape[0]
    tm = min(128, T)
    hb = tm // FFN_HALO
    nt = T // tm
    H = FFN_HALO
    W = FFN_CONV_WIDTH

    def body(a_ref, g_ref, av_ref, gv_ref, nav_ref, ngv_ref, df_ref, ndf_ref, wa_ref, wg_ref, dup_ref, dw_ref, db_ref):
        i = pl.program_id(0)
        first = i == 0
        av = jnp.concatenate([av_ref[...], nav_ref[...]], axis=0)
        gv = jnp.concatenate([gv_ref[...], ngv_ref[...]], axis=0)
        df = jnp.concatenate([df_ref[...], ndf_ref[...] * jnp.where(i == nt - 1, 0.0, 1.0)], axis=0)
        sg = _sigmoid(gv)
        d_av = df * gv * sg
        d_gv = df * av * (sg * (1.0 + gv * (1.0 - sg)))
        dua, dwa = _taps_bwd(d_av, a_ref[...], wa_ref, W, tm)
        dug, dwg = _taps_bwd(d_gv, g_ref[...], wg_ref, W, tm)
        dup_ref[:, :D_FF] = dua.astype(dup_ref.dtype)
        dup_ref[:, D_FF:] = dug.astype(dup_ref.dtype)
        dw = jnp.concatenate([dwa, dwg], axis=1)
        db = jnp.concatenate([jnp.sum(d_av[:tm], axis=0, keepdims=True), jnp.sum(d_gv[:tm], axis=0, keepdims=True)], axis=1)
        _acc_out(dw_ref, dw, first)
        _acc_out(db_ref, db, first)

    col = lambda j: _bs((tm, D_FF), lambda i: (i, j))
    nxt = lambda j: _bs((H, D_FF), lambda i: (jnp.minimum((i + 1) * hb, nt * hb - 1), j))
    wspec = lambda j: _bs((W, D_FF), lambda i: (0, j))
    return pl.pallas_call(
        body, grid=(nt,),
        in_specs=[col(0), col(1), col(0), col(1), nxt(0), nxt(1), col(0), nxt(0), wspec(0), wspec(1)],
        out_specs=[_bs((tm, 2 * D_FF), lambda i: (i, 0)), _bs((W, 2 * D_FF), lambda i: (0, 0)),
                   _bs((1, 2 * D_FF), lambda i: (0, 0))],
        out_shape=[S((T, 2 * D_FF), MXU), S((W, 2 * D_FF), F32), S((1, 2 * D_FF), F32)],
        name="ffn_act_bwd", compiler_params=_cparams("arbitrary"))(
            up0, up0, up, up, up, up, d_f, d_f, fw, fw)


def _branch_fwd(attn, u, cross, z, b_gate, wa, wc, wx):
    T = z.shape[0]
    tm = min(512, T)

    def body(a_ref, u_ref, x_ref, g0_ref, g1_ref, g2_ref, b_ref, wa_ref, wc_ref, wx_ref, o_ref):
        acc = None
        for j, (act, g_ref, w_ref) in enumerate(((a_ref, g0_ref, wa_ref), (u_ref, g1_ref, wc_ref), (x_ref, g2_ref, wx_ref))):
            gate = _sigmoid(g_ref[...] + b_ref[:, j * D_MODEL:(j + 1) * D_MODEL])
            term = gate * _dot(act[...], w_ref[...])
            acc = term if acc is None else acc + term
        o_ref[...] = acc.astype(o_ref.dtype)

    act = _bs((tm, GW), lambda i: (i, 0))
    gcol = lambda j: _bs((tm, D_MODEL), lambda i: (i, j))
    wfull = _bs((GW, D_MODEL), lambda i: (0, 0))
    return pl.pallas_call(
        body, grid=(T // tm,),
        in_specs=[act, act, act, gcol(0), gcol(1), gcol(2), _bs((1, 3 * D_MODEL), lambda i: (0, 0)), wfull, wfull, wfull],
        out_specs=_bs((tm, D_MODEL), lambda i: (i, 0)), out_shape=S((T, D_MODEL), MXU), name="branch_fwd",
        compiler_params=_cparams("parallel"))(attn, u, cross, z, z, z, b_gate, wa, wc, wx)


def _branch_bwd(d_merged, attn, u, cross, z, b_gate, wa, wc, wx, dils):
    T = z.shape[0]
    tm = min(512, T)
    d1, d2 = dils[1], dils[2]

    def body(dm_ref, a_ref, u_ref, x_ref, g0_ref, g1_ref, g2_ref, b_ref, wa_ref, wc_ref, wx_ref,
             dzg_ref, da_ref, du_ref, dx_ref, dwa_ref, dwc_ref, dwx_ref, db_ref, da1_ref, da2_ref, slab_ref):
        first = pl.program_id(0) == 0
        dm = dm_ref[...]
        dbs = []
        for j, (act, g_ref, w_ref, dact_ref, dw_ref) in enumerate((
                (a_ref, g0_ref, wa_ref, da_ref, dwa_ref), (u_ref, g1_ref, wc_ref, du_ref, dwc_ref),
                (x_ref, g2_ref, wx_ref, dx_ref, dwx_ref))):
            av = act[...]
            gate = _sigmoid(g_ref[...] + b_ref[:, j * D_MODEL:(j + 1) * D_MODEL])
            y = _dot(av, w_ref[...])
            dzg = dm * y * gate * (1.0 - gate)
            dzg_ref[:, j * D_MODEL:(j + 1) * D_MODEL] = dzg.astype(dzg_ref.dtype)
            dbs.append(jnp.sum(dzg, axis=0, keepdims=True))
            dy = (gate * dm).astype(MXU)
            dact = _dot_nt(dy, w_ref[...])
            dact_ref[...] = dact
            if j == 0:
                for h in range(HEADS):
                    slab_ref[h] = dact[:, h * HEAD_DIM:(h + 1) * HEAD_DIM]
                _tokens_to_classes(slab_ref, da1_ref, d1, tm)
                _tokens_to_classes(slab_ref, da2_ref, d2, tm)
            _acc_out(dw_ref, _dot_tn(av, dy), first)
        _acc_out(db_ref, jnp.concatenate(dbs, axis=1), first)

    act = _bs((tm, GW), lambda i: (i, 0))
    gcol = lambda j: _bs((tm, D_MODEL), lambda i: (i, j))
    wfull = _bs((GW, D_MODEL), lambda i: (0, 0))
    bvec = _bs((1, 3 * D_MODEL), lambda i: (0, 0))
    cls = lambda d: _bs((d, tm // d, GW), lambda i: (0, i, 0))
    outs = pl.pallas_call(
        body, grid=(T // tm,),
        in_specs=[_bs((tm, D_MODEL), lambda i: (i, 0)), act, act, act, gcol(0), gcol(1), gcol(2), bvec, wfull, wfull, wfull],
        out_specs=[_bs((tm, 3 * D_MODEL), lambda i: (i, 0)), act, act, act, wfull, wfull, wfull, bvec, cls(d1), cls(d2)],
        out_shape=[S((T, 3 * D_MODEL), MXU)] + [S((T, GW), F32)] * 3 + [S((GW, D_MODEL), F32)] * 3 + [S((1, 3 * D_MODEL), F32)]
        + [S((d1, T // d1, GW), F32), S((d2, T // d2, GW), F32)],
        scratch_shapes=[pltpu.VMEM((HEADS, tm, HEAD_DIM), F32)],
        name="branch_bwd", compiler_params=_cparams("arbitrary"))(d_merged, attn, u, cross, z, z, z, b_gate, wa, wc, wx)
    d_zg, d_attn, d_u, d_cross, dwa, dwc, dwx, db, da1, da2 = outs
    return d_zg, [d_attn, da1.reshape(T, GW), da2.reshape(T, GW)], d_u, d_cross, dwa, dwc, dwx, db


def _loss_head(y, target):
    T, D = y.shape
    tm = min(512, T)

    def body(y_ref, t_ref, dy_ref, l_ref):
        e = y_ref[...] - t_ref[...]
        dy_ref[...] = e * (1.0 / D)
        part = jnp.full((8, 128), jnp.sum(e * e), F32)
        _acc_out(l_ref, part, pl.program_id(0) == 0)

    row = _bs((tm, D), lambda i: (i, 0))
    return pl.pallas_call(
        body, grid=(T // tm,), in_specs=[row, row], out_specs=[row, _bs((8, 128), lambda i: (0, 0))],
        out_shape=[S((T, D), F32), S((8, 128), F32)], name="loss_head", compiler_params=_cparams("arbitrary"))(y, target)


def _peer(mask):
    x, y, c = lax.axis_index("x"), lax.axis_index("y"), lax.axis_index("c")
    px = 1 - x if mask & 4 else x
    py = 1 - y if mask & 2 else y
    pc = 1 - c if mask & 1 else c
    return (px, py, pc), 4 * px + 2 * py + pc


def _exchange(arrs, scatter, name):
    n = len(arrs)
    outs_shape = [S(a.shape if scatter else (N_DEV,) + a.shape, a.dtype) for a in arrs]

    def body(*refs):
        ins, outs = refs[:n], refs[n:2 * n]
        send_sems, recv_sems, local_sems = refs[2 * n:]
        me = 4 * lax.axis_index("x") + 2 * lax.axis_index("y") + lax.axis_index("c")
        copies = []
        for w in range(n):
            src = ins[w].at[me] if scatter else ins[w]
            cp = pltpu.make_async_copy(src, outs[w].at[me], local_sems.at[w])
            cp.start()
            copies.append(cp)
        for k in range(1, N_DEV):
            peer, pidx = _peer(k)
            for w in range(n):
                src = ins[w].at[pidx] if scatter else ins[w]
                cp = pltpu.make_async_remote_copy(
                    src_ref=src, dst_ref=outs[w].at[me], send_sem=send_sems.at[w, k - 1], recv_sem=recv_sems.at[w, k - 1],
                    device_id=peer, device_id_type=pl.DeviceIdType.MESH)
                cp.start()
                copies.append(cp)
        for cp in copies:
            cp.wait()

    hbm = pl.BlockSpec(memory_space=pl.ANY)
    return pl.pallas_call(
        body, in_specs=[hbm] * n, out_specs=[hbm] * n, out_shape=outs_shape,
        scratch_shapes=[pltpu.SemaphoreType.DMA((n, N_DEV - 1)), pltpu.SemaphoreType.DMA((n, N_DEV - 1)),
                        pltpu.SemaphoreType.DMA((n,))],
        name=name)(*arrs)


def _exchange_copies(ins, lands, send_sems, recv_sems, local_sems, scatter):
    n = len(ins)
    me = 4 * lax.axis_index("x") + 2 * lax.axis_index("y") + lax.axis_index("c")
    copies = []
    for w in range(n):
        src = ins[w].at[me] if scatter else ins[w]
        copies.append(pltpu.make_async_copy(src, lands[w].at[me], local_sems.at[w]))
    for k in range(1, N_DEV):
        peer, pidx = _peer(k)
        for w in range(n):
            src = ins[w].at[pidx] if scatter else ins[w]
            copies.append(pltpu.make_async_remote_copy(
                src_ref=src, dst_ref=lands[w].at[me], send_sem=send_sems.at[w * (N_DEV - 1) + k - 1],
                recv_sem=recv_sems.at[w * (N_DEV - 1) + k - 1],
                device_id=peer, device_id_type=pl.DeviceIdType.MESH))
    return copies


_HBM_SPEC = pl.BlockSpec(memory_space=pltpu.HBM)
_SEM_SPEC = pl.BlockSpec(memory_space=pltpu.SEMAPHORE)
_DATAFLOW = pltpu.SideEffectType.DATAFLOW_SIDE_EFFECTING


def _exchange_start(arrs, scatter, name):
    n = len(arrs)
    land_shapes = [a.shape if scatter else (N_DEV,) + a.shape for a in arrs]

    def body(*refs):
        ins, lands = refs[:n], refs[n:2 * n]
        send_sems, recv_sems, local_sems = refs[2 * n:2 * n + 3]
        token = refs[-1]
        for cp in _exchange_copies(ins, lands, send_sems, recv_sems, local_sems, scatter):
            cp.start()
        token[...] = jnp.zeros_like(token)

    out_shape = ([pltpu.SemaphoreType.DMA((n * (N_DEV - 1),)), pltpu.SemaphoreType.DMA((n * (N_DEV - 1),)),
                  pltpu.SemaphoreType.DMA((n,))]
                 + [pltpu.HBM(a.shape, a.dtype) for a in arrs]
                 + [pltpu.HBM(s, a.dtype) for s, a in zip(land_shapes, arrs)]
                 + [S((8, 128), F32)])
    args = ([pltpu.with_memory_space_constraint(a, pltpu.HBM) for a in arrs]
            + [pltpu.with_memory_space_constraint(lax.empty(s, a.dtype), pltpu.HBM) for s, a in zip(land_shapes, arrs)])
    outs = pl.pallas_call(
        body, in_specs=[_HBM_SPEC] * (2 * n),
        out_specs=[_SEM_SPEC] * 3 + [_HBM_SPEC] * (2 * n) + [pl.BlockSpec(memory_space=pltpu.VMEM)],
        out_shape=out_shape, input_output_aliases={j: 3 + j for j in range(2 * n)},
        name=name, compiler_params=pltpu.CompilerParams(has_side_effects=_DATAFLOW))(*args)
    return (n, scatter, outs[:3], outs[3:3 + n], outs[3 + n:3 + 2 * n]), outs[-1]


def _exchange_wait(state, after, name):
    n, scatter, sems, ins, lands = state

    def body(*refs):
        ins_r, lands_r = refs[:n], refs[n:2 * n]
        send_sems, recv_sems, local_sems = refs[2 * n:2 * n + 3]
        for cp in _exchange_copies(ins_r, lands_r, send_sems, recv_sems, local_sems, scatter):
            cp.wait()

    outs = pl.pallas_call(
        body, in_specs=[_HBM_SPEC] * (2 * n) + [_SEM_SPEC] * 3 + [pl.BlockSpec(memory_space=pl.ANY)],
        out_specs=[_HBM_SPEC] * (2 * n),
        out_shape=[pltpu.HBM(a.shape, a.dtype) for a in ins] + [pltpu.HBM(a.shape, a.dtype) for a in lands],
        input_output_aliases={j: j for j in range(2 * n)},
        name=name, compiler_params=pltpu.CompilerParams(has_side_effects=_DATAFLOW))(*ins, *lands, *sems, after)
    return list(outs[n:])


def _adamw(w, m, v, parts, name):
    R, C = w.shape
    P = parts.shape[0]
    tr = _pick(R, tuple(t for t in (256, 176, 128, 64, 32, 16, 8) if P * t * C * 4 <= ADAMW_BLOCK_BYTES))
    c1 = 1.0 / (1.0 - ADAM_B1 ** ADAM_STEP)
    c2 = 1.0 / (1.0 - ADAM_B2 ** ADAM_STEP)

    def body(w_ref, m_ref, v_ref, p_ref, g_ref, d_ref, nm_ref, nv_ref):
        g = p_ref[0].astype(F32)
        for j in range(1, P):
            g = g + p_ref[j].astype(F32)
        m2 = ADAM_B1 * m_ref[...] + (1.0 - ADAM_B1) * g
        v2 = ADAM_B2 * v_ref[...] + (1.0 - ADAM_B2) * (g * g)
        g_ref[...] = g
        nm_ref[...] = m2
        nv_ref[...] = v2
        d_ref[...] = -ADAM_LR * ((m2 * c1) / (jnp.sqrt(v2 * c2) + ADAM_EPS) + ADAM_WD * w_ref[...])

    row = _bs((tr, C), lambda i: (i, 0))
    return pl.pallas_call(
        body, grid=(R // tr,), in_specs=[row, row, row, _bs((P, tr, C), lambda i: (0, i, 0))], out_specs=[row] * 4,
        out_shape=[S((R, C), F32)] * 4, name=name, compiler_params=_cparams("parallel"))(w, m, v, parts)


def _sum_parts(parts, name):
    P, R, C = parts.shape

    def body(p_ref, o_ref):
        g = p_ref[0]
        for j in range(1, P):
            g = g + p_ref[j]
        o_ref[...] = g

    return pl.pallas_call(body, out_shape=S((R, C), F32), name=name, compiler_params=_cparams())(parts)


def _pack(arrs):
    flat = jnp.concatenate([a.reshape(-1) for a in arrs])
    rows = -(-flat.shape[0] // 1024) * 8
    return jnp.pad(flat, (0, rows * 128 - flat.shape[0])).reshape(rows, 128)


def _unpack(packed, shapes):
    flat = packed.reshape(-1)
    out, off = [], 0
    for s in shapes:
        n = int(np.prod(s))
        out.append(flat[off:off + n].reshape(s))
        off += n
    return out


def _behind(a, token):
    return a if token is None else a + token[0, 0]


def _local_step(x, mem, target, p, comm=None):
    table = p["rel_bias_table"]
    dils = [dil for _, dil in ATTN_GROUPS]
    xn, xn_classes = _rms_fwd_classes(x, p["attn_norm_w"], dils[1:], "attn_norm_fwd")
    xn_c = [xn] + xn_classes
    w_in = p["w_in"]
    qkv_w = 3 * N_GROUPS * GW
    wq = [jnp.concatenate([w_in[:, (N_GROUPS * part + g) * GW:(N_GROUPS * part + g + 1) * GW] for part in range(3)], axis=1)
          for g in range(N_GROUPS)]
    wc = w_in[:, qkv_w:qkv_w + 3 * GW]
    wg = w_in[:, qkv_w + 3 * GW:]
    zq = [_matmul(xn_c[g], wq[g], out_dtype=MXU, name=f"mm_in_qkv{g}") for g in range(N_GROUPS)]
    zc = _matmul(xn, wc, name="mm_in_c")
    zg = _matmul(xn, wg, name="mm_in_g")
    os_, lses = [], []
    for g, dil in enumerate(dils):
        o, l = _attn_fwd(zq[g], table, p["q_norm_w"][g:g + 1], p["k_norm_w"][g:g + 1], g, dil)
        os_.append(o)
        lses.append(l)
    attn, lse, attn_c, lse_c = _attn_merge(os_, lses, dils)
    u = _conv_fwd(zc, p["conv_dw_w"], p["conv_dw_b"], p["conv_ln_w"], p["conv_ln_b"])
    if comm is not None:
        p = {**p, **comm.late_weights(after=u)}
    mk, mv = _mem_fwd(mem, p["mem_norm_w"], p["w_mem_kv"], p["xk_norm_w"])
    cross = _cross_fwd(zc, mk, mv, p["xq_norm_w"])
    merged = _branch_fwd(attn, u, cross, zg, p["b_gate"], p["w_attn_o"], p["w_conv_o"], p["w_cross_o"])
    h1 = _matmul(merged, p["w_out"], residual=x, name="mm_out")
    hn = _rms_fwd(h1, p["ffn_norm_w"], "ffn_norm_fwd")
    up0 = _matmul(hn, p["w_up"], name="mm_up")
    f, up = _ffn_act_fwd(up0, p["ffn_conv_w"], p["ffn_conv_b"])
    h2 = _matmul(f, p["w_down"], residual=h1, name="mm_down")
    dh2, lsum = _loss_head(h2, target)
    g = {}
    d_f = _matmul(dh2, p["w_down"], tb=True, name="mm_down_dx")
    g["w_down"] = _matmul(f, dh2, ta=True, name="mm_down_dw")
    d_up0, g["ffn_conv_w"], g["ffn_conv_b"] = _ffn_act_bwd(up0, up, p["ffn_conv_w"], d_f)
    dhn = _matmul(d_up0, p["w_up"], tb=True, name="mm_up_dx")
    g["w_up"] = _matmul(hn, d_up0, ta=True, name="mm_up_dw")
    dh1, g["ffn_norm_w"] = _rms_bwd(h1, p["ffn_norm_w"], [dhn], dh2, "ffn_norm_bwd")
    d_merged = _matmul(dh1, p["w_out"], tb=True, name="mm_out_dx")
    g["w_out"] = _matmul(merged, dh1, ta=True, name="mm_out_dw")
    (d_zg, d_attn_c, d_u, d_cross, g["w_attn_o"], g["w_conv_o"], g["w_cross_o"], g["b_gate"]) = _branch_bwd(
        d_merged, attn, u, cross, zg, p["b_gate"], p["w_attn_o"], p["w_conv_o"], p["w_cross_o"], dils)
    d_xq, dmk, dmv, g["xq_norm_w"] = _cross_bwd(zc, mk, mv, p["xq_norm_w"], d_cross)
    g["w_mem_kv"], g["mem_norm_w"], g["xk_norm_w"] = _mem_bwd(mem, p["mem_norm_w"], p["w_mem_kv"], p["xk_norm_w"], dmk, dmv)
    tok = comm.start_early_grads(g) if comm is not None else None
    d_val, d_gate, g["conv_dw_w"], g["conv_dw_b"], g["conv_ln_w"], g["conv_ln_b"] = _conv_bwd(
        zc, p["conv_dw_w"], _behind(p["conv_dw_b"], tok), p["conv_ln_w"], p["conv_ln_b"], d_u)
    dzq, dqw, dkw, dtab = [], [], [], []
    for gi, dil in enumerate(dils):
        r = _attn_bwd(zq[gi], table, p["q_norm_w"][gi:gi + 1], p["k_norm_w"][gi:gi + 1], d_attn_c[gi], attn_c[gi],
                      lse_c[gi], gi, dil)
        for lst, val in zip((dzq, dqw, dkw, dtab), r):
            lst.append(val)
    g["q_norm_w"] = jnp.concatenate(dqw, axis=0)
    g["k_norm_w"] = jnp.concatenate(dkw, axis=0)
    g["rel_bias_table"] = jnp.concatenate(dtab, axis=1)
    d_zc = jnp.concatenate([d_val, d_gate, d_xq], axis=1)
    gq = [_matmul(xn_c[gi], dzq[gi], ta=True, name=f"mm_in_qkv{gi}_dw") for gi in range(N_GROUPS)]
    gc = _matmul(xn, d_zc, ta=True, name="mm_in_c_dw")
    gg = _matmul(xn, d_zg, ta=True, name="mm_in_g_dw")
    g["w_in"] = jnp.concatenate(
        [gq[gi][:, part * GW:(part + 1) * GW] for part in range(3) for gi in range(N_GROUPS)] + [gc, gg], axis=1)
    tok = comm.start_w_in_grad(g["w_in"]) if comm is not None else None
    dxn = _matmul(d_zg, wg, tb=True, after=tok, name="mm_in_g_dx")
    dxn = _matmul(d_zc, wc, tb=True, residual=dxn, name="mm_in_c_dx")
    dxn = _matmul(dzq[0], wq[0], tb=True, residual=dxn, name="mm_in_qkv0_dx")
    dx_classes = [(_matmul(dzq[gi], wq[gi], tb=True, name=f"mm_in_qkv{gi}_dx"), dils[gi]) for gi in range(1, N_GROUPS)]
    grad_x, g["attn_norm_w"] = _rms_bwd(x, p["attn_norm_w"], [dxn], dh1, "attn_norm_bwd", class_dys=dx_classes)
    return lsum[0, 0], grad_x, g


WEIGHT_NAMES = ["rel_bias_table", "attn_norm_w", "w_in", "b_gate", "q_norm_w", "k_norm_w", "w_attn_o", "conv_dw_w",
                "conv_dw_b", "conv_ln_w", "conv_ln_b", "w_conv_o", "mem_norm_w", "w_mem_kv", "xq_norm_w", "xk_norm_w",
                "w_cross_o", "w_out", "ffn_norm_w", "w_up", "ffn_conv_w", "ffn_conv_b", "w_down"]
COL_SHARDED = ("w_in", "w_attn_o", "w_conv_o", "w_cross_o", "w_up")
ROW_SHARDED = ("w_mem_kv", "w_out", "w_down")
SMALL_COL_SHARDED = ("conv_dw_w", "ffn_conv_w")
BIG = COL_SHARDED + ROW_SHARDED


def _cols_to_blocks(a):
    k, n8 = a.shape
    return a.reshape(k, N_DEV, n8 // N_DEV).transpose(1, 0, 2)


def _blocks_to_cols(a):
    return a.transpose(1, 0, 2).reshape(a.shape[1], N_DEV * a.shape[2])


def _step(x, mem, target, w, m, v):
    me = 4 * lax.axis_index("x") + 2 * lax.axis_index("y") + lax.axis_index("c")

    def to_full(n, blocks):
        return _blocks_to_cols(blocks) if n in COL_SHARDED + SMALL_COL_SHARDED else blocks.reshape(-1, blocks.shape[-1])

    def to_blocks(n, grad):
        blocks = _cols_to_blocks(grad) if n in COL_SHARDED else grad.reshape(N_DEV, -1, grad.shape[-1])
        return blocks.astype(MXU)

    first = ("w_in",) + SMALL_COL_SHARDED
    late = tuple(n for n in BIG if n != "w_in")
    cast = lambda n: w[n].astype(MXU) if n in BIG else w[n]
    first_state, _ = _exchange_start([cast(n) for n in first], False, "gather_first_start")
    late_state, late_token = _exchange_start([cast(n) for n in late], False, "gather_late_start")
    got = _exchange_wait(first_state, late_token, "gather_first_wait")
    p = {n: w[n] for n in WEIGHT_NAMES if n not in BIG + SMALL_COL_SHARDED}
    p.update({n: to_full(n, b) for n, b in zip(first, got)})

    class Comm:
        def late_weights(self, after):
            return {n: to_full(n, b) for n, b in zip(late, _exchange_wait(late_state, after, "gather_late_wait"))}

        def start_early_grads(self, g):
            self.early_state, token = _exchange_start([to_blocks(n, g[n]) for n in late], True, "scatter_early_start")
            return token

        def start_w_in_grad(self, grad):
            self.w_in_state, token = _exchange_start([to_blocks("w_in", grad)], True, "scatter_w_in_start")
            return token

    comm = Comm()
    lsum, grad_x, g = _local_step(x, mem, target, p, comm)
    small_names = [n for n in WEIGHT_NAMES if n not in BIG]
    small_shapes = [g[n].shape for n in small_names]
    small_parts = _exchange([_pack([g[n] for n in small_names])], False, "gather_small_grads")[0]
    gsmall = dict(zip(small_names, _unpack(_sum_parts(small_parts, "sum_small_grads"), small_shapes)))
    for n in SMALL_COL_SHARDED:
        width = w[n].shape[-1]
        gsmall[n] = lax.dynamic_slice_in_dim(gsmall[n], me * width, width, axis=1)
    res = {}
    parts = dict(zip(late, _exchange_wait(comm.early_state, grad_x, "scatter_early_wait")))
    for n in late:
        res[n] = _adamw(w[n], m[n], v[n], parts[n], "adamw_" + n)
    w_in_parts = _exchange_wait(comm.w_in_state, res[late[-1]][1], "scatter_w_in_wait")[0]
    res["w_in"] = _adamw(w["w_in"], m["w_in"], v["w_in"], w_in_parts, "adamw_w_in")
    shapes = [w[n].shape for n in small_names]
    packed = [_pack([d[n] for n in small_names]) for d in (w, m, v, gsmall)]
    outs = _adamw(packed[0], packed[1], packed[2], packed[3][None], "adamw_small")
    unpacked = [_unpack(o, shapes) for o in outs]
    for j, n in enumerate(small_names):
        res[n] = tuple(unpacked[q][j] for q in range(4))
    return lsum, grad_x, res


def kernel(x, mem, rel_bias_table, attn_norm_w, w_in, b_gate, q_norm_w, k_norm_w, w_attn_o, conv_dw_w, conv_dw_b, conv_ln_w, conv_ln_b, w_conv_o, mem_norm_w, w_mem_kv, xq_norm_w, xk_norm_w, w_cross_o, w_out, ffn_norm_w, w_up, ffn_conv_w, ffn_conv_b, w_down, loss_target, m_rel_bias_table, m_attn_norm_w, m_w_in, m_b_gate, m_q_norm_w, m_k_norm_w, m_w_attn_o, m_conv_dw_w, m_conv_dw_b, m_conv_ln_w, m_conv_ln_b, m_w_conv_o, m_mem_norm_w, m_w_mem_kv, m_xq_norm_w, m_xk_norm_w, m_w_cross_o, m_w_out, m_ffn_norm_w, m_w_up, m_ffn_conv_w, m_ffn_conv_b, m_w_down, v_rel_bias_table, v_attn_norm_w, v_w_in, v_b_gate, v_q_norm_w, v_k_norm_w, v_w_attn_o, v_conv_dw_w, v_conv_dw_b, v_conv_ln_w, v_conv_ln_b, v_w_conv_o, v_mem_norm_w, v_w_mem_kv, v_xq_norm_w, v_xk_norm_w, v_w_cross_o, v_w_out, v_ffn_norm_w, v_w_up, v_ffn_conv_w, v_ffn_conv_b, v_w_down):
    ws = dict(zip(WEIGHT_NAMES, (rel_bias_table, attn_norm_w, w_in, b_gate, q_norm_w, k_norm_w, w_attn_o, conv_dw_w, conv_dw_b, conv_ln_w, conv_ln_b, w_conv_o, mem_norm_w, w_mem_kv, xq_norm_w, xk_norm_w, w_cross_o, w_out, ffn_norm_w, w_up, ffn_conv_w, ffn_conv_b, w_down)))
    ms = dict(zip(WEIGHT_NAMES, (m_rel_bias_table, m_attn_norm_w, m_w_in, m_b_gate, m_q_norm_w, m_k_norm_w, m_w_attn_o, m_conv_dw_w, m_conv_dw_b, m_conv_ln_w, m_conv_ln_b, m_w_conv_o, m_mem_norm_w, m_w_mem_kv, m_xq_norm_w, m_xk_norm_w, m_w_cross_o, m_w_out, m_ffn_norm_w, m_w_up, m_ffn_conv_w, m_ffn_conv_b, m_w_down)))
    vs = dict(zip(WEIGHT_NAMES, (v_rel_bias_table, v_attn_norm_w, v_w_in, v_b_gate, v_q_norm_w, v_k_norm_w, v_w_attn_o, v_conv_dw_w, v_conv_dw_b, v_conv_ln_w, v_conv_ln_b, v_w_conv_o, v_mem_norm_w, v_w_mem_kv, v_xq_norm_w, v_xk_norm_w, v_w_cross_o, v_w_out, v_ffn_norm_w, v_w_up, v_ffn_conv_w, v_ffn_conv_b, v_w_down)))
    full_shapes = {n: ws[n].shape for n in WEIGHT_NAMES}

    def squeeze(d):
        return {n: (a if n == "rel_bias_table" else a[0]) for n, a in d.items()}

    w, m, v = squeeze(ws), squeeze(ms), squeeze(vs)
    for d in (w, m, v):
        for n in WEIGHT_NAMES:
            if d[n].ndim == 1:
                d[n] = d[n][None]
    lsum, grad_x, res = _step(x[0], mem[0], loss_target[0], w, m, v)
    loss = lax.psum(0.5 / D_MODEL * lsum, ("x", "y", "c"))
    outs = [loss, grad_x[None]]
    for q in range(4):
        outs += [res[n][q].reshape(full_shapes[n]) for n in WEIGHT_NAMES]
    return tuple(outs)
```
